```python
import math
import numpy as np
import jax
import jax.numpy as jnp
from jax import lax

D_MODEL = 2048
BATCH = 1
SEQ = 8192
DEPTH = 2

GRID_W = 64
CTX_LEN = 256
HEAD_DIM = 128
NA_HEADS = 8
NB_Q_HEADS = 8
NB_KV_HEADS = 2
NA_ROWS = 8
NA_COLS = 16
NA_QCOLS = 16
NA_KCOLS = NA_QCOLS + NA_COLS
SW_RADIUS = 128
SW_BLOCK = 128
ROPE_BASE = 10000.0
SSM_GROUP = 16
SSM_GROUPS = D_MODEL // SSM_GROUP
SSM_STATE = 64
D_FF = -(-8 * D_MODEL // (3 * 256)) * 256
A_WIDTH = NA_HEADS * HEAD_DIM
B_Q_WIDTH = NB_Q_HEADS * HEAD_DIM
B_KV_WIDTH = NB_KV_HEADS * HEAD_DIM
IN_WIDTH = 3 * A_WIDTH + B_Q_WIDTH + 2 * B_KV_WIDTH
MIX_WIDTH = A_WIDTH + B_Q_WIDTH
IN_SPLITS = (A_WIDTH, 2 * A_WIDTH, 3 * A_WIDTH, 3 * A_WIDTH + B_Q_WIDTH, 3 * A_WIDTH + B_Q_WIDTH + B_KV_WIDTH)
EPS = 1e-6
NEG_INF = -1e30

kernel_name = 'hybrid_natten_swa_s5_prefix_dit'


def rms_norm(x, g):
    xf = x.astype(jnp.float32)
    y = xf * lax.rsqrt(jnp.mean(xf * xf, axis=-1, keepdims=True) + EPS)
    return (y * g.astype(jnp.float32)).astype(x.dtype)


def ada_mod(cvec, w, b):
    return jnp.split(jax.nn.silu(cvec) @ w + b, 6, axis=-1)


def swiglu(h, w1, w3, w2):
    return (jax.nn.silu(h @ w1) * (h @ w3)) @ w2


def axial_rope(x, pos_r, pos_c):
    half = x.shape[-1] // 2
    quarter = half // 2
    inv_freq = ROPE_BASE ** (-jnp.arange(quarter, dtype=jnp.float32) / quarter)
    xf = x.astype(jnp.float32)

    def rotate(xa, pos):
        ang = pos[:, None] * inv_freq[None, :]
        cos = jnp.cos(ang)[None, :, None, :]
        sin = jnp.sin(ang)[None, :, None, :]
        x1, x2 = xa[..., :quarter], xa[..., quarter:]
        return jnp.concatenate([x1 * cos - x2 * sin, x2 * cos + x1 * sin], axis=-1)

    return jnp.concatenate([rotate(xf[..., :half], pos_r), rotate(xf[..., half:], pos_c)], axis=-1).astype(x.dtype)


def context_attention(q, k, v, sink):
    b, m, hq, d = q.shape
    hkv = k.shape[2]
    grp = hq // hkv
    qg = q.reshape(b, m, hkv, grp, d)
    s = jnp.einsum('bqhgd,bkhd->bhgqk', qg, k, preferred_element_type=jnp.float32) * (d ** -0.5)
    if sink is not None:
        s_sink = jnp.broadcast_to(sink.astype(jnp.float32).reshape(1, hkv, grp, 1, 1), s.shape[:-1] + (1,))
        s = jnp.concatenate([s, s_sink], axis=-1)
    p = jax.nn.softmax(s, axis=-1)[..., :m].astype(v.dtype)
    return jnp.einsum('bhgqk,bkhd->bqhgd', p, v).reshape(b, m, hq, d)


def neighbourhood_attention(q, k, v, k_ctx, v_ctx, rpb):
    b, seq, h, d = q.shape
    rows = seq // GRID_W
    kh = min(NA_ROWS, rows)
    ncb = GRID_W // NA_QCOLS
    r = jnp.arange(rows)
    key_rows = jnp.clip(r - kh // 2, 0, rows - kh)[:, None] + jnp.arange(kh)[None, :]
    blk = jnp.arange(ncb)
    key_cols = (jnp.clip(blk * NA_QCOLS - NA_COLS // 2, 0, GRID_W - NA_KCOLS)[:, None]
                + jnp.arange(NA_KCOLS)[None, :])
    q_cols = blk[:, None] * NA_QCOLS + jnp.arange(NA_QCOLS)[None, :]
    win_start = jnp.clip(q_cols - NA_COLS // 2, 0, GRID_W - NA_COLS)[..., None]
    kc_b = key_cols[:, None, :]
    col_valid = (kc_b >= win_start) & (kc_b < win_start + NA_COLS)
    row_idx = key_rows - r[:, None] + NA_ROWS - 1
    col_idx = jnp.clip(kc_b - q_cols[..., None] + NA_COLS - 1, 0, 2 * NA_COLS - 2)
    bias = rpb.astype(jnp.float32)[:, row_idx[:, None, None, :, None], col_idx[None, :, :, None, :]]

    gather_r = key_rows[:, None, :, None]
    gather_c = key_cols[None, :, None, :]
    kg = k.reshape(b, rows, GRID_W, h, d)[:, gather_r, gather_c]
    vg = v.reshape(b, rows, GRID_W, h, d)[:, gather_r, gather_c]
    qg = q.reshape(b, rows, ncb, NA_QCOLS, h, d)
    scale = d ** -0.5
    n_loc = kh * NA_KCOLS
    s_loc = jnp.einsum('brnqhd,brnikhd->bhrnqik', qg, kg, preferred_element_type=jnp.float32) * scale + bias[None]
    s_loc = jnp.where(col_valid[:, :, None, :], s_loc, NEG_INF).reshape(b, h, rows, ncb, NA_QCOLS, n_loc)
    s_ctx = jnp.einsum('brnqhd,bmhd->bhrnqm', qg, k_ctx, preferred_element_type=jnp.float32) * scale
    p = jax.nn.softmax(jnp.concatenate([s_loc, s_ctx], axis=-1), axis=-1).astype(v.dtype)
    p_loc = p[..., :n_loc].reshape(b, h, rows, ncb, NA_QCOLS, kh, NA_KCOLS)
    o = (jnp.einsum('bhrnqik,brnikhd->brnqhd', p_loc, vg)
         + jnp.einsum('bhrnqm,bmhd->brnqhd', p[..., n_loc:], v_ctx))
    return o.reshape(b, seq, h, d)


def window_attention(q, k, v, k_ctx, v_ctx, sink):
    b, seq, hq, d = q.shape
    hkv = k.shape[2]
    grp = hq // hkv
    nb = seq // SW_BLOCK
    qb = q.reshape(b, nb, SW_BLOCK, hkv, grp, d)

    def band(t):
        tb = jnp.pad(t, ((0, 0), (SW_BLOCK, SW_BLOCK), (0, 0), (0, 0))).reshape(b, nb + 2, SW_BLOCK, hkv, d)
        return jnp.concatenate([tb[:, :-2], tb[:, 1:-1], tb[:, 2:]], axis=2)

    kb, vb = band(k), band(v)
    blocks = jnp.arange(nb)[:, None]
    q_pos = blocks * SW_BLOCK + jnp.arange(SW_BLOCK)[None, :]
    k_pos = (blocks - 1) * SW_BLOCK + jnp.arange(3 * SW_BLOCK)[None, :]
    k_pos_b = k_pos[:, None, :]
    valid = (jnp.abs(k_pos_b - q_pos[:, :, None]) <= SW_RADIUS) & (k_pos_b >= 0) & (k_pos_b < seq)
    scale = d ** -0.5
    s_loc = jnp.einsum('bnqhgd,bnkhd->bhgnqk', qb, kb, preferred_element_type=jnp.float32) * scale
    s_loc = jnp.where(valid, s_loc, NEG_INF)
    s_ctx = jnp.einsum('bnqhgd,bmhd->bhgnqm', qb, k_ctx, preferred_element_type=jnp.float32) * scale
    s_sink = jnp.broadcast_to(sink.astype(jnp.float32).reshape(1, hkv, grp, 1, 1, 1), s_loc.shape[:-1] + (1,))
    p = jax.nn.softmax(jnp.concatenate([s_loc, s_ctx, s_sink], axis=-1), axis=-1).astype(v.dtype)
    n_loc = 3 * SW_BLOCK
    m = k_ctx.shape[1]
    o = (jnp.einsum('bhgnqk,bnkhd->bnqhgd', p[..., :n_loc], vb)
         + jnp.einsum('bhgnqm,bmhd->bnqhgd', p[..., n_loc:n_loc + m], v_ctx))
    return o.reshape(b, seq, hq, d)


def hybrid_attention(hx, hc, w_in, w_out, rpb, sink, pos_r, pos_c, need_ctx):
    def project(h):
        b, t, _ = h.shape
        parts = jnp.split(h @ w_in, IN_SPLITS, axis=-1)
        return [z.reshape(b, t, -1, HEAD_DIM) for z in parts]

    b, seq, _ = hx.shape
    qa, ka, va, qb, kb, vb = project(hx)
    qa_c, ka_c, va_c, qb_c, kb_c, vb_c = project(hc)
    oa = neighbourhood_attention(qa, ka, va, ka_c, va_c, rpb)
    ob = window_attention(axial_rope(qb, pos_r, pos_c), axial_rope(kb, pos_r, pos_c), vb, kb_c, vb_c, sink)
    yx = jnp.concatenate([oa.reshape(b, seq, A_WIDTH), ob.reshape(b, seq, B_Q_WIDTH)], axis=-1) @ w_out
    if not need_ctx:
        return yx, None
    m = hc.shape[1]
    oa_c = context_attention(qa_c, ka_c, va_c, None)
    ob_c = context_attention(qb_c, kb_c, vb_c, sink)
    yc = jnp.concatenate([oa_c.reshape(b, m, A_WIDTH), ob_c.reshape(b, m, B_Q_WIDTH)], axis=-1) @ w_out
    return yx, yc


def s5_discretise(a_re, a_im, log_dt, b_re, b_im):
    ar = a_re.astype(jnp.float32)
    ai = a_im.astype(jnp.float32)
    dt = jnp.exp(log_dt.astype(jnp.float32))[:, None]
    mag = jnp.exp(ar * dt)
    lam_r, lam_i = mag * jnp.cos(ai * dt), mag * jnp.sin(ai * dt)
    den = ar * ar + ai * ai
    nr = lam_r - 1.0
    coef_r = (nr * ar + lam_i * ai) / den
    coef_i = (lam_i * ar - nr * ai) / den
    br, bi = b_re.astype(jnp.float32), b_im.astype(jnp.float32)
    bbar_r = coef_r[..., None] * br - coef_i[..., None] * bi
    bbar_i = coef_r[..., None] * bi + coef_i[..., None] * br
    return lam_r, lam_i, bbar_r, bbar_i


def complex_diag_scan(lam_r, lam_i, u_r, u_i, reverse):
    a_r = jnp.broadcast_to(lam_r, u_r.shape)
    a_i = jnp.broadcast_to(lam_i, u_i.shape)

    def combine(e1, e2):
        a1r, a1i, b1r, b1i = e1
        a2r, a2i, b2r, b2i = e2
        return (a2r * a1r - a2i * a1i, a2r * a1i + a2i * a1r,
                a2r * b1r - a2i * b1i + b2r, a2r * b1i + a2i * b1r + b2i)

    return lax.associative_scan(combine, (a_r, a_i, u_r, u_i), reverse=reverse, axis=1)


def s5_direction(ux, uc, a_re, a_im, log_dt, b_re, b_im, c_re, c_im, reverse, need_ctx):
    lam_r, lam_i, bbar_r, bbar_i = s5_discretise(a_re, a_im, log_dt, b_re, b_im)
    cr, ci = c_re.astype(jnp.float32), c_im.astype(jnp.float32)

    def drive(u):
        ug = u.astype(jnp.float32).reshape(u.shape[0], u.shape[1], SSM_GROUPS, SSM_GROUP)
        return jnp.einsum('btgh,gph->btgp', ug, bbar_r), jnp.einsum('btgh,gph->btgp', ug, bbar_i)

    def readout(s_r, s_i):
        y = jnp.einsum('btgp,ghp->btgh', s_r, cr) - jnp.einsum('btgp,ghp->btgh', s_i, ci)
        return y.reshape(y.shape[0], y.shape[1], -1)

    uc_r, uc_i = drive(uc)
    _, _, sc_r, sc_i = complex_diag_scan(lam_r, lam_i, uc_r, uc_i, reverse)
    end = 0 if reverse else -1
    s0_r, s0_i = sc_r[:, end][:, None], sc_i[:, end][:, None]
    ux_r, ux_i = drive(ux)
    pw_r, pw_i, sx_r, sx_i = complex_diag_scan(lam_r, lam_i, ux_r, ux_i, reverse)
    sx_r = sx_r + pw_r * s0_r - pw_i * s0_i
    sx_i = sx_i + pw_r * s0_i + pw_i * s0_r
    y_ctx = readout(sc_r, sc_i) if need_ctx else None
    return readout(sx_r, sx_i), y_ctx


def s5_glu_mixer(ux, uc, a_re, a_im, log_dt, b_re, b_im, c_re, c_im, d_skip, w_glu, b_glu, need_ctx):
    d = d_skip.astype(jnp.float32)
    y_x = d * ux.astype(jnp.float32)
    y_c = d * uc.astype(jnp.float32) if need_ctx else None
    for direction in range(2):
        yx_d, yc_d = s5_direction(ux, uc, a_re[direction], a_im[direction], log_dt[direction],
                                  b_re[direction], b_im[direction], c_re[direction], c_im[direction],
                                  direction == 1, need_ctx)
        y_x = y_x + yx_d
        if need_ctx:
            y_c = y_c + yc_d

    def glu(y, dtype):
        z = jax.nn.gelu(y).astype(dtype) @ w_glu + b_glu
        val, gate = jnp.split(z, 2, axis=-1)
        return val * jax.nn.sigmoid(gate)

    return glu(y_x, ux.dtype), (glu(y_c, uc.dtype) if need_ctx else None)


def setup_inputs(seed: int = 0) -> dict:
    key = jax.random.key(seed)
    keys = iter(jax.random.split(key, 32))
    f32 = jnp.float32

    def normal(shape, scale):
        return jax.random.normal(next(keys), shape, f32) * scale

    d, g, h, p = D_MODEL, SSM_GROUPS, SSM_GROUP, SSM_STATE
    n_attn, n_ssm = (DEPTH + 1) // 2, DEPTH // 2
    a_im0 = math.pi * jnp.arange(p, dtype=f32)
    return {
        'x': normal((BATCH, SEQ, d), 1.0),
        'c': normal((BATCH, d), 1.0),
        'ctx': normal((BATCH, CTX_LEN, d), 1.0),
        'c_ctx': normal((d,), 1.0),
        'ada_w': normal((DEPTH, d, 6 * d), 0.5 * d ** -0.5),
        'ada_b': normal((DEPTH, 6 * d), 0.01),
        'norm_mix': 1.0 + normal((DEPTH, d), 0.02),
        'norm_ffn': 1.0 + normal((DEPTH, d), 0.02),
        'ffn_w1': normal((DEPTH, d, D_FF), d ** -0.5),
        'ffn_w3': normal((DEPTH, d, D_FF), d ** -0.5),
        'ffn_w2': normal((DEPTH, D_FF, d), D_FF ** -0.5),
        'attn_w_in': normal((n_attn, d, IN_WIDTH), d ** -0.5),
        'attn_w_out': normal((n_attn, MIX_WIDTH, d), MIX_WIDTH ** -0.5),
        'attn_rpb': normal((n_attn, NA_HEADS, 2 * NA_ROWS - 1, 2 * NA_COLS - 1), 0.02),
        'attn_sink': normal((n_attn, NB_Q_HEADS), 0.5),
        'ssm_a_re': -0.5 + normal((n_ssm, 2, g, p), 0.01),
        'ssm_a_im': a_im0 + normal((n_ssm, 2, g, p), 0.01),
        'ssm_log_dt': jax.random.uniform(next(keys), (n_ssm, 2, g), f32, math.log(1e-3), math.log(1e-1)),
        'ssm_b_re': normal((n_ssm, 2, g, p, h), (2 * h) ** -0.5),
        'ssm_b_im': normal((n_ssm, 2, g, p, h), (2 * h) ** -0.5),
        'ssm_c_re': normal((n_ssm, 2, g, h, p), p ** -0.5),
        'ssm_c_im': normal((n_ssm, 2, g, h, p), p ** -0.5),
        'ssm_d': normal((n_ssm, d), 0.5),
        'ssm_w_glu': normal((n_ssm, d, 2 * d), d ** -0.5),
        'ssm_b_glu': normal((n_ssm, 2 * d), 0.01),
        'norm_final': 1.0 + normal((d,), 0.02),
    }


def reference(x, c, ctx, c_ctx, ada_w, ada_b, norm_mix, norm_ffn, ffn_w1, ffn_w3, ffn_w2,
              attn_w_in, attn_w_out, attn_rpb, attn_sink,
              ssm_a_re, ssm_a_im, ssm_log_dt, ssm_b_re, ssm_b_im, ssm_c_re, ssm_c_im,
              ssm_d, ssm_w_glu, ssm_b_glu, norm_final):
    seq = x.shape[1]
    t = jnp.arange(seq)
    pos_r = (t // GRID_W).astype(jnp.float32)
    pos_c = (t % GRID_W).astype(jnp.float32)
    for layer in range(DEPTH):
        need_ctx = layer < DEPTH - 1
        i = layer // 2
        sh1, sc1, g1, sh2, sc2, g2 = [m[:, None, :] for m in ada_mod(c, ada_w[layer], ada_b[layer])]
        csh1, csc1, cg1, csh2, csc2, cg2 = ada_mod(c_ctx, ada_w[layer], ada_b[layer])
        hx = rms_norm(x, norm_mix[layer]) * (1.0 + sc1) + sh1
        hc = rms_norm(ctx, norm_mix[layer]) * (1.0 + csc1) + csh1
        if layer % 2 == 0:
            yx, yc = hybrid_attention(hx, hc, attn_w_in[i], attn_w_out[i], attn_rpb[i], attn_sink[i],
                                      pos_r, pos_c, need_ctx)
        else:
            yx, yc = s5_glu_mixer(hx, hc, ssm_a_re[i], ssm_a_im[i], ssm_log_dt[i], ssm_b_re[i], ssm_b_im[i],
                                  ssm_c_re[i], ssm_c_im[i], ssm_d[i], ssm_w_glu[i], ssm_b_glu[i], need_ctx)
        x = x + g1 * yx
        x = x + g2 * swiglu(rms_norm(x, norm_ffn[layer]) * (1.0 + sc2) + sh2,
                            ffn_w1[layer], ffn_w3[layer], ffn_w2[layer])
        if need_ctx:
            ctx = ctx + cg1 * yc
            ctx = ctx + cg2 * swiglu(rms_norm(ctx, norm_ffn[layer]) * (1.0 + csc2) + csh2,
                                     ffn_w1[layer], ffn_w3[layer], ffn_w2[layer])
    return rms_norm(x, norm_final)
```

```python
import functools
import math

import jax
import jax.numpy as jnp
from jax import lax
from jax.experimental import pallas as pl
from jax.experimental.pallas import tpu as pltpu

F32 = jnp.float32
BF16 = jnp.bfloat16

GRID_W = 64
HEAD_DIM = 128
NA_HEADS = 8
NB_Q_HEADS = 8
NB_KV_HEADS = 2
NB_GROUP = NB_Q_HEADS // NB_KV_HEADS
NA_ROWS = 8
NA_COLS = 16
SW_BLOCK = 128
ROPE_BASE = 10000.0
SSM_GROUP = 16
SSM_STATE = 64
EPS = 1e-6
NEG_INF = -1e30
ATTN_SCALE = HEAD_DIM ** -0.5

A_WIDTH = NA_HEADS * HEAD_DIM
B_Q_WIDTH = NB_Q_HEADS * HEAD_DIM
B_KV_WIDTH = NB_KV_HEADS * HEAD_DIM
QA_BLK = 0
KA_BLK = NA_HEADS
VA_BLK = 2 * NA_HEADS
QB_BLK = 3 * NA_HEADS
KB_BLK = QB_BLK + NB_Q_HEADS
VB_BLK = KB_BLK + NB_KV_HEADS

NA_QROWS = 8
NA_KROWS = 16
NA_KBLK = 4
S5_CHUNK = 16
S5_GROUPS_PER_STEP = 8
VMEM_LIMIT = 56 * 1024 * 1024


def _cparams(*sem):
    return pltpu.CompilerParams(dimension_semantics=sem, vmem_limit_bytes=VMEM_LIMIT)


def _silu(v):
    return v * (1.0 / (1.0 + jnp.exp(-v)))


def _norm_mod(x, g, sc, sh):
    ms = jnp.mean(x * x, axis=-1, keepdims=True)
    return (x * lax.rsqrt(ms + EPS) * g) * (1.0 + sc) + sh


def _dot_nt(a, b):
    return lax.dot_general(a, b, (((1,), (1,)), ((), ())), preferred_element_type=F32)


def _ada_kernel(cb_ref, w_ref, b_ref, o_ref, *, rows_per_step):
    d, tn = w_ref.shape
    rep = tn // 128

    def body(i, acc):
        r = pl.multiple_of(i * rows_per_step, rows_per_step)
        w = w_ref[pl.ds(r, rows_per_step), :]
        out = []
        for v in range(2):
            s = _silu(cb_ref[v, pl.ds(r, rows_per_step), :])
            st = jnp.concatenate([s] * rep, axis=1)
            out.append(acc[v] + jnp.sum((w * st).reshape(rows_per_step // 8, 8, tn), axis=0))
        return tuple(out)

    zero = jnp.zeros((8, tn), F32)
    acc = lax.fori_loop(0, d // rows_per_step, body, (zero, zero))
    o_ref[...] = jnp.concatenate([jnp.sum(a, axis=0, keepdims=True) for a in acc], axis=0) + b_ref[...]


def _ada_mod(c, c_ctx, ada_w, ada_b):
    depth, d, n = ada_w.shape
    tn = 512
    cb = jnp.stack([jnp.broadcast_to(c.reshape(d, 1), (d, 128)),
                    jnp.broadcast_to(c_ctx.reshape(d, 1), (d, 128))])
    return pl.pallas_call(
        functools.partial(_ada_kernel, rows_per_step=64),
        grid=(depth, n // tn),
        in_specs=[pl.BlockSpec((2, d, 128), lambda l, j: (0, 0, 0)),
                  pl.BlockSpec((None, d, tn), lambda l, j: (l, 0, j)),
                  pl.BlockSpec((None, 1, tn), lambda l, j: (l, 0, j))],
        out_specs=pl.BlockSpec((None, 2, tn), lambda l, j: (l, 0, j)),
        out_shape=jax.ShapeDtypeStruct((depth, 2, n), F32),
        compiler_params=_cparams("arbitrary", "arbitrary"),
        name="ada_mod",
    )(cb, ada_w, ada_b.reshape(depth, 1, n))


def _rope(a, cos, sin):
    lane = lax.broadcasted_iota(jnp.int32, a.shape, 1)
    partner = jnp.where((lane & 63) < 32, pltpu.roll(a, 96, axis=1), pltpu.roll(a, 32, axis=1))
    return a * cos + partner * sin


def _qkv_kernel(x_ref, g_ref, sc_ref, sh_ref, w_ref, cos_ref, sin_ref, o_ref, h_scr, *, rope):
    j = pl.program_id(1)
    tn = o_ref.shape[1]
    heads = tn // HEAD_DIM

    @pl.when(j == 0)
    def _():
        h_scr[...] = _norm_mod(x_ref[...], g_ref[...], sc_ref[...], sh_ref[...]).astype(BF16)

    acc = jnp.dot(h_scr[...], w_ref[...], preferred_element_type=F32)
    col0 = j * heads

    def rotated(n_heads, scale):
        parts = []
        for hh in range(heads):
            a = acc[:, hh * HEAD_DIM:(hh + 1) * HEAD_DIM]
            if hh < n_heads:
                if rope:
                    a = _rope(a, cos_ref[...], sin_ref[...])
                a = a * scale
            parts.append(a)
        return jnp.concatenate(parts, axis=1)

    is_qa = col0 < KA_BLK
    is_qb = (col0 >= QB_BLK) & (col0 < KB_BLK)
    is_kb = col0 == KB_BLK

    @pl.when(is_qa)
    def _():
        o_ref[...] = (acc * ATTN_SCALE).astype(o_ref.dtype)

    @pl.when(is_qb)
    def _():
        o_ref[...] = rotated(heads, ATTN_SCALE).astype(o_ref.dtype)

    @pl.when(is_kb)
    def _():
        o_ref[...] = rotated(NB_KV_HEADS, 1.0).astype(o_ref.dtype)

    @pl.when(jnp.logical_not(is_qa | is_qb | is_kb))
    def _():
        o_ref[...] = acc.astype(o_ref.dtype)


def _qkv_proj(x, g, sc, sh, w, cos, sin, *, rope, tm):
    m, d = x.shape
    n = w.shape[1]
    tn = 512
    assert KB_BLK % (tn // HEAD_DIM) == 0 and m % tm == 0 and n % tn == 0
    vec = pl.BlockSpec((1, d), lambda i, j: (0, 0))
    tab = pl.BlockSpec((tm, HEAD_DIM), lambda i, j: (i, 0))
    return pl.pallas_call(
        functools.partial(_qkv_kernel, rope=rope),
        grid=(m // tm, n // tn),
        in_specs=[pl.BlockSpec((tm, d), lambda i, j: (i, 0)), vec, vec, vec,
                  pl.BlockSpec((d, tn), lambda i, j: (0, j)), tab, tab],
        out_specs=pl.BlockSpec((tm, tn), lambda i, j: (i, j)),
        out_shape=jax.ShapeDtypeStruct((m, n), BF16),
        scratch_shapes=[pltpu.VMEM((tm, d), BF16)],
        compiler_params=_cparams("arbitrary", "arbitrary"),
        name="qkv_rope" if rope else "qkv_ctx",
    )(x, g, sc, sh, w, cos, sin)


def _rope_tables(seq):
    quarter = HEAD_DIM // 4
    t = jnp.arange(seq)
    pos_r = (t // GRID_W).astype(F32)
    pos_c = (t % GRID_W).astype(F32)
    inv_freq = ROPE_BASE ** (-jnp.arange(quarter, dtype=F32) / quarter)
    ang_r = pos_r[:, None] * inv_freq[None, :]
    ang_c = pos_c[:, None] * inv_freq[None, :]
    cos = jnp.concatenate([jnp.cos(ang_r)] * 2 + [jnp.cos(ang_c)] * 2, axis=1)
    sin = jnp.concatenate([-jnp.sin(ang_r), jnp.sin(ang_r), -jnp.sin(ang_c), jnp.sin(ang_c)], axis=1)
    return cos, sin


def _win_kernel(sink_ref, q_ref, k0_ref, k1_ref, k2_ref, v0_ref, v1_ref, v2_ref, kc_ref, vc_ref, o_ref,
                *, seq):
    h = pl.program_id(0)
    n = pl.program_id(1)
    blk = SW_BLOCK
    q = q_ref[...]
    qs = jnp.concatenate([q[:, g * HEAD_DIM:(g + 1) * HEAD_DIM] for g in range(NB_GROUP)], axis=0)
    k = jnp.concatenate([k0_ref[...], k1_ref[...], k2_ref[...]], axis=0)
    v = jnp.concatenate([v0_ref[...], v1_ref[...], v2_ref[...]], axis=0)
    s = _dot_nt(qs, k)
    row = lax.broadcasted_iota(jnp.int32, s.shape, 0)
    col = lax.broadcasted_iota(jnp.int32, s.shape, 1)
    rel = col - (row & (blk - 1))
    kpos = (n - 1) * blk + col
    valid = (rel >= 0) & (rel <= 2 * blk) & (kpos >= 0) & (kpos < seq)
    s = jnp.where(valid, s, NEG_INF)
    s_ctx = _dot_nt(qs, kc_ref[...])
    grp = lax.broadcasted_iota(jnp.int32, (NB_GROUP * blk, 1), 0) // blk
    sink = jnp.zeros((NB_GROUP * blk, 1), F32)
    for g in range(NB_GROUP):
        sink = jnp.where(grp == g, sink_ref[h * NB_GROUP + g], sink)
    m = jnp.maximum(jnp.maximum(jnp.max(s, axis=-1, keepdims=True), jnp.max(s_ctx, axis=-1, keepdims=True)), sink)
    p = jnp.exp(s - m)
    p_ctx = jnp.exp(s_ctx - m)
    den = jnp.sum(p, axis=-1, keepdims=True) + jnp.sum(p_ctx, axis=-1, keepdims=True) + jnp.exp(sink - m)
    o = (jnp.dot(p.astype(BF16), v, preferred_element_type=F32)
         + jnp.dot(p_ctx.astype(BF16), vc_ref[...], preferred_element_type=F32))
    o = o * (1.0 / den)
    o_ref[...] = jnp.concatenate([o[g * blk:(g + 1) * blk] for g in range(NB_GROUP)], axis=1).astype(o_ref.dtype)


def _window_attention(qkv, qkv_ctx, sink):
    seq = qkv.shape[0]
    m = qkv_ctx.shape[0]
    nb = seq // SW_BLOCK
    qw = NB_GROUP * HEAD_DIM

    def kv_spec(col_blk, shift):
        return pl.BlockSpec((SW_BLOCK, HEAD_DIM),
                            lambda h, n: (jnp.clip(n + shift, 0, nb - 1), col_blk + h))

    return pl.pallas_call(
        functools.partial(_win_kernel, seq=seq),
        grid=(NB_KV_HEADS, nb),
        in_specs=[pl.BlockSpec(memory_space=pltpu.SMEM),
                  pl.BlockSpec((SW_BLOCK, qw), lambda h, n: (n, QB_BLK // NB_GROUP + h)),
                  kv_spec(KB_BLK, -1), kv_spec(KB_BLK, 0), kv_spec(KB_BLK, 1),
                  kv_spec(VB_BLK, -1), kv_spec(VB_BLK, 0), kv_spec(VB_BLK, 1),
                  pl.BlockSpec((m, HEAD_DIM), lambda h, n: (0, KB_BLK + h)),
                  pl.BlockSpec((m, HEAD_DIM), lambda h, n: (0, VB_BLK + h))],
        out_specs=pl.BlockSpec((SW_BLOCK, qw), lambda h, n: (n, h)),
        out_shape=jax.ShapeDtypeStruct((seq, B_Q_WIDTH), BF16),
        compiler_params=_cparams("arbitrary", "arbitrary"),
        name="window_attn",
    )(sink, qkv, qkv, qkv, qkv, qkv, qkv, qkv, qkv_ctx, qkv_ctx)


def _na_kernel(rpb_ref, q_ref, k0_ref, k1_ref, k2_ref, k3_ref, v0_ref, v1_ref, v2_ref, v3_ref,
               kc_ref, vc_ref, o_ref, cb_scr, bias_scr, *, rows):
    h = pl.program_id(0)
    t = pl.program_id(1)
    n_tiles = rows // NA_QROWS
    n_dr = 2 * NA_ROWS - 1
    n_dc = 2 * NA_COLS - 1
    half = GRID_W

    @pl.when(t == 0)
    def _():
        shape = (GRID_W, 2 * half)
        cq = lax.broadcasted_iota(jnp.int32, shape, 0)
        lane = lax.broadcasted_iota(jnp.int32, shape, 1)
        ck = lane & (half - 1)
        hi = lane >= half
        bidx = jnp.clip(ck - cq + (NA_COLS - 1), 0, n_dc - 1)
        col_ok = (ck - jnp.clip(cq - NA_COLS // 2, 0, GRID_W - NA_COLS))
        col_ok = (col_ok >= 0) & (col_ok < NA_COLS)
        for i in range(n_dr + 1):
            val = jnp.full(shape, NEG_INF, F32)
            for e in range(2):
                a = i - 1 + e
                if 0 <= a < n_dr:
                    sel = hi if e else jnp.logical_not(hi)
                    for b in range(n_dc):
                        val = jnp.where(sel & (bidx == b), rpb_ref[(h * n_dr + a) * n_dc + b], val)
            cb_scr[i] = jnp.where(col_ok, val, NEG_INF)

    r0 = t * NA_QROWS
    kb0 = jnp.clip(r0 - NA_ROWS // 2, 0, rows - NA_KROWS)

    @pl.when((t <= 1) | (t == n_tiles - 1))
    def _():
        lane = lax.broadcasted_iota(jnp.int32, (GRID_W, 2 * half), 1)
        lo = lane < half
        for rq in range(NA_QROWS):
            rq_abs = r0 + rq
            ws = jnp.clip(rq_abs - NA_ROWS // 2, 0, rows - NA_ROWS)
            for jj in range(NA_KROWS // 2):
                rk0 = kb0 + 2 * jj
                ok0 = (rk0 >= ws) & (rk0 < ws + NA_ROWS)
                ok1 = (rk0 + 1 >= ws) & (rk0 + 1 < ws + NA_ROWS)
                idx = jnp.clip(rk0 - rq_abs + NA_ROWS, 0, n_dr)
                tile = cb_scr[idx]
                keep = (lo & ok0) | (jnp.logical_not(lo) & ok1)
                bias_scr[rq * GRID_W:(rq + 1) * GRID_W, jj * 2 * half:(jj + 1) * 2 * half] = (
                    jnp.where(keep, tile, NEG_INF))

    q = q_ref[...]
    k = jnp.concatenate([k0_ref[...], k1_ref[...], k2_ref[...], k3_ref[...]], axis=0)
    v = jnp.concatenate([v0_ref[...], v1_ref[...], v2_ref[...], v3_ref[...]], axis=0)
    s = _dot_nt(q, k) + bias_scr[...]
    s_ctx = _dot_nt(q, kc_ref[...])
    m = jnp.maximum(jnp.max(s, axis=-1, keepdims=True), jnp.max(s_ctx, axis=-1, keepdims=True))
    p = jnp.exp(s - m)
    p_ctx = jnp.exp(s_ctx - m)
    den = jnp.sum(p, axis=-1, keepdims=True) + jnp.sum(p_ctx, axis=-1, keepdims=True)
    o = (jnp.dot(p.astype(BF16), v, preferred_element_type=F32)
         + jnp.dot(p_ctx.astype(BF16), vc_ref[...], preferred_element_type=F32))
    o_ref[...] = (o * (1.0 / den)).astype(o_ref.dtype)


def _neighbourhood_attention(qkv, qkv_ctx, rpb):
    seq = qkv.shape[0]
    m = qkv_ctx.shape[0]
    rows = seq // GRID_W
    assert rows % NA_QROWS == 0 and rows >= NA_KROWS + NA_QROWS
    n_tiles = rows // NA_QROWS
    tq = NA_QROWS * GRID_W
    tk = NA_KBLK * GRID_W
    n_kblk = NA_KROWS // NA_KBLK

    def kv_spec(col_blk, i):
        def index(h, t):
            first = jnp.clip(t * (NA_QROWS // NA_KBLK) - 1, 0, rows // NA_KBLK - n_kblk)
            return (first + i, col_blk + h)
        return pl.BlockSpec((tk, HEAD_DIM), index)

    return pl.pallas_call(
        functools.partial(_na_kernel, rows=rows),
        grid=(NA_HEADS, n_tiles),
        in_specs=[pl.BlockSpec(memory_space=pltpu.SMEM),
                  pl.BlockSpec((tq, HEAD_DIM), lambda h, t: (t, QA_BLK + h))]
                 + [kv_spec(KA_BLK, i) for i in range(n_kblk)]
                 + [kv_spec(VA_BLK, i) for i in range(n_kblk)]
                 + [pl.BlockSpec((m, HEAD_DIM), lambda h, t: (0, KA_BLK + h)),
                    pl.BlockSpec((m, HEAD_DIM), lambda h, t: (0, VA_BLK + h))],
        out_specs=pl.BlockSpec((tq, HEAD_DIM), lambda h, t: (t, h)),
        out_shape=jax.ShapeDtypeStruct((seq, A_WIDTH), BF16),
        scratch_shapes=[pltpu.VMEM((2 * NA_ROWS, GRID_W, 2 * GRID_W), F32),
                        pltpu.VMEM((tq, NA_KROWS * GRID_W), F32)],
        compiler_params=_cparams("arbitrary", "arbitrary"),
        name="neighbourhood_attn",
    )(rpb.reshape(-1), qkv, *([qkv] * (2 * n_kblk)), qkv_ctx, qkv_ctx)


def _ctx_attn_kernel(sink_ref, q_ref, k_ref, v_ref, o_ref):
    j = pl.program_id(0)
    s = _dot_nt(q_ref[...], k_ref[...])
    has_sink = j >= NA_HEADS
    sink = jnp.where(has_sink, sink_ref[jnp.maximum(j - NA_HEADS, 0)], NEG_INF)
    m = jnp.maximum(jnp.max(s, axis=-1, keepdims=True), sink)
    p = jnp.exp(s - m)
    den = jnp.sum(p, axis=-1, keepdims=True) + jnp.where(has_sink, jnp.exp(sink - m), 0.0)
    o = jnp.dot(p.astype(BF16), v_ref[...], preferred_element_type=F32)
    o_ref[...] = (o * (1.0 / den)).astype(o_ref.dtype)


def _context_attention(qkv_ctx, sink):
    m = qkv_ctx.shape[0]

    def q_idx(j):
        return (0, jnp.where(j < NA_HEADS, QA_BLK + j, QB_BLK + j - NA_HEADS))

    def k_idx(j):
        return (0, jnp.where(j < NA_HEADS, KA_BLK + j, KB_BLK + (j - NA_HEADS) // NB_GROUP))

    def v_idx(j):
        return (0, jnp.where(j < NA_HEADS, VA_BLK + j, VB_BLK + (j - NA_HEADS) // NB_GROUP))

    return pl.pallas_call(
        _ctx_attn_kernel,
        grid=(NA_HEADS + NB_Q_HEADS,),
        in_specs=[pl.BlockSpec(memory_space=pltpu.SMEM),
                  pl.BlockSpec((m, HEAD_DIM), q_idx),
                  pl.BlockSpec((m, HEAD_DIM), k_idx),
                  pl.BlockSpec((m, HEAD_DIM), v_idx)],
        out_specs=pl.BlockSpec((m, HEAD_DIM), lambda j: (0, j)),
        out_shape=jax.ShapeDtypeStruct((m, A_WIDTH + B_Q_WIDTH), BF16),
        compiler_params=_cparams("arbitrary"),
        name="context_attn",
    )(sink, qkv_ctx, qkv_ctx, qkv_ctx)


def _out_proj_kernel(a_ref, b_ref, wa_ref, wb_ref, x_ref, g_ref, o_ref):
    y = (jnp.dot(a_ref[...], wa_ref[...], preferred_element_type=F32)
         + jnp.dot(b_ref[...], wb_ref[...], preferred_element_type=F32))
    o_ref[...] = x_ref[...] + g_ref[...] * y


def _out_proj(oa, ob, a_blk, b_blk, w, x, gate, *, tm):
    m, d = x.shape
    kh = w.shape[0] // 2
    return pl.pallas_call(
        _out_proj_kernel,
        grid=(m // tm,),
        in_specs=[pl.BlockSpec((tm, kh), lambda i: (i, a_blk)),
                  pl.BlockSpec((tm, kh), lambda i: (i, b_blk)),
                  pl.BlockSpec((kh, d), lambda i: (0, 0)),
                  pl.BlockSpec((kh, d), lambda i: (1, 0)),
                  pl.BlockSpec((tm, d), lambda i: (i, 0)),
                  pl.BlockSpec((1, d), lambda i: (0, 0))],
        out_specs=pl.BlockSpec((tm, d), lambda i: (i, 0)),
        out_shape=jax.ShapeDtypeStruct((m, d), F32),
        compiler_params=_cparams("arbitrary"),
        name="out_proj",
    )(oa, ob, w, w, x, gate)


def _ffn_kernel(x_ref, g_ref, sc_ref, sh_ref, gate_ref, gf_ref, w1_ref, w3_ref, w2_ref, o_ref, h_scr,
                *, final_norm):
    f = pl.program_id(1)

    @pl.when(f == 0)
    def _():
        h_scr[...] = _norm_mod(x_ref[...], g_ref[...], sc_ref[...], sh_ref[...]).astype(BF16)

    h = h_scr[...]
    a = jnp.dot(h, w1_ref[...], preferred_element_type=F32)
    b = jnp.dot(h, w3_ref[...], preferred_element_type=F32)
    y = jnp.dot((_silu(a) * b).astype(BF16), w2_ref[...], preferred_element_type=F32)

    @pl.when(f == 0)
    def _():
        o_ref[...] = y

    @pl.when(f > 0)
    def _():
        o_ref[...] += y

    @pl.when(f == pl.num_programs(1) - 1)
    def _():
        r = x_ref[...] + gate_ref[...] * o_ref[...]
        if final_norm:
            ms = jnp.mean(r * r, axis=-1, keepdims=True)
            r = r * lax.rsqrt(ms + EPS) * gf_ref[...]
        o_ref[...] = r


def _ffn(x, g, sc, sh, gate, gf, w1, w3, w2, *, final_norm, tm):
    m, d = x.shape
    dff = w1.shape[1]
    tf = 512
    vec = pl.BlockSpec((1, d), lambda i, f: (0, 0))
    return pl.pallas_call(
        functools.partial(_ffn_kernel, final_norm=final_norm),
        grid=(m // tm, dff // tf),
        in_specs=[pl.BlockSpec((tm, d), lambda i, f: (i, 0)), vec, vec, vec, vec, vec,
                  pl.BlockSpec((d, tf), lambda i, f: (0, f)),
                  pl.BlockSpec((d, tf), lambda i, f: (0, f)),
                  pl.BlockSpec((tf, d), lambda i, f: (f, 0))],
        out_specs=pl.BlockSpec((tm, d), lambda i, f: (i, 0)),
        out_shape=jax.ShapeDtypeStruct((m, d), F32),
        scratch_shapes=[pltpu.VMEM((tm, d), BF16)],
        compiler_params=_cparams("arbitrary", "arbitrary"),
        name="swiglu_final" if final_norm else "swiglu",
    )(x, g, sc, sh, gate, gf, w1, w3, w2)


def _s5_input_kernel(x_ref, c_ref, g_ref, scx_ref, shx_ref, scc_ref, shc_ref, o_ref):
    i = pl.program_id(0)
    is_ctx = (i == 0) | (i == pl.num_programs(0) - 1)

    @pl.when(is_ctx)
    def _():
        o_ref[...] = _norm_mod(c_ref[...], g_ref[...], scc_ref[...], shc_ref[...]).astype(o_ref.dtype)

    @pl.when(jnp.logical_not(is_ctx))
    def _():
        o_ref[...] = _norm_mod(x_ref[...], g_ref[...], scx_ref[...], shx_ref[...]).astype(o_ref.dtype)


def _s5_input(x, ctx, g, scx, shx, scc, shc):
    seq, d = x.shape
    m = ctx.shape[0]
    assert seq % m == 0
    nx = seq // m
    vec = pl.BlockSpec((1, d), lambda i: (0, 0))
    return pl.pallas_call(
        _s5_input_kernel,
        grid=(nx + 2,),
        in_specs=[pl.BlockSpec((m, d), lambda i: (jnp.clip(i - 1, 0, nx - 1), 0)),
                  pl.BlockSpec((m, d), lambda i: (0, 0)), vec, vec, vec, vec, vec],
        out_specs=pl.BlockSpec((m, d), lambda i: (i, 0)),
        out_shape=jax.ShapeDtypeStruct((seq + 2 * m, d), BF16),
        compiler_params=_cparams("arbitrary"),
        name="s5_input",
    )(x, ctx, g, scx, shx, scc, shc)


def _s5_kernel(u_ref, wm_ref, wb_ref, wc_ref, lam_ref, o_ref, b_scr, hf_scr, hb_scr, *, pitch):
    gs, nc, _ = u_ref.shape
    half = SSM_STATE
    for gi in range(gs):
        b = jnp.dot(u_ref[gi], wb_ref[gi], preferred_element_type=F32)
        b_scr[0, gi * pitch:gi * pitch + nc, :] = b[:, :2 * half]
        b_scr[1, gi * pitch:gi * pitch + nc, :] = b[:, 2 * half:]

    a_re = lam_ref[:, :2 * half]
    a_im = lam_ref[:, 2 * half:]
    fwd = lax.broadcasted_iota(jnp.int32, (gs, 2 * half), 1) < half

    def step(k, carry):
        s_re, s_im = carry
        cf = k
        cb = nc - 1 - k
        rows_f = pl.ds(cf, gs, stride=pitch)
        rows_b = pl.ds(cb, gs, stride=pitch)
        hf_scr[0, rows_f, :] = s_re
        hf_scr[1, rows_f, :] = s_im
        hb_scr[0, rows_b, :] = s_re
        hb_scr[1, rows_b, :] = s_im
        v_re = jnp.where(fwd, b_scr[0, rows_f, :], b_scr[0, rows_b, :])
        v_im = jnp.where(fwd, b_scr[1, rows_f, :], b_scr[1, rows_b, :])
        return (a_re * s_re - a_im * s_im + v_re, a_re * s_im + a_im * s_re + v_im)

    zero = jnp.zeros((gs, 2 * half), F32)
    lax.fori_loop(0, nc, step, (zero, zero))

    for gi in range(gs):
        rows = slice(gi * pitch, gi * pitch + nc)
        hcat = jnp.concatenate([hf_scr[0, rows, :], hb_scr[0, rows, :], hf_scr[1, rows, :], hb_scr[1, rows, :]],
                               axis=1).astype(BF16)
        o_ref[gi] = (jnp.dot(u_ref[gi], wm_ref[gi], preferred_element_type=F32)
                     + jnp.dot(hcat, wc_ref[gi], preferred_element_type=F32))


def _s5_scan(ut, wm, wb, wc, lam):
    groups, nc, w = ut.shape
    gs = S5_GROUPS_PER_STEP
    pitch = -(-nc // 8) * 8
    if (pitch // 8) % 2 == 0:
        pitch += 8
    return pl.pallas_call(
        functools.partial(_s5_kernel, pitch=pitch),
        grid=(groups // gs,),
        in_specs=[pl.BlockSpec((gs, nc, w), lambda i: (i, 0, 0)),
                  pl.BlockSpec((gs, w, w), lambda i: (i, 0, 0)),
                  pl.BlockSpec((gs, w, w), lambda i: (i, 0, 0)),
                  pl.BlockSpec((gs, 2 * w, w), lambda i: (i, 0, 0)),
                  pl.BlockSpec((gs, 4 * SSM_STATE), lambda i: (i, 0))],
        out_specs=pl.BlockSpec((gs, nc, w), lambda i: (i, 0, 0)),
        out_shape=jax.ShapeDtypeStruct((groups, nc, w), F32),
        scratch_shapes=[pltpu.VMEM((2, gs * pitch, 2 * SSM_STATE), F32),
                        pltpu.VMEM((2, gs * pitch, 2 * SSM_STATE), F32),
                        pltpu.VMEM((2, gs * pitch, 2 * SSM_STATE), F32)],
        compiler_params=_cparams("arbitrary"),
        name="s5_scan",
    )(ut, wm, wb, wc, lam)


def _s5_operators(a_re, a_im, log_dt, b_re, b_im, c_re, c_im):
    hp = lax.Precision.HIGHEST
    L = S5_CHUNK
    g, p, h = b_re.shape[1:]
    dt = jnp.exp(log_dt)[..., None]
    mag = jnp.exp(a_re * dt)
    lam_r, lam_i = mag * jnp.cos(a_im * dt), mag * jnp.sin(a_im * dt)
    den = a_re * a_re + a_im * a_im
    nr = lam_r - 1.0
    coef_r = (nr * a_re + lam_i * a_im) / den
    coef_i = (lam_i * a_re - nr * a_im) / den
    bb_r = coef_r[..., None] * b_re - coef_i[..., None] * b_im
    bb_i = coef_r[..., None] * b_im + coef_i[..., None] * b_re
    pw_r, pw_i = [jnp.ones_like(lam_r)], [jnp.zeros_like(lam_i)]
    for _ in range(L):
        pr, pi = pw_r[-1], pw_i[-1]
        pw_r.append(pr * lam_r - pi * lam_i)
        pw_i.append(pr * lam_i + pi * lam_r)
    pw_r, pw_i = jnp.stack(pw_r, axis=1), jnp.stack(pw_i, axis=1)
    cl_r = c_re[:, None] * pw_r[:, :, :, None, :] - c_im[:, None] * pw_i[:, :, :, None, :]
    cl_i = c_re[:, None] * pw_i[:, :, :, None, :] + c_im[:, None] * pw_r[:, :, :, None, :]
    lb_r = pw_r[..., None] * bb_r[:, None] - pw_i[..., None] * bb_i[:, None]
    lb_i = pw_r[..., None] * bb_i[:, None] + pw_i[..., None] * bb_r[:, None]
    kk = (jnp.einsum('dtghp,dgpk->dtghk', cl_r, bb_r, precision=hp)
          - jnp.einsum('dtghp,dgpk->dtghk', cl_i, bb_i, precision=hp))
    li = jnp.arange(L)
    tau_f = li[None, :] - li[:, None]
    kf = jnp.where((tau_f >= 0)[:, :, None, None, None], kk[0][jnp.clip(tau_f, 0, L)], 0.0)
    kb = jnp.where((tau_f <= 0)[:, :, None, None, None], kk[1][jnp.clip(-tau_f, 0, L)], 0.0)
    wm = jnp.transpose(kf + kb, (2, 0, 4, 1, 3)).reshape(g, L * h, L * h)
    def summ(arr_f, arr_b):
        f = jnp.transpose(arr_f[0][L - 1 - li], (1, 0, 3, 2))
        b = jnp.transpose(arr_b[1][li], (1, 0, 3, 2))
        return f.reshape(g, L * h, p), b.reshape(g, L * h, p)
    f_re, b_re_ = summ(lb_r, lb_r)
    f_im, b_im_ = summ(lb_i, lb_i)
    wb = jnp.concatenate([f_re, b_re_, f_im, b_im_], axis=-1)
    def carry(arr, d, idx):
        return jnp.transpose(arr[d][idx], (1, 3, 0, 2)).reshape(g, p, L * h)
    zero = jnp.zeros((g, p, L * h), F32)
    wc = jnp.concatenate([carry(cl_r, 0, li + 1), zero, zero, carry(cl_r, 1, L - li),
                          -carry(cl_i, 0, li + 1), zero, zero, -carry(cl_i, 1, L - li)], axis=1)
    lam = jnp.concatenate([pw_r[0, L], pw_r[1, L], pw_i[0, L], pw_i[1, L]], axis=-1)
    return wm.astype(BF16), wb.astype(BF16), wc.astype(BF16), lam


def _gelu_tanh(y):
    return 0.5 * y * (1.0 + jnp.tanh(math.sqrt(2.0 / math.pi) * (y + 0.044715 * (y * y * y))))


def _glu_kernel(x_ref, ys_ref, g_ref, sc_ref, sh_ref, d_ref, gate_ref, bv_ref, bg_ref, wv_ref, wg_ref, o_ref,
                act_scr):
    j = pl.program_id(1)
    tn = o_ref.shape[1]

    @pl.when(j == 0)
    def _():
        hx = _norm_mod(x_ref[...], g_ref[...], sc_ref[...], sh_ref[...])
        act_scr[...] = _gelu_tanh(d_ref[...] * hx + ys_ref[...]).astype(BF16)

    act = act_scr[...]
    val = jnp.dot(act, wv_ref[...], preferred_element_type=F32) + bv_ref[...]
    gate = jnp.dot(act, wg_ref[...], preferred_element_type=F32) + bg_ref[...]
    cols = pl.ds(pl.multiple_of(j * tn, tn), tn)
    o_ref[...] = x_ref[:, cols] + gate_ref[:, cols] * (val * (1.0 / (1.0 + jnp.exp(-gate))))


def _glu(x, ys, g, sc, sh, d_skip, gate, w, b, *, tm):
    m, d = x.shape
    tn = 512
    nj = d // tn
    vec = pl.BlockSpec((1, d), lambda i, j: (0, 0))
    return pl.pallas_call(
        _glu_kernel,
        grid=(m // tm, nj),
        in_specs=[pl.BlockSpec((tm, d), lambda i, j: (i, 0)),
                  pl.BlockSpec((tm, d), lambda i, j: (i, 0)), vec, vec, vec, vec, vec,
                  pl.BlockSpec((1, tn), lambda i, j: (0, j)),
                  pl.BlockSpec((1, tn), lambda i, j: (0, nj + j)),
                  pl.BlockSpec((d, tn), lambda i, j: (0, j)),
                  pl.BlockSpec((d, tn), lambda i, j: (0, nj + j))],
        out_specs=pl.BlockSpec((tm, tn), lambda i, j: (i, j)),
        out_shape=jax.ShapeDtypeStruct((m, d), F32),
        scratch_shapes=[pltpu.VMEM((tm, d), BF16)],
        compiler_params=_cparams("arbitrary", "arbitrary"),
        name="s5_glu",
    )(x, ys, g, sc, sh, d_skip, gate, b, b, w, w)


def kernel(x, c, ctx, c_ctx, ada_w, ada_b, norm_mix, norm_ffn, ffn_w1, ffn_w3, ffn_w2, attn_w_in, attn_w_out,
           attn_rpb, attn_sink, ssm_a_re, ssm_a_im, ssm_log_dt, ssm_b_re, ssm_b_im, ssm_c_re, ssm_c_im,
           ssm_d, ssm_w_glu, ssm_b_glu, norm_final):
    batch, seq, d = x.shape
    assert batch == 1 and ada_w.shape[0] == 2
    m = ctx.shape[1]
    xs, cs = x[0], ctx[0]
    mods = _ada_mod(c, c_ctx, ada_w, ada_b)

    def mod(layer, who):
        return [mods[layer, who, i * d:(i + 1) * d].reshape(1, d) for i in range(6)]

    row = lambda v: v.reshape(1, d)
    tm = 512

    sh1, sc1, g1, sh2, sc2, g2 = mod(0, 0)
    csh1, csc1, cg1, csh2, csc2, cg2 = mod(0, 1)
    cos, sin = _rope_tables(seq)
    w_in = attn_w_in[0].astype(BF16)
    w_out = attn_w_out[0].astype(BF16)
    w1, w3, w2 = ffn_w1[0].astype(BF16), ffn_w3[0].astype(BF16), ffn_w2[0].astype(BF16)
    nm, nf = row(norm_mix[0]), row(norm_ffn[0])
    qkv = _qkv_proj(xs, nm, sc1, sh1, w_in, cos, sin, rope=True, tm=tm)
    qkv_c = _qkv_proj(cs, nm, csc1, csh1, w_in, cos[:m], sin[:m], rope=False, tm=m)
    oa = _neighbourhood_attention(qkv, qkv_c, attn_rpb[0])
    ob = _window_attention(qkv, qkv_c, attn_sink[0])
    oc = _context_attention(qkv_c, attn_sink[0])
    xs = _out_proj(oa, ob, 0, 0, w_out, xs, g1, tm=tm)
    cs = _out_proj(oc, oc, 0, 1, w_out, cs, cg1, tm=m)
    xs = _ffn(xs, nf, sc2, sh2, g2, nf, w1, w3, w2, final_norm=False, tm=tm)
    cs = _ffn(cs, nf, csc2, csh2, cg2, nf, w1, w3, w2, final_norm=False, tm=m)

    sh1, sc1, g1, sh2, sc2, g2 = mod(1, 0)
    csh1, csc1, _, _, _, _ = mod(1, 1)
    nm, nf = row(norm_mix[1]), row(norm_ffn[1])
    z = _s5_input(xs, cs, nm, sc1, sh1, csc1, csh1)
    n_chunks = z.shape[0] // S5_CHUNK
    groups = d // SSM_GROUP
    ut = z.reshape(n_chunks, S5_CHUNK, groups, SSM_GROUP).transpose(2, 0, 1, 3).reshape(
        groups, n_chunks, S5_CHUNK * SSM_GROUP)
    wm, wb, wc, lam = _s5_operators(ssm_a_re[0], ssm_a_im[0], ssm_log_dt[0], ssm_b_re[0], ssm_b_im[0],
                                    ssm_c_re[0], ssm_c_im[0])
    yt = _s5_scan(ut, wm, wb, wc, lam)
    c0 = m // S5_CHUNK
    ys = yt[:, c0:c0 + seq // S5_CHUNK].reshape(groups, seq // S5_CHUNK, S5_CHUNK, SSM_GROUP).transpose(
        1, 2, 0, 3).reshape(seq, d)
    xs = _glu(xs, ys, nm, sc1, sh1, row(ssm_d[0]), g1, ssm_w_glu[0].astype(BF16), ssm_b_glu[0].reshape(1, 2 * d),
              tm=tm)
    w1, w3, w2 = ffn_w1[1].astype(BF16), ffn_w3[1].astype(BF16), ffn_w2[1].astype(BF16)
    xs = _ffn(xs, nf, sc2, sh2, g2, row(norm_final), w1, w3, w2, final_norm=True, tm=tm)
    return xs[None]
```

```python
import functools
import math

import jax
import jax.numpy as jnp
from jax import lax
from jax.experimental import pallas as pl
from jax.experimental.pallas import tpu as pltpu

F32 = jnp.float32
BF16 = jnp.bfloat16

GRID_W = 64
HEAD_DIM = 128
NA_HEADS = 8
NB_Q_HEADS = 8
NB_KV_HEADS = 2
NB_GROUP = NB_Q_HEADS // NB_KV_HEADS
NA_ROWS = 8
NA_COLS = 16
SW_BLOCK = 128
ROPE_BASE = 10000.0
SSM_GROUP = 16
SSM_STATE = 64
EPS = 1e-6
NEG_INF = -1e30
ATTN_SCALE = HEAD_DIM ** -0.5

A_WIDTH = NA_HEADS * HEAD_DIM
B_Q_WIDTH = NB_Q_HEADS * HEAD_DIM
B_KV_WIDTH = NB_KV_HEADS * HEAD_DIM
QA_BLK = 0
KA_BLK = NA_HEADS
VA_BLK = 2 * NA_HEADS
QB_BLK = 3 * NA_HEADS
KB_BLK = QB_BLK + NB_Q_HEADS
VB_BLK = KB_BLK + NB_KV_HEADS

NA_QROWS = 8
NA_KROWS = 16
NA_KBLK = 4
S5_CHUNK = 16
LANES = 128
FIELD = SSM_GROUP
FIELDS = LANES // FIELD
VMEM_LIMIT = 56 * 1024 * 1024


def _cparams(*sem):
    return pltpu.CompilerParams(dimension_semantics=sem, vmem_limit_bytes=VMEM_LIMIT)


def _silu(v):
    return v * (1.0 / (1.0 + jnp.exp(-v)))


def _norm_mod(x, g, sc, sh):
    ms = jnp.mean(x * x, axis=-1, keepdims=True)
    return (x * lax.rsqrt(ms + EPS) * g) * (1.0 + sc) + sh


def _dot_nt(a, b, precision=None):
    return lax.dot_general(a, b, (((1,), (1,)), ((), ())), preferred_element_type=F32, precision=precision)


def _ada_kernel(cb_ref, w_ref, b_ref, o_ref, *, rows_per_step):
    d, tn = w_ref.shape
    rep = tn // 128

    def body(i, acc):
        r = pl.multiple_of(i * rows_per_step, rows_per_step)
        w = w_ref[pl.ds(r, rows_per_step), :]
        out = []
        for v in range(2):
            s = _silu(cb_ref[v, pl.ds(r, rows_per_step), :])
            st = jnp.concatenate([s] * rep, axis=1)
            out.append(acc[v] + jnp.sum((w * st).reshape(rows_per_step // 8, 8, tn), axis=0))
        return tuple(out)

    zero = jnp.zeros((8, tn), F32)
    acc = lax.fori_loop(0, d // rows_per_step, body, (zero, zero))
    o_ref[...] = jnp.concatenate([jnp.sum(a, axis=0, keepdims=True) for a in acc], axis=0) + b_ref[...]


def _ada_mod(c, c_ctx, ada_w, ada_b):
    depth, d, n = ada_w.shape
    tn = 512
    cb = jnp.stack([jnp.broadcast_to(c.reshape(d, 1), (d, 128)),
                    jnp.broadcast_to(c_ctx.reshape(d, 1), (d, 128))])
    return pl.pallas_call(
        functools.partial(_ada_kernel, rows_per_step=64),
        grid=(depth, n // tn),
        in_specs=[pl.BlockSpec((2, d, 128), lambda l, j: (0, 0, 0)),
                  pl.BlockSpec((None, d, tn), lambda l, j: (l, 0, j)),
                  pl.BlockSpec((None, 1, tn), lambda l, j: (l, 0, j))],
        out_specs=pl.BlockSpec((None, 2, tn), lambda l, j: (l, 0, j)),
        out_shape=jax.ShapeDtypeStruct((depth, 2, n), F32),
        compiler_params=_cparams("arbitrary", "arbitrary"),
        name="ada_mod",
    )(cb, ada_w, ada_b.reshape(depth, 1, n))


def _rope(a, cos, sin):
    lane = lax.broadcasted_iota(jnp.int32, a.shape, 1)
    partner = jnp.where((lane & 63) < 32, pltpu.roll(a, 96, axis=1), pltpu.roll(a, 32, axis=1))
    return a * cos + partner * sin


def _qkv_kernel(x_ref, g_ref, sc_ref, sh_ref, w_ref, cos_ref, sin_ref, o_ref, h_scr, *, rope):
    j = pl.program_id(1)
    tn = o_ref.shape[1]
    heads = tn // HEAD_DIM

    @pl.when(j == 0)
    def _():
        h_scr[...] = _norm_mod(x_ref[...], g_ref[...], sc_ref[...], sh_ref[...]).astype(BF16)

    acc = jnp.dot(h_scr[...], w_ref[...], preferred_element_type=F32)
    col0 = j * heads

    def rotated(n_heads, scale):
        parts = []
        for hh in range(heads):
            a = acc[:, hh * HEAD_DIM:(hh + 1) * HEAD_DIM]
            if hh < n_heads:
                if rope:
                    a = _rope(a, cos_ref[...], sin_ref[...])
                a = a * scale
            parts.append(a)
        return jnp.concatenate(parts, axis=1)

    is_qa = col0 < KA_BLK
    is_qb = (col0 >= QB_BLK) & (col0 < KB_BLK)
    is_kb = col0 == KB_BLK

    @pl.when(is_qa)
    def _():
        o_ref[...] = (acc * ATTN_SCALE).astype(o_ref.dtype)

    @pl.when(is_qb)
    def _():
        o_ref[...] = rotated(heads, ATTN_SCALE).astype(o_ref.dtype)

    @pl.when(is_kb)
    def _():
        o_ref[...] = rotated(NB_KV_HEADS, 1.0).astype(o_ref.dtype)

    @pl.when(jnp.logical_not(is_qa | is_qb | is_kb))
    def _():
        o_ref[...] = acc.astype(o_ref.dtype)


def _qkv_proj(x, g, sc, sh, w, cos, sin, *, rope, tm):
    m, d = x.shape
    n = w.shape[1]
    tn = 512
    assert KB_BLK % (tn // HEAD_DIM) == 0 and m % tm == 0 and n % tn == 0
    vec = pl.BlockSpec((1, d), lambda i, j: (0, 0))
    tab = pl.BlockSpec((tm, HEAD_DIM), lambda i, j: (i, 0))
    return pl.pallas_call(
        functools.partial(_qkv_kernel, rope=rope),
        grid=(m // tm, n // tn),
        in_specs=[pl.BlockSpec((tm, d), lambda i, j: (i, 0)), vec, vec, vec,
                  pl.BlockSpec((d, tn), lambda i, j: (0, j)), tab, tab],
        out_specs=pl.BlockSpec((tm, tn), lambda i, j: (i, j)),
        out_shape=jax.ShapeDtypeStruct((m, n), BF16),
        scratch_shapes=[pltpu.VMEM((tm, d), BF16)],
        compiler_params=_cparams("arbitrary", "arbitrary"),
        name="qkv_rope" if rope else "qkv_ctx",
    )(x, g, sc, sh, w, cos, sin)


def _rope_tables(seq):
    quarter = HEAD_DIM // 4
    t = jnp.arange(seq)
    pos_r = (t // GRID_W).astype(F32)
    pos_c = (t % GRID_W).astype(F32)
    inv_freq = ROPE_BASE ** (-jnp.arange(quarter, dtype=F32) / quarter)
    ang_r = pos_r[:, None] * inv_freq[None, :]
    ang_c = pos_c[:, None] * inv_freq[None, :]
    cos = jnp.concatenate([jnp.cos(ang_r)] * 2 + [jnp.cos(ang_c)] * 2, axis=1)
    sin = jnp.concatenate([-jnp.sin(ang_r), jnp.sin(ang_r), -jnp.sin(ang_c), jnp.sin(ang_c)], axis=1)
    return cos, sin


def _win_kernel(sink_ref, q_ref, k0_ref, k1_ref, k2_ref, v0_ref, v1_ref, v2_ref, kc_ref, vc_ref, o_ref,
                *, seq):
    h = pl.program_id(0)
    n = pl.program_id(1)
    blk = SW_BLOCK
    q = q_ref[...]
    qs = jnp.concatenate([q[:, g * HEAD_DIM:(g + 1) * HEAD_DIM] for g in range(NB_GROUP)], axis=0)
    k = jnp.concatenate([k0_ref[...], k1_ref[...], k2_ref[...]], axis=0)
    v = jnp.concatenate([v0_ref[...], v1_ref[...], v2_ref[...]], axis=0)
    s = _dot_nt(qs, k)
    row = lax.broadcasted_iota(jnp.int32, s.shape, 0)
    col = lax.broadcasted_iota(jnp.int32, s.shape, 1)
    rel = col - (row & (blk - 1))
    kpos = (n - 1) * blk + col
    valid = (rel >= 0) & (rel <= 2 * blk) & (kpos >= 0) & (kpos < seq)
    s = jnp.where(valid, s, NEG_INF)
    s_ctx = _dot_nt(qs, kc_ref[...])
    grp = lax.broadcasted_iota(jnp.int32, (NB_GROUP * blk, 1), 0) // blk
    sink = jnp.zeros((NB_GROUP * blk, 1), F32)
    for g in range(NB_GROUP):
        sink = jnp.where(grp == g, sink_ref[h * NB_GROUP + g], sink)
    m = jnp.maximum(jnp.maximum(jnp.max(s, axis=-1, keepdims=True), jnp.max(s_ctx, axis=-1, keepdims=True)), sink)
    p = jnp.exp(s - m)
    p_ctx = jnp.exp(s_ctx - m)
    den = jnp.sum(p, axis=-1, keepdims=True) + jnp.sum(p_ctx, axis=-1, keepdims=True) + jnp.exp(sink - m)
    o = (jnp.dot(p.astype(BF16), v, preferred_element_type=F32)
         + jnp.dot(p_ctx.astype(BF16), vc_ref[...], preferred_element_type=F32))
    o = o * (1.0 / den)
    o_ref[...] = jnp.concatenate([o[g * blk:(g + 1) * blk] for g in range(NB_GROUP)], axis=1).astype(o_ref.dtype)


def _window_attention(qkv, qkv_ctx, sink):
    seq = qkv.shape[0]
    m = qkv_ctx.shape[0]
    nb = seq // SW_BLOCK
    qw = NB_GROUP * HEAD_DIM

    def kv_spec(col_blk, shift):
        return pl.BlockSpec((SW_BLOCK, HEAD_DIM),
                            lambda h, n: (jnp.clip(n + shift, 0, nb - 1), col_blk + h))

    return pl.pallas_call(
        functools.partial(_win_kernel, seq=seq),
        grid=(NB_KV_HEADS, nb),
        in_specs=[pl.BlockSpec(memory_space=pltpu.SMEM),
                  pl.BlockSpec((SW_BLOCK, qw), lambda h, n: (n, QB_BLK // NB_GROUP + h)),
                  kv_spec(KB_BLK, -1), kv_spec(KB_BLK, 0), kv_spec(KB_BLK, 1),
                  kv_spec(VB_BLK, -1), kv_spec(VB_BLK, 0), kv_spec(VB_BLK, 1),
                  pl.BlockSpec((m, HEAD_DIM), lambda h, n: (0, KB_BLK + h)),
                  pl.BlockSpec((m, HEAD_DIM), lambda h, n: (0, VB_BLK + h))],
        out_specs=pl.BlockSpec((SW_BLOCK, qw), lambda h, n: (n, h)),
        out_shape=jax.ShapeDtypeStruct((seq, B_Q_WIDTH), BF16),
        compiler_params=_cparams("arbitrary", "arbitrary"),
        name="window_attn",
    )(sink, qkv, qkv, qkv, qkv, qkv, qkv, qkv, qkv_ctx, qkv_ctx)


def _na_kernel(rpb_ref, q_ref, k0_ref, k1_ref, k2_ref, k3_ref, v0_ref, v1_ref, v2_ref, v3_ref,
               kc_ref, vc_ref, o_ref, cb_scr, bias_scr, *, rows):
    h = pl.program_id(0)
    t = pl.program_id(1)
    n_tiles = rows // NA_QROWS
    n_dr = 2 * NA_ROWS - 1
    n_dc = 2 * NA_COLS - 1
    half = GRID_W

    @pl.when(t == 0)
    def _():
        shape = (GRID_W, 2 * half)
        cq = lax.broadcasted_iota(jnp.int32, shape, 0)
        lane = lax.broadcasted_iota(jnp.int32, shape, 1)
        ck = lane & (half - 1)
        hi = lane >= half
        bidx = jnp.clip(ck - cq + (NA_COLS - 1), 0, n_dc - 1)
        col_ok = (ck - jnp.clip(cq - NA_COLS // 2, 0, GRID_W - NA_COLS))
        col_ok = (col_ok >= 0) & (col_ok < NA_COLS)
        for i in range(n_dr + 1):
            val = jnp.full(shape, NEG_INF, F32)
            for e in range(2):
                a = i - 1 + e
                if 0 <= a < n_dr:
                    sel = hi if e else jnp.logical_not(hi)
                    for b in range(n_dc):
                        val = jnp.where(sel & (bidx == b), rpb_ref[(h * n_dr + a) * n_dc + b], val)
            cb_scr[i] = jnp.where(col_ok, val, NEG_INF)

    r0 = t * NA_QROWS
    kb0 = jnp.clip(r0 - NA_ROWS // 2, 0, rows - NA_KROWS)

    @pl.when((t <= 1) | (t == n_tiles - 1))
    def _():
        lane = lax.broadcasted_iota(jnp.int32, (GRID_W, 2 * half), 1)
        lo = lane < half
        for rq in range(NA_QROWS):
            rq_abs = r0 + rq
            ws = jnp.clip(rq_abs - NA_ROWS // 2, 0, rows - NA_ROWS)
            for jj in range(NA_KROWS // 2):
                rk0 = kb0 + 2 * jj
                ok0 = (rk0 >= ws) & (rk0 < ws + NA_ROWS)
                ok1 = (rk0 + 1 >= ws) & (rk0 + 1 < ws + NA_ROWS)
                idx = jnp.clip(rk0 - rq_abs + NA_ROWS, 0, n_dr)
                tile = cb_scr[idx]
                keep = (lo & ok0) | (jnp.logical_not(lo) & ok1)
                bias_scr[rq * GRID_W:(rq + 1) * GRID_W, jj * 2 * half:(jj + 1) * 2 * half] = (
                    jnp.where(keep, tile, NEG_INF))

    q = q_ref[...]
    k = jnp.concatenate([k0_ref[...], k1_ref[...], k2_ref[...], k3_ref[...]], axis=0)
    v = jnp.concatenate([v0_ref[...], v1_ref[...], v2_ref[...], v3_ref[...]], axis=0)
    s = _dot_nt(q, k) + bias_scr[...]
    s_ctx = _dot_nt(q, kc_ref[...])
    m = jnp.maximum(jnp.max(s, axis=-1, keepdims=True), jnp.max(s_ctx, axis=-1, keepdims=True))
    p = jnp.exp(s - m)
    p_ctx = jnp.exp(s_ctx - m)
    den = jnp.sum(p, axis=-1, keepdims=True) + jnp.sum(p_ctx, axis=-1, keepdims=True)
    o = (jnp.dot(p.astype(BF16), v, preferred_element_type=F32)
         + jnp.dot(p_ctx.astype(BF16), vc_ref[...], preferred_element_type=F32))
    o_ref[...] = (o * (1.0 / den)).astype(o_ref.dtype)


def _neighbourhood_attention(qkv, qkv_ctx, rpb):
    seq = qkv.shape[0]
    m = qkv_ctx.shape[0]
    rows = seq // GRID_W
    assert rows % NA_QROWS == 0 and rows >= NA_KROWS + NA_QROWS
    n_tiles = rows // NA_QROWS
    tq = NA_QROWS * GRID_W
    tk = NA_KBLK * GRID_W
    n_kblk = NA_KROWS // NA_KBLK

    def kv_spec(col_blk, i):
        def index(h, t):
            first = jnp.clip(t * (NA_QROWS // NA_KBLK) - 1, 0, rows // NA_KBLK - n_kblk)
            return (first + i, col_blk + h)
        return pl.BlockSpec((tk, HEAD_DIM), index)

    return pl.pallas_call(
        functools.partial(_na_kernel, rows=rows),
        grid=(NA_HEADS, n_tiles),
        in_specs=[pl.BlockSpec(memory_space=pltpu.SMEM),
                  pl.BlockSpec((tq, HEAD_DIM), lambda h, t: (t, QA_BLK + h))]
                 + [kv_spec(KA_BLK, i) for i in range(n_kblk)]
                 + [kv_spec(VA_BLK, i) for i in range(n_kblk)]
                 + [pl.BlockSpec((m, HEAD_DIM), lambda h, t: (0, KA_BLK + h)),
                    pl.BlockSpec((m, HEAD_DIM), lambda h, t: (0, VA_BLK + h))],
        out_specs=pl.BlockSpec((tq, HEAD_DIM), lambda h, t: (t, h)),
        out_shape=jax.ShapeDtypeStruct((seq, A_WIDTH), BF16),
        scratch_shapes=[pltpu.VMEM((2 * NA_ROWS, GRID_W, 2 * GRID_W), F32),
                        pltpu.VMEM((tq, NA_KROWS * GRID_W), F32)],
        compiler_params=_cparams("arbitrary", "arbitrary"),
        name="neighbourhood_attn",
    )(rpb.reshape(-1), qkv, *([qkv] * (2 * n_kblk)), qkv_ctx, qkv_ctx)


def _ctx_attn_kernel(sink_ref, q_ref, k_ref, v_ref, o_ref):
    j = pl.program_id(0)
    s = _dot_nt(q_ref[...], k_ref[...])
    has_sink = j >= NA_HEADS
    sink = jnp.where(has_sink, sink_ref[jnp.maximum(j - NA_HEADS, 0)], NEG_INF)
    m = jnp.maximum(jnp.max(s, axis=-1, keepdims=True), sink)
    p = jnp.exp(s - m)
    den = jnp.sum(p, axis=-1, keepdims=True) + jnp.where(has_sink, jnp.exp(sink - m), 0.0)
    o = jnp.dot(p.astype(BF16), v_ref[...], preferred_element_type=F32)
    o_ref[...] = (o * (1.0 / den)).astype(o_ref.dtype)


def _context_attention(qkv_ctx, sink):
    m = qkv_ctx.shape[0]

    def q_idx(j):
        return (0, jnp.where(j < NA_HEADS, QA_BLK + j, QB_BLK + j - NA_HEADS))

    def k_idx(j):
        return (0, jnp.where(j < NA_HEADS, KA_BLK + j, KB_BLK + (j - NA_HEADS) // NB_GROUP))

    def v_idx(j):
        return (0, jnp.where(j < NA_HEADS, VA_BLK + j, VB_BLK + (j - NA_HEADS) // NB_GROUP))

    return pl.pallas_call(
        _ctx_attn_kernel,
        grid=(NA_HEADS + NB_Q_HEADS,),
        in_specs=[pl.BlockSpec(memory_space=pltpu.SMEM),
                  pl.BlockSpec((m, HEAD_DIM), q_idx),
                  pl.BlockSpec((m, HEAD_DIM), k_idx),
                  pl.BlockSpec((m, HEAD_DIM), v_idx)],
        out_specs=pl.BlockSpec((m, HEAD_DIM), lambda j: (0, j)),
        out_shape=jax.ShapeDtypeStruct((m, A_WIDTH + B_Q_WIDTH), BF16),
        compiler_params=_cparams("arbitrary"),
        name="context_attn",
    )(sink, qkv_ctx, qkv_ctx, qkv_ctx)


def _out_proj_kernel(a_ref, b_ref, wa_ref, wb_ref, x_ref, g_ref, o_ref):
    y = (jnp.dot(a_ref[...], wa_ref[...], preferred_element_type=F32)
         + jnp.dot(b_ref[...], wb_ref[...], preferred_element_type=F32))
    o_ref[...] = x_ref[...] + g_ref[...] * y


def _out_proj(oa, ob, a_blk, b_blk, w, x, gate, *, tm):
    m, d = x.shape
    kh = w.shape[0] // 2
    return pl.pallas_call(
        _out_proj_kernel,
        grid=(m // tm,),
        in_specs=[pl.BlockSpec((tm, kh), lambda i: (i, a_blk)),
                  pl.BlockSpec((tm, kh), lambda i: (i, b_blk)),
                  pl.BlockSpec((kh, d), lambda i: (0, 0)),
                  pl.BlockSpec((kh, d), lambda i: (1, 0)),
                  pl.BlockSpec((tm, d), lambda i: (i, 0)),
                  pl.BlockSpec((1, d), lambda i: (0, 0))],
        out_specs=pl.BlockSpec((tm, d), lambda i: (i, 0)),
        out_shape=jax.ShapeDtypeStruct((m, d), F32),
        compiler_params=_cparams("arbitrary"),
        name="out_proj",
    )(oa, ob, w, w, x, gate)


def _ffn_kernel(x_ref, g_ref, sc_ref, sh_ref, gate_ref, gf_ref, w1_ref, w3_ref, w2_ref, o_ref, h_scr,
                *, final_norm):
    f = pl.program_id(1)

    @pl.when(f == 0)
    def _():
        h_scr[...] = _norm_mod(x_ref[...], g_ref[...], sc_ref[...], sh_ref[...]).astype(BF16)

    h = h_scr[...]
    a = jnp.dot(h, w1_ref[...], preferred_element_type=F32)
    b = jnp.dot(h, w3_ref[...], preferred_element_type=F32)
    y = jnp.dot((_silu(a) * b).astype(BF16), w2_ref[...], preferred_element_type=F32)

    @pl.when(f == 0)
    def _():
        o_ref[...] = y

    @pl.when(f > 0)
    def _():
        o_ref[...] += y

    @pl.when(f == pl.num_programs(1) - 1)
    def _():
        r = x_ref[...] + gate_ref[...] * o_ref[...]
        if final_norm:
            ms = jnp.mean(r * r, axis=-1, keepdims=True)
            r = r * lax.rsqrt(ms + EPS) * gf_ref[...]
        o_ref[...] = r


def _ffn(x, g, sc, sh, gate, gf, w1, w3, w2, *, final_norm, tm):
    m, d = x.shape
    dff = w1.shape[1]
    tf = 512
    vec = pl.BlockSpec((1, d), lambda i, f: (0, 0))
    return pl.pallas_call(
        functools.partial(_ffn_kernel, final_norm=final_norm),
        grid=(m // tm, dff // tf),
        in_specs=[pl.BlockSpec((tm, d), lambda i, f: (i, 0)), vec, vec, vec, vec, vec,
                  pl.BlockSpec((d, tf), lambda i, f: (0, f)),
                  pl.BlockSpec((d, tf), lambda i, f: (0, f)),
                  pl.BlockSpec((tf, d), lambda i, f: (f, 0))],
        out_specs=pl.BlockSpec((tm, d), lambda i, f: (i, 0)),
        out_shape=jax.ShapeDtypeStruct((m, d), F32),
        scratch_shapes=[pltpu.VMEM((tm, d), BF16)],
        compiler_params=_cparams("arbitrary", "arbitrary"),
        name="swiglu_final" if final_norm else "swiglu",
    )(x, g, sc, sh, gate, gf, w1, w3, w2)


def _chunk_perm(n, to_steps):
    r = lax.broadcasted_iota(jnp.int32, (n, n), 0)
    c = lax.broadcasted_iota(jnp.int32, (n, n), 1)
    nch = n // S5_CHUNK
    src = (r % nch) * S5_CHUNK + r // nch if to_steps else (r % S5_CHUNK) * nch + r // S5_CHUNK
    return (c == src).astype(BF16)


def _field_transpose(v):
    v = list(v)
    lane = lax.broadcasted_iota(jnp.int32, v[0].shape, 1)
    s = FIELDS // 2
    while s:
        keep = ((lane // FIELD) & s) == 0
        for i in range(FIELDS):
            if i & s:
                continue
            a, b = v[i], v[i + s]
            v[i] = jnp.where(keep, a, pltpu.roll(b, FIELD * s, axis=1))
            v[i + s] = jnp.where(keep, pltpu.roll(a, LANES - FIELD * s, axis=1), b)
        s //= 2
    return v


def _s5_input_kernel(x_ref, c_ref, g_ref, scx_ref, shx_ref, scc_ref, shc_ref, o_ref):
    i = pl.program_id(0)
    is_ctx = (i == 0) | (i == pl.num_programs(0) - 1)
    n = x_ref.shape[0]

    def emit(h):
        hp = jnp.dot(_chunk_perm(n, True), h.astype(BF16), preferred_element_type=F32)
        o_ref[...] = hp.astype(BF16).reshape(o_ref.shape)

    @pl.when(is_ctx)
    def _():
        emit(_norm_mod(c_ref[...], g_ref[...], scc_ref[...], shc_ref[...]))

    @pl.when(jnp.logical_not(is_ctx))
    def _():
        emit(_norm_mod(x_ref[...], g_ref[...], scx_ref[...], shx_ref[...]))


def _s5_input(x, ctx, g, scx, shx, scc, shc):
    seq, d = x.shape
    m = ctx.shape[0]
    assert m % (S5_CHUNK * 8) == 0 and seq % m == 0
    nx = seq // m
    nc = (seq + 2 * m) // S5_CHUNK
    vec = pl.BlockSpec((1, d), lambda i: (0, 0))
    return pl.pallas_call(
        _s5_input_kernel,
        grid=(nx + 2,),
        in_specs=[pl.BlockSpec((m, d), lambda i: (jnp.clip(i - 1, 0, nx - 1), 0)),
                  pl.BlockSpec((m, d), lambda i: (0, 0)), vec, vec, vec, vec, vec],
        out_specs=pl.BlockSpec((S5_CHUNK, m // S5_CHUNK, d), lambda i: (0, i, 0)),
        out_shape=jax.ShapeDtypeStruct((S5_CHUNK, nc, d), BF16),
        compiler_params=_cparams("arbitrary"),
        name="s5_input",
    )(x, ctx, g, scx, shx, scc, shc)


def _s5_kernel(z_ref, wm_ref, wb_ref, wc_ref, lam_ref, o_ref, u_scr, b_scr, hf_scr, hb_scr, *, pitch, c_lo):
    L, nc, _ = z_ref.shape
    n_out = o_ref.shape[1]
    gs = FIELDS
    half = SSM_STATE
    for j in range(L // FIELDS):
        v = [pltpu.bitcast(z_ref[j * FIELDS + i], jnp.uint32) for i in range(FIELDS)]
        w = _field_transpose(v)
        for gi in range(gs):
            u_scr[gi, :, j * LANES:(j + 1) * LANES] = pltpu.bitcast(w[gi], BF16)

    for gi in range(gs):
        b = jnp.dot(u_scr[gi], wb_ref[gi], preferred_element_type=F32)
        b_scr[0, gi * pitch:gi * pitch + nc, :] = b[:, :2 * half]
        b_scr[1, gi * pitch:gi * pitch + nc, :] = b[:, 2 * half:]

    a_re = lam_ref[:, 0, :2 * half]
    a_im = lam_ref[:, 0, 2 * half:]
    fwd = lax.broadcasted_iota(jnp.int32, (gs, 2 * half), 1) < half

    def step(k, carry):
        s_re, s_im = carry
        rows_f = pl.ds(k, gs, stride=pitch)
        rows_b = pl.ds(nc - 1 - k, gs, stride=pitch)
        hf_scr[0, rows_f, :] = s_re
        hf_scr[1, rows_f, :] = s_im
        hb_scr[0, rows_b, :] = s_re
        hb_scr[1, rows_b, :] = s_im
        v_re = jnp.where(fwd, b_scr[0, rows_f, :], b_scr[0, rows_b, :])
        v_im = jnp.where(fwd, b_scr[1, rows_f, :], b_scr[1, rows_b, :])
        return (a_re * s_re - a_im * s_im + v_re, a_re * s_im + a_im * s_re + v_im)

    zero = jnp.zeros((gs, 2 * half), F32)
    lax.fori_loop(0, nc, step, (zero, zero))

    ys = []
    for gi in range(gs):
        rows = slice(gi * pitch + c_lo, gi * pitch + c_lo + n_out)
        hcat = jnp.concatenate([hf_scr[0, rows, :], hb_scr[0, rows, :], hf_scr[1, rows, :], hb_scr[1, rows, :]],
                               axis=1).astype(BF16)
        ys.append(jnp.dot(u_scr[gi, c_lo:c_lo + n_out, :], wm_ref[gi], preferred_element_type=F32)
                  + _dot_nt(hcat, wc_ref[gi]))
    for j in range(L // FIELDS):
        w = _field_transpose([y[:, j * LANES:(j + 1) * LANES] for y in ys])
        for i in range(FIELDS):
            o_ref[j * FIELDS + i] = w[i]


def _s5_scan(zp, wm, wb, wc, lam, *, c_lo, n_out):
    L, nc, d = zp.shape
    w = wm.shape[-1]
    gs = FIELDS
    pitch = -(-nc // 8) * 8
    if (pitch // 8) % 2 == 0:
        pitch += 8
    return pl.pallas_call(
        functools.partial(_s5_kernel, pitch=pitch, c_lo=c_lo),
        grid=(d // LANES,),
        in_specs=[pl.BlockSpec((L, nc, LANES), lambda i: (0, 0, i)),
                  pl.BlockSpec((gs, w, w), lambda i: (i, 0, 0)),
                  pl.BlockSpec((gs, w, w), lambda i: (i, 0, 0)),
                  pl.BlockSpec((gs, w, 2 * w), lambda i: (i, 0, 0)),
                  pl.BlockSpec((gs, 1, w), lambda i: (i, 0, 0))],
        out_specs=pl.BlockSpec((L, n_out, LANES), lambda i: (0, 0, i)),
        out_shape=jax.ShapeDtypeStruct((L, n_out, d), F32),
        scratch_shapes=[pltpu.VMEM((gs, nc, w), BF16),
                        pltpu.VMEM((2, gs * pitch, 2 * SSM_STATE), F32),
                        pltpu.VMEM((2, gs * pitch, 2 * SSM_STATE), F32),
                        pltpu.VMEM((2, gs * pitch, 2 * SSM_STATE), F32)],
        compiler_params=_cparams("arbitrary"),
        name="s5_scan",
    )(zp, wm, wb, wc, lam)


def _shift_lanes(x, s):
    w = LANES
    x0, x1 = x[:, :w], x[:, w:]
    zero = jnp.zeros_like(x0)
    lane = lax.broadcasted_iota(jnp.int32, x0.shape, 1)
    if s == 0:
        return x
    if s > 0:
        if s >= w:
            t = s - w
            y1 = x0 if t == 0 else jnp.where(lane < t, 0.0, pltpu.roll(x0, t, axis=1))
            return jnp.concatenate([zero, y1], axis=1)
        r0, r1 = pltpu.roll(x0, s, axis=1), pltpu.roll(x1, s, axis=1)
        return jnp.concatenate([jnp.where(lane < s, 0.0, r0), jnp.where(lane < s, r0, r1)], axis=1)
    s = -s
    if s >= w:
        t = s - w
        y0 = x1 if t == 0 else jnp.where(lane >= w - t, 0.0, pltpu.roll(x1, w - t, axis=1))
        return jnp.concatenate([y0, zero], axis=1)
    r0, r1 = pltpu.roll(x0, w - s, axis=1), pltpu.roll(x1, w - s, axis=1)
    return jnp.concatenate([jnp.where(lane >= w - s, r1, r0), jnp.where(lane >= w - s, 0.0, r1)], axis=1)


def _s5_op_kernel(a_re_ref, a_im_ref, ldt_ref, b_re_ref, b_im_ref, c_re_ref, c_im_ref,
                  wm_ref, wb_ref, wc_ref, lam_ref, e_scr):
    gs = a_re_ref.shape[0]
    L = S5_CHUNK
    h = SSM_GROUP
    half = SSM_STATE
    fwd1 = lax.broadcasted_iota(jnp.int32, (1, 2 * half), 1) < half
    fwd = lax.broadcasted_iota(jnp.int32, (h, 2 * half), 1) < half

    def cmul(ar, ai, br, bi):
        return ar * br - ai * bi, ar * bi + ai * br

    for gi in range(gs):
        ar, ai = a_re_ref[gi], a_im_ref[gi]
        dt = jnp.exp(ldt_ref[gi])
        mag = jnp.exp(ar * dt)
        lr, li = mag * jnp.cos(ai * dt), mag * jnp.sin(ai * dt)
        den = ar * ar + ai * ai
        nr = lr - 1.0
        coef_r = (nr * ar + li * ai) / den
        coef_i = (li * ar - nr * ai) / den
        bb_r, bb_i = cmul(coef_r, coef_i, b_re_ref[gi], b_im_ref[gi])
        c_r, c_i = c_re_ref[gi], c_im_ref[gi]
        pw = [(jnp.ones_like(lr), jnp.zeros_like(lr))]
        for _ in range(L):
            pw.append(cmul(pw[-1][0], pw[-1][1], lr, li))

        def mixed(jf, jb):
            return (jnp.where(fwd1, pw[jf][0], pw[jb][0]), jnp.where(fwd1, pw[jf][1], pw[jb][1]))

        for l in range(L):
            s_r, s_i = cmul(*mixed(L - 1 - l, l), bb_r, bb_i)
            wb_ref[gi, l * h:(l + 1) * h, :] = jnp.concatenate([s_r, s_i], axis=1).astype(BF16)
            f_r, f_i = cmul(*mixed(l + 1, L - l), c_r, c_i)
            wc_ref[gi, l * h:(l + 1) * h, :] = jnp.concatenate(
                [jnp.where(fwd, f_r, 0.0), jnp.where(fwd, 0.0, f_r),
                 jnp.where(fwd, -f_i, 0.0), jnp.where(fwd, 0.0, -f_i)], axis=1).astype(BF16)
            e_r, e_i = cmul(*mixed(l, L - 1 - l), c_r, c_i)
            e_scr[l * h:(l + 1) * h, :] = jnp.concatenate([e_r, e_i], axis=1)

        lhs = jnp.concatenate(
            [jnp.concatenate([jnp.where(fwd, bb_r, 0.0), jnp.where(fwd, -bb_i, 0.0)], axis=1),
             jnp.concatenate([jnp.where(fwd, 0.0, bb_r), jnp.where(fwd, 0.0, -bb_i)], axis=1)], axis=0)
        kt = _dot_nt(lhs, e_scr[...], precision=lax.Precision.HIGHEST)
        kt_f, kt_b = kt[:h], kt[h:]
        for l in range(L):
            blk = _shift_lanes(kt_f, h * l) + _shift_lanes(kt_b, -h * (L - 1 - l))
            wm_ref[gi, l * h:(l + 1) * h, :] = blk.astype(BF16)
        lam_ref[gi] = jnp.concatenate([pw[L][0], pw[L][1]], axis=1)


def _s5_operators(a_re, a_im, log_dt, b_re, b_im, c_re, c_im):
    _, g, p = a_re.shape
    h = b_re.shape[-1]
    gs = FIELDS
    w = S5_CHUNK * h
    assert w == 2 * LANES and 2 * p == LANES

    def lanes(v):
        return jnp.concatenate([v[0], v[1]], axis=-1)

    def chan_rows(v):
        return lanes(jnp.swapaxes(v, -1, -2))

    a_re2 = lanes(a_re).reshape(g, 1, 2 * p)
    a_im2 = lanes(a_im).reshape(g, 1, 2 * p)
    ldt2 = jnp.repeat(log_dt.T, p, axis=1).reshape(g, 1, 2 * p)
    vec = pl.BlockSpec((gs, 1, 2 * p), lambda i: (i, 0, 0))
    mat = pl.BlockSpec((gs, h, 2 * p), lambda i: (i, 0, 0))
    return pl.pallas_call(
        _s5_op_kernel,
        grid=(g // gs,),
        in_specs=[vec, vec, vec, mat, mat, mat, mat],
        out_specs=[pl.BlockSpec((gs, w, w), lambda i: (i, 0, 0)),
                   pl.BlockSpec((gs, w, w), lambda i: (i, 0, 0)),
                   pl.BlockSpec((gs, w, 2 * w), lambda i: (i, 0, 0)),
                   pl.BlockSpec((gs, 1, w), lambda i: (i, 0, 0))],
        out_shape=[jax.ShapeDtypeStruct((g, w, w), BF16), jax.ShapeDtypeStruct((g, w, w), BF16),
                   jax.ShapeDtypeStruct((g, w, 2 * w), BF16), jax.ShapeDtypeStruct((g, 1, w), F32)],
        scratch_shapes=[pltpu.VMEM((w, w), F32)],
        compiler_params=_cparams("arbitrary"),
        name="s5_operators",
    )(a_re2, a_im2, ldt2, chan_rows(b_re), chan_rows(b_im), lanes(c_re), lanes(c_im))


def _gelu_tanh(y):
    return 0.5 * y * (1.0 + jnp.tanh(math.sqrt(2.0 / math.pi) * (y + 0.044715 * (y * y * y))))


def _glu_kernel(x_ref, ys_ref, g_ref, sc_ref, sh_ref, d_ref, gate_ref, bv_ref, bg_ref, wv_ref, wg_ref, o_ref,
                act_scr):
    j = pl.program_id(1)
    tn = o_ref.shape[1]

    @pl.when(j == 0)
    def _():
        tm = x_ref.shape[0]
        perm = _chunk_perm(tm, False)
        rest = ys_ref[...].reshape(tm, x_ref.shape[1])
        ys = jnp.zeros_like(rest)
        for _ in range(3):
            piece = rest.astype(BF16)
            rest = rest - piece.astype(F32)
            ys = ys + jnp.dot(perm, piece, preferred_element_type=F32)
        hx = _norm_mod(x_ref[...], g_ref[...], sc_ref[...], sh_ref[...])
        act_scr[...] = _gelu_tanh(d_ref[...] * hx + ys).astype(BF16)

    act = act_scr[...]
    val = jnp.dot(act, wv_ref[...], preferred_element_type=F32) + bv_ref[...]
    gate = jnp.dot(act, wg_ref[...], preferred_element_type=F32) + bg_ref[...]
    cols = pl.ds(pl.multiple_of(j * tn, tn), tn)
    o_ref[...] = x_ref[:, cols] + gate_ref[:, cols] * (val * (1.0 / (1.0 + jnp.exp(-gate))))


def _glu(x, ys, g, sc, sh, d_skip, gate, w, b, *, tm):
    m, d = x.shape
    tn = 512
    nj = d // tn
    vec = pl.BlockSpec((1, d), lambda i, j: (0, 0))
    return pl.pallas_call(
        _glu_kernel,
        grid=(m // tm, nj),
        in_specs=[pl.BlockSpec((tm, d), lambda i, j: (i, 0)),
                  pl.BlockSpec((S5_CHUNK, tm // S5_CHUNK, d), lambda i, j: (0, i, 0)), vec, vec, vec, vec, vec,
                  pl.BlockSpec((1, tn), lambda i, j: (0, j)),
                  pl.BlockSpec((1, tn), lambda i, j: (0, nj + j)),
                  pl.BlockSpec((d, tn), lambda i, j: (0, j)),
                  pl.BlockSpec((d, tn), lambda i, j: (0, nj + j))],
        out_specs=pl.BlockSpec((tm, tn), lambda i, j: (i, j)),
        out_shape=jax.ShapeDtypeStruct((m, d), F32),
        scratch_shapes=[pltpu.VMEM((tm, d), BF16)],
        compiler_params=_cparams("arbitrary", "arbitrary"),
        name="s5_glu",
    )(x, ys, g, sc, sh, d_skip, gate, b, b, w, w)


def kernel(x, c, ctx, c_ctx, ada_w, ada_b, norm_mix, norm_ffn, ffn_w1, ffn_w3, ffn_w2, attn_w_in, attn_w_out,
           attn_rpb, attn_sink, ssm_a_re, ssm_a_im, ssm_log_dt, ssm_b_re, ssm_b_im, ssm_c_re, ssm_c_im,
           ssm_d, ssm_w_glu, ssm_b_glu, norm_final):
    batch, seq, d = x.shape
    assert batch == 1 and ada_w.shape[0] == 2
    m = ctx.shape[1]
    xs, cs = x[0], ctx[0]
    mods = _ada_mod(c, c_ctx, ada_w, ada_b)

    def mod(layer, who):
        return [mods[layer, who, i * d:(i + 1) * d].reshape(1, d) for i in range(6)]

    row = lambda v: v.reshape(1, d)
    tm = 512

    sh1, sc1, g1, sh2, sc2, g2 = mod(0, 0)
    csh1, csc1, cg1, csh2, csc2, cg2 = mod(0, 1)
    cos, sin = _rope_tables(seq)
    w_in = attn_w_in[0].astype(BF16)
    w_out = attn_w_out[0].astype(BF16)
    w1, w3, w2 = ffn_w1[0].astype(BF16), ffn_w3[0].astype(BF16), ffn_w2[0].astype(BF16)
    nm, nf = row(norm_mix[0]), row(norm_ffn[0])
    qkv = _qkv_proj(xs, nm, sc1, sh1, w_in, cos, sin, rope=True, tm=tm)
    qkv_c = _qkv_proj(cs, nm, csc1, csh1, w_in, cos[:m], sin[:m], rope=False, tm=m)
    oa = _neighbourhood_attention(qkv, qkv_c, attn_rpb[0])
    ob = _window_attention(qkv, qkv_c, attn_sink[0])
    oc = _context_attention(qkv_c, attn_sink[0])
    xs = _out_proj(oa, ob, 0, 0, w_out, xs, g1, tm=tm)
    cs = _out_proj(oc, oc, 0, 1, w_out, cs, cg1, tm=m)
    xs = _ffn(xs, nf, sc2, sh2, g2, nf, w1, w3, w2, final_norm=False, tm=tm)
    cs = _ffn(cs, nf, csc2, csh2, cg2, nf, w1, w3, w2, final_norm=False, tm=m)

    sh1, sc1, g1, sh2, sc2, g2 = mod(1, 0)
    csh1, csc1, _, _, _, _ = mod(1, 1)
    nm, nf = row(norm_mix[1]), row(norm_ffn[1])
    z = _s5_input(xs, cs, nm, sc1, sh1, csc1, csh1)
    wm, wb, wc, lam = _s5_operators(ssm_a_re[0], ssm_a_im[0], ssm_log_dt[0], ssm_b_re[0], ssm_b_im[0],
                                    ssm_c_re[0], ssm_c_im[0])
    ys = _s5_scan(z, wm, wb, wc, lam, c_lo=m // S5_CHUNK, n_out=seq // S5_CHUNK)
    xs = _glu(xs, ys, nm, sc1, sh1, row(ssm_d[0]), g1, ssm_w_glu[0].astype(BF16), ssm_b_glu[0].reshape(1, 2 * d),
              tm=tm)
    w1, w3, w2 = ffn_w1[1].astype(BF16), ffn_w3[1].astype(BF16), ffn_w2[1].astype(BF16)
    xs = _ffn(xs, nf, sc2, sh2, g2, row(norm_final), w1, w3, w2, final_norm=True, tm=tm)
    return xs[None]
```

```python
import functools
import math

import jax
import jax.numpy as jnp
from jax import lax
from jax.experimental import pallas as pl
from jax.experimental.pallas import tpu as pltpu

F32 = jnp.float32
BF16 = jnp.bfloat16

GRID_W = 64
HEAD_DIM = 128
NA_HEADS = 8
NB_Q_HEADS = 8
NB_KV_HEADS = 2
NB_GROUP = NB_Q_HEADS // NB_KV_HEADS
NA_ROWS = 8
NA_COLS = 16
SW_BLOCK = 128
ROPE_BASE = 10000.0
SSM_GROUP = 16
SSM_STATE = 64
EPS = 1e-6
NEG_INF = -1e30
ATTN_SCALE = HEAD_DIM ** -0.5

A_WIDTH = NA_HEADS * HEAD_DIM
B_Q_WIDTH = NB_Q_HEADS * HEAD_DIM
B_KV_WIDTH = NB_KV_HEADS * HEAD_DIM
QA_BLK = 0
KA_BLK = NA_HEADS
VA_BLK = 2 * NA_HEADS
QB_BLK = 3 * NA_HEADS
KB_BLK = QB_BLK + NB_Q_HEADS
VB_BLK = KB_BLK + NB_KV_HEADS

NA_QROWS = 8
NA_KROWS = 16
NA_KBLK = 4
S5_CHUNK = 16
LANES = 128
FIELD = SSM_GROUP
FIELDS = LANES // FIELD
VMEM_LIMIT = 56 * 1024 * 1024


def _cparams(*sem):
    return pltpu.CompilerParams(dimension_semantics=sem, vmem_limit_bytes=VMEM_LIMIT)


def _silu(v):
    return v * (1.0 / (1.0 + jnp.exp(-v)))


def _norm_mod(x, g, sc, sh):
    ms = jnp.mean(x * x, axis=-1, keepdims=True)
    return (x * lax.rsqrt(ms + EPS) * g) * (1.0 + sc) + sh


def _dot_nt(a, b, precision=None):
    return lax.dot_general(a, b, (((1,), (1,)), ((), ())), preferred_element_type=F32, precision=precision)


def _ada_kernel(cb_ref, w_ref, b_ref, o_ref, *, rows_per_step):
    d, tn = w_ref.shape
    rep = tn // 128

    def body(i, acc):
        r = pl.multiple_of(i * rows_per_step, rows_per_step)
        w = w_ref[pl.ds(r, rows_per_step), :]
        out = []
        for v in range(2):
            s = _silu(cb_ref[v, pl.ds(r, rows_per_step), :])
            st = jnp.concatenate([s] * rep, axis=1)
            out.append(acc[v] + jnp.sum((w * st).reshape(rows_per_step // 8, 8, tn), axis=0))
        return tuple(out)

    zero = jnp.zeros((8, tn), F32)
    acc = lax.fori_loop(0, d // rows_per_step, body, (zero, zero))
    o_ref[...] = jnp.concatenate([jnp.sum(a, axis=0, keepdims=True) for a in acc], axis=0) + b_ref[...]


def _ada_mod(c, c_ctx, ada_w, ada_b):
    depth, d, n = ada_w.shape
    tn = 512
    cb = jnp.stack([jnp.broadcast_to(c.reshape(d, 1), (d, 128)),
                    jnp.broadcast_to(c_ctx.reshape(d, 1), (d, 128))])
    return pl.pallas_call(
        functools.partial(_ada_kernel, rows_per_step=64),
        grid=(depth, n // tn),
        in_specs=[pl.BlockSpec((2, d, 128), lambda l, j: (0, 0, 0)),
                  pl.BlockSpec((None, d, tn), lambda l, j: (l, 0, j)),
                  pl.BlockSpec((None, 1, tn), lambda l, j: (l, 0, j))],
        out_specs=pl.BlockSpec((None, 2, tn), lambda l, j: (l, 0, j)),
        out_shape=jax.ShapeDtypeStruct((depth, 2, n), F32),
        compiler_params=_cparams("arbitrary", "arbitrary"),
        name="ada_mod",
    )(cb, ada_w, ada_b.reshape(depth, 1, n))


def _rope(a, cos, sin):
    lane = lax.broadcasted_iota(jnp.int32, a.shape, 1)
    partner = jnp.where((lane & 63) < 32, pltpu.roll(a, 96, axis=1), pltpu.roll(a, 32, axis=1))
    return a * cos + partner * sin


def _qkv_kernel(x_ref, g_ref, sc_ref, sh_ref, w_ref, cos_ref, sin_ref, o_ref, h_scr, *, rope):
    j = pl.program_id(1)
    tn = o_ref.shape[1]
    heads = tn // HEAD_DIM

    @pl.when(j == 0)
    def _():
        h_scr[...] = _norm_mod(x_ref[...], g_ref[...], sc_ref[...], sh_ref[...]).astype(BF16)

    acc = jnp.dot(h_scr[...], w_ref[...], preferred_element_type=F32)
    col0 = j * heads

    def rotated(n_heads, scale):
        parts = []
        for hh in range(heads):
            a = acc[:, hh * HEAD_DIM:(hh + 1) * HEAD_DIM]
            if hh < n_heads:
                if rope:
                    a = _rope(a, cos_ref[...], sin_ref[...])
                a = a * scale
            parts.append(a)
        return jnp.concatenate(parts, axis=1)

    is_qa = col0 < KA_BLK
    is_qb = (col0 >= QB_BLK) & (col0 < KB_BLK)
    is_kb = col0 == KB_BLK

    @pl.when(is_qa)
    def _():
        o_ref[...] = (acc * ATTN_SCALE).astype(o_ref.dtype)

    @pl.when(is_qb)
    def _():
        o_ref[...] = rotated(heads, ATTN_SCALE).astype(o_ref.dtype)

    @pl.when(is_kb)
    def _():
        o_ref[...] = rotated(NB_KV_HEADS, 1.0).astype(o_ref.dtype)

    @pl.when(jnp.logical_not(is_qa | is_qb | is_kb))
    def _():
        o_ref[...] = acc.astype(o_ref.dtype)


def _qkv_proj(x, g, sc, sh, w, cos, sin, *, rope, tm):
    m, d = x.shape
    n = w.shape[1]
    tn = 512
    assert KB_BLK % (tn // HEAD_DIM) == 0 and m % tm == 0 and n % tn == 0
    vec = pl.BlockSpec((1, d), lambda i, j: (0, 0))
    tab = pl.BlockSpec((tm, HEAD_DIM), lambda i, j: (i, 0))
    return pl.pallas_call(
        functools.partial(_qkv_kernel, rope=rope),
        grid=(m // tm, n // tn),
        in_specs=[pl.BlockSpec((tm, d), lambda i, j: (i, 0)), vec, vec, vec,
                  pl.BlockSpec((d, tn), lambda i, j: (0, j)), tab, tab],
        out_specs=pl.BlockSpec((tm, tn), lambda i, j: (i, j)),
        out_shape=jax.ShapeDtypeStruct((m, n), BF16),
        scratch_shapes=[pltpu.VMEM((tm, d), BF16)],
        compiler_params=_cparams("arbitrary", "arbitrary"),
        name="qkv_rope" if rope else "qkv_ctx",
    )(x, g, sc, sh, w, cos, sin)


def _rope_tables(seq):
    quarter = HEAD_DIM // 4
    rows = seq // GRID_W
    inv_freq = ROPE_BASE ** (-jnp.arange(quarter, dtype=F32) / quarter)
    ang_r = jnp.arange(rows, dtype=F32)[:, None] * inv_freq[None, :]
    ang_c = jnp.arange(GRID_W, dtype=F32)[:, None] * inv_freq[None, :]

    def tokens(tab_r, tab_c, sign):
        r = jnp.broadcast_to(tab_r[:, None, :], (rows, GRID_W, quarter))
        c = jnp.broadcast_to(tab_c[None, :, :], (rows, GRID_W, quarter))
        return jnp.concatenate([sign * r, r, sign * c, c], axis=-1).reshape(seq, HEAD_DIM)

    return tokens(jnp.cos(ang_r), jnp.cos(ang_c), 1.0), tokens(jnp.sin(ang_r), jnp.sin(ang_c), -1.0)


def _win_kernel(sink_ref, q_ref, k0_ref, k1_ref, k2_ref, v0_ref, v1_ref, v2_ref, kc_ref, vc_ref, o_ref,
                *, seq):
    h = pl.program_id(0)
    n = pl.program_id(1)
    blk = SW_BLOCK
    q = q_ref[...]
    qs = jnp.concatenate([q[:, g * HEAD_DIM:(g + 1) * HEAD_DIM] for g in range(NB_GROUP)], axis=0)
    k = jnp.concatenate([k0_ref[...], k1_ref[...], k2_ref[...]], axis=0)
    v = jnp.concatenate([v0_ref[...], v1_ref[...], v2_ref[...]], axis=0)
    s = _dot_nt(qs, k)
    row = lax.broadcasted_iota(jnp.int32, s.shape, 0)
    col = lax.broadcasted_iota(jnp.int32, s.shape, 1)
    rel = col - (row & (blk - 1))
    kpos = (n - 1) * blk + col
    valid = (rel >= 0) & (rel <= 2 * blk) & (kpos >= 0) & (kpos < seq)
    s = jnp.where(valid, s, NEG_INF)
    s_ctx = _dot_nt(qs, kc_ref[...])
    grp = lax.broadcasted_iota(jnp.int32, (NB_GROUP * blk, 1), 0) // blk
    sink = jnp.zeros((NB_GROUP * blk, 1), F32)
    for g in range(NB_GROUP):
        sink = jnp.where(grp == g, sink_ref[h * NB_GROUP + g], sink)
    m = jnp.maximum(jnp.maximum(jnp.max(s, axis=-1, keepdims=True), jnp.max(s_ctx, axis=-1, keepdims=True)), sink)
    p = jnp.exp(s - m)
    p_ctx = jnp.exp(s_ctx - m)
    den = jnp.sum(p, axis=-1, keepdims=True) + jnp.sum(p_ctx, axis=-1, keepdims=True) + jnp.exp(sink - m)
    o = (jnp.dot(p.astype(BF16), v, preferred_element_type=F32)
         + jnp.dot(p_ctx.astype(BF16), vc_ref[...], preferred_element_type=F32))
    o = o * (1.0 / den)
    o_ref[...] = jnp.concatenate([o[g * blk:(g + 1) * blk] for g in range(NB_GROUP)], axis=1).astype(o_ref.dtype)


def _window_attention(qkv, qkv_ctx, sink):
    seq = qkv.shape[0]
    m = qkv_ctx.shape[0]
    nb = seq // SW_BLOCK
    qw = NB_GROUP * HEAD_DIM

    def kv_spec(col_blk, shift):
        return pl.BlockSpec((SW_BLOCK, HEAD_DIM),
                            lambda h, n: (jnp.clip(n + shift, 0, nb - 1), col_blk + h))

    return pl.pallas_call(
        functools.partial(_win_kernel, seq=seq),
        grid=(NB_KV_HEADS, nb),
        in_specs=[pl.BlockSpec(memory_space=pltpu.SMEM),
                  pl.BlockSpec((SW_BLOCK, qw), lambda h, n: (n, QB_BLK // NB_GROUP + h)),
                  kv_spec(KB_BLK, -1), kv_spec(KB_BLK, 0), kv_spec(KB_BLK, 1),
                  kv_spec(VB_BLK, -1), kv_spec(VB_BLK, 0), kv_spec(VB_BLK, 1),
                  pl.BlockSpec((m, HEAD_DIM), lambda h, n: (0, KB_BLK + h)),
                  pl.BlockSpec((m, HEAD_DIM), lambda h, n: (0, VB_BLK + h))],
        out_specs=pl.BlockSpec((SW_BLOCK, qw), lambda h, n: (n, h)),
        out_shape=jax.ShapeDtypeStruct((seq, B_Q_WIDTH), BF16),
        compiler_params=_cparams("arbitrary", "arbitrary"),
        name="window_attn",
    )(sink, qkv, qkv, qkv, qkv, qkv, qkv, qkv, qkv_ctx, qkv_ctx)


def _na_kernel(rpb_ref, q_ref, k0_ref, k1_ref, k2_ref, k3_ref, v0_ref, v1_ref, v2_ref, v3_ref,
               kc_ref, vc_ref, o_ref, cb_scr, bias_scr, *, rows):
    h = pl.program_id(0)
    t = pl.program_id(1)
    n_tiles = rows // NA_QROWS
    n_dr = 2 * NA_ROWS - 1
    n_dc = 2 * NA_COLS - 1
    half = GRID_W

    @pl.when(t == 0)
    def _():
        shape = (GRID_W, 2 * half)
        cq = lax.broadcasted_iota(jnp.int32, shape, 0)
        lane = lax.broadcasted_iota(jnp.int32, shape, 1)
        ck = lane & (half - 1)
        hi = lane >= half
        bidx = jnp.clip(ck - cq + (NA_COLS - 1), 0, n_dc - 1)
        col_ok = (ck - jnp.clip(cq - NA_COLS // 2, 0, GRID_W - NA_COLS))
        col_ok = (col_ok >= 0) & (col_ok < NA_COLS)
        for i in range(n_dr + 1):
            val = jnp.full(shape, NEG_INF, F32)
            for e in range(2):
                a = i - 1 + e
                if 0 <= a < n_dr:
                    sel = hi if e else jnp.logical_not(hi)
                    for b in range(n_dc):
                        val = jnp.where(sel & (bidx == b), rpb_ref[(h * n_dr + a) * n_dc + b], val)
            cb_scr[i] = jnp.where(col_ok, val, NEG_INF)

    r0 = t * NA_QROWS
    kb0 = jnp.clip(r0 - NA_ROWS // 2, 0, rows - NA_KROWS)

    @pl.when((t <= 1) | (t == n_tiles - 1))
    def _():
        lane = lax.broadcasted_iota(jnp.int32, (GRID_W, 2 * half), 1)
        lo = lane < half
        for rq in range(NA_QROWS):
            rq_abs = r0 + rq
            ws = jnp.clip(rq_abs - NA_ROWS // 2, 0, rows - NA_ROWS)
            for jj in range(NA_KROWS // 2):
                rk0 = kb0 + 2 * jj
                ok0 = (rk0 >= ws) & (rk0 < ws + NA_ROWS)
                ok1 = (rk0 + 1 >= ws) & (rk0 + 1 < ws + NA_ROWS)
                idx = jnp.clip(rk0 - rq_abs + NA_ROWS, 0, n_dr)
                tile = cb_scr[idx]
                keep = (lo & ok0) | (jnp.logical_not(lo) & ok1)
                bias_scr[rq * GRID_W:(rq + 1) * GRID_W, jj * 2 * half:(jj + 1) * 2 * half] = (
                    jnp.where(keep, tile, NEG_INF))

    q = q_ref[...]
    k = jnp.concatenate([k0_ref[...], k1_ref[...], k2_ref[...], k3_ref[...]], axis=0)
    v = jnp.concatenate([v0_ref[...], v1_ref[...], v2_ref[...], v3_ref[...]], axis=0)
    s = _dot_nt(q, k) + bias_scr[...]
    s_ctx = _dot_nt(q, kc_ref[...])
    m = jnp.maximum(jnp.max(s, axis=-1, keepdims=True), jnp.max(s_ctx, axis=-1, keepdims=True))
    p = jnp.exp(s - m)
    p_ctx = jnp.exp(s_ctx - m)
    den = jnp.sum(p, axis=-1, keepdims=True) + jnp.sum(p_ctx, axis=-1, keepdims=True)
    o = (jnp.dot(p.astype(BF16), v, preferred_element_type=F32)
         + jnp.dot(p_ctx.astype(BF16), vc_ref[...], preferred_element_type=F32))
    o_ref[...] = (o * (1.0 / den)).astype(o_ref.dtype)


def _neighbourhood_attention(qkv, qkv_ctx, rpb):
    seq = qkv.shape[0]
    m = qkv_ctx.shape[0]
    rows = seq // GRID_W
    assert rows % NA_QROWS == 0 and rows >= NA_KROWS + NA_QROWS
    n_tiles = rows // NA_QROWS
    tq = NA_QROWS * GRID_W
    tk = NA_KBLK * GRID_W
    n_kblk = NA_KROWS // NA_KBLK

    def kv_spec(col_blk, i):
        def index(h, t):
            first = jnp.clip(t * (NA_QROWS // NA_KBLK) - 1, 0, rows // NA_KBLK - n_kblk)
            return (first + i, col_blk + h)
        return pl.BlockSpec((tk, HEAD_DIM), index)

    return pl.pallas_call(
        functools.partial(_na_kernel, rows=rows),
        grid=(NA_HEADS, n_tiles),
        in_specs=[pl.BlockSpec(memory_space=pltpu.SMEM),
                  pl.BlockSpec((tq, HEAD_DIM), lambda h, t: (t, QA_BLK + h))]
                 + [kv_spec(KA_BLK, i) for i in range(n_kblk)]
                 + [kv_spec(VA_BLK, i) for i in range(n_kblk)]
                 + [pl.BlockSpec((m, HEAD_DIM), lambda h, t: (0, KA_BLK + h)),
                    pl.BlockSpec((m, HEAD_DIM), lambda h, t: (0, VA_BLK + h))],
        out_specs=pl.BlockSpec((tq, HEAD_DIM), lambda h, t: (t, h)),
        out_shape=jax.ShapeDtypeStruct((seq, A_WIDTH), BF16),
        scratch_shapes=[pltpu.VMEM((2 * NA_ROWS, GRID_W, 2 * GRID_W), F32),
                        pltpu.VMEM((tq, NA_KROWS * GRID_W), F32)],
        compiler_params=_cparams("arbitrary", "arbitrary"),
        name="neighbourhood_attn",
    )(rpb.reshape(-1), qkv, *([qkv] * (2 * n_kblk)), qkv_ctx, qkv_ctx)


def _ctx_attn_kernel(sink_ref, q_ref, k_ref, v_ref, o_ref):
    j = pl.program_id(0)
    s = _dot_nt(q_ref[...], k_ref[...])
    has_sink = j >= NA_HEADS
    sink = jnp.where(has_sink, sink_ref[jnp.maximum(j - NA_HEADS, 0)], NEG_INF)
    m = jnp.maximum(jnp.max(s, axis=-1, keepdims=True), sink)
    p = jnp.exp(s - m)
    den = jnp.sum(p, axis=-1, keepdims=True) + jnp.where(has_sink, jnp.exp(sink - m), 0.0)
    o = jnp.dot(p.astype(BF16), v_ref[...], preferred_element_type=F32)
    o_ref[...] = (o * (1.0 / den)).astype(o_ref.dtype)


def _context_attention(qkv_ctx, sink):
    m = qkv_ctx.shape[0]

    def q_idx(j):
        return (0, jnp.where(j < NA_HEADS, QA_BLK + j, QB_BLK + j - NA_HEADS))

    def k_idx(j):
        return (0, jnp.where(j < NA_HEADS, KA_BLK + j, KB_BLK + (j - NA_HEADS) // NB_GROUP))

    def v_idx(j):
        return (0, jnp.where(j < NA_HEADS, VA_BLK + j, VB_BLK + (j - NA_HEADS) // NB_GROUP))

    return pl.pallas_call(
        _ctx_attn_kernel,
        grid=(NA_HEADS + NB_Q_HEADS,),
        in_specs=[pl.BlockSpec(memory_space=pltpu.SMEM),
                  pl.BlockSpec((m, HEAD_DIM), q_idx),
                  pl.BlockSpec((m, HEAD_DIM), k_idx),
                  pl.BlockSpec((m, HEAD_DIM), v_idx)],
        out_specs=pl.BlockSpec((m, HEAD_DIM), lambda j: (0, j)),
        out_shape=jax.ShapeDtypeStruct((m, A_WIDTH + B_Q_WIDTH), BF16),
        compiler_params=_cparams("arbitrary"),
        name="context_attn",
    )(sink, qkv_ctx, qkv_ctx, qkv_ctx)


def _out_proj_kernel(a_ref, b_ref, wa_ref, wb_ref, x_ref, g_ref, o_ref):
    y = (jnp.dot(a_ref[...], wa_ref[...], preferred_element_type=F32)
         + jnp.dot(b_ref[...], wb_ref[...], preferred_element_type=F32))
    o_ref[...] = x_ref[...] + g_ref[...] * y


def _out_proj(oa, ob, a_blk, b_blk, w, x, gate, *, tm):
    m, d = x.shape
    kh = w.shape[0] // 2
    return pl.pallas_call(
        _out_proj_kernel,
        grid=(m // tm,),
        in_specs=[pl.BlockSpec((tm, kh), lambda i: (i, a_blk)),
                  pl.BlockSpec((tm, kh), lambda i: (i, b_blk)),
                  pl.BlockSpec((kh, d), lambda i: (0, 0)),
                  pl.BlockSpec((kh, d), lambda i: (1, 0)),
                  pl.BlockSpec((tm, d), lambda i: (i, 0)),
                  pl.BlockSpec((1, d), lambda i: (0, 0))],
        out_specs=pl.BlockSpec((tm, d), lambda i: (i, 0)),
        out_shape=jax.ShapeDtypeStruct((m, d), F32),
        compiler_params=_cparams("arbitrary"),
        name="out_proj",
    )(oa, ob, w, w, x, gate)


def _ffn_kernel(x_ref, g_ref, sc_ref, sh_ref, gate_ref, gf_ref, w1_ref, w3_ref, w2_ref, o_ref, h_scr,
                *, final_norm):
    f = pl.program_id(1)

    @pl.when(f == 0)
    def _():
        h_scr[...] = _norm_mod(x_ref[...], g_ref[...], sc_ref[...], sh_ref[...]).astype(BF16)

    h = h_scr[...]
    a = jnp.dot(h, w1_ref[...], preferred_element_type=F32)
    b = jnp.dot(h, w3_ref[...], preferred_element_type=F32)
    y = jnp.dot((_silu(a) * b).astype(BF16), w2_ref[...], preferred_element_type=F32)

    @pl.when(f == 0)
    def _():
        o_ref[...] = y

    @pl.when(f > 0)
    def _():
        o_ref[...] += y

    @pl.when(f == pl.num_programs(1) - 1)
    def _():
        r = x_ref[...] + gate_ref[...] * o_ref[...]
        if final_norm:
            ms = jnp.mean(r * r, axis=-1, keepdims=True)
            r = r * lax.rsqrt(ms + EPS) * gf_ref[...]
        o_ref[...] = r


def _ffn(x, g, sc, sh, gate, gf, w1, w3, w2, layer, *, final_norm, tm):
    m, d = x.shape
    dff = w1.shape[-1]
    tf = 512
    vec = pl.BlockSpec((1, d), lambda i, f: (0, 0))
    once = pl.Buffered(1) if m > tm else None
    return pl.pallas_call(
        functools.partial(_ffn_kernel, final_norm=final_norm),
        grid=(m // tm, dff // tf),
        in_specs=[pl.BlockSpec((tm, d), lambda i, f: (i, 0), pipeline_mode=once), vec, vec, vec, vec, vec,
                  pl.BlockSpec((None, d, tf), lambda i, f: (layer, 0, f)),
                  pl.BlockSpec((None, d, tf), lambda i, f: (layer, 0, f)),
                  pl.BlockSpec((None, tf, d), lambda i, f: (layer, f, 0))],
        out_specs=pl.BlockSpec((tm, d), lambda i, f: (i, 0), pipeline_mode=once),
        out_shape=jax.ShapeDtypeStruct((m, d), F32),
        scratch_shapes=[pltpu.VMEM((tm, d), BF16)],
        compiler_params=_cparams("arbitrary", "arbitrary"),
        name="swiglu_final" if final_norm else "swiglu",
    )(x, g, sc, sh, gate, gf, w1, w3, w2)


def _chunk_perm(n, to_steps):
    r = lax.broadcasted_iota(jnp.int32, (n, n), 0)
    c = lax.broadcasted_iota(jnp.int32, (n, n), 1)
    nch = n // S5_CHUNK
    src = (r % nch) * S5_CHUNK + r // nch if to_steps else (r % S5_CHUNK) * nch + r // S5_CHUNK
    return (c == src).astype(BF16)


def _field_transpose(v):
    v = list(v)
    lane = lax.broadcasted_iota(jnp.int32, v[0].shape, 1)
    s = FIELDS // 2
    while s:
        keep = ((lane // FIELD) & s) == 0
        for i in range(FIELDS):
            if i & s:
                continue
            a, b = v[i], v[i + s]
            v[i] = jnp.where(keep, a, pltpu.roll(b, FIELD * s, axis=1))
            v[i + s] = jnp.where(keep, pltpu.roll(a, LANES - FIELD * s, axis=1), b)
        s //= 2
    return v


def _s5_input_kernel(x_ref, c_ref, g_ref, scx_ref, shx_ref, scc_ref, shc_ref, o_ref):
    i = pl.program_id(0)
    is_ctx = (i == 0) | (i == pl.num_programs(0) - 1)
    n = x_ref.shape[0]

    def emit(h):
        hp = jnp.dot(_chunk_perm(n, True), h.astype(BF16), preferred_element_type=F32)
        o_ref[...] = hp.astype(BF16).reshape(o_ref.shape)

    @pl.when(is_ctx)
    def _():
        emit(_norm_mod(c_ref[...], g_ref[...], scc_ref[...], shc_ref[...]))

    @pl.when(jnp.logical_not(is_ctx))
    def _():
        emit(_norm_mod(x_ref[...], g_ref[...], scx_ref[...], shx_ref[...]))


def _s5_input(x, ctx, g, scx, shx, scc, shc):
    seq, d = x.shape
    m = ctx.shape[0]
    assert m % (S5_CHUNK * 8) == 0 and seq % m == 0
    nx = seq // m
    nc = (seq + 2 * m) // S5_CHUNK
    vec = pl.BlockSpec((1, d), lambda i: (0, 0))
    return pl.pallas_call(
        _s5_input_kernel,
        grid=(nx + 2,),
        in_specs=[pl.BlockSpec((m, d), lambda i: (jnp.clip(i - 1, 0, nx - 1), 0)),
                  pl.BlockSpec((m, d), lambda i: (0, 0)), vec, vec, vec, vec, vec],
        out_specs=pl.BlockSpec((S5_CHUNK, m // S5_CHUNK, d), lambda i: (0, i, 0)),
        out_shape=jax.ShapeDtypeStruct((S5_CHUNK, nc, d), BF16),
        compiler_params=_cparams("arbitrary"),
        name="s5_input",
    )(x, ctx, g, scx, shx, scc, shc)


def _s5_kernel(z_ref, wm_ref, wb_ref, wc_ref, lam_ref, o_ref, u_scr, b_scr, hf_scr, hb_scr, *, pitch, c_lo):
    L, nc, _ = z_ref.shape
    n_out = o_ref.shape[1]
    gs = FIELDS
    half = SSM_STATE
    for j in range(L // FIELDS):
        v = [pltpu.bitcast(z_ref[j * FIELDS + i], jnp.uint32) for i in range(FIELDS)]
        w = _field_transpose(v)
        for gi in range(gs):
            u_scr[gi, :, j * LANES:(j + 1) * LANES] = pltpu.bitcast(w[gi], BF16)

    for gi in range(gs):
        b = jnp.dot(u_scr[gi], wb_ref[gi], preferred_element_type=F32)
        b_scr[0, gi * pitch:gi * pitch + nc, :] = b[:, :2 * half]
        b_scr[1, gi * pitch:gi * pitch + nc, :] = b[:, 2 * half:]

    a_re = lam_ref[:, 0, :2 * half]
    a_im = lam_ref[:, 0, 2 * half:]
    fwd = lax.broadcasted_iota(jnp.int32, (gs, 2 * half), 1) < half

    def step(k, carry):
        s_re, s_im = carry
        rows_f = pl.ds(k, gs, stride=pitch)
        rows_b = pl.ds(nc - 1 - k, gs, stride=pitch)
        hf_scr[0, rows_f, :] = s_re
        hf_scr[1, rows_f, :] = s_im
        hb_scr[0, rows_b, :] = s_re
        hb_scr[1, rows_b, :] = s_im
        v_re = jnp.where(fwd, b_scr[0, rows_f, :], b_scr[0, rows_b, :])
        v_im = jnp.where(fwd, b_scr[1, rows_f, :], b_scr[1, rows_b, :])
        return (a_re * s_re - a_im * s_im + v_re, a_re * s_im + a_im * s_re + v_im)

    zero = jnp.zeros((gs, 2 * half), F32)
    lax.fori_loop(0, nc, step, (zero, zero))

    ys = []
    for gi in range(gs):
        rows = slice(gi * pitch + c_lo, gi * pitch + c_lo + n_out)
        hcat = jnp.concatenate([hf_scr[0, rows, :], hb_scr[0, rows, :], hf_scr[1, rows, :], hb_scr[1, rows, :]],
                               axis=1).astype(BF16)
        ys.append(jnp.dot(u_scr[gi, c_lo:c_lo + n_out, :], wm_ref[gi], preferred_element_type=F32)
                  + _dot_nt(hcat, wc_ref[gi]))
    for j in range(L // FIELDS):
        w = _field_transpose([y[:, j * LANES:(j + 1) * LANES] for y in ys])
        for i in range(FIELDS):
            o_ref[j * FIELDS + i] = w[i]


def _s5_scan(zp, wm, wb, wc, lam, *, c_lo, n_out):
    L, nc, d = zp.shape
    w = wm.shape[-1]
    gs = FIELDS
    pitch = -(-nc // 8) * 8
    if (pitch // 8) % 2 == 0:
        pitch += 8
    return pl.pallas_call(
        functools.partial(_s5_kernel, pitch=pitch, c_lo=c_lo),
        grid=(d // LANES,),
        in_specs=[pl.BlockSpec((L, nc, LANES), lambda i: (0, 0, i)),
                  pl.BlockSpec((gs, w, w), lambda i: (i, 0, 0)),
                  pl.BlockSpec((gs, w, w), lambda i: (i, 0, 0)),
                  pl.BlockSpec((gs, w, 2 * w), lambda i: (i, 0, 0)),
                  pl.BlockSpec((gs, 1, w), lambda i: (i, 0, 0))],
        out_specs=pl.BlockSpec((L, n_out, LANES), lambda i: (0, 0, i)),
        out_shape=jax.ShapeDtypeStruct((L, n_out, d), F32),
        scratch_shapes=[pltpu.VMEM((gs, nc, w), BF16),
                        pltpu.VMEM((2, gs * pitch, 2 * SSM_STATE), F32),
                        pltpu.VMEM((2, gs * pitch, 2 * SSM_STATE), F32),
                        pltpu.VMEM((2, gs * pitch, 2 * SSM_STATE), F32)],
        compiler_params=_cparams("arbitrary"),
        name="s5_scan",
    )(zp, wm, wb, wc, lam)


def _shift_lanes(x, s):
    w = LANES
    x0, x1 = x[:, :w], x[:, w:]
    zero = jnp.zeros_like(x0)
    lane = lax.broadcasted_iota(jnp.int32, x0.shape, 1)
    if s == 0:
        return x
    if s > 0:
        if s >= w:
            t = s - w
            y1 = x0 if t == 0 else jnp.where(lane < t, 0.0, pltpu.roll(x0, t, axis=1))
            return jnp.concatenate([zero, y1], axis=1)
        r0, r1 = pltpu.roll(x0, s, axis=1), pltpu.roll(x1, s, axis=1)
        return jnp.concatenate([jnp.where(lane < s, 0.0, r0), jnp.where(lane < s, r0, r1)], axis=1)
    s = -s
    if s >= w:
        t = s - w
        y0 = x1 if t == 0 else jnp.where(lane >= w - t, 0.0, pltpu.roll(x1, w - t, axis=1))
        return jnp.concatenate([y0, zero], axis=1)
    r0, r1 = pltpu.roll(x0, w - s, axis=1), pltpu.roll(x1, w - s, axis=1)
    return jnp.concatenate([jnp.where(lane >= w - s, r1, r0), jnp.where(lane >= w - s, 0.0, r1)], axis=1)


def _s5_op_kernel(a_re_ref, a_im_ref, ldt_ref, b_re_ref, b_im_ref, c_re_ref, c_im_ref,
                  wm_ref, wb_ref, wc_ref, lam_ref, e_scr):
    gs = a_re_ref.shape[0]
    L = S5_CHUNK
    h = SSM_GROUP
    half = SSM_STATE
    fwd1 = lax.broadcasted_iota(jnp.int32, (1, 2 * half), 1) < half
    fwd = lax.broadcasted_iota(jnp.int32, (h, 2 * half), 1) < half

    def cmul(ar, ai, br, bi):
        return ar * br - ai * bi, ar * bi + ai * br

    for gi in range(gs):
        ar, ai = a_re_ref[gi], a_im_ref[gi]
        dt = jnp.exp(ldt_ref[gi])
        mag = jnp.exp(ar * dt)
        lr, li = mag * jnp.cos(ai * dt), mag * jnp.sin(ai * dt)
        den = ar * ar + ai * ai
        nr = lr - 1.0
        coef_r = (nr * ar + li * ai) / den
        coef_i = (li * ar - nr * ai) / den
        bb_r, bb_i = cmul(coef_r, coef_i, b_re_ref[gi], b_im_ref[gi])
        c_r, c_i = c_re_ref[gi], c_im_ref[gi]
        pw = [(jnp.ones_like(lr), jnp.zeros_like(lr))]
        for _ in range(L):
            pw.append(cmul(pw[-1][0], pw[-1][1], lr, li))

        def mixed(jf, jb):
            return (jnp.where(fwd1, pw[jf][0], pw[jb][0]), jnp.where(fwd1, pw[jf][1], pw[jb][1]))

        for l in range(L):
            s_r, s_i = cmul(*mixed(L - 1 - l, l), bb_r, bb_i)
            wb_ref[gi, l * h:(l + 1) * h, :] = jnp.concatenate([s_r, s_i], axis=1).astype(BF16)
            f_r, f_i = cmul(*mixed(l + 1, L - l), c_r, c_i)
            wc_ref[gi, l * h:(l + 1) * h, :] = jnp.concatenate(
                [jnp.where(fwd, f_r, 0.0), jnp.where(fwd, 0.0, f_r),
                 jnp.where(fwd, -f_i, 0.0), jnp.where(fwd, 0.0, -f_i)], axis=1).astype(BF16)
            e_r, e_i = cmul(*mixed(l, L - 1 - l), c_r, c_i)
            e_scr[l * h:(l + 1) * h, :] = jnp.concatenate([e_r, e_i], axis=1)

        lhs = jnp.concatenate(
            [jnp.concatenate([jnp.where(fwd, bb_r, 0.0), jnp.where(fwd, -bb_i, 0.0)], axis=1),
             jnp.concatenate([jnp.where(fwd, 0.0, bb_r), jnp.where(fwd, 0.0, -bb_i)], axis=1)], axis=0)
        kt = _dot_nt(lhs, e_scr[...], precision=lax.Precision.HIGHEST)
        kt_f, kt_b = kt[:h], kt[h:]
        for l in range(L):
            blk = _shift_lanes(kt_f, h * l) + _shift_lanes(kt_b, -h * (L - 1 - l))
            wm_ref[gi, l * h:(l + 1) * h, :] = blk.astype(BF16)
        lam_ref[gi] = jnp.concatenate([pw[L][0], pw[L][1]], axis=1)


def _s5_operators(a_re, a_im, log_dt, b_re, b_im, c_re, c_im):
    _, g, p = a_re.shape
    h = b_re.shape[-1]
    gs = FIELDS
    w = S5_CHUNK * h
    assert w == 2 * LANES and 2 * p == LANES

    def lanes(v):
        return jnp.concatenate([v[0], v[1]], axis=-1)

    def chan_rows(v):
        return lanes(jnp.swapaxes(v, -1, -2))

    a_re2 = lanes(a_re).reshape(g, 1, 2 * p)
    a_im2 = lanes(a_im).reshape(g, 1, 2 * p)
    ldt2 = jnp.repeat(log_dt.T, p, axis=1).reshape(g, 1, 2 * p)
    vec = pl.BlockSpec((gs, 1, 2 * p), lambda i: (i, 0, 0))
    mat = pl.BlockSpec((gs, h, 2 * p), lambda i: (i, 0, 0))
    return pl.pallas_call(
        _s5_op_kernel,
        grid=(g // gs,),
        in_specs=[vec, vec, vec, mat, mat, mat, mat],
        out_specs=[pl.BlockSpec((gs, w, w), lambda i: (i, 0, 0)),
                   pl.BlockSpec((gs, w, w), lambda i: (i, 0, 0)),
                   pl.BlockSpec((gs, w, 2 * w), lambda i: (i, 0, 0)),
                   pl.BlockSpec((gs, 1, w), lambda i: (i, 0, 0))],
        out_shape=[jax.ShapeDtypeStruct((g, w, w), BF16), jax.ShapeDtypeStruct((g, w, w), BF16),
                   jax.ShapeDtypeStruct((g, w, 2 * w), BF16), jax.ShapeDtypeStruct((g, 1, w), F32)],
        scratch_shapes=[pltpu.VMEM((w, w), F32)],
        compiler_params=_cparams("arbitrary"),
        name="s5_operators",
    )(a_re2, a_im2, ldt2, chan_rows(b_re), chan_rows(b_im), lanes(c_re), lanes(c_im))


def _gelu_tanh(y):
    return 0.5 * y * (1.0 + jnp.tanh(math.sqrt(2.0 / math.pi) * (y + 0.044715 * (y * y * y))))


def _glu_kernel(x_ref, ys_ref, g_ref, sc_ref, sh_ref, d_ref, gate_ref, bv_ref, bg_ref, wv_ref, wg_ref, o_ref,
                act_scr):
    j = pl.program_id(1)
    tn = o_ref.shape[1]

    @pl.when(j == 0)
    def _():
        tm, d = x_ref.shape
        nb = S5_CHUNK * S5_CHUNK
        perm = _chunk_perm(nb, False)
        for blk in range(tm // nb):
            rest = ys_ref[:, blk * S5_CHUNK:(blk + 1) * S5_CHUNK, :].reshape(nb, d)
            ys = jnp.zeros_like(rest)
            for _ in range(3):
                piece = rest.astype(BF16)
                rest = rest - piece.astype(F32)
                ys = ys + jnp.dot(perm, piece, preferred_element_type=F32)
            rows = slice(blk * nb, (blk + 1) * nb)
            hx = _norm_mod(x_ref[rows, :], g_ref[...], sc_ref[...], sh_ref[...])
            act_scr[rows, :] = _gelu_tanh(d_ref[...] * hx + ys).astype(BF16)

    act = act_scr[...]
    val = jnp.dot(act, wv_ref[...], preferred_element_type=F32) + bv_ref[...]
    gate = jnp.dot(act, wg_ref[...], preferred_element_type=F32) + bg_ref[...]
    cols = pl.ds(pl.multiple_of(j * tn, tn), tn)
    o_ref[...] = x_ref[:, cols] + gate_ref[:, cols] * (val * (1.0 / (1.0 + jnp.exp(-gate))))


def _glu(x, ys, g, sc, sh, d_skip, gate, w, b, *, tm):
    m, d = x.shape
    tn = 512
    nj = d // tn
    vec = pl.BlockSpec((1, d), lambda i, j: (0, 0))
    return pl.pallas_call(
        _glu_kernel,
        grid=(m // tm, nj),
        in_specs=[pl.BlockSpec((tm, d), lambda i, j: (i, 0)),
                  pl.BlockSpec((S5_CHUNK, tm // S5_CHUNK, d), lambda i, j: (0, i, 0)), vec, vec, vec, vec, vec,
                  pl.BlockSpec((1, tn), lambda i, j: (0, j)),
                  pl.BlockSpec((1, tn), lambda i, j: (0, nj + j)),
                  pl.BlockSpec((d, tn), lambda i, j: (0, j)),
                  pl.BlockSpec((d, tn), lambda i, j: (0, nj + j))],
        out_specs=pl.BlockSpec((tm, tn), lambda i, j: (i, j)),
        out_shape=jax.ShapeDtypeStruct((m, d), F32),
        scratch_shapes=[pltpu.VMEM((tm, d), BF16)],
        compiler_params=_cparams("arbitrary", "arbitrary"),
        name="s5_glu",
    )(x, ys, g, sc, sh, d_skip, gate, b, b, w, w)


def kernel(x, c, ctx, c_ctx, ada_w, ada_b, norm_mix, norm_ffn, ffn_w1, ffn_w3, ffn_w2, attn_w_in, attn_w_out,
           attn_rpb, attn_sink, ssm_a_re, ssm_a_im, ssm_log_dt, ssm_b_re, ssm_b_im, ssm_c_re, ssm_c_im,
           ssm_d, ssm_w_glu, ssm_b_glu, norm_final):
    batch, seq, d = x.shape
    assert batch == 1 and ada_w.shape[0] == 2
    m = ctx.shape[1]
    xs, cs = x[0], ctx[0]
    mods = _ada_mod(c, c_ctx, ada_w, ada_b)

    def mod(layer, who):
        return [mods[layer, who, i * d:(i + 1) * d].reshape(1, d) for i in range(6)]

    row = lambda v: v.reshape(1, d)
    tm = 512
    tm_ffn = 1024

    sh1, sc1, g1, sh2, sc2, g2 = mod(0, 0)
    csh1, csc1, cg1, csh2, csc2, cg2 = mod(0, 1)
    cos, sin = _rope_tables(seq)
    w_in = attn_w_in[0].astype(BF16)
    w_out = attn_w_out[0].astype(BF16)
    w1, w3, w2 = ffn_w1.astype(BF16), ffn_w3.astype(BF16), ffn_w2.astype(BF16)
    nm, nf = row(norm_mix[0]), row(norm_ffn[0])
    qkv = _qkv_proj(xs, nm, sc1, sh1, w_in, cos, sin, rope=True, tm=tm_ffn)
    qkv_c = _qkv_proj(cs, nm, csc1, csh1, w_in, cos[:m], sin[:m], rope=False, tm=m)
    oa = _neighbourhood_attention(qkv, qkv_c, attn_rpb[0])
    ob = _window_attention(qkv, qkv_c, attn_sink[0])
    oc = _context_attention(qkv_c, attn_sink[0])
    xs = _out_proj(oa, ob, 0, 0, w_out, xs, g1, tm=tm)
    cs = _out_proj(oc, oc, 0, 1, w_out, cs, cg1, tm=m)
    xs = _ffn(xs, nf, sc2, sh2, g2, nf, w1, w3, w2, 0, final_norm=False, tm=tm_ffn)
    cs = _ffn(cs, nf, csc2, csh2, cg2, nf, w1, w3, w2, 0, final_norm=False, tm=m)

    sh1, sc1, g1, sh2, sc2, g2 = mod(1, 0)
    csh1, csc1, _, _, _, _ = mod(1, 1)
    nm, nf = row(norm_mix[1]), row(norm_ffn[1])
    z = _s5_input(xs, cs, nm, sc1, sh1, csc1, csh1)
    wm, wb, wc, lam = _s5_operators(ssm_a_re[0], ssm_a_im[0], ssm_log_dt[0], ssm_b_re[0], ssm_b_im[0],
                                    ssm_c_re[0], ssm_c_im[0])
    ys = _s5_scan(z, wm, wb, wc, lam, c_lo=m // S5_CHUNK, n_out=seq // S5_CHUNK)
    xs = _glu(xs, ys, nm, sc1, sh1, row(ssm_d[0]), g1, ssm_w_glu[0].astype(BF16), ssm_b_glu[0].reshape(1, 2 * d),
              tm=tm)
    xs = _ffn(xs, nf, sc2, sh2, g2, row(norm_final), w1, w3, w2, 1, final_norm=True, tm=tm_ffn)
    return xs[None]
```

```python
import functools
import math

import jax
import jax.numpy as jnp
from jax import lax
from jax.experimental import pallas as pl
from jax.experimental.pallas import tpu as pltpu

F32 = jnp.float32
BF16 = jnp.bfloat16

GRID_W = 64
HEAD_DIM = 128
NA_HEADS = 8
NB_Q_HEADS = 8
NB_KV_HEADS = 2
NB_GROUP = NB_Q_HEADS // NB_KV_HEADS
NA_ROWS = 8
NA_COLS = 16
SW_BLOCK = 128
ROPE_BASE = 10000.0
SSM_GROUP = 16
SSM_STATE = 64
EPS = 1e-6
NEG_INF = -1e30
ATTN_SCALE = HEAD_DIM ** -0.5

A_WIDTH = NA_HEADS * HEAD_DIM
B_Q_WIDTH = NB_Q_HEADS * HEAD_DIM
B_KV_WIDTH = NB_KV_HEADS * HEAD_DIM
QA_BLK = 0
KA_BLK = NA_HEADS
VA_BLK = 2 * NA_HEADS
QB_BLK = 3 * NA_HEADS
KB_BLK = QB_BLK + NB_Q_HEADS
VB_BLK = KB_BLK + NB_KV_HEADS

NA_QROWS = 8
NA_KROWS = 16
NA_KBLK = 4
NA_HEADS_PER_STEP = 2
S5_CHUNK = 16
LANES = 128
FIELD = SSM_GROUP
FIELDS = LANES // FIELD
VMEM_LIMIT = 56 * 1024 * 1024


def _cparams(*sem):
    return pltpu.CompilerParams(dimension_semantics=sem, vmem_limit_bytes=VMEM_LIMIT)


def _silu(v):
    return v * (1.0 / (1.0 + jnp.exp(-v)))


def _norm_mod(x, g, sc, sh):
    ms = jnp.mean(x * x, axis=-1, keepdims=True)
    return (x * lax.rsqrt(ms + EPS) * g) * (1.0 + sc) + sh


def _dot_nt(a, b, precision=None):
    return lax.dot_general(a, b, (((1,), (1,)), ((), ())), preferred_element_type=F32, precision=precision)


def _ada_kernel(cb_ref, w_ref, b_ref, o_ref, s_scr, *, rows_per_step):
    d, tn = w_ref.shape
    rep = tn // 128

    @pl.when((pl.program_id(0) == 0) & (pl.program_id(1) == 0))
    def _():
        s_scr[...] = _silu(cb_ref[...])

    def body(i, acc):
        r = pl.multiple_of(i * rows_per_step, rows_per_step)
        w = w_ref[pl.ds(r, rows_per_step), :]
        out = []
        for v in range(2):
            s = s_scr[v, pl.ds(r, rows_per_step), :]
            st = jnp.concatenate([s] * rep, axis=1)
            out.append(acc[v] + jnp.sum((w * st).reshape(rows_per_step // 8, 8, tn), axis=0))
        return tuple(out)

    zero = jnp.zeros((8, tn), F32)
    acc = lax.fori_loop(0, d // rows_per_step, body, (zero, zero))
    o_ref[...] = jnp.concatenate([jnp.sum(a, axis=0, keepdims=True) for a in acc], axis=0) + b_ref[...]


def _ada_mod(c, c_ctx, ada_w, ada_b):
    depth, d, n = ada_w.shape
    tn = 512
    cb = jnp.stack([jnp.broadcast_to(c.reshape(d, 1), (d, 128)),
                    jnp.broadcast_to(c_ctx.reshape(d, 1), (d, 128))])
    return pl.pallas_call(
        functools.partial(_ada_kernel, rows_per_step=64),
        grid=(depth, n // tn),
        in_specs=[pl.BlockSpec((2, d, 128), lambda l, j: (0, 0, 0)),
                  pl.BlockSpec((None, d, tn), lambda l, j: (l, 0, j)),
                  pl.BlockSpec((None, 1, tn), lambda l, j: (l, 0, j))],
        out_specs=pl.BlockSpec((None, 2, tn), lambda l, j: (l, 0, j)),
        out_shape=jax.ShapeDtypeStruct((depth, 2, n), F32),
        scratch_shapes=[pltpu.VMEM((2, d, 128), F32)],
        compiler_params=_cparams("arbitrary", "arbitrary"),
        name="ada_mod",
    )(cb, ada_w, ada_b.reshape(depth, 1, n))


def _rope(a, cos, sin):
    lane = lax.broadcasted_iota(jnp.int32, a.shape, 1)
    partner = jnp.where((lane & 63) < 32, pltpu.roll(a, 96, axis=1), pltpu.roll(a, 32, axis=1))
    return a * cos + partner * sin


def _qkv_kernel(x_ref, g_ref, sc_ref, sh_ref, w_ref, cos_ref, sin_ref, o_ref, h_scr, *, rope):
    j = pl.program_id(1)
    tn = o_ref.shape[1]
    heads = tn // HEAD_DIM

    @pl.when(j == 0)
    def _():
        h_scr[...] = _norm_mod(x_ref[...], g_ref[...], sc_ref[...], sh_ref[...]).astype(BF16)

    acc = jnp.dot(h_scr[...], w_ref[...], preferred_element_type=F32)
    col0 = j * heads

    def rotated(n_heads, scale):
        parts = []
        for hh in range(heads):
            a = acc[:, hh * HEAD_DIM:(hh + 1) * HEAD_DIM]
            if hh < n_heads:
                if rope:
                    a = _rope(a, cos_ref[...], sin_ref[...])
                a = a * scale
            parts.append(a)
        return jnp.concatenate(parts, axis=1)

    is_qa = col0 < KA_BLK
    is_qb = (col0 >= QB_BLK) & (col0 < KB_BLK)
    is_kb = col0 == KB_BLK

    @pl.when(is_qa)
    def _():
        o_ref[...] = (acc * ATTN_SCALE).astype(o_ref.dtype)

    @pl.when(is_qb)
    def _():
        o_ref[...] = rotated(heads, ATTN_SCALE).astype(o_ref.dtype)

    @pl.when(is_kb)
    def _():
        o_ref[...] = rotated(NB_KV_HEADS, 1.0).astype(o_ref.dtype)

    @pl.when(jnp.logical_not(is_qa | is_qb | is_kb))
    def _():
        o_ref[...] = acc.astype(o_ref.dtype)


def _qkv_proj(x, g, sc, sh, w, cos, sin, *, rope, tm):
    m, d = x.shape
    n = w.shape[1]
    tn = 512
    assert KB_BLK % (tn // HEAD_DIM) == 0 and m % tm == 0 and n % tn == 0
    vec = pl.BlockSpec((1, d), lambda i, j: (0, 0))
    tab = pl.BlockSpec((tm, HEAD_DIM), lambda i, j: (i, 0))
    return pl.pallas_call(
        functools.partial(_qkv_kernel, rope=rope),
        grid=(m // tm, n // tn),
        in_specs=[pl.BlockSpec((tm, d), lambda i, j: (i, 0)), vec, vec, vec,
                  pl.BlockSpec((d, tn), lambda i, j: (0, j)), tab, tab],
        out_specs=pl.BlockSpec((tm, tn), lambda i, j: (i, j)),
        out_shape=jax.ShapeDtypeStruct((m, n), BF16),
        scratch_shapes=[pltpu.VMEM((tm, d), BF16)],
        compiler_params=_cparams("arbitrary", "arbitrary"),
        name="qkv_rope" if rope else "qkv_ctx",
    )(x, g, sc, sh, w, cos, sin)


def _rope_tables(seq):
    quarter = HEAD_DIM // 4
    rows = seq // GRID_W
    inv_freq = ROPE_BASE ** (-jnp.arange(quarter, dtype=F32) / quarter)
    ang_r = jnp.arange(rows, dtype=F32)[:, None] * inv_freq[None, :]
    ang_c = jnp.arange(GRID_W, dtype=F32)[:, None] * inv_freq[None, :]

    def tokens(tab_r, tab_c, sign):
        r = jnp.broadcast_to(tab_r[:, None, :], (rows, GRID_W, quarter))
        c = jnp.broadcast_to(tab_c[None, :, :], (rows, GRID_W, quarter))
        return jnp.concatenate([sign * r, r, sign * c, c], axis=-1).reshape(seq, HEAD_DIM)

    return tokens(jnp.cos(ang_r), jnp.cos(ang_c), 1.0), tokens(jnp.sin(ang_r), jnp.sin(ang_c), -1.0)


def _win_kernel(sink_ref, q_ref, k0_ref, k1_ref, k2_ref, v0_ref, v1_ref, v2_ref, kc_ref, vc_ref, o_ref,
                *, seq):
    n = pl.program_id(0)
    blk = SW_BLOCK
    shape = (NB_GROUP * blk, 3 * blk)
    row = lax.broadcasted_iota(jnp.int32, shape, 0)
    col = lax.broadcasted_iota(jnp.int32, shape, 1)
    rel = col - (row & (blk - 1))
    kpos = (n - 1) * blk + col
    valid = (rel >= 0) & (rel <= 2 * blk) & (kpos >= 0) & (kpos < seq)
    grp = lax.broadcasted_iota(jnp.int32, (NB_GROUP * blk, 1), 0) // blk
    outs = []
    for h in range(NB_KV_HEADS):
        hd = slice(h * HEAD_DIM, (h + 1) * HEAD_DIM)
        qs = jnp.concatenate([q_ref[:, (h * NB_GROUP + g) * HEAD_DIM:(h * NB_GROUP + g + 1) * HEAD_DIM]
                              for g in range(NB_GROUP)], axis=0)
        k = jnp.concatenate([k0_ref[:, hd], k1_ref[:, hd], k2_ref[:, hd]], axis=0)
        v = jnp.concatenate([v0_ref[:, hd], v1_ref[:, hd], v2_ref[:, hd]], axis=0)
        s = jnp.where(valid, _dot_nt(qs, k), NEG_INF)
        s_ctx = _dot_nt(qs, kc_ref[:, hd])
        sink = jnp.zeros((NB_GROUP * blk, 1), F32)
        for g in range(NB_GROUP):
            sink = jnp.where(grp == g, sink_ref[h * NB_GROUP + g], sink)
        m = jnp.maximum(jnp.maximum(jnp.max(s, axis=-1, keepdims=True), jnp.max(s_ctx, axis=-1, keepdims=True)),
                        sink)
        p = jnp.exp(s - m)
        p_ctx = jnp.exp(s_ctx - m)
        den = jnp.sum(p, axis=-1, keepdims=True) + jnp.sum(p_ctx, axis=-1, keepdims=True) + jnp.exp(sink - m)
        o = (jnp.dot(p.astype(BF16), v, preferred_element_type=F32)
             + jnp.dot(p_ctx.astype(BF16), vc_ref[:, hd], preferred_element_type=F32))
        o = o * (1.0 / den)
        outs += [o[g * blk:(g + 1) * blk] for g in range(NB_GROUP)]
    o_ref[...] = jnp.concatenate(outs, axis=1).astype(o_ref.dtype)


def _window_attention(qkv, qkv_ctx, sink):
    seq = qkv.shape[0]
    m = qkv_ctx.shape[0]
    nb = seq // SW_BLOCK
    assert QB_BLK % NB_Q_HEADS == 0 and KB_BLK % NB_KV_HEADS == 0 and VB_BLK % NB_KV_HEADS == 0

    def kv_spec(col_blk, shift):
        return pl.BlockSpec((SW_BLOCK, B_KV_WIDTH),
                            lambda n: (jnp.clip(n + shift, 0, nb - 1), col_blk // NB_KV_HEADS))

    return pl.pallas_call(
        functools.partial(_win_kernel, seq=seq),
        grid=(nb,),
        in_specs=[pl.BlockSpec(memory_space=pltpu.SMEM),
                  pl.BlockSpec((SW_BLOCK, B_Q_WIDTH), lambda n: (n, QB_BLK // NB_Q_HEADS)),
                  kv_spec(KB_BLK, -1), kv_spec(KB_BLK, 0), kv_spec(KB_BLK, 1),
                  kv_spec(VB_BLK, -1), kv_spec(VB_BLK, 0), kv_spec(VB_BLK, 1),
                  pl.BlockSpec((m, B_KV_WIDTH), lambda n: (0, KB_BLK // NB_KV_HEADS)),
                  pl.BlockSpec((m, B_KV_WIDTH), lambda n: (0, VB_BLK // NB_KV_HEADS))],
        out_specs=pl.BlockSpec((SW_BLOCK, B_Q_WIDTH), lambda n: (n, 0)),
        out_shape=jax.ShapeDtypeStruct((seq, B_Q_WIDTH), BF16),
        compiler_params=_cparams("arbitrary"),
        name="window_attn",
    )(sink, qkv, qkv, qkv, qkv, qkv, qkv, qkv, qkv_ctx, qkv_ctx)


def _na_kernel(rpb_ref, q_ref, k0_ref, k1_ref, k2_ref, k3_ref, v0_ref, v1_ref, v2_ref, v3_ref,
               kc_ref, vc_ref, o_ref, cb_scr, bias_scr, *, rows):
    h0 = pl.program_id(0) * NA_HEADS_PER_STEP
    t = pl.program_id(1)
    n_tiles = rows // NA_QROWS
    n_dr = 2 * NA_ROWS - 1
    n_dc = 2 * NA_COLS - 1
    half = GRID_W

    @pl.when(t == 0)
    def _():
        shape = (GRID_W, 2 * half)
        cq = lax.broadcasted_iota(jnp.int32, shape, 0)
        lane = lax.broadcasted_iota(jnp.int32, shape, 1)
        ck = lane & (half - 1)
        hi = lane >= half
        bidx = jnp.clip(ck - cq + (NA_COLS - 1), 0, n_dc - 1)
        col_ok = (ck - jnp.clip(cq - NA_COLS // 2, 0, GRID_W - NA_COLS))
        col_ok = (col_ok >= 0) & (col_ok < NA_COLS)
        for hh in range(NA_HEADS_PER_STEP):
            for i in range(n_dr + 1):
                val = jnp.full(shape, NEG_INF, F32)
                for e in range(2):
                    a = i - 1 + e
                    if 0 <= a < n_dr:
                        sel = hi if e else jnp.logical_not(hi)
                        for b in range(n_dc):
                            val = jnp.where(sel & (bidx == b), rpb_ref[((h0 + hh) * n_dr + a) * n_dc + b], val)
                cb_scr[hh, i] = jnp.where(col_ok, val, NEG_INF)

    r0 = t * NA_QROWS
    kb0 = jnp.clip(r0 - NA_ROWS // 2, 0, rows - NA_KROWS)

    @pl.when((t <= 1) | (t == n_tiles - 1))
    def _():
        lane = lax.broadcasted_iota(jnp.int32, (GRID_W, 2 * half), 1)
        lo = lane < half
        for rq in range(NA_QROWS):
            rq_abs = r0 + rq
            ws = jnp.clip(rq_abs - NA_ROWS // 2, 0, rows - NA_ROWS)
            for jj in range(NA_KROWS // 2):
                rk0 = kb0 + 2 * jj
                ok0 = (rk0 >= ws) & (rk0 < ws + NA_ROWS)
                ok1 = (rk0 + 1 >= ws) & (rk0 + 1 < ws + NA_ROWS)
                idx = jnp.clip(rk0 - rq_abs + NA_ROWS, 0, n_dr)
                keep = (lo & ok0) | (jnp.logical_not(lo) & ok1)
                for hh in range(NA_HEADS_PER_STEP):
                    bias_scr[hh, rq * GRID_W:(rq + 1) * GRID_W, jj * 2 * half:(jj + 1) * 2 * half] = (
                        jnp.where(keep, cb_scr[hh, idx], NEG_INF))

    outs = []
    for hh in range(NA_HEADS_PER_STEP):
        hd = slice(hh * HEAD_DIM, (hh + 1) * HEAD_DIM)
        q = q_ref[:, hd]
        k = jnp.concatenate([k0_ref[:, hd], k1_ref[:, hd], k2_ref[:, hd], k3_ref[:, hd]], axis=0)
        v = jnp.concatenate([v0_ref[:, hd], v1_ref[:, hd], v2_ref[:, hd], v3_ref[:, hd]], axis=0)
        s = _dot_nt(q, k) + bias_scr[hh]
        s_ctx = _dot_nt(q, kc_ref[:, hd])
        m = jnp.maximum(jnp.max(s, axis=-1, keepdims=True), jnp.max(s_ctx, axis=-1, keepdims=True))
        p = jnp.exp(s - m)
        p_ctx = jnp.exp(s_ctx - m)
        den = jnp.sum(p, axis=-1, keepdims=True) + jnp.sum(p_ctx, axis=-1, keepdims=True)
        o = (jnp.dot(p.astype(BF16), v, preferred_element_type=F32)
             + jnp.dot(p_ctx.astype(BF16), vc_ref[:, hd], preferred_element_type=F32))
        outs.append(o * (1.0 / den))
    o_ref[...] = jnp.concatenate(outs, axis=1).astype(o_ref.dtype)


def _neighbourhood_attention(qkv, qkv_ctx, rpb):
    seq = qkv.shape[0]
    m = qkv_ctx.shape[0]
    rows = seq // GRID_W
    assert rows % NA_QROWS == 0 and rows >= NA_KROWS + NA_QROWS
    n_tiles = rows // NA_QROWS
    tq = NA_QROWS * GRID_W
    tk = NA_KBLK * GRID_W
    n_kblk = NA_KROWS // NA_KBLK

    hp = NA_HEADS_PER_STEP
    wh = hp * HEAD_DIM
    assert NA_HEADS % hp == 0 and QA_BLK % hp == 0 and KA_BLK % hp == 0 and VA_BLK % hp == 0

    def kv_spec(col_blk, i):
        def index(h, t):
            first = jnp.clip(t * (NA_QROWS // NA_KBLK) - 1, 0, rows // NA_KBLK - n_kblk)
            return (first + i, col_blk // hp + h)
        return pl.BlockSpec((tk, wh), index)

    return pl.pallas_call(
        functools.partial(_na_kernel, rows=rows),
        grid=(NA_HEADS // hp, n_tiles),
        in_specs=[pl.BlockSpec(memory_space=pltpu.SMEM),
                  pl.BlockSpec((tq, wh), lambda h, t: (t, QA_BLK // hp + h))]
                 + [kv_spec(KA_BLK, i) for i in range(n_kblk)]
                 + [kv_spec(VA_BLK, i) for i in range(n_kblk)]
                 + [pl.BlockSpec((m, wh), lambda h, t: (0, KA_BLK // hp + h)),
                    pl.BlockSpec((m, wh), lambda h, t: (0, VA_BLK // hp + h))],
        out_specs=pl.BlockSpec((tq, wh), lambda h, t: (t, h)),
        out_shape=jax.ShapeDtypeStruct((seq, A_WIDTH), BF16),
        scratch_shapes=[pltpu.VMEM((hp, 2 * NA_ROWS, GRID_W, 2 * GRID_W), F32),
                        pltpu.VMEM((hp, tq, NA_KROWS * GRID_W), F32)],
        compiler_params=_cparams("arbitrary", "arbitrary"),
        name="neighbourhood_attn",
    )(rpb.reshape(-1), qkv, *([qkv] * (2 * n_kblk)), qkv_ctx, qkv_ctx)


def _ctx_attn_kernel(sink_ref, q_ref, k_ref, v_ref, o_ref):
    j = pl.program_id(0)
    s = _dot_nt(q_ref[...], k_ref[...])
    has_sink = j >= NA_HEADS
    sink = jnp.where(has_sink, sink_ref[jnp.maximum(j - NA_HEADS, 0)], NEG_INF)
    m = jnp.maximum(jnp.max(s, axis=-1, keepdims=True), sink)
    p = jnp.exp(s - m)
    den = jnp.sum(p, axis=-1, keepdims=True) + jnp.where(has_sink, jnp.exp(sink - m), 0.0)
    o = jnp.dot(p.astype(BF16), v_ref[...], preferred_element_type=F32)
    o_ref[...] = (o * (1.0 / den)).astype(o_ref.dtype)


def _context_attention(qkv_ctx, sink):
    m = qkv_ctx.shape[0]

    def q_idx(j):
        return (0, jnp.where(j < NA_HEADS, QA_BLK + j, QB_BLK + j - NA_HEADS))

    def k_idx(j):
        return (0, jnp.where(j < NA_HEADS, KA_BLK + j, KB_BLK + (j - NA_HEADS) // NB_GROUP))

    def v_idx(j):
        return (0, jnp.where(j < NA_HEADS, VA_BLK + j, VB_BLK + (j - NA_HEADS) // NB_GROUP))

    return pl.pallas_call(
        _ctx_attn_kernel,
        grid=(NA_HEADS + NB_Q_HEADS,),
        in_specs=[pl.BlockSpec(memory_space=pltpu.SMEM),
                  pl.BlockSpec((m, HEAD_DIM), q_idx),
                  pl.BlockSpec((m, HEAD_DIM), k_idx),
                  pl.BlockSpec((m, HEAD_DIM), v_idx)],
        out_specs=pl.BlockSpec((m, HEAD_DIM), lambda j: (0, j)),
        out_shape=jax.ShapeDtypeStruct((m, A_WIDTH + B_Q_WIDTH), BF16),
        compiler_params=_cparams("arbitrary"),
        name="context_attn",
    )(sink, qkv_ctx, qkv_ctx, qkv_ctx)


def _out_proj_kernel(a_ref, b_ref, wa_ref, wb_ref, x_ref, g_ref, o_ref):
    y = (jnp.dot(a_ref[...], wa_ref[...], preferred_element_type=F32)
         + jnp.dot(b_ref[...], wb_ref[...], preferred_element_type=F32))
    o_ref[...] = x_ref[...] + g_ref[...] * y


def _out_proj(oa, ob, a_blk, b_blk, w, x, gate, *, tm):
    m, d = x.shape
    kh = w.shape[0] // 2
    return pl.pallas_call(
        _out_proj_kernel,
        grid=(m // tm,),
        in_specs=[pl.BlockSpec((tm, kh), lambda i: (i, a_blk)),
                  pl.BlockSpec((tm, kh), lambda i: (i, b_blk)),
                  pl.BlockSpec((kh, d), lambda i: (0, 0)),
                  pl.BlockSpec((kh, d), lambda i: (1, 0)),
                  pl.BlockSpec((tm, d), lambda i: (i, 0)),
                  pl.BlockSpec((1, d), lambda i: (0, 0))],
        out_specs=pl.BlockSpec((tm, d), lambda i: (i, 0)),
        out_shape=jax.ShapeDtypeStruct((m, d), F32),
        compiler_params=_cparams("arbitrary"),
        name="out_proj",
    )(oa, ob, w, w, x, gate)


def _ffn_kernel(x_ref, g_ref, sc_ref, sh_ref, gate_ref, gf_ref, w1_ref, w3_ref, w2_ref, o_ref, h_scr,
                *, final_norm):
    f = pl.program_id(1)

    @pl.when(f == 0)
    def _():
        h_scr[...] = _norm_mod(x_ref[...], g_ref[...], sc_ref[...], sh_ref[...]).astype(BF16)

    h = h_scr[...]
    a = jnp.dot(h, w1_ref[...], preferred_element_type=F32)
    b = jnp.dot(h, w3_ref[...], preferred_element_type=F32)
    y = jnp.dot((_silu(a) * b).astype(BF16), w2_ref[...], preferred_element_type=F32)

    @pl.when(f == 0)
    def _():
        o_ref[...] = y

    @pl.when(f > 0)
    def _():
        o_ref[...] += y

    @pl.when(f == pl.num_programs(1) - 1)
    def _():
        r = x_ref[...] + gate_ref[...] * o_ref[...]
        if final_norm:
            ms = jnp.mean(r * r, axis=-1, keepdims=True)
            r = r * lax.rsqrt(ms + EPS) * gf_ref[...]
        o_ref[...] = r


def _ffn(x, g, sc, sh, gate, gf, w1, w3, w2, layer, *, final_norm, tm):
    m, d = x.shape
    dff = w1.shape[-1]
    tf = dff // 4 if dff % (4 * LANES) == 0 else 512
    assert dff % tf == 0
    vec = pl.BlockSpec((1, d), lambda i, f: (0, 0))
    once = pl.Buffered(1) if m > tm else None
    return pl.pallas_call(
        functools.partial(_ffn_kernel, final_norm=final_norm),
        grid=(m // tm, dff // tf),
        in_specs=[pl.BlockSpec((tm, d), lambda i, f: (i, 0), pipeline_mode=once), vec, vec, vec, vec, vec,
                  pl.BlockSpec((None, d, tf), lambda i, f: (layer, 0, f)),
                  pl.BlockSpec((None, d, tf), lambda i, f: (layer, 0, f)),
                  pl.BlockSpec((None, tf, d), lambda i, f: (layer, f, 0))],
        out_specs=pl.BlockSpec((tm, d), lambda i, f: (i, 0), pipeline_mode=once),
        out_shape=jax.ShapeDtypeStruct((m, d), F32),
        scratch_shapes=[pltpu.VMEM((tm, d), BF16)],
        compiler_params=_cparams("arbitrary", "arbitrary"),
        name="swiglu_final" if final_norm else "swiglu",
    )(x, g, sc, sh, gate, gf, w1, w3, w2)


def _chunk_perm(n, to_steps):
    r = lax.broadcasted_iota(jnp.int32, (n, n), 0)
    c = lax.broadcasted_iota(jnp.int32, (n, n), 1)
    nch = n // S5_CHUNK
    src = (r % nch) * S5_CHUNK + r // nch if to_steps else (r % S5_CHUNK) * nch + r // S5_CHUNK
    return (c == src).astype(BF16)


def _field_transpose(v):
    v = list(v)
    lane = lax.broadcasted_iota(jnp.int32, v[0].shape, 1)
    s = FIELDS // 2
    while s:
        keep = ((lane // FIELD) & s) == 0
        for i in range(FIELDS):
            if i & s:
                continue
            a, b = v[i], v[i + s]
            v[i] = jnp.where(keep, a, pltpu.roll(b, FIELD * s, axis=1))
            v[i + s] = jnp.where(keep, pltpu.roll(a, LANES - FIELD * s, axis=1), b)
        s //= 2
    return v


def _s5_input_kernel(x_ref, c_ref, g_ref, scx_ref, shx_ref, scc_ref, shc_ref, o_ref):
    i = pl.program_id(0)
    is_ctx = (i == 0) | (i == pl.num_programs(0) - 1)
    n = x_ref.shape[0]

    def emit(h):
        hp = jnp.dot(_chunk_perm(n, True), h.astype(BF16), preferred_element_type=F32)
        o_ref[...] = hp.astype(BF16).reshape(o_ref.shape)

    @pl.when(is_ctx)
    def _():
        emit(_norm_mod(c_ref[...], g_ref[...], scc_ref[...], shc_ref[...]))

    @pl.when(jnp.logical_not(is_ctx))
    def _():
        emit(_norm_mod(x_ref[...], g_ref[...], scx_ref[...], shx_ref[...]))


def _s5_input(x, ctx, g, scx, shx, scc, shc):
    seq, d = x.shape
    m = ctx.shape[0]
    assert m % (S5_CHUNK * 8) == 0 and seq % m == 0
    nx = seq // m
    nc = (seq + 2 * m) // S5_CHUNK
    vec = pl.BlockSpec((1, d), lambda i: (0, 0))
    return pl.pallas_call(
        _s5_input_kernel,
        grid=(nx + 2,),
        in_specs=[pl.BlockSpec((m, d), lambda i: (jnp.clip(i - 1, 0, nx - 1), 0)),
                  pl.BlockSpec((m, d), lambda i: (0, 0)), vec, vec, vec, vec, vec],
        out_specs=pl.BlockSpec((S5_CHUNK, m // S5_CHUNK, d), lambda i: (0, i, 0)),
        out_shape=jax.ShapeDtypeStruct((S5_CHUNK, nc, d), BF16),
        compiler_params=_cparams("arbitrary"),
        name="s5_input",
    )(x, ctx, g, scx, shx, scc, shc)


def _sublane_transpose(v):
    n = v[0].shape[0]
    v = [a.reshape(n // 8, 8, LANES) for a in v]
    row = lax.broadcasted_iota(jnp.int32, v[0].shape, 1)
    s = 4
    while s:
        keep = (row & s) == 0
        for i in range(8):
            if i & s:
                continue
            a, b = v[i], v[i + s]
            v[i] = jnp.where(keep, a, pltpu.roll(b, s, axis=1))
            v[i + s] = jnp.where(keep, pltpu.roll(a, 8 - s, axis=1), b)
        s //= 2
    return [a.reshape(n, LANES) for a in v]


def _s5_kernel(z_ref, wm_ref, wb_ref, wc_ref, lam_ref, o_ref, u_scr, b_scr, hf_scr, hb_scr, *, c_lo):
    L, nc, _ = z_ref.shape
    n_out = o_ref.shape[1]
    gs = FIELDS
    half = SSM_STATE
    for j in range(L // FIELDS):
        v = [pltpu.bitcast(z_ref[j * FIELDS + i], jnp.uint32) for i in range(FIELDS)]
        w = _field_transpose(v)
        for gi in range(gs):
            u_scr[gi, :, j * LANES:(j + 1) * LANES] = pltpu.bitcast(w[gi], BF16)

    bs = [jnp.dot(u_scr[gi], wb_ref[gi], preferred_element_type=F32) for gi in range(gs)]
    for comp in range(2):
        t = _sublane_transpose([b[:, comp * LANES:(comp + 1) * LANES] for b in bs])
        for j in range(8):
            b_scr[comp, j] = t[j]

    a_re = lam_ref[:, :2 * half]
    a_im = lam_ref[:, 2 * half:]
    fwd = lax.broadcasted_iota(jnp.int32, (gs, 2 * half), 1) < half

    def step(t, carry):
        s_re, s_im = carry
        rf = pl.ds(pl.multiple_of(t * 8, 8), 8)
        rb = pl.ds(pl.multiple_of(nc - 8 - t * 8, 8), 8)
        for jf in range(8):
            jb = 7 - jf
            hf_scr[0, jf, rf, :] = s_re
            hf_scr[1, jf, rf, :] = s_im
            hb_scr[0, jb, rb, :] = s_re
            hb_scr[1, jb, rb, :] = s_im
            v_re = jnp.where(fwd, b_scr[0, jf, rf, :], b_scr[0, jb, rb, :])
            v_im = jnp.where(fwd, b_scr[1, jf, rf, :], b_scr[1, jb, rb, :])
            s_re, s_im = a_re * s_re - a_im * s_im + v_re, a_re * s_im + a_im * s_re + v_im
        return s_re, s_im

    zero = jnp.zeros((gs, 2 * half), F32)
    lax.fori_loop(0, nc // 8, step, (zero, zero))

    fwd_all = lax.broadcasted_iota(jnp.int32, (nc, 2 * half), 1) < half
    h_in = [_sublane_transpose([jnp.where(fwd_all, hf_scr[comp, j], hb_scr[comp, j]) for j in range(8)])
            for comp in range(2)]
    ys = []
    for gi in range(gs):
        hcat = jnp.concatenate([h_in[0][gi], h_in[1][gi]], axis=1)[c_lo:c_lo + n_out].astype(BF16)
        ys.append(jnp.dot(u_scr[gi, c_lo:c_lo + n_out, :], wm_ref[gi], preferred_element_type=F32)
                  + _dot_nt(hcat, wc_ref[gi]))
    for j in range(L // FIELDS):
        w = _field_transpose([y[:, j * LANES:(j + 1) * LANES] for y in ys])
        for i in range(FIELDS):
            o_ref[j * FIELDS + i] = w[i]


def _s5_scan(zp, wm, wb, wc, lam, *, c_lo, n_out):
    L, nc, d = zp.shape
    w = wm.shape[-1]
    gs = FIELDS
    assert nc % 8 == 0 and gs == 8
    slab = pltpu.VMEM((2, 8, nc, 2 * SSM_STATE), F32)
    return pl.pallas_call(
        functools.partial(_s5_kernel, c_lo=c_lo),
        grid=(d // LANES,),
        in_specs=[pl.BlockSpec((L, nc, LANES), lambda i: (0, 0, i)),
                  pl.BlockSpec((gs, w, w), lambda i: (i, 0, 0)),
                  pl.BlockSpec((gs, w, w), lambda i: (i, 0, 0)),
                  pl.BlockSpec((gs, w, w), lambda i: (i, 0, 0)),
                  pl.BlockSpec((gs, w), lambda i: (i, 0))],
        out_specs=pl.BlockSpec((L, n_out, LANES), lambda i: (0, 0, i)),
        out_shape=jax.ShapeDtypeStruct((L, n_out, d), F32),
        scratch_shapes=[pltpu.VMEM((gs, nc, w), BF16), slab, slab, slab],
        compiler_params=_cparams("arbitrary"),
        name="s5_scan",
    )(zp, wm, wb, wc, lam)


def _shift_lanes(x, s):
    w = LANES
    x0, x1 = x[:, :w], x[:, w:]
    zero = jnp.zeros_like(x0)
    lane = lax.broadcasted_iota(jnp.int32, x0.shape, 1)
    if s == 0:
        return x
    if s > 0:
        if s >= w:
            t = s - w
            y1 = x0 if t == 0 else jnp.where(lane < t, 0.0, pltpu.roll(x0, t, axis=1))
            return jnp.concatenate([zero, y1], axis=1)
        r0, r1 = pltpu.roll(x0, s, axis=1), pltpu.roll(x1, s, axis=1)
        return jnp.concatenate([jnp.where(lane < s, 0.0, r0), jnp.where(lane < s, r0, r1)], axis=1)
    s = -s
    if s >= w:
        t = s - w
        y0 = x1 if t == 0 else jnp.where(lane >= w - t, 0.0, pltpu.roll(x1, w - t, axis=1))
        return jnp.concatenate([y0, zero], axis=1)
    r0, r1 = pltpu.roll(x0, w - s, axis=1), pltpu.roll(x1, w - s, axis=1)
    return jnp.concatenate([jnp.where(lane >= w - s, r1, r0), jnp.where(lane >= w - s, 0.0, r1)], axis=1)


def _s5_op_kernel(a_re_ref, a_im_ref, ldt_ref, b_re_ref, b_im_ref, c_re_ref, c_im_ref,
                  wm_ref, wb_ref, wc_ref, lam_ref, e_scr):
    gs = a_re_ref.shape[0]
    L = S5_CHUNK
    h = SSM_GROUP
    half = SSM_STATE
    fwd1 = lax.broadcasted_iota(jnp.int32, (1, 2 * half), 1) < half
    fwd = lax.broadcasted_iota(jnp.int32, (h, 2 * half), 1) < half

    def cmul(ar, ai, br, bi):
        return ar * br - ai * bi, ar * bi + ai * br

    for gi in range(gs):
        ar, ai = a_re_ref[gi], a_im_ref[gi]
        dt = jnp.exp(ldt_ref[gi])
        mag = jnp.exp(ar * dt)
        lr, li = mag * jnp.cos(ai * dt), mag * jnp.sin(ai * dt)
        den = ar * ar + ai * ai
        nr = lr - 1.0
        coef_r = (nr * ar + li * ai) / den
        coef_i = (li * ar - nr * ai) / den
        bb_r, bb_i = cmul(coef_r, coef_i, b_re_ref[gi], b_im_ref[gi])
        c_r, c_i = c_re_ref[gi], c_im_ref[gi]
        pw = [(jnp.ones_like(lr), jnp.zeros_like(lr))]
        for _ in range(L):
            pw.append(cmul(pw[-1][0], pw[-1][1], lr, li))

        def mixed(jf, jb):
            return (jnp.where(fwd1, pw[jf][0], pw[jb][0]), jnp.where(fwd1, pw[jf][1], pw[jb][1]))

        for l in range(L):
            s_r, s_i = cmul(*mixed(L - 1 - l, l), bb_r, bb_i)
            wb_ref[gi, l * h:(l + 1) * h, :] = jnp.concatenate([s_r, s_i], axis=1).astype(BF16)
            f_r, f_i = cmul(*mixed(l + 1, L - l), c_r, c_i)
            wc_ref[gi, l * h:(l + 1) * h, :] = jnp.concatenate([f_r, -f_i], axis=1).astype(BF16)
            e_r, e_i = cmul(*mixed(l, L - 1 - l), c_r, c_i)
            e_scr[l * h:(l + 1) * h, :] = jnp.concatenate([e_r, e_i], axis=1)

        lhs = jnp.concatenate(
            [jnp.concatenate([jnp.where(fwd, bb_r, 0.0), jnp.where(fwd, -bb_i, 0.0)], axis=1),
             jnp.concatenate([jnp.where(fwd, 0.0, bb_r), jnp.where(fwd, 0.0, -bb_i)], axis=1)], axis=0)
        kt = _dot_nt(lhs, e_scr[...], precision=lax.Precision.HIGHEST)
        kt_f, kt_b = kt[:h], kt[h:]
        for l in range(L):
            blk = _shift_lanes(kt_f, h * l) + _shift_lanes(kt_b, -h * (L - 1 - l))
            wm_ref[gi, l * h:(l + 1) * h, :] = blk.astype(BF16)
        lam_ref[gi:gi + 1, :] = jnp.concatenate([pw[L][0], pw[L][1]], axis=1)


def _s5_operators(a_re, a_im, log_dt, b_re, b_im, c_re, c_im):
    _, g, p = a_re.shape
    h = b_re.shape[-1]
    gs = FIELDS
    w = S5_CHUNK * h
    assert w == 2 * LANES and 2 * p == LANES

    def lanes(v):
        return jnp.concatenate([v[0], v[1]], axis=-1)

    def chan_rows(v):
        return lanes(jnp.swapaxes(v, -1, -2))

    a_re2 = lanes(a_re).reshape(g, 1, 2 * p)
    a_im2 = lanes(a_im).reshape(g, 1, 2 * p)
    ldt2 = jnp.repeat(log_dt.T, p, axis=1).reshape(g, 1, 2 * p)
    vec = pl.BlockSpec((gs, 1, 2 * p), lambda i: (i, 0, 0))
    mat = pl.BlockSpec((gs, h, 2 * p), lambda i: (i, 0, 0))
    return pl.pallas_call(
        _s5_op_kernel,
        grid=(g // gs,),
        in_specs=[vec, vec, vec, mat, mat, mat, mat],
        out_specs=[pl.BlockSpec((gs, w, w), lambda i: (i, 0, 0)),
                   pl.BlockSpec((gs, w, w), lambda i: (i, 0, 0)),
                   pl.BlockSpec((gs, w, w), lambda i: (i, 0, 0)),
                   pl.BlockSpec((gs, w), lambda i: (i, 0))],
        out_shape=[jax.ShapeDtypeStruct((g, w, w), BF16), jax.ShapeDtypeStruct((g, w, w), BF16),
                   jax.ShapeDtypeStruct((g, w, w), BF16), jax.ShapeDtypeStruct((g, w), F32)],
        scratch_shapes=[pltpu.VMEM((w, w), F32)],
        compiler_params=_cparams("arbitrary"),
        name="s5_operators",
    )(a_re2, a_im2, ldt2, chan_rows(b_re), chan_rows(b_im), lanes(c_re), lanes(c_im))


def _gelu_tanh(y):
    return 0.5 * y * (1.0 + jnp.tanh(math.sqrt(2.0 / math.pi) * (y + 0.044715 * (y * y * y))))


def _glu_kernel(x_ref, ys_ref, g_ref, sc_ref, sh_ref, d_ref, gate_ref, bv_ref, bg_ref, wv_ref, wg_ref, o_ref,
                act_scr):
    j = pl.program_id(1)
    tn = o_ref.shape[1]

    @pl.when(j == 0)
    def _():
        tm, d = x_ref.shape
        nb = S5_CHUNK * S5_CHUNK
        perm = _chunk_perm(nb, False)
        for blk in range(tm // nb):
            rest = ys_ref[:, blk * S5_CHUNK:(blk + 1) * S5_CHUNK, :].reshape(nb, d)
            ys = jnp.zeros_like(rest)
            for _ in range(3):
                piece = rest.astype(BF16)
                rest = rest - piece.astype(F32)
                ys = ys + jnp.dot(perm, piece, preferred_element_type=F32)
            rows = slice(blk * nb, (blk + 1) * nb)
            hx = _norm_mod(x_ref[rows, :], g_ref[...], sc_ref[...], sh_ref[...])
            act_scr[rows, :] = _gelu_tanh(d_ref[...] * hx + ys).astype(BF16)

    act = act_scr[...]
    val = jnp.dot(act, wv_ref[...], preferred_element_type=F32) + bv_ref[...]
    gate = jnp.dot(act, wg_ref[...], preferred_element_type=F32) + bg_ref[...]
    cols = pl.ds(pl.multiple_of(j * tn, tn), tn)
    o_ref[...] = x_ref[:, cols] + gate_ref[:, cols] * (val * (1.0 / (1.0 + jnp.exp(-gate))))


def _glu(x, ys, g, sc, sh, d_skip, gate, w, b, *, tm):
    m, d = x.shape
    tn = 512
    nj = d // tn
    vec = pl.BlockSpec((1, d), lambda i, j: (0, 0))
    return pl.pallas_call(
        _glu_kernel,
        grid=(m // tm, nj),
        in_specs=[pl.BlockSpec((tm, d), lambda i, j: (i, 0)),
                  pl.BlockSpec((S5_CHUNK, tm // S5_CHUNK, d), lambda i, j: (0, i, 0)), vec, vec, vec, vec, vec,
                  pl.BlockSpec((1, tn), lambda i, j: (0, j)),
                  pl.BlockSpec((1, tn), lambda i, j: (0, nj + j)),
                  pl.BlockSpec((d, tn), lambda i, j: (0, j)),
                  pl.BlockSpec((d, tn), lambda i, j: (0, nj + j))],
        out_specs=pl.BlockSpec((tm, tn), lambda i, j: (i, j)),
        out_shape=jax.ShapeDtypeStruct((m, d), F32),
        scratch_shapes=[pltpu.VMEM((tm, d), BF16)],
        compiler_params=_cparams("arbitrary", "arbitrary"),
        name="s5_glu",
    )(x, ys, g, sc, sh, d_skip, gate, b, b, w, w)


def kernel(x, c, ctx, c_ctx, ada_w, ada_b, norm_mix, norm_ffn, ffn_w1, ffn_w3, ffn_w2, attn_w_in, attn_w_out,
           attn_rpb, attn_sink, ssm_a_re, ssm_a_im, ssm_log_dt, ssm_b_re, ssm_b_im, ssm_c_re, ssm_c_im,
           ssm_d, ssm_w_glu, ssm_b_glu, norm_final):
    batch, seq, d = x.shape
    assert batch == 1 and ada_w.shape[0] == 2
    m = ctx.shape[1]
    xs, cs = x[0], ctx[0]
    mods = _ada_mod(c, c_ctx, ada_w, ada_b)

    def mod(layer, who):
        return [mods[layer, who, i * d:(i + 1) * d].reshape(1, d) for i in range(6)]

    row = lambda v: v.reshape(1, d)
    tm = 512
    tm_qkv = 1024

    sh1, sc1, g1, sh2, sc2, g2 = mod(0, 0)
    csh1, csc1, cg1, csh2, csc2, cg2 = mod(0, 1)
    cos, sin = _rope_tables(seq)
    w_in = attn_w_in[0].astype(BF16)
    w_out = attn_w_out[0].astype(BF16)
    w1, w3, w2 = ffn_w1.astype(BF16), ffn_w3.astype(BF16), ffn_w2.astype(BF16)
    nm, nf = row(norm_mix[0]), row(norm_ffn[0])
    qkv = _qkv_proj(xs, nm, sc1, sh1, w_in, cos, sin, rope=True, tm=tm_qkv)
    qkv_c = _qkv_proj(cs, nm, csc1, csh1, w_in, cos[:m], sin[:m], rope=False, tm=m)
    oa = _neighbourhood_attention(qkv, qkv_c, attn_rpb[0])
    ob = _window_attention(qkv, qkv_c, attn_sink[0])
    oc = _context_attention(qkv_c, attn_sink[0])
    xs = _out_proj(oa, ob, 0, 0, w_out, xs, g1, tm=tm)
    cs = _out_proj(oc, oc, 0, 1, w_out, cs, cg1, tm=m)
    xs = _ffn(xs, nf, sc2, sh2, g2, nf, w1, w3, w2, 0, final_norm=False, tm=tm)
    cs = _ffn(cs, nf, csc2, csh2, cg2, nf, w1, w3, w2, 0, final_norm=False, tm=m)

    sh1, sc1, g1, sh2, sc2, g2 = mod(1, 0)
    csh1, csc1, _, _, _, _ = mod(1, 1)
    nm, nf = row(norm_mix[1]), row(norm_ffn[1])
    z = _s5_input(xs, cs, nm, sc1, sh1, csc1, csh1)
    wm, wb, wc, lam = _s5_operators(ssm_a_re[0], ssm_a_im[0], ssm_log_dt[0], ssm_b_re[0], ssm_b_im[0],
                                    ssm_c_re[0], ssm_c_im[0])
    ys = _s5_scan(z, wm, wb, wc, lam, c_lo=m // S5_CHUNK, n_out=seq // S5_CHUNK)
    xs = _glu(xs, ys, nm, sc1, sh1, row(ssm_d[0]), g1, ssm_w_glu[0].astype(BF16), ssm_b_glu[0].reshape(1, 2 * d),
              tm=tm)
    xs = _ffn(xs, nf, sc2, sh2, g2, row(norm_final), w1, w3, w2, 1, final_norm=True, tm=tm)
    return xs[None]
```

```python
import functools
import math

import jax
import jax.numpy as jnp
from jax import lax
from jax.experimental import pallas as pl
from jax.experimental.pallas import tpu as pltpu

F32 = jnp.float32
BF16 = jnp.bfloat16

GRID_W = 64
HEAD_DIM = 128
NA_HEADS = 8
NB_Q_HEADS = 8
NB_KV_HEADS = 2
NB_GROUP = NB_Q_HEADS // NB_KV_HEADS
NA_ROWS = 8
NA_COLS = 16
SW_BLOCK = 128
ROPE_BASE = 10000.0
SSM_GROUP = 16
SSM_STATE = 64
EPS = 1e-6
NEG_INF = -1e30
ATTN_SCALE = HEAD_DIM ** -0.5

A_WIDTH = NA_HEADS * HEAD_DIM
B_Q_WIDTH = NB_Q_HEADS * HEAD_DIM
B_KV_WIDTH = NB_KV_HEADS * HEAD_DIM
QA_BLK = 0
KA_BLK = NA_HEADS
VA_BLK = 2 * NA_HEADS
QB_BLK = 3 * NA_HEADS
KB_BLK = QB_BLK + NB_Q_HEADS
VB_BLK = KB_BLK + NB_KV_HEADS

NA_QROWS = 8
NA_KROWS = 16
NA_KBLK = 4
NA_HEADS_PER_STEP = 2
S5_CHUNK = 16
LANES = 128
FIELD = SSM_GROUP
FIELDS = LANES // FIELD
VMEM_LIMIT = 56 * 1024 * 1024


def _cparams(*sem):
    return pltpu.CompilerParams(dimension_semantics=sem, vmem_limit_bytes=VMEM_LIMIT)


def _silu(v):
    return v * (1.0 / (1.0 + jnp.exp(-v)))


def _norm_mod(x, g, sc, sh):
    ms = jnp.mean(x * x, axis=-1, keepdims=True)
    return (x * lax.rsqrt(ms + EPS) * g) * (1.0 + sc) + sh


def _dot_nt(a, b, precision=None):
    return lax.dot_general(a, b, (((1,), (1,)), ((), ())), preferred_element_type=F32, precision=precision)


def _ada_kernel(cb_ref, w_ref, b_ref, o_ref, s_scr, *, rows_per_step):
    d, tn = w_ref.shape
    rep = tn // 128

    @pl.when((pl.program_id(0) == 0) & (pl.program_id(1) == 0))
    def _():
        s_scr[...] = _silu(cb_ref[...])

    def body(i, acc):
        r = pl.multiple_of(i * rows_per_step, rows_per_step)
        w = w_ref[pl.ds(r, rows_per_step), :]
        out = []
        for v in range(2):
            s = s_scr[v, pl.ds(r, rows_per_step), :]
            st = jnp.concatenate([s] * rep, axis=1)
            out.append(acc[v] + jnp.sum((w * st).reshape(rows_per_step // 8, 8, tn), axis=0))
        return tuple(out)

    zero = jnp.zeros((8, tn), F32)
    acc = lax.fori_loop(0, d // rows_per_step, body, (zero, zero))
    o_ref[...] = jnp.concatenate([jnp.sum(a, axis=0, keepdims=True) for a in acc], axis=0) + b_ref[...]


def _ada_mod(c, c_ctx, ada_w, ada_b):
    depth, d, n = ada_w.shape
    tn = 512
    cb = jnp.stack([jnp.broadcast_to(c.reshape(d, 1), (d, 128)),
                    jnp.broadcast_to(c_ctx.reshape(d, 1), (d, 128))])
    return pl.pallas_call(
        functools.partial(_ada_kernel, rows_per_step=64),
        grid=(depth, n // tn),
        in_specs=[pl.BlockSpec((2, d, 128), lambda l, j: (0, 0, 0)),
                  pl.BlockSpec((None, d, tn), lambda l, j: (l, 0, j)),
                  pl.BlockSpec((None, 1, tn), lambda l, j: (l, 0, j))],
        out_specs=pl.BlockSpec((None, 2, tn), lambda l, j: (l, 0, j)),
        out_shape=jax.ShapeDtypeStruct((depth, 2, n), F32),
        scratch_shapes=[pltpu.VMEM((2, d, 128), F32)],
        compiler_params=_cparams("arbitrary", "arbitrary"),
        name="ada_mod",
    )(cb, ada_w, ada_b.reshape(depth, 1, n))


def _rope(a, cos, sin):
    lane = lax.broadcasted_iota(jnp.int32, a.shape, 1)
    partner = jnp.where((lane & 63) < 32, pltpu.roll(a, 96, axis=1), pltpu.roll(a, 32, axis=1))
    return a * cos + partner * sin


def _qkv_kernel(x_ref, g_ref, sc_ref, sh_ref, w_ref, cos_ref, sin_ref, o_ref, h_scr, *, rope):
    j = pl.program_id(1)
    tn = o_ref.shape[1]
    heads = tn // HEAD_DIM

    @pl.when(j == 0)
    def _():
        h_scr[...] = _norm_mod(x_ref[...], g_ref[...], sc_ref[...], sh_ref[...]).astype(BF16)

    acc = jnp.dot(h_scr[...], w_ref[...], preferred_element_type=F32)
    col0 = j * heads

    def rotated(n_heads, scale):
        parts = []
        for hh in range(heads):
            a = acc[:, hh * HEAD_DIM:(hh + 1) * HEAD_DIM]
            if hh < n_heads:
                if rope:
                    a = _rope(a, cos_ref[...], sin_ref[...])
                a = a * scale
            parts.append(a)
        return jnp.concatenate(parts, axis=1)

    is_qa = col0 < KA_BLK
    is_qb = (col0 >= QB_BLK) & (col0 < KB_BLK)
    is_kb = col0 == KB_BLK

    @pl.when(is_qa)
    def _():
        o_ref[...] = (acc * ATTN_SCALE).astype(o_ref.dtype)

    @pl.when(is_qb)
    def _():
        o_ref[...] = rotated(heads, ATTN_SCALE).astype(o_ref.dtype)

    @pl.when(is_kb)
    def _():
        o_ref[...] = rotated(NB_KV_HEADS, 1.0).astype(o_ref.dtype)

    @pl.when(jnp.logical_not(is_qa | is_qb | is_kb))
    def _():
        o_ref[...] = acc.astype(o_ref.dtype)


def _qkv_proj(x, g, sc, sh, w, cos, sin, *, rope, tm):
    m, d = x.shape
    n = w.shape[1]
    tn = 512
    assert KB_BLK % (tn // HEAD_DIM) == 0 and m % tm == 0 and n % tn == 0
    vec = pl.BlockSpec((1, d), lambda i, j: (0, 0))
    tab = pl.BlockSpec((tm, HEAD_DIM), lambda i, j: (i, 0))
    return pl.pallas_call(
        functools.partial(_qkv_kernel, rope=rope),
        grid=(m // tm, n // tn),
        in_specs=[pl.BlockSpec((tm, d), lambda i, j: (i, 0)), vec, vec, vec,
                  pl.BlockSpec((d, tn), lambda i, j: (0, j)), tab, tab],
        out_specs=pl.BlockSpec((tm, tn), lambda i, j: (i, j)),
        out_shape=jax.ShapeDtypeStruct((m, n), BF16),
        scratch_shapes=[pltpu.VMEM((tm, d), BF16)],
        compiler_params=_cparams("arbitrary", "arbitrary"),
        name="qkv_rope" if rope else "qkv_ctx",
    )(x, g, sc, sh, w, cos, sin)


def _rope_tables(seq):
    quarter = HEAD_DIM // 4
    rows = seq // GRID_W
    inv_freq = ROPE_BASE ** (-jnp.arange(quarter, dtype=F32) / quarter)
    ang_r = jnp.arange(rows, dtype=F32)[:, None] * inv_freq[None, :]
    ang_c = jnp.arange(GRID_W, dtype=F32)[:, None] * inv_freq[None, :]

    def tokens(tab_r, tab_c, sign):
        r = jnp.broadcast_to(tab_r[:, None, :], (rows, GRID_W, quarter))
        c = jnp.broadcast_to(tab_c[None, :, :], (rows, GRID_W, quarter))
        return jnp.concatenate([sign * r, r, sign * c, c], axis=-1).reshape(seq, HEAD_DIM)

    return tokens(jnp.cos(ang_r), jnp.cos(ang_c), 1.0), tokens(jnp.sin(ang_r), jnp.sin(ang_c), -1.0)


def _win_kernel(sink_ref, q_ref, k0_ref, k1_ref, k2_ref, v0_ref, v1_ref, v2_ref, kc_ref, vc_ref, o_ref,
                *, seq):
    n = pl.program_id(0)
    blk = SW_BLOCK
    shape = (NB_GROUP * blk, 3 * blk)
    row = lax.broadcasted_iota(jnp.int32, shape, 0)
    col = lax.broadcasted_iota(jnp.int32, shape, 1)
    rel = col - (row & (blk - 1))
    kpos = (n - 1) * blk + col
    valid = (rel >= 0) & (rel <= 2 * blk) & (kpos >= 0) & (kpos < seq)
    grp = lax.broadcasted_iota(jnp.int32, (NB_GROUP * blk, 1), 0) // blk
    outs = []
    for h in range(NB_KV_HEADS):
        hd = slice(h * HEAD_DIM, (h + 1) * HEAD_DIM)
        qs = jnp.concatenate([q_ref[:, (h * NB_GROUP + g) * HEAD_DIM:(h * NB_GROUP + g + 1) * HEAD_DIM]
                              for g in range(NB_GROUP)], axis=0)
        k = jnp.concatenate([k0_ref[:, hd], k1_ref[:, hd], k2_ref[:, hd]], axis=0)
        v = jnp.concatenate([v0_ref[:, hd], v1_ref[:, hd], v2_ref[:, hd]], axis=0)
        s = jnp.where(valid, _dot_nt(qs, k), NEG_INF)
        s_ctx = _dot_nt(qs, kc_ref[:, hd])
        sink = jnp.zeros((NB_GROUP * blk, 1), F32)
        for g in range(NB_GROUP):
            sink = jnp.where(grp == g, sink_ref[h * NB_GROUP + g], sink)
        m = jnp.maximum(jnp.maximum(jnp.max(s, axis=-1, keepdims=True), jnp.max(s_ctx, axis=-1, keepdims=True)),
                        sink)
        p = jnp.exp(s - m)
        p_ctx = jnp.exp(s_ctx - m)
        den = jnp.sum(p, axis=-1, keepdims=True) + jnp.sum(p_ctx, axis=-1, keepdims=True) + jnp.exp(sink - m)
        o = (jnp.dot(p.astype(BF16), v, preferred_element_type=F32)
             + jnp.dot(p_ctx.astype(BF16), vc_ref[:, hd], preferred_element_type=F32))
        o = o * (1.0 / den)
        outs += [o[g * blk:(g + 1) * blk] for g in range(NB_GROUP)]
    o_ref[...] = jnp.concatenate(outs, axis=1).astype(o_ref.dtype)


def _window_attention(qkv, qkv_ctx, sink):
    seq = qkv.shape[0]
    m = qkv_ctx.shape[0]
    nb = seq // SW_BLOCK
    assert QB_BLK % NB_Q_HEADS == 0 and KB_BLK % NB_KV_HEADS == 0 and VB_BLK % NB_KV_HEADS == 0

    def kv_spec(col_blk, shift):
        return pl.BlockSpec((SW_BLOCK, B_KV_WIDTH),
                            lambda n: (jnp.clip(n + shift, 0, nb - 1), col_blk // NB_KV_HEADS))

    return pl.pallas_call(
        functools.partial(_win_kernel, seq=seq),
        grid=(nb,),
        in_specs=[pl.BlockSpec(memory_space=pltpu.SMEM),
                  pl.BlockSpec((SW_BLOCK, B_Q_WIDTH), lambda n: (n, QB_BLK // NB_Q_HEADS)),
                  kv_spec(KB_BLK, -1), kv_spec(KB_BLK, 0), kv_spec(KB_BLK, 1),
                  kv_spec(VB_BLK, -1), kv_spec(VB_BLK, 0), kv_spec(VB_BLK, 1),
                  pl.BlockSpec((m, B_KV_WIDTH), lambda n: (0, KB_BLK // NB_KV_HEADS)),
                  pl.BlockSpec((m, B_KV_WIDTH), lambda n: (0, VB_BLK // NB_KV_HEADS))],
        out_specs=pl.BlockSpec((SW_BLOCK, B_Q_WIDTH), lambda n: (n, 0)),
        out_shape=jax.ShapeDtypeStruct((seq, B_Q_WIDTH), BF16),
        compiler_params=_cparams("arbitrary"),
        name="window_attn",
    )(sink, qkv, qkv, qkv, qkv, qkv, qkv, qkv, qkv_ctx, qkv_ctx)


def _na_kernel(rpb_ref, q_ref, k0_ref, k1_ref, k2_ref, k3_ref, v0_ref, v1_ref, v2_ref, v3_ref,
               kc_ref, vc_ref, o_ref, cb_scr, bias_scr, *, rows):
    h0 = pl.program_id(0) * NA_HEADS_PER_STEP
    t = pl.program_id(1)
    n_tiles = rows // NA_QROWS
    n_dr = 2 * NA_ROWS - 1
    n_dc = 2 * NA_COLS - 1
    half = GRID_W

    @pl.when(t == 0)
    def _():
        shape = (GRID_W, 2 * half)
        cq = lax.broadcasted_iota(jnp.int32, shape, 0)
        lane = lax.broadcasted_iota(jnp.int32, shape, 1)
        ck = lane & (half - 1)
        hi = lane >= half
        bidx = jnp.clip(ck - cq + (NA_COLS - 1), 0, n_dc - 1)
        col_ok = (ck - jnp.clip(cq - NA_COLS // 2, 0, GRID_W - NA_COLS))
        col_ok = (col_ok >= 0) & (col_ok < NA_COLS)
        for hh in range(NA_HEADS_PER_STEP):
            for i in range(n_dr + 1):
                val = jnp.full(shape, NEG_INF, F32)
                for e in range(2):
                    a = i - 1 + e
                    if 0 <= a < n_dr:
                        sel = hi if e else jnp.logical_not(hi)
                        for b in range(n_dc):
                            val = jnp.where(sel & (bidx == b), rpb_ref[((h0 + hh) * n_dr + a) * n_dc + b], val)
                cb_scr[hh, i] = jnp.where(col_ok, val, NEG_INF)

    r0 = t * NA_QROWS
    kb0 = jnp.clip(r0 - NA_ROWS // 2, 0, rows - NA_KROWS)

    @pl.when((t <= 1) | (t == n_tiles - 1))
    def _():
        lane = lax.broadcasted_iota(jnp.int32, (GRID_W, 2 * half), 1)
        lo = lane < half
        for rq in range(NA_QROWS):
            rq_abs = r0 + rq
            ws = jnp.clip(rq_abs - NA_ROWS // 2, 0, rows - NA_ROWS)
            for jj in range(NA_KROWS // 2):
                rk0 = kb0 + 2 * jj
                ok0 = (rk0 >= ws) & (rk0 < ws + NA_ROWS)
                ok1 = (rk0 + 1 >= ws) & (rk0 + 1 < ws + NA_ROWS)
                idx = jnp.clip(rk0 - rq_abs + NA_ROWS, 0, n_dr)
                keep = (lo & ok0) | (jnp.logical_not(lo) & ok1)
                for hh in range(NA_HEADS_PER_STEP):
                    bias_scr[hh, rq * GRID_W:(rq + 1) * GRID_W, jj * 2 * half:(jj + 1) * 2 * half] = (
                        jnp.where(keep, cb_scr[hh, idx], NEG_INF))

    outs = []
    for hh in range(NA_HEADS_PER_STEP):
        hd = slice(hh * HEAD_DIM, (hh + 1) * HEAD_DIM)
        q = q_ref[:, hd]
        k = jnp.concatenate([k0_ref[:, hd], k1_ref[:, hd], k2_ref[:, hd], k3_ref[:, hd]], axis=0)
        v = jnp.concatenate([v0_ref[:, hd], v1_ref[:, hd], v2_ref[:, hd], v3_ref[:, hd]], axis=0)
        s = _dot_nt(q, k) + bias_scr[hh]
        s_ctx = _dot_nt(q, kc_ref[:, hd])
        m = jnp.maximum(jnp.max(s, axis=-1, keepdims=True), jnp.max(s_ctx, axis=-1, keepdims=True))
        p = jnp.exp(s - m)
        p_ctx = jnp.exp(s_ctx - m)
        den = jnp.sum(p, axis=-1, keepdims=True) + jnp.sum(p_ctx, axis=-1, keepdims=True)
        o = (jnp.dot(p.astype(BF16), v, preferred_element_type=F32)
             + jnp.dot(p_ctx.astype(BF16), vc_ref[:, hd], preferred_element_type=F32))
        outs.append(o * (1.0 / den))
    o_ref[...] = jnp.concatenate(outs, axis=1).astype(o_ref.dtype)


def _neighbourhood_attention(qkv, qkv_ctx, rpb):
    seq = qkv.shape[0]
    m = qkv_ctx.shape[0]
    rows = seq // GRID_W
    assert rows % NA_QROWS == 0 and rows >= NA_KROWS + NA_QROWS
    n_tiles = rows // NA_QROWS
    tq = NA_QROWS * GRID_W
    tk = NA_KBLK * GRID_W
    n_kblk = NA_KROWS // NA_KBLK

    hp = NA_HEADS_PER_STEP
    wh = hp * HEAD_DIM
    assert NA_HEADS % hp == 0 and QA_BLK % hp == 0 and KA_BLK % hp == 0 and VA_BLK % hp == 0

    def kv_spec(col_blk, i):
        def index(h, t):
            first = jnp.clip(t * (NA_QROWS // NA_KBLK) - 1, 0, rows // NA_KBLK - n_kblk)
            return (first + i, col_blk // hp + h)
        return pl.BlockSpec((tk, wh), index)

    return pl.pallas_call(
        functools.partial(_na_kernel, rows=rows),
        grid=(NA_HEADS // hp, n_tiles),
        in_specs=[pl.BlockSpec(memory_space=pltpu.SMEM),
                  pl.BlockSpec((tq, wh), lambda h, t: (t, QA_BLK // hp + h))]
                 + [kv_spec(KA_BLK, i) for i in range(n_kblk)]
                 + [kv_spec(VA_BLK, i) for i in range(n_kblk)]
                 + [pl.BlockSpec((m, wh), lambda h, t: (0, KA_BLK // hp + h)),
                    pl.BlockSpec((m, wh), lambda h, t: (0, VA_BLK // hp + h))],
        out_specs=pl.BlockSpec((tq, wh), lambda h, t: (t, h)),
        out_shape=jax.ShapeDtypeStruct((seq, A_WIDTH), BF16),
        scratch_shapes=[pltpu.VMEM((hp, 2 * NA_ROWS, GRID_W, 2 * GRID_W), F32),
                        pltpu.VMEM((hp, tq, NA_KROWS * GRID_W), F32)],
        compiler_params=_cparams("arbitrary", "arbitrary"),
        name="neighbourhood_attn",
    )(rpb.reshape(-1), qkv, *([qkv] * (2 * n_kblk)), qkv_ctx, qkv_ctx)


def _ctx_attn_kernel(sink_ref, q_ref, k_ref, v_ref, o_ref):
    j = pl.program_id(0)
    s = _dot_nt(q_ref[...], k_ref[...])
    has_sink = j >= NA_HEADS
    sink = jnp.where(has_sink, sink_ref[jnp.maximum(j - NA_HEADS, 0)], NEG_INF)
    m = jnp.maximum(jnp.max(s, axis=-1, keepdims=True), sink)
    p = jnp.exp(s - m)
    den = jnp.sum(p, axis=-1, keepdims=True) + jnp.where(has_sink, jnp.exp(sink - m), 0.0)
    o = jnp.dot(p.astype(BF16), v_ref[...], preferred_element_type=F32)
    o_ref[...] = (o * (1.0 / den)).astype(o_ref.dtype)


def _context_attention(qkv_ctx, sink):
    m = qkv_ctx.shape[0]

    def q_idx(j):
        return (0, jnp.where(j < NA_HEADS, QA_BLK + j, QB_BLK + j - NA_HEADS))

    def k_idx(j):
        return (0, jnp.where(j < NA_HEADS, KA_BLK + j, KB_BLK + (j - NA_HEADS) // NB_GROUP))

    def v_idx(j):
        return (0, jnp.where(j < NA_HEADS, VA_BLK + j, VB_BLK + (j - NA_HEADS) // NB_GROUP))

    return pl.pallas_call(
        _ctx_attn_kernel,
        grid=(NA_HEADS + NB_Q_HEADS,),
        in_specs=[pl.BlockSpec(memory_space=pltpu.SMEM),
                  pl.BlockSpec((m, HEAD_DIM), q_idx),
                  pl.BlockSpec((m, HEAD_DIM), k_idx),
                  pl.BlockSpec((m, HEAD_DIM), v_idx)],
        out_specs=pl.BlockSpec((m, HEAD_DIM), lambda j: (0, j)),
        out_shape=jax.ShapeDtypeStruct((m, A_WIDTH + B_Q_WIDTH), BF16),
        compiler_params=_cparams("arbitrary"),
        name="context_attn",
    )(sink, qkv_ctx, qkv_ctx, qkv_ctx)


def _out_proj_kernel(a_ref, b_ref, wa_ref, wb_ref, x_ref, g_ref, o_ref):
    y = (jnp.dot(a_ref[...], wa_ref[...], preferred_element_type=F32)
         + jnp.dot(b_ref[...], wb_ref[...], preferred_element_type=F32))
    o_ref[...] = x_ref[...] + g_ref[...] * y


def _out_proj(oa, ob, a_blk, b_blk, w, x, gate, *, tm):
    m, d = x.shape
    kh = w.shape[0] // 2
    return pl.pallas_call(
        _out_proj_kernel,
        grid=(m // tm,),
        in_specs=[pl.BlockSpec((tm, kh), lambda i: (i, a_blk)),
                  pl.BlockSpec((tm, kh), lambda i: (i, b_blk)),
                  pl.BlockSpec((kh, d), lambda i: (0, 0)),
                  pl.BlockSpec((kh, d), lambda i: (1, 0)),
                  pl.BlockSpec((tm, d), lambda i: (i, 0)),
                  pl.BlockSpec((1, d), lambda i: (0, 0))],
        out_specs=pl.BlockSpec((tm, d), lambda i: (i, 0)),
        out_shape=jax.ShapeDtypeStruct((m, d), F32),
        compiler_params=_cparams("arbitrary"),
        name="out_proj",
    )(oa, ob, w, w, x, gate)


def _ffn_kernel(x_ref, g_ref, sc_ref, sh_ref, gate_ref, gf_ref, w1_ref, w3_ref, w2_ref, o_ref, h_scr, act_scr,
                *, final_norm, n_up):
    s = pl.program_id(1)
    tf = w1_ref.shape[1]
    tn = w2_ref.shape[1]

    @pl.when(s == 0)
    def _():
        h_scr[...] = _norm_mod(x_ref[...], g_ref[...], sc_ref[...], sh_ref[...]).astype(BF16)

    @pl.when(s < n_up)
    def _():
        h = h_scr[...]
        a = jnp.dot(h, w1_ref[...], preferred_element_type=F32)
        b = jnp.dot(h, w3_ref[...], preferred_element_type=F32)
        act_scr[:, pl.ds(pl.multiple_of(s * tf, tf), tf)] = (_silu(a) * b).astype(BF16)

    @pl.when(s >= n_up)
    def _():
        cols = pl.ds(pl.multiple_of((s - n_up) * tn, tn), tn)
        y = jnp.dot(act_scr[...], w2_ref[...], preferred_element_type=F32)
        o_ref[:, cols] = x_ref[:, cols] + gate_ref[:, cols] * y

    if final_norm:
        @pl.when(s == pl.num_programs(1) - 1)
        def _():
            r = o_ref[...]
            ms = jnp.mean(r * r, axis=-1, keepdims=True)
            o_ref[...] = r * lax.rsqrt(ms + EPS) * gf_ref[...]


def _ffn(x, g, sc, sh, gate, gf, w1, w3, w2, layer, *, final_norm, tm):
    m, d = x.shape
    dff = w1.shape[-1]
    tf, tn = 512, 256
    assert dff % tf == 0 and d % tn == 0
    n_up, n_dn = dff // tf, d // tn
    vec = pl.BlockSpec((1, d), lambda i, s: (0, 0))
    once = pl.Buffered(1) if m > tm else None
    return pl.pallas_call(
        functools.partial(_ffn_kernel, final_norm=final_norm, n_up=n_up),
        grid=(m // tm, n_up + n_dn),
        in_specs=[pl.BlockSpec((tm, d), lambda i, s: (i, 0), pipeline_mode=once), vec, vec, vec, vec, vec,
                  pl.BlockSpec((None, d, tf), lambda i, s: (layer, 0, jnp.minimum(s, n_up - 1))),
                  pl.BlockSpec((None, d, tf), lambda i, s: (layer, 0, jnp.minimum(s, n_up - 1))),
                  pl.BlockSpec((None, dff, tn), lambda i, s: (layer, 0, jnp.maximum(s - n_up, 0)))],
        out_specs=pl.BlockSpec((tm, d), lambda i, s: (i, 0), pipeline_mode=once),
        out_shape=jax.ShapeDtypeStruct((m, d), F32),
        scratch_shapes=[pltpu.VMEM((tm, d), BF16), pltpu.VMEM((tm, dff), BF16)],
        compiler_params=_cparams("arbitrary", "arbitrary"),
        name="swiglu_final" if final_norm else "swiglu",
    )(x, g, sc, sh, gate, gf, w1, w3, w2)


def _chunk_perm(n, to_steps):
    r = lax.broadcasted_iota(jnp.int32, (n, n), 0)
    c = lax.broadcasted_iota(jnp.int32, (n, n), 1)
    nch = n // S5_CHUNK
    src = (r % nch) * S5_CHUNK + r // nch if to_steps else (r % S5_CHUNK) * nch + r // S5_CHUNK
    return (c == src).astype(BF16)


def _field_transpose(v):
    v = list(v)
    lane = lax.broadcasted_iota(jnp.int32, v[0].shape, 1)
    s = FIELDS // 2
    while s:
        keep = ((lane // FIELD) & s) == 0
        for i in range(FIELDS):
            if i & s:
                continue
            a, b = v[i], v[i + s]
            v[i] = jnp.where(keep, a, pltpu.roll(b, FIELD * s, axis=1))
            v[i + s] = jnp.where(keep, pltpu.roll(a, LANES - FIELD * s, axis=1), b)
        s //= 2
    return v


def _s5_input_kernel(x_ref, c_ref, g_ref, scx_ref, shx_ref, scc_ref, shc_ref, o_ref):
    i = pl.program_id(0)
    is_ctx = (i == 0) | (i == pl.num_programs(0) - 1)
    n = x_ref.shape[0]

    def emit(h):
        hp = jnp.dot(_chunk_perm(n, True), h.astype(BF16), preferred_element_type=F32)
        o_ref[...] = hp.astype(BF16).reshape(o_ref.shape)

    @pl.when(is_ctx)
    def _():
        emit(_norm_mod(c_ref[...], g_ref[...], scc_ref[...], shc_ref[...]))

    @pl.when(jnp.logical_not(is_ctx))
    def _():
        emit(_norm_mod(x_ref[...], g_ref[...], scx_ref[...], shx_ref[...]))


def _s5_input(x, ctx, g, scx, shx, scc, shc):
    seq, d = x.shape
    m = ctx.shape[0]
    assert m % (S5_CHUNK * 8) == 0 and seq % m == 0
    nx = seq // m
    nc = (seq + 2 * m) // S5_CHUNK
    vec = pl.BlockSpec((1, d), lambda i: (0, 0))
    return pl.pallas_call(
        _s5_input_kernel,
        grid=(nx + 2,),
        in_specs=[pl.BlockSpec((m, d), lambda i: (jnp.clip(i - 1, 0, nx - 1), 0)),
                  pl.BlockSpec((m, d), lambda i: (0, 0)), vec, vec, vec, vec, vec],
        out_specs=pl.BlockSpec((S5_CHUNK, m // S5_CHUNK, d), lambda i: (0, i, 0)),
        out_shape=jax.ShapeDtypeStruct((S5_CHUNK, nc, d), BF16),
        compiler_params=_cparams("arbitrary"),
        name="s5_input",
    )(x, ctx, g, scx, shx, scc, shc)


def _sublane_transpose(v):
    n = v[0].shape[0]
    v = [a.reshape(n // 8, 8, LANES) for a in v]
    row = lax.broadcasted_iota(jnp.int32, v[0].shape, 1)
    s = 4
    while s:
        keep = (row & s) == 0
        for i in range(8):
            if i & s:
                continue
            a, b = v[i], v[i + s]
            v[i] = jnp.where(keep, a, pltpu.roll(b, s, axis=1))
            v[i + s] = jnp.where(keep, pltpu.roll(a, 8 - s, axis=1), b)
        s //= 2
    return [a.reshape(n, LANES) for a in v]


def _s5_kernel(z_ref, wm_ref, wb_ref, wc_ref, lam_ref, o_ref, u_scr, b_scr, hf_scr, hb_scr, *, c_lo):
    L, nc, _ = z_ref.shape
    n_out = o_ref.shape[1]
    gs = FIELDS
    half = SSM_STATE
    for j in range(L // FIELDS):
        v = [pltpu.bitcast(z_ref[j * FIELDS + i], jnp.uint32) for i in range(FIELDS)]
        w = _field_transpose(v)
        for gi in range(gs):
            u_scr[gi, :, j * LANES:(j + 1) * LANES] = pltpu.bitcast(w[gi], BF16)

    bs = [jnp.dot(u_scr[gi], wb_ref[gi], preferred_element_type=F32) for gi in range(gs)]
    for comp in range(2):
        t = _sublane_transpose([b[:, comp * LANES:(comp + 1) * LANES] for b in bs])
        for j in range(8):
            b_scr[comp, j] = t[j]

    a_re = lam_ref[:, :2 * half]
    a_im = lam_ref[:, 2 * half:]
    fwd = lax.broadcasted_iota(jnp.int32, (gs, 2 * half), 1) < half

    def step(t, carry):
        s_re, s_im = carry
        rf = pl.ds(pl.multiple_of(t * 8, 8), 8)
        rb = pl.ds(pl.multiple_of(nc - 8 - t * 8, 8), 8)
        for jf in range(8):
            jb = 7 - jf
            hf_scr[0, jf, rf, :] = s_re
            hf_scr[1, jf, rf, :] = s_im
            hb_scr[0, jb, rb, :] = s_re
            hb_scr[1, jb, rb, :] = s_im
            v_re = jnp.where(fwd, b_scr[0, jf, rf, :], b_scr[0, jb, rb, :])
            v_im = jnp.where(fwd, b_scr[1, jf, rf, :], b_scr[1, jb, rb, :])
            s_re, s_im = a_re * s_re - a_im * s_im + v_re, a_re * s_im + a_im * s_re + v_im
        return s_re, s_im

    zero = jnp.zeros((gs, 2 * half), F32)
    lax.fori_loop(0, nc // 8, step, (zero, zero))

    fwd_all = lax.broadcasted_iota(jnp.int32, (nc, 2 * half), 1) < half
    h_in = [_sublane_transpose([jnp.where(fwd_all, hf_scr[comp, j], hb_scr[comp, j]) for j in range(8)])
            for comp in range(2)]
    ys = []
    for gi in range(gs):
        hcat = jnp.concatenate([h_in[0][gi], h_in[1][gi]], axis=1)[c_lo:c_lo + n_out].astype(BF16)
        ys.append(jnp.dot(u_scr[gi, c_lo:c_lo + n_out, :], wm_ref[gi], preferred_element_type=F32)
                  + _dot_nt(hcat, wc_ref[gi]))
    for j in range(L // FIELDS):
        w = _field_transpose([y[:, j * LANES:(j + 1) * LANES] for y in ys])
        for i in range(FIELDS):
            o_ref[j * FIELDS + i] = w[i]


def _s5_scan(zp, wm, wb, wc, lam, *, c_lo, n_out):
    L, nc, d = zp.shape
    w = wm.shape[-1]
    gs = FIELDS
    assert nc % 8 == 0 and gs == 8
    slab = pltpu.VMEM((2, 8, nc, 2 * SSM_STATE), F32)
    return pl.pallas_call(
        functools.partial(_s5_kernel, c_lo=c_lo),
        grid=(d // LANES,),
        in_specs=[pl.BlockSpec((L, nc, LANES), lambda i: (0, 0, i)),
                  pl.BlockSpec((gs, w, w), lambda i: (i, 0, 0)),
                  pl.BlockSpec((gs, w, w), lambda i: (i, 0, 0)),
                  pl.BlockSpec((gs, w, w), lambda i: (i, 0, 0)),
                  pl.BlockSpec((gs, w), lambda i: (i, 0))],
        out_specs=pl.BlockSpec((L, n_out, LANES), lambda i: (0, 0, i)),
        out_shape=jax.ShapeDtypeStruct((L, n_out, d), F32),
        scratch_shapes=[pltpu.VMEM((gs, nc, w), BF16), slab, slab, slab],
        compiler_params=_cparams("arbitrary"),
        name="s5_scan",
    )(zp, wm, wb, wc, lam)


def _shift_lanes(x, s):
    w = LANES
    x0, x1 = x[:, :w], x[:, w:]
    zero = jnp.zeros_like(x0)
    lane = lax.broadcasted_iota(jnp.int32, x0.shape, 1)
    if s == 0:
        return x
    if s > 0:
        if s >= w:
            t = s - w
            y1 = x0 if t == 0 else jnp.where(lane < t, 0.0, pltpu.roll(x0, t, axis=1))
            return jnp.concatenate([zero, y1], axis=1)
        r0, r1 = pltpu.roll(x0, s, axis=1), pltpu.roll(x1, s, axis=1)
        return jnp.concatenate([jnp.where(lane < s, 0.0, r0), jnp.where(lane < s, r0, r1)], axis=1)
    s = -s
    if s >= w:
        t = s - w
        y0 = x1 if t == 0 else jnp.where(lane >= w - t, 0.0, pltpu.roll(x1, w - t, axis=1))
        return jnp.concatenate([y0, zero], axis=1)
    r0, r1 = pltpu.roll(x0, w - s, axis=1), pltpu.roll(x1, w - s, axis=1)
    return jnp.concatenate([jnp.where(lane >= w - s, r1, r0), jnp.where(lane >= w - s, 0.0, r1)], axis=1)


def _s5_op_kernel(a_re_ref, a_im_ref, ldt_ref, b_re_ref, b_im_ref, c_re_ref, c_im_ref,
                  wm_ref, wb_ref, wc_ref, lam_ref, e_scr):
    gs = a_re_ref.shape[0]
    L = S5_CHUNK
    h = SSM_GROUP
    half = SSM_STATE
    fwd1 = lax.broadcasted_iota(jnp.int32, (1, 2 * half), 1) < half
    fwd = lax.broadcasted_iota(jnp.int32, (h, 2 * half), 1) < half

    def cmul(ar, ai, br, bi):
        return ar * br - ai * bi, ar * bi + ai * br

    for gi in range(gs):
        ar, ai = a_re_ref[gi], a_im_ref[gi]
        dt = jnp.exp(ldt_ref[gi])
        mag = jnp.exp(ar * dt)
        lr, li = mag * jnp.cos(ai * dt), mag * jnp.sin(ai * dt)
        den = ar * ar + ai * ai
        nr = lr - 1.0
        coef_r = (nr * ar + li * ai) / den
        coef_i = (li * ar - nr * ai) / den
        bb_r, bb_i = cmul(coef_r, coef_i, b_re_ref[gi], b_im_ref[gi])
        c_r, c_i = c_re_ref[gi], c_im_ref[gi]
        pw = [(jnp.ones_like(lr), jnp.zeros_like(lr))]
        for _ in range(L):
            pw.append(cmul(pw[-1][0], pw[-1][1], lr, li))

        def mixed(jf, jb):
            return (jnp.where(fwd1, pw[jf][0], pw[jb][0]), jnp.where(fwd1, pw[jf][1], pw[jb][1]))

        for l in range(L):
            s_r, s_i = cmul(*mixed(L - 1 - l, l), bb_r, bb_i)
            wb_ref[gi, l * h:(l + 1) * h, :] = jnp.concatenate([s_r, s_i], axis=1).astype(BF16)
            f_r, f_i = cmul(*mixed(l + 1, L - l), c_r, c_i)
            wc_ref[gi, l * h:(l + 1) * h, :] = jnp.concatenate([f_r, -f_i], axis=1).astype(BF16)
            e_r, e_i = cmul(*mixed(l, L - 1 - l), c_r, c_i)
            e_scr[l * h:(l + 1) * h, :] = jnp.concatenate([e_r, e_i], axis=1)

        lhs = jnp.concatenate(
            [jnp.concatenate([jnp.where(fwd, bb_r, 0.0), jnp.where(fwd, -bb_i, 0.0)], axis=1),
             jnp.concatenate([jnp.where(fwd, 0.0, bb_r), jnp.where(fwd, 0.0, -bb_i)], axis=1)], axis=0)
        kt = _dot_nt(lhs, e_scr[...], precision=lax.Precision.HIGHEST)
        kt_f, kt_b = kt[:h], kt[h:]
        for l in range(L):
            blk = _shift_lanes(kt_f, h * l) + _shift_lanes(kt_b, -h * (L - 1 - l))
            wm_ref[gi, l * h:(l + 1) * h, :] = blk.astype(BF16)
        lam_ref[gi:gi + 1, :] = jnp.concatenate([pw[L][0], pw[L][1]], axis=1)


def _s5_operators(a_re, a_im, log_dt, b_re, b_im, c_re, c_im):
    _, g, p = a_re.shape
    h = b_re.shape[-1]
    gs = FIELDS
    w = S5_CHUNK * h
    assert w == 2 * LANES and 2 * p == LANES

    def lanes(v):
        return jnp.concatenate([v[0], v[1]], axis=-1)

    def chan_rows(v):
        return lanes(jnp.swapaxes(v, -1, -2))

    a_re2 = lanes(a_re).reshape(g, 1, 2 * p)
    a_im2 = lanes(a_im).reshape(g, 1, 2 * p)
    ldt2 = jnp.repeat(log_dt.T, p, axis=1).reshape(g, 1, 2 * p)
    vec = pl.BlockSpec((gs, 1, 2 * p), lambda i: (i, 0, 0))
    mat = pl.BlockSpec((gs, h, 2 * p), lambda i: (i, 0, 0))
    return pl.pallas_call(
        _s5_op_kernel,
        grid=(g // gs,),
        in_specs=[vec, vec, vec, mat, mat, mat, mat],
        out_specs=[pl.BlockSpec((gs, w, w), lambda i: (i, 0, 0)),
                   pl.BlockSpec((gs, w, w), lambda i: (i, 0, 0)),
                   pl.BlockSpec((gs, w, w), lambda i: (i, 0, 0)),
                   pl.BlockSpec((gs, w), lambda i: (i, 0))],
        out_shape=[jax.ShapeDtypeStruct((g, w, w), BF16), jax.ShapeDtypeStruct((g, w, w), BF16),
                   jax.ShapeDtypeStruct((g, w, w), BF16), jax.ShapeDtypeStruct((g, w), F32)],
        scratch_shapes=[pltpu.VMEM((w, w), F32)],
        compiler_params=_cparams("arbitrary"),
        name="s5_operators",
    )(a_re2, a_im2, ldt2, chan_rows(b_re), chan_rows(b_im), lanes(c_re), lanes(c_im))


def _gelu_tanh(y):
    return 0.5 * y * (1.0 + jnp.tanh(math.sqrt(2.0 / math.pi) * (y + 0.044715 * (y * y * y))))


def _glu_kernel(x_ref, ys_ref, g_ref, sc_ref, sh_ref, d_ref, gate_ref, b_ref, w_ref, o_ref, act_scr):
    j = pl.program_id(1)
    tn = o_ref.shape[1]
    d_model = x_ref.shape[1]

    @pl.when(j == 0)
    def _():
        tm, d = x_ref.shape
        nb = S5_CHUNK * S5_CHUNK
        perm = _chunk_perm(nb, False)
        for blk in range(tm // nb):
            rest = ys_ref[:, blk * S5_CHUNK:(blk + 1) * S5_CHUNK, :].reshape(nb, d)
            ys = jnp.zeros_like(rest)
            for _ in range(3):
                piece = rest.astype(BF16)
                rest = rest - piece.astype(F32)
                ys = ys + jnp.dot(perm, piece, preferred_element_type=F32)
            rows = slice(blk * nb, (blk + 1) * nb)
            hx = _norm_mod(x_ref[rows, :], g_ref[...], sc_ref[...], sh_ref[...])
            act_scr[rows, :] = _gelu_tanh(d_ref[...] * hx + ys).astype(BF16)

    act = act_scr[...]
    cols = pl.ds(pl.multiple_of(j * tn, tn), tn)
    gcols = pl.ds(pl.multiple_of(d_model + j * tn, tn), tn)
    val = jnp.dot(act, w_ref[:, cols], preferred_element_type=F32) + b_ref[:, cols]
    gate = jnp.dot(act, w_ref[:, gcols], preferred_element_type=F32) + b_ref[:, gcols]
    o_ref[...] = x_ref[:, cols] + gate_ref[:, cols] * (val * (1.0 / (1.0 + jnp.exp(-gate))))


def _glu(x, ys, g, sc, sh, d_skip, gate, w, b, *, tm):
    m, d = x.shape
    tn = 512
    nj = d // tn
    vec = pl.BlockSpec((1, d), lambda i, j: (0, 0))
    return pl.pallas_call(
        _glu_kernel,
        grid=(m // tm, nj),
        in_specs=[pl.BlockSpec((tm, d), lambda i, j: (i, 0)),
                  pl.BlockSpec((S5_CHUNK, tm // S5_CHUNK, d), lambda i, j: (0, i, 0)), vec, vec, vec, vec, vec,
                  pl.BlockSpec((1, 2 * d), lambda i, j: (0, 0)),
                  pl.BlockSpec((d, 2 * d), lambda i, j: (0, 0), pipeline_mode=pl.Buffered(1))],
        out_specs=pl.BlockSpec((tm, tn), lambda i, j: (i, j)),
        out_shape=jax.ShapeDtypeStruct((m, d), F32),
        scratch_shapes=[pltpu.VMEM((tm, d), BF16)],
        compiler_params=_cparams("arbitrary", "arbitrary"),
        name="s5_glu",
    )(x, ys, g, sc, sh, d_skip, gate, b, w)


def kernel(x, c, ctx, c_ctx, ada_w, ada_b, norm_mix, norm_ffn, ffn_w1, ffn_w3, ffn_w2, attn_w_in, attn_w_out,
           attn_rpb, attn_sink, ssm_a_re, ssm_a_im, ssm_log_dt, ssm_b_re, ssm_b_im, ssm_c_re, ssm_c_im,
           ssm_d, ssm_w_glu, ssm_b_glu, norm_final):
    batch, seq, d = x.shape
    assert batch == 1 and ada_w.shape[0] == 2
    m = ctx.shape[1]
    xs, cs = x[0], ctx[0]
    mods = _ada_mod(c, c_ctx, ada_w, ada_b)

    def mod(layer, who):
        return [mods[layer, who, i * d:(i + 1) * d].reshape(1, d) for i in range(6)]

    row = lambda v: v.reshape(1, d)
    tm = 512
    tm_qkv = 1024
    tm_ffn = 1024

    sh1, sc1, g1, sh2, sc2, g2 = mod(0, 0)
    csh1, csc1, cg1, csh2, csc2, cg2 = mod(0, 1)
    cos, sin = _rope_tables(seq)
    w_in = attn_w_in[0].astype(BF16)
    w_out = attn_w_out[0].astype(BF16)
    w1, w3, w2 = ffn_w1.astype(BF16), ffn_w3.astype(BF16), ffn_w2.astype(BF16)
    nm, nf = row(norm_mix[0]), row(norm_ffn[0])
    qkv = _qkv_proj(xs, nm, sc1, sh1, w_in, cos, sin, rope=True, tm=tm_qkv)
    qkv_c = _qkv_proj(cs, nm, csc1, csh1, w_in, cos[:m], sin[:m], rope=False, tm=m)
    oa = _neighbourhood_attention(qkv, qkv_c, attn_rpb[0])
    ob = _window_attention(qkv, qkv_c, attn_sink[0])
    oc = _context_attention(qkv_c, attn_sink[0])
    xs = _out_proj(oa, ob, 0, 0, w_out, xs, g1, tm=tm)
    cs = _out_proj(oc, oc, 0, 1, w_out, cs, cg1, tm=m)
    xs = _ffn(xs, nf, sc2, sh2, g2, nf, w1, w3, w2, 0, final_norm=False, tm=tm_ffn)
    cs = _ffn(cs, nf, csc2, csh2, cg2, nf, w1, w3, w2, 0, final_norm=False, tm=m)

    sh1, sc1, g1, sh2, sc2, g2 = mod(1, 0)
    csh1, csc1, _, _, _, _ = mod(1, 1)
    nm, nf = row(norm_mix[1]), row(norm_ffn[1])
    z = _s5_input(xs, cs, nm, sc1, sh1, csc1, csh1)
    wm, wb, wc, lam = _s5_operators(ssm_a_re[0], ssm_a_im[0], ssm_log_dt[0], ssm_b_re[0], ssm_b_im[0],
                                    ssm_c_re[0], ssm_c_im[0])
    ys = _s5_scan(z, wm, wb, wc, lam, c_lo=m // S5_CHUNK, n_out=seq // S5_CHUNK)
    xs = _glu(xs, ys, nm, sc1, sh1, row(ssm_d[0]), g1, ssm_w_glu[0].astype(BF16), ssm_b_glu[0].reshape(1, 2 * d),
              tm=tm)
    xs = _ffn(xs, nf, sc2, sh2, g2, row(norm_final), w1, w3, w2, 1, final_norm=True, tm=tm_ffn)
    return xs[None]
```

```python
import functools
import math

import jax
import jax.numpy as jnp
from jax import lax
from jax.experimental import pallas as pl
from jax.experimental.pallas import tpu as pltpu

F32 = jnp.float32
BF16 = jnp.bfloat16

GRID_W = 64
HEAD_DIM = 128
NA_HEADS = 8
NB_Q_HEADS = 8
NB_KV_HEADS = 2
NB_GROUP = NB_Q_HEADS // NB_KV_HEADS
NA_ROWS = 8
NA_COLS = 16
SW_BLOCK = 128
ROPE_BASE = 10000.0
SSM_GROUP = 16
SSM_STATE = 64
EPS = 1e-6
NEG_INF = -1e30
ATTN_SCALE = HEAD_DIM ** -0.5

A_WIDTH = NA_HEADS * HEAD_DIM
B_Q_WIDTH = NB_Q_HEADS * HEAD_DIM
B_KV_WIDTH = NB_KV_HEADS * HEAD_DIM
QA_BLK = 0
KA_BLK = NA_HEADS
VA_BLK = 2 * NA_HEADS
QB_BLK = 3 * NA_HEADS
KB_BLK = QB_BLK + NB_Q_HEADS
VB_BLK = KB_BLK + NB_KV_HEADS

NA_QROWS = 8
NA_KROWS = 16
NA_KBLK = 4
NA_HEADS_PER_STEP = 2
S5_CHUNK = 16
S5_REORDER_PIECES = 2
LANES = 128
FIELD = SSM_GROUP
FIELDS = LANES // FIELD
VMEM_LIMIT = 56 * 1024 * 1024
FFN_VMEM_LIMIT = 62 * 1024 * 1024


def _cparams(*sem, limit=VMEM_LIMIT):
    return pltpu.CompilerParams(dimension_semantics=sem, vmem_limit_bytes=limit)


def _silu(v):
    return v * (1.0 / (1.0 + jnp.exp(-v)))


def _norm_mod(x, g, sc, sh):
    ms = jnp.mean(x * x, axis=-1, keepdims=True)
    return (x * lax.rsqrt(ms + EPS) * g) * (1.0 + sc) + sh


def _dot_nt(a, b, precision=None):
    return lax.dot_general(a, b, (((1,), (1,)), ((), ())), preferred_element_type=F32, precision=precision)


def _ada_kernel(cb_ref, w_ref, b_ref, o_ref, s_scr, *, rows_per_step):
    d, tn = w_ref.shape
    rep = tn // 128

    @pl.when((pl.program_id(0) == 0) & (pl.program_id(1) == 0))
    def _():
        s_scr[...] = _silu(cb_ref[...])

    def body(i, acc):
        r = pl.multiple_of(i * rows_per_step, rows_per_step)
        w = w_ref[pl.ds(r, rows_per_step), :]
        out = []
        for v in range(2):
            s = s_scr[v, pl.ds(r, rows_per_step), :]
            st = jnp.concatenate([s] * rep, axis=1)
            out.append(acc[v] + jnp.sum((w * st).reshape(rows_per_step // 8, 8, tn), axis=0))
        return tuple(out)

    zero = jnp.zeros((8, tn), F32)
    acc = lax.fori_loop(0, d // rows_per_step, body, (zero, zero))
    o_ref[...] = jnp.concatenate([jnp.sum(a, axis=0, keepdims=True) for a in acc], axis=0) + b_ref[...]


def _ada_mod(c, c_ctx, ada_w, ada_b):
    depth, d, n = ada_w.shape
    tn = 1024
    cb = jnp.stack([jnp.broadcast_to(c.reshape(d, 1), (d, 128)),
                    jnp.broadcast_to(c_ctx.reshape(d, 1), (d, 128))])
    return pl.pallas_call(
        functools.partial(_ada_kernel, rows_per_step=64),
        grid=(depth, n // tn),
        in_specs=[pl.BlockSpec((2, d, 128), lambda l, j: (0, 0, 0)),
                  pl.BlockSpec((None, d, tn), lambda l, j: (l, 0, j)),
                  pl.BlockSpec((None, 1, tn), lambda l, j: (l, 0, j))],
        out_specs=pl.BlockSpec((None, 2, tn), lambda l, j: (l, 0, j)),
        out_shape=jax.ShapeDtypeStruct((depth, 2, n), F32),
        scratch_shapes=[pltpu.VMEM((2, d, 128), F32)],
        compiler_params=_cparams("arbitrary", "arbitrary"),
        name="ada_mod",
    )(cb, ada_w, ada_b.reshape(depth, 1, n))


def _rope(a, cos, sin):
    lane = lax.broadcasted_iota(jnp.int32, a.shape, 1)
    partner = jnp.where((lane & 63) < 32, pltpu.roll(a, 96, axis=1), pltpu.roll(a, 32, axis=1))
    return a * cos + partner * sin


def _qkv_kernel(x_ref, g_ref, sc_ref, sh_ref, w_ref, cos_ref, sin_ref, o_ref, h_scr, *, rope):
    j = pl.program_id(1)
    tn = o_ref.shape[1]
    heads = tn // HEAD_DIM

    @pl.when(j == 0)
    def _():
        h_scr[...] = _norm_mod(x_ref[...], g_ref[...], sc_ref[...], sh_ref[...]).astype(BF16)

    acc = jnp.dot(h_scr[...], w_ref[...], preferred_element_type=F32)
    col0 = j * heads

    def rotated(n_heads, scale):
        parts = []
        for hh in range(heads):
            a = acc[:, hh * HEAD_DIM:(hh + 1) * HEAD_DIM]
            if hh < n_heads:
                if rope:
                    a = _rope(a, cos_ref[...], sin_ref[...])
                a = a * scale
            parts.append(a)
        return jnp.concatenate(parts, axis=1)

    is_qa = col0 < KA_BLK
    is_qb = (col0 >= QB_BLK) & (col0 < KB_BLK)
    is_kb = col0 == KB_BLK

    @pl.when(is_qa)
    def _():
        o_ref[...] = (acc * ATTN_SCALE).astype(o_ref.dtype)

    @pl.when(is_qb)
    def _():
        o_ref[...] = rotated(heads, ATTN_SCALE).astype(o_ref.dtype)

    @pl.when(is_kb)
    def _():
        o_ref[...] = rotated(NB_KV_HEADS, 1.0).astype(o_ref.dtype)

    @pl.when(jnp.logical_not(is_qa | is_qb | is_kb))
    def _():
        o_ref[...] = acc.astype(o_ref.dtype)


def _qkv_proj(x, g, sc, sh, w, cos, sin, *, rope, tm):
    m, d = x.shape
    n = w.shape[1]
    tn = 512
    assert KB_BLK % (tn // HEAD_DIM) == 0 and m % tm == 0 and n % tn == 0
    vec = pl.BlockSpec((1, d), lambda i, j: (0, 0))
    tab = pl.BlockSpec((tm, HEAD_DIM), lambda i, j: (i, 0))
    return pl.pallas_call(
        functools.partial(_qkv_kernel, rope=rope),
        grid=(m // tm, n // tn),
        in_specs=[pl.BlockSpec((tm, d), lambda i, j: (i, 0)), vec, vec, vec,
                  pl.BlockSpec((d, tn), lambda i, j: (0, j)), tab, tab],
        out_specs=pl.BlockSpec((tm, tn), lambda i, j: (i, j)),
        out_shape=jax.ShapeDtypeStruct((m, n), BF16),
        scratch_shapes=[pltpu.VMEM((tm, d), BF16)],
        compiler_params=_cparams("arbitrary", "arbitrary"),
        name="qkv_rope" if rope else "qkv_ctx",
    )(x, g, sc, sh, w, cos, sin)


def _rope_tables(seq):
    quarter = HEAD_DIM // 4
    rows = seq // GRID_W
    inv_freq = ROPE_BASE ** (-jnp.arange(quarter, dtype=F32) / quarter)
    ang_r = jnp.arange(rows, dtype=F32)[:, None] * inv_freq[None, :]
    ang_c = jnp.arange(GRID_W, dtype=F32)[:, None] * inv_freq[None, :]

    def tokens(tab_r, tab_c, sign):
        r = jnp.broadcast_to(tab_r[:, None, :], (rows, GRID_W, quarter))
        c = jnp.broadcast_to(tab_c[None, :, :], (rows, GRID_W, quarter))
        return jnp.concatenate([sign * r, r, sign * c, c], axis=-1).reshape(seq, HEAD_DIM)

    return tokens(jnp.cos(ang_r), jnp.cos(ang_c), 1.0), tokens(jnp.sin(ang_r), jnp.sin(ang_c), -1.0)


def _win_kernel(sink_ref, q_ref, k0_ref, k1_ref, k2_ref, v0_ref, v1_ref, v2_ref, kc_ref, vc_ref, o_ref,
                *, seq):
    n = pl.program_id(0)
    blk = SW_BLOCK
    shape = (NB_GROUP * blk, 3 * blk)
    row = lax.broadcasted_iota(jnp.int32, shape, 0)
    col = lax.broadcasted_iota(jnp.int32, shape, 1)
    rel = col - (row & (blk - 1))
    kpos = (n - 1) * blk + col
    valid = (rel >= 0) & (rel <= 2 * blk) & (kpos >= 0) & (kpos < seq)
    grp = lax.broadcasted_iota(jnp.int32, (NB_GROUP * blk, 1), 0) // blk
    outs = []
    for h in range(NB_KV_HEADS):
        hd = slice(h * HEAD_DIM, (h + 1) * HEAD_DIM)
        qs = jnp.concatenate([q_ref[:, (h * NB_GROUP + g) * HEAD_DIM:(h * NB_GROUP + g + 1) * HEAD_DIM]
                              for g in range(NB_GROUP)], axis=0)
        k = jnp.concatenate([k0_ref[:, hd], k1_ref[:, hd], k2_ref[:, hd]], axis=0)
        v = jnp.concatenate([v0_ref[:, hd], v1_ref[:, hd], v2_ref[:, hd]], axis=0)
        s = jnp.where(valid, _dot_nt(qs, k), NEG_INF)
        s_ctx = _dot_nt(qs, kc_ref[:, hd])
        sink = jnp.zeros((NB_GROUP * blk, 1), F32)
        for g in range(NB_GROUP):
            sink = jnp.where(grp == g, sink_ref[h * NB_GROUP + g], sink)
        m = jnp.maximum(jnp.maximum(jnp.max(s, axis=-1, keepdims=True), jnp.max(s_ctx, axis=-1, keepdims=True)),
                        sink)
        p = jnp.exp(s - m)
        p_ctx = jnp.exp(s_ctx - m)
        den = jnp.sum(p, axis=-1, keepdims=True) + jnp.sum(p_ctx, axis=-1, keepdims=True) + jnp.exp(sink - m)
        o = (jnp.dot(p.astype(BF16), v, preferred_element_type=F32)
             + jnp.dot(p_ctx.astype(BF16), vc_ref[:, hd], preferred_element_type=F32))
        o = o * (1.0 / den)
        outs += [o[g * blk:(g + 1) * blk] for g in range(NB_GROUP)]
    o_ref[...] = jnp.concatenate(outs, axis=1).astype(o_ref.dtype)


def _window_attention(qkv, qkv_ctx, sink):
    seq = qkv.shape[0]
    m = qkv_ctx.shape[0]
    nb = seq // SW_BLOCK
    assert QB_BLK % NB_Q_HEADS == 0 and KB_BLK % NB_KV_HEADS == 0 and VB_BLK % NB_KV_HEADS == 0

    def kv_spec(col_blk, shift):
        return pl.BlockSpec((SW_BLOCK, B_KV_WIDTH),
                            lambda n: (jnp.clip(n + shift, 0, nb - 1), col_blk // NB_KV_HEADS))

    return pl.pallas_call(
        functools.partial(_win_kernel, seq=seq),
        grid=(nb,),
        in_specs=[pl.BlockSpec(memory_space=pltpu.SMEM),
                  pl.BlockSpec((SW_BLOCK, B_Q_WIDTH), lambda n: (n, QB_BLK // NB_Q_HEADS)),
                  kv_spec(KB_BLK, -1), kv_spec(KB_BLK, 0), kv_spec(KB_BLK, 1),
                  kv_spec(VB_BLK, -1), kv_spec(VB_BLK, 0), kv_spec(VB_BLK, 1),
                  pl.BlockSpec((m, B_KV_WIDTH), lambda n: (0, KB_BLK // NB_KV_HEADS)),
                  pl.BlockSpec((m, B_KV_WIDTH), lambda n: (0, VB_BLK // NB_KV_HEADS))],
        out_specs=pl.BlockSpec((SW_BLOCK, B_Q_WIDTH), lambda n: (n, 0)),
        out_shape=jax.ShapeDtypeStruct((seq, B_Q_WIDTH), BF16),
        compiler_params=_cparams("arbitrary"),
        name="window_attn",
    )(sink, qkv, qkv, qkv, qkv, qkv, qkv, qkv, qkv_ctx, qkv_ctx)


def _na_kernel(rpb_ref, q_ref, k0_ref, k1_ref, k2_ref, k3_ref, v0_ref, v1_ref, v2_ref, v3_ref,
               kc_ref, vc_ref, o_ref, cb_scr, bias_scr, *, rows):
    h0 = pl.program_id(0) * NA_HEADS_PER_STEP
    t = pl.program_id(1)
    n_tiles = rows // NA_QROWS
    n_dr = 2 * NA_ROWS - 1
    n_dc = 2 * NA_COLS - 1
    half = GRID_W

    @pl.when(t == 0)
    def _():
        shape = (GRID_W, 2 * half)
        cq = lax.broadcasted_iota(jnp.int32, shape, 0)
        lane = lax.broadcasted_iota(jnp.int32, shape, 1)
        ck = lane & (half - 1)
        hi = lane >= half
        bidx = jnp.clip(ck - cq + (NA_COLS - 1), 0, n_dc - 1)
        col_ok = (ck - jnp.clip(cq - NA_COLS // 2, 0, GRID_W - NA_COLS))
        col_ok = (col_ok >= 0) & (col_ok < NA_COLS)
        for hh in range(NA_HEADS_PER_STEP):
            for i in range(n_dr + 1):
                val = jnp.full(shape, NEG_INF, F32)
                for e in range(2):
                    a = i - 1 + e
                    if 0 <= a < n_dr:
                        sel = hi if e else jnp.logical_not(hi)
                        for b in range(n_dc):
                            val = jnp.where(sel & (bidx == b), rpb_ref[((h0 + hh) * n_dr + a) * n_dc + b], val)
                cb_scr[hh, i] = jnp.where(col_ok, val, NEG_INF)

    r0 = t * NA_QROWS
    kb0 = jnp.clip(r0 - NA_ROWS // 2, 0, rows - NA_KROWS)

    @pl.when((t <= 1) | (t == n_tiles - 1))
    def _():
        lane = lax.broadcasted_iota(jnp.int32, (GRID_W, 2 * half), 1)
        lo = lane < half
        for rq in range(NA_QROWS):
            rq_abs = r0 + rq
            ws = jnp.clip(rq_abs - NA_ROWS // 2, 0, rows - NA_ROWS)
            for jj in range(NA_KROWS // 2):
                rk0 = kb0 + 2 * jj
                ok0 = (rk0 >= ws) & (rk0 < ws + NA_ROWS)
                ok1 = (rk0 + 1 >= ws) & (rk0 + 1 < ws + NA_ROWS)
                idx = jnp.clip(rk0 - rq_abs + NA_ROWS, 0, n_dr)
                keep = (lo & ok0) | (jnp.logical_not(lo) & ok1)
                for hh in range(NA_HEADS_PER_STEP):
                    bias_scr[hh, rq * GRID_W:(rq + 1) * GRID_W, jj * 2 * half:(jj + 1) * 2 * half] = (
                        jnp.where(keep, cb_scr[hh, idx], NEG_INF))

    outs = []
    for hh in range(NA_HEADS_PER_STEP):
        hd = slice(hh * HEAD_DIM, (hh + 1) * HEAD_DIM)
        q = q_ref[:, hd]
        k = jnp.concatenate([k0_ref[:, hd], k1_ref[:, hd], k2_ref[:, hd], k3_ref[:, hd]], axis=0)
        v = jnp.concatenate([v0_ref[:, hd], v1_ref[:, hd], v2_ref[:, hd], v3_ref[:, hd]], axis=0)
        s = _dot_nt(q, k) + bias_scr[hh]
        s_ctx = _dot_nt(q, kc_ref[:, hd])
        m = jnp.maximum(jnp.max(s, axis=-1, keepdims=True), jnp.max(s_ctx, axis=-1, keepdims=True))
        p = jnp.exp(s - m)
        p_ctx = jnp.exp(s_ctx - m)
        den = jnp.sum(p, axis=-1, keepdims=True) + jnp.sum(p_ctx, axis=-1, keepdims=True)
        o = (jnp.dot(p.astype(BF16), v, preferred_element_type=F32)
             + jnp.dot(p_ctx.astype(BF16), vc_ref[:, hd], preferred_element_type=F32))
        outs.append(o * (1.0 / den))
    o_ref[...] = jnp.concatenate(outs, axis=1).astype(o_ref.dtype)


def _neighbourhood_attention(qkv, qkv_ctx, rpb):
    seq = qkv.shape[0]
    m = qkv_ctx.shape[0]
    rows = seq // GRID_W
    assert rows % NA_QROWS == 0 and rows >= NA_KROWS + NA_QROWS
    n_tiles = rows // NA_QROWS
    tq = NA_QROWS * GRID_W
    tk = NA_KBLK * GRID_W
    n_kblk = NA_KROWS // NA_KBLK

    hp = NA_HEADS_PER_STEP
    wh = hp * HEAD_DIM
    assert NA_HEADS % hp == 0 and QA_BLK % hp == 0 and KA_BLK % hp == 0 and VA_BLK % hp == 0

    def kv_spec(col_blk, i):
        def index(h, t):
            first = jnp.clip(t * (NA_QROWS // NA_KBLK) - 1, 0, rows // NA_KBLK - n_kblk)
            return (first + i, col_blk // hp + h)
        return pl.BlockSpec((tk, wh), index)

    return pl.pallas_call(
        functools.partial(_na_kernel, rows=rows),
        grid=(NA_HEADS // hp, n_tiles),
        in_specs=[pl.BlockSpec(memory_space=pltpu.SMEM),
                  pl.BlockSpec((tq, wh), lambda h, t: (t, QA_BLK // hp + h))]
                 + [kv_spec(KA_BLK, i) for i in range(n_kblk)]
                 + [kv_spec(VA_BLK, i) for i in range(n_kblk)]
                 + [pl.BlockSpec((m, wh), lambda h, t: (0, KA_BLK // hp + h)),
                    pl.BlockSpec((m, wh), lambda h, t: (0, VA_BLK // hp + h))],
        out_specs=pl.BlockSpec((tq, wh), lambda h, t: (t, h)),
        out_shape=jax.ShapeDtypeStruct((seq, A_WIDTH), BF16),
        scratch_shapes=[pltpu.VMEM((hp, 2 * NA_ROWS, GRID_W, 2 * GRID_W), F32),
                        pltpu.VMEM((hp, tq, NA_KROWS * GRID_W), F32)],
        compiler_params=_cparams("arbitrary", "arbitrary"),
        name="neighbourhood_attn",
    )(rpb.reshape(-1), qkv, *([qkv] * (2 * n_kblk)), qkv_ctx, qkv_ctx)


def _ctx_attn_kernel(sink_ref, q_ref, k_ref, v_ref, o_ref):
    j = pl.program_id(0)
    s = _dot_nt(q_ref[...], k_ref[...])
    has_sink = j >= NA_HEADS
    sink = jnp.where(has_sink, sink_ref[jnp.maximum(j - NA_HEADS, 0)], NEG_INF)
    m = jnp.maximum(jnp.max(s, axis=-1, keepdims=True), sink)
    p = jnp.exp(s - m)
    den = jnp.sum(p, axis=-1, keepdims=True) + jnp.where(has_sink, jnp.exp(sink - m), 0.0)
    o = jnp.dot(p.astype(BF16), v_ref[...], preferred_element_type=F32)
    o_ref[...] = (o * (1.0 / den)).astype(o_ref.dtype)


def _context_attention(qkv_ctx, sink):
    m = qkv_ctx.shape[0]

    def q_idx(j):
        return (0, jnp.where(j < NA_HEADS, QA_BLK + j, QB_BLK + j - NA_HEADS))

    def k_idx(j):
        return (0, jnp.where(j < NA_HEADS, KA_BLK + j, KB_BLK + (j - NA_HEADS) // NB_GROUP))

    def v_idx(j):
        return (0, jnp.where(j < NA_HEADS, VA_BLK + j, VB_BLK + (j - NA_HEADS) // NB_GROUP))

    return pl.pallas_call(
        _ctx_attn_kernel,
        grid=(NA_HEADS + NB_Q_HEADS,),
        in_specs=[pl.BlockSpec(memory_space=pltpu.SMEM),
                  pl.BlockSpec((m, HEAD_DIM), q_idx),
                  pl.BlockSpec((m, HEAD_DIM), k_idx),
                  pl.BlockSpec((m, HEAD_DIM), v_idx)],
        out_specs=pl.BlockSpec((m, HEAD_DIM), lambda j: (0, j)),
        out_shape=jax.ShapeDtypeStruct((m, A_WIDTH + B_Q_WIDTH), BF16),
        compiler_params=_cparams("arbitrary"),
        name="context_attn",
    )(sink, qkv_ctx, qkv_ctx, qkv_ctx)


def _out_proj_kernel(a_ref, b_ref, wa_ref, wb_ref, x_ref, g_ref, o_ref):
    y = (jnp.dot(a_ref[...], wa_ref[...], preferred_element_type=F32)
         + jnp.dot(b_ref[...], wb_ref[...], preferred_element_type=F32))
    o_ref[...] = x_ref[...] + g_ref[...] * y


def _out_proj(oa, ob, a_blk, b_blk, w, x, gate, *, tm):
    m, d = x.shape
    kh = w.shape[0] // 2
    return pl.pallas_call(
        _out_proj_kernel,
        grid=(m // tm,),
        in_specs=[pl.BlockSpec((tm, kh), lambda i: (i, a_blk)),
                  pl.BlockSpec((tm, kh), lambda i: (i, b_blk)),
                  pl.BlockSpec((kh, d), lambda i: (0, 0)),
                  pl.BlockSpec((kh, d), lambda i: (1, 0)),
                  pl.BlockSpec((tm, d), lambda i: (i, 0)),
                  pl.BlockSpec((1, d), lambda i: (0, 0))],
        out_specs=pl.BlockSpec((tm, d), lambda i: (i, 0)),
        out_shape=jax.ShapeDtypeStruct((m, d), F32),
        compiler_params=_cparams("arbitrary"),
        name="out_proj",
    )(oa, ob, w, w, x, gate)


def _ffn_kernel(x_ref, g_ref, sc_ref, sh_ref, gate_ref, gf_ref, w1_ref, w3_ref, w2_ref, o_ref, h_scr, act_scr,
                *, final_norm, n_up):
    s = pl.program_id(1)
    tf = w1_ref.shape[1]
    tn = w2_ref.shape[1]

    @pl.when(s == 0)
    def _():
        h_scr[...] = _norm_mod(x_ref[...], g_ref[...], sc_ref[...], sh_ref[...]).astype(BF16)

    @pl.when(s < n_up)
    def _():
        h = h_scr[...]
        a = jnp.dot(h, w1_ref[...], preferred_element_type=F32)
        b = jnp.dot(h, w3_ref[...], preferred_element_type=F32)
        act_scr[:, pl.ds(pl.multiple_of(s * tf, tf), tf)] = (_silu(a) * b).astype(BF16)

    @pl.when(s >= n_up)
    def _():
        cols = pl.ds(pl.multiple_of((s - n_up) * tn, tn), tn)
        y = jnp.dot(act_scr[...], w2_ref[...], preferred_element_type=F32)
        o_ref[:, cols] = x_ref[:, cols] + gate_ref[:, cols] * y

    if final_norm:
        @pl.when(s == pl.num_programs(1) - 1)
        def _():
            r = o_ref[...]
            ms = jnp.mean(r * r, axis=-1, keepdims=True)
            o_ref[...] = r * lax.rsqrt(ms + EPS) * gf_ref[...]


def _ffn(x, g, sc, sh, gate, gf, w1, w3, w2, layer, *, final_norm, tm):
    m, d = x.shape
    dff = w1.shape[-1]
    tf, tn = 512, 256
    assert dff % tf == 0 and d % tn == 0
    n_up, n_dn = dff // tf, d // tn
    vec = pl.BlockSpec((1, d), lambda i, s: (0, 0))
    once = pl.Buffered(1) if m > tm else None
    return pl.pallas_call(
        functools.partial(_ffn_kernel, final_norm=final_norm, n_up=n_up),
        grid=(m // tm, n_up + n_dn),
        in_specs=[pl.BlockSpec((tm, d), lambda i, s: (i, 0)), vec, vec, vec, vec, vec,
                  pl.BlockSpec((None, d, tf), lambda i, s: (layer, 0, jnp.minimum(s, n_up - 1))),
                  pl.BlockSpec((None, d, tf), lambda i, s: (layer, 0, jnp.minimum(s, n_up - 1))),
                  pl.BlockSpec((None, dff, tn), lambda i, s: (layer, 0, jnp.maximum(s - n_up, 0)))],
        out_specs=pl.BlockSpec((tm, d), lambda i, s: (i, 0), pipeline_mode=once),
        out_shape=jax.ShapeDtypeStruct((m, d), F32),
        scratch_shapes=[pltpu.VMEM((tm, d), BF16), pltpu.VMEM((tm, dff), BF16)],
        compiler_params=_cparams("arbitrary", "arbitrary", limit=FFN_VMEM_LIMIT),
        name="swiglu_final" if final_norm else "swiglu",
    )(x, g, sc, sh, gate, gf, w1, w3, w2)


def _chunk_perm(n, to_steps):
    r = lax.broadcasted_iota(jnp.int32, (n, n), 0)
    c = lax.broadcasted_iota(jnp.int32, (n, n), 1)
    nch = n // S5_CHUNK
    src = (r % nch) * S5_CHUNK + r // nch if to_steps else (r % S5_CHUNK) * nch + r // S5_CHUNK
    return (c == src).astype(BF16)


def _field_transpose(v):
    v = list(v)
    lane = lax.broadcasted_iota(jnp.int32, v[0].shape, 1)
    s = FIELDS // 2
    while s:
        keep = ((lane // FIELD) & s) == 0
        for i in range(FIELDS):
            if i & s:
                continue
            a, b = v[i], v[i + s]
            v[i] = jnp.where(keep, a, pltpu.roll(b, FIELD * s, axis=1))
            v[i + s] = jnp.where(keep, pltpu.roll(a, LANES - FIELD * s, axis=1), b)
        s //= 2
    return v


def _s5_input_kernel(x_ref, c_ref, g_ref, scx_ref, shx_ref, scc_ref, shc_ref, o_ref):
    i = pl.program_id(0)
    is_ctx = (i == 0) | (i == pl.num_programs(0) - 1)
    n = x_ref.shape[0]

    def emit(h):
        hp = jnp.dot(_chunk_perm(n, True), h.astype(BF16), preferred_element_type=F32)
        o_ref[...] = hp.astype(BF16).reshape(o_ref.shape)

    @pl.when(is_ctx)
    def _():
        emit(_norm_mod(c_ref[...], g_ref[...], scc_ref[...], shc_ref[...]))

    @pl.when(jnp.logical_not(is_ctx))
    def _():
        emit(_norm_mod(x_ref[...], g_ref[...], scx_ref[...], shx_ref[...]))


def _s5_input(x, ctx, g, scx, shx, scc, shc):
    seq, d = x.shape
    m = ctx.shape[0]
    assert m % (S5_CHUNK * 8) == 0 and seq % m == 0
    nx = seq // m
    nc = (seq + 2 * m) // S5_CHUNK
    vec = pl.BlockSpec((1, d), lambda i: (0, 0))
    return pl.pallas_call(
        _s5_input_kernel,
        grid=(nx + 2,),
        in_specs=[pl.BlockSpec((m, d), lambda i: (jnp.clip(i - 1, 0, nx - 1), 0)),
                  pl.BlockSpec((m, d), lambda i: (0, 0)), vec, vec, vec, vec, vec],
        out_specs=pl.BlockSpec((S5_CHUNK, m // S5_CHUNK, d), lambda i: (0, i, 0)),
        out_shape=jax.ShapeDtypeStruct((S5_CHUNK, nc, d), BF16),
        compiler_params=_cparams("arbitrary"),
        name="s5_input",
    )(x, ctx, g, scx, shx, scc, shc)


def _sublane_transpose(v):
    n = v[0].shape[0]
    v = [a.reshape(n // 8, 8, LANES) for a in v]
    row = lax.broadcasted_iota(jnp.int32, v[0].shape, 1)
    s = 4
    while s:
        keep = (row & s) == 0
        for i in range(8):
            if i & s:
                continue
            a, b = v[i], v[i + s]
            v[i] = jnp.where(keep, a, pltpu.roll(b, s, axis=1))
            v[i + s] = jnp.where(keep, pltpu.roll(a, 8 - s, axis=1), b)
        s //= 2
    return [a.reshape(n, LANES) for a in v]


def _s5_kernel(z_ref, wm_ref, wb_ref, wc_ref, lam_ref, o_ref, u_scr, b_scr, hf_scr, hb_scr, *, c_lo):
    L, nc, _ = z_ref.shape
    n_out = o_ref.shape[1]
    gs = FIELDS
    half = SSM_STATE
    for j in range(L // FIELDS):
        v = [pltpu.bitcast(z_ref[j * FIELDS + i], jnp.uint32) for i in range(FIELDS)]
        w = _field_transpose(v)
        for gi in range(gs):
            u_scr[gi, :, j * LANES:(j + 1) * LANES] = pltpu.bitcast(w[gi], BF16)

    bs = [jnp.dot(u_scr[gi], wb_ref[gi], preferred_element_type=F32) for gi in range(gs)]
    for comp in range(2):
        t = _sublane_transpose([b[:, comp * LANES:(comp + 1) * LANES] for b in bs])
        for j in range(8):
            b_scr[comp, j] = t[j]

    a_re = lam_ref[:, :2 * half]
    a_im = lam_ref[:, 2 * half:]
    fwd = lax.broadcasted_iota(jnp.int32, (gs, 2 * half), 1) < half

    def step(t, carry):
        s_re, s_im = carry
        rf = pl.ds(pl.multiple_of(t * 8, 8), 8)
        rb = pl.ds(pl.multiple_of(nc - 8 - t * 8, 8), 8)
        for jf in range(8):
            jb = 7 - jf
            hf_scr[0, jf, rf, :] = s_re
            hf_scr[1, jf, rf, :] = s_im
            hb_scr[0, jb, rb, :] = s_re
            hb_scr[1, jb, rb, :] = s_im
            v_re = jnp.where(fwd, b_scr[0, jf, rf, :], b_scr[0, jb, rb, :])
            v_im = jnp.where(fwd, b_scr[1, jf, rf, :], b_scr[1, jb, rb, :])
            s_re, s_im = a_re * s_re - a_im * s_im + v_re, a_re * s_im + a_im * s_re + v_im
        return s_re, s_im

    zero = jnp.zeros((gs, 2 * half), F32)
    lax.fori_loop(0, nc // 8, step, (zero, zero))

    fwd_all = lax.broadcasted_iota(jnp.int32, (nc, 2 * half), 1) < half
    h_in = [_sublane_transpose([jnp.where(fwd_all, hf_scr[comp, j], hb_scr[comp, j]) for j in range(8)])
            for comp in range(2)]
    ys = []
    for gi in range(gs):
        hcat = jnp.concatenate([h_in[0][gi], h_in[1][gi]], axis=1)[c_lo:c_lo + n_out].astype(BF16)
        ys.append(jnp.dot(u_scr[gi, c_lo:c_lo + n_out, :], wm_ref[gi], preferred_element_type=F32)
                  + _dot_nt(hcat, wc_ref[gi]))
    for j in range(L // FIELDS):
        w = _field_transpose([y[:, j * LANES:(j + 1) * LANES] for y in ys])
        for i in range(FIELDS):
            o_ref[j * FIELDS + i] = w[i]


def _s5_scan(zp, wm, wb, wc, lam, *, c_lo, n_out):
    L, nc, d = zp.shape
    w = wm.shape[-1]
    gs = FIELDS
    assert nc % 8 == 0 and gs == 8
    slab = pltpu.VMEM((2, 8, nc, 2 * SSM_STATE), F32)
    return pl.pallas_call(
        functools.partial(_s5_kernel, c_lo=c_lo),
        grid=(d // LANES,),
        in_specs=[pl.BlockSpec((L, nc, LANES), lambda i: (0, 0, i)),
                  pl.BlockSpec((gs, w, w), lambda i: (i, 0, 0)),
                  pl.BlockSpec((gs, w, w), lambda i: (i, 0, 0)),
                  pl.BlockSpec((gs, w, w), lambda i: (i, 0, 0)),
                  pl.BlockSpec((gs, w), lambda i: (i, 0))],
        out_specs=pl.BlockSpec((L, n_out, LANES), lambda i: (0, 0, i)),
        out_shape=jax.ShapeDtypeStruct((L, n_out, d), F32),
        scratch_shapes=[pltpu.VMEM((gs, nc, w), BF16), slab, slab, slab],
        compiler_params=_cparams("arbitrary"),
        name="s5_scan",
    )(zp, wm, wb, wc, lam)


def _shift_lanes(x, s):
    w = LANES
    x0, x1 = x[:, :w], x[:, w:]
    zero = jnp.zeros_like(x0)
    lane = lax.broadcasted_iota(jnp.int32, x0.shape, 1)
    if s == 0:
        return x
    if s > 0:
        if s >= w:
            t = s - w
            y1 = x0 if t == 0 else jnp.where(lane < t, 0.0, pltpu.roll(x0, t, axis=1))
            return jnp.concatenate([zero, y1], axis=1)
        r0, r1 = pltpu.roll(x0, s, axis=1), pltpu.roll(x1, s, axis=1)
        return jnp.concatenate([jnp.where(lane < s, 0.0, r0), jnp.where(lane < s, r0, r1)], axis=1)
    s = -s
    if s >= w:
        t = s - w
        y0 = x1 if t == 0 else jnp.where(lane >= w - t, 0.0, pltpu.roll(x1, w - t, axis=1))
        return jnp.concatenate([y0, zero], axis=1)
    r0, r1 = pltpu.roll(x0, w - s, axis=1), pltpu.roll(x1, w - s, axis=1)
    return jnp.concatenate([jnp.where(lane >= w - s, r1, r0), jnp.where(lane >= w - s, 0.0, r1)], axis=1)


def _s5_op_kernel(a_re_ref, a_im_ref, ldt_ref, b_re_ref, b_im_ref, c_re_ref, c_im_ref,
                  wm_ref, wb_ref, wc_ref, lam_ref, e_scr):
    gs = a_re_ref.shape[0]
    L = S5_CHUNK
    h = SSM_GROUP
    half = SSM_STATE
    fwd1 = lax.broadcasted_iota(jnp.int32, (1, 2 * half), 1) < half
    fwd = lax.broadcasted_iota(jnp.int32, (h, 2 * half), 1) < half

    def cmul(ar, ai, br, bi):
        return ar * br - ai * bi, ar * bi + ai * br

    for gi in range(gs):
        ar, ai = a_re_ref[gi], a_im_ref[gi]
        dt = jnp.exp(ldt_ref[gi])
        mag = jnp.exp(ar * dt)
        lr, li = mag * jnp.cos(ai * dt), mag * jnp.sin(ai * dt)
        den = ar * ar + ai * ai
        nr = lr - 1.0
        coef_r = (nr * ar + li * ai) / den
        coef_i = (li * ar - nr * ai) / den
        bb_r, bb_i = cmul(coef_r, coef_i, b_re_ref[gi], b_im_ref[gi])
        c_r, c_i = c_re_ref[gi], c_im_ref[gi]
        pw = [(jnp.ones_like(lr), jnp.zeros_like(lr))]
        for _ in range(L):
            pw.append(cmul(pw[-1][0], pw[-1][1], lr, li))

        def mixed(jf, jb):
            return (jnp.where(fwd1, pw[jf][0], pw[jb][0]), jnp.where(fwd1, pw[jf][1], pw[jb][1]))

        for l in range(L):
            s_r, s_i = cmul(*mixed(L - 1 - l, l), bb_r, bb_i)
            wb_ref[gi, l * h:(l + 1) * h, :] = jnp.concatenate([s_r, s_i], axis=1).astype(BF16)
            f_r, f_i = cmul(*mixed(l + 1, L - l), c_r, c_i)
            wc_ref[gi, l * h:(l + 1) * h, :] = jnp.concatenate([f_r, -f_i], axis=1).astype(BF16)
            e_r, e_i = cmul(*mixed(l, L - 1 - l), c_r, c_i)
            e_scr[l * h:(l + 1) * h, :] = jnp.concatenate([e_r, e_i], axis=1)

        lhs = jnp.concatenate(
            [jnp.concatenate([jnp.where(fwd, bb_r, 0.0), jnp.where(fwd, -bb_i, 0.0)], axis=1),
             jnp.concatenate([jnp.where(fwd, 0.0, bb_r), jnp.where(fwd, 0.0, -bb_i)], axis=1)], axis=0)
        kt = _dot_nt(lhs, e_scr[...], precision=lax.Precision.HIGHEST)
        kt_f, kt_b = kt[:h], kt[h:]
        for l in range(L):
            blk = _shift_lanes(kt_f, h * l) + _shift_lanes(kt_b, -h * (L - 1 - l))
            wm_ref[gi, l * h:(l + 1) * h, :] = blk.astype(BF16)
        lam_ref[gi:gi + 1, :] = jnp.concatenate([pw[L][0], pw[L][1]], axis=1)


def _s5_operators(a_re, a_im, log_dt, b_re, b_im, c_re, c_im):
    _, g, p = a_re.shape
    h = b_re.shape[-1]
    gs = FIELDS
    w = S5_CHUNK * h
    assert w == 2 * LANES and 2 * p == LANES

    def lanes(v):
        return jnp.concatenate([v[0], v[1]], axis=-1)

    def chan_rows(v):
        return lanes(jnp.swapaxes(v, -1, -2))

    a_re2 = lanes(a_re).reshape(g, 1, 2 * p)
    a_im2 = lanes(a_im).reshape(g, 1, 2 * p)
    ldt2 = jnp.repeat(log_dt.T, p, axis=1).reshape(g, 1, 2 * p)
    vec = pl.BlockSpec((gs, 1, 2 * p), lambda i: (i, 0, 0))
    mat = pl.BlockSpec((gs, h, 2 * p), lambda i: (i, 0, 0))
    return pl.pallas_call(
        _s5_op_kernel,
        grid=(g // gs,),
        in_specs=[vec, vec, vec, mat, mat, mat, mat],
        out_specs=[pl.BlockSpec((gs, w, w), lambda i: (i, 0, 0)),
                   pl.BlockSpec((gs, w, w), lambda i: (i, 0, 0)),
                   pl.BlockSpec((gs, w, w), lambda i: (i, 0, 0)),
                   pl.BlockSpec((gs, w), lambda i: (i, 0))],
        out_shape=[jax.ShapeDtypeStruct((g, w, w), BF16), jax.ShapeDtypeStruct((g, w, w), BF16),
                   jax.ShapeDtypeStruct((g, w, w), BF16), jax.ShapeDtypeStruct((g, w), F32)],
        scratch_shapes=[pltpu.VMEM((w, w), F32)],
        compiler_params=_cparams("arbitrary"),
        name="s5_operators",
    )(a_re2, a_im2, ldt2, chan_rows(b_re), chan_rows(b_im), lanes(c_re), lanes(c_im))


def _gelu_tanh(y):
    z2 = (2.0 * math.sqrt(2.0 / math.pi)) * (y + 0.044715 * (y * y * y))
    return y * (1.0 / (1.0 + jnp.exp(-z2)))


def _glu_kernel(x_ref, ys_ref, g_ref, sc_ref, sh_ref, d_ref, gate_ref, b_ref, w_ref, o_ref, act_scr):
    j = pl.program_id(1)
    tn = o_ref.shape[1]
    d_model = x_ref.shape[1]

    @pl.when(j == 0)
    def _():
        tm, d = x_ref.shape
        nb = S5_CHUNK * S5_CHUNK
        perm = _chunk_perm(nb, False)
        for blk in range(tm // nb):
            rest = ys_ref[:, blk * S5_CHUNK:(blk + 1) * S5_CHUNK, :].reshape(nb, d)
            ys = jnp.zeros_like(rest)
            for _ in range(S5_REORDER_PIECES):
                piece = rest.astype(BF16)
                rest = rest - piece.astype(F32)
                ys = ys + jnp.dot(perm, piece, preferred_element_type=F32)
            rows = slice(blk * nb, (blk + 1) * nb)
            hx = _norm_mod(x_ref[rows, :], g_ref[...], sc_ref[...], sh_ref[...])
            act_scr[rows, :] = _gelu_tanh(d_ref[...] * hx + ys).astype(BF16)

    act = act_scr[...]
    cols = pl.ds(pl.multiple_of(j * tn, tn), tn)
    gcols = pl.ds(pl.multiple_of(d_model + j * tn, tn), tn)
    val = jnp.dot(act, w_ref[:, cols], preferred_element_type=F32) + b_ref[:, cols]
    gate = jnp.dot(act, w_ref[:, gcols], preferred_element_type=F32) + b_ref[:, gcols]
    o_ref[...] = x_ref[:, cols] + gate_ref[:, cols] * (val * (1.0 / (1.0 + jnp.exp(-gate))))


def _glu(x, ys, g, sc, sh, d_skip, gate, w, b, *, tm):
    m, d = x.shape
    tn = 512
    nj = d // tn
    vec = pl.BlockSpec((1, d), lambda i, j: (0, 0))
    return pl.pallas_call(
        _glu_kernel,
        grid=(m // tm, nj),
        in_specs=[pl.BlockSpec((tm, d), lambda i, j: (i, 0)),
                  pl.BlockSpec((S5_CHUNK, tm // S5_CHUNK, d), lambda i, j: (0, i, 0)), vec, vec, vec, vec, vec,
                  pl.BlockSpec((1, 2 * d), lambda i, j: (0, 0)),
                  pl.BlockSpec((d, 2 * d), lambda i, j: (0, 0), pipeline_mode=pl.Buffered(1))],
        out_specs=pl.BlockSpec((tm, tn), lambda i, j: (i, j)),
        out_shape=jax.ShapeDtypeStruct((m, d), F32),
        scratch_shapes=[pltpu.VMEM((tm, d), BF16)],
        compiler_params=_cparams("arbitrary", "arbitrary"),
        name="s5_glu",
    )(x, ys, g, sc, sh, d_skip, gate, b, w)


def kernel(x, c, ctx, c_ctx, ada_w, ada_b, norm_mix, norm_ffn, ffn_w1, ffn_w3, ffn_w2, attn_w_in, attn_w_out,
           attn_rpb, attn_sink, ssm_a_re, ssm_a_im, ssm_log_dt, ssm_b_re, ssm_b_im, ssm_c_re, ssm_c_im,
           ssm_d, ssm_w_glu, ssm_b_glu, norm_final):
    batch, seq, d = x.shape
    assert batch == 1 and ada_w.shape[0] == 2
    m = ctx.shape[1]
    xs, cs = x[0], ctx[0]
    mods = _ada_mod(c, c_ctx, ada_w, ada_b)

    def mod(layer, who):
        return [mods[layer, who, i * d:(i + 1) * d].reshape(1, d) for i in range(6)]

    row = lambda v: v.reshape(1, d)
    tm = 512
    tm_qkv = 1024
    tm_ffn = 1024

    sh1, sc1, g1, sh2, sc2, g2 = mod(0, 0)
    csh1, csc1, cg1, csh2, csc2, cg2 = mod(0, 1)
    cos, sin = _rope_tables(seq)
    w_in = attn_w_in[0].astype(BF16)
    w_out = attn_w_out[0].astype(BF16)
    w1, w3, w2 = ffn_w1.astype(BF16), ffn_w3.astype(BF16), ffn_w2.astype(BF16)
    nm, nf = row(norm_mix[0]), row(norm_ffn[0])
    qkv = _qkv_proj(xs, nm, sc1, sh1, w_in, cos, sin, rope=True, tm=tm_qkv)
    qkv_c = _qkv_proj(cs, nm, csc1, csh1, w_in, cos[:m], sin[:m], rope=False, tm=m)
    oa = _neighbourhood_attention(qkv, qkv_c, attn_rpb[0])
    ob = _window_attention(qkv, qkv_c, attn_sink[0])
    oc = _context_attention(qkv_c, attn_sink[0])
    xs = _out_proj(oa, ob, 0, 0, w_out, xs, g1, tm=tm)
    cs = _out_proj(oc, oc, 0, 1, w_out, cs, cg1, tm=m)
    xs = _ffn(xs, nf, sc2, sh2, g2, nf, w1, w3, w2, 0, final_norm=False, tm=tm_ffn)
    cs = _ffn(cs, nf, csc2, csh2, cg2, nf, w1, w3, w2, 0, final_norm=False, tm=m)

    sh1, sc1, g1, sh2, sc2, g2 = mod(1, 0)
    csh1, csc1, _, _, _, _ = mod(1, 1)
    nm, nf = row(norm_mix[1]), row(norm_ffn[1])
    z = _s5_input(xs, cs, nm, sc1, sh1, csc1, csh1)
    wm, wb, wc, lam = _s5_operators(ssm_a_re[0], ssm_a_im[0], ssm_log_dt[0], ssm_b_re[0], ssm_b_im[0],
                                    ssm_c_re[0], ssm_c_im[0])
    ys = _s5_scan(z, wm, wb, wc, lam, c_lo=m // S5_CHUNK, n_out=seq // S5_CHUNK)
    xs = _glu(xs, ys, nm, sc1, sh1, row(ssm_d[0]), g1, ssm_w_glu[0].astype(BF16), ssm_b_glu[0].reshape(1, 2 * d),
              tm=tm)
    xs = _ffn(xs, nf, sc2, sh2, g2, row(norm_final), w1, w3, w2, 1, final_norm=True, tm=tm_ffn)
    return xs[None]
```

```python
import functools
import math

import jax
import jax.numpy as jnp
from jax import lax
from jax.experimental import pallas as pl
from jax.experimental.pallas import tpu as pltpu

F32 = jnp.float32
BF16 = jnp.bfloat16

GRID_W = 64
HEAD_DIM = 128
NA_HEADS = 8
NB_Q_HEADS = 8
NB_KV_HEADS = 2
NB_GROUP = NB_Q_HEADS // NB_KV_HEADS
NA_ROWS = 8
NA_COLS = 16
SW_BLOCK = 128
ROPE_BASE = 10000.0
SSM_GROUP = 16
SSM_STATE = 64
EPS = 1e-6
NEG_INF = -1e30
ATTN_SCALE = HEAD_DIM ** -0.5

A_WIDTH = NA_HEADS * HEAD_DIM
B_Q_WIDTH = NB_Q_HEADS * HEAD_DIM
B_KV_WIDTH = NB_KV_HEADS * HEAD_DIM
QA_BLK = 0
KA_BLK = NA_HEADS
VA_BLK = 2 * NA_HEADS
QB_BLK = 3 * NA_HEADS
KB_BLK = QB_BLK + NB_Q_HEADS
VB_BLK = KB_BLK + NB_KV_HEADS

NA_QROWS = 8
NA_KROWS = 16
NA_KBLK = 4
NA_HEADS_PER_STEP = 2
PROLOGUE_PARTS = 8
S5_CHUNK = 16
S5_REORDER_PIECES = 2
LANES = 128
FIELD = SSM_GROUP
FIELDS = LANES // FIELD
VMEM_LIMIT = 56 * 1024 * 1024
FFN_VMEM_LIMIT = 62 * 1024 * 1024


def _cparams(*sem, limit=VMEM_LIMIT):
    return pltpu.CompilerParams(dimension_semantics=sem, vmem_limit_bytes=limit)


def _silu(v):
    return v * (1.0 / (1.0 + jnp.exp(-v)))


def _norm_mod(x, g, sc, sh):
    ms = jnp.mean(x * x, axis=-1, keepdims=True)
    return (x * lax.rsqrt(ms + EPS) * g) * (1.0 + sc) + sh


def _dot_nt(a, b, precision=None):
    return lax.dot_general(a, b, (((1,), (1,)), ((), ())), preferred_element_type=F32, precision=precision)


def _ada_kernel(cb_ref, w_ref, b_ref, o_ref, s_scr, *, rows_per_step):
    d, tn = w_ref.shape
    rep = tn // 128

    @pl.when((pl.program_id(0) == 0) & (pl.program_id(1) == 0))
    def _():
        s_scr[...] = _silu(cb_ref[...])

    def body(i, acc):
        r = pl.multiple_of(i * rows_per_step, rows_per_step)
        w = w_ref[pl.ds(r, rows_per_step), :]
        out = []
        for v in range(2):
            s = s_scr[v, pl.ds(r, rows_per_step), :]
            st = jnp.concatenate([s] * rep, axis=1)
            out.append(acc[v] + jnp.sum((w * st).reshape(rows_per_step // 8, 8, tn), axis=0))
        return tuple(out)

    zero = jnp.zeros((8, tn), F32)
    acc = lax.fori_loop(0, d // rows_per_step, body, (zero, zero))
    o_ref[...] = jnp.concatenate([jnp.sum(a, axis=0, keepdims=True) for a in acc], axis=0) + b_ref[...]


def _ada_mod(c, c_ctx, ada_w, ada_b):
    depth, d, n = ada_w.shape
    tn = 1024
    cb = jnp.stack([jnp.broadcast_to(c.reshape(d, 1), (d, 128)),
                    jnp.broadcast_to(c_ctx.reshape(d, 1), (d, 128))])
    return pl.pallas_call(
        functools.partial(_ada_kernel, rows_per_step=64),
        grid=(depth, n // tn),
        in_specs=[pl.BlockSpec((2, d, 128), lambda l, j: (0, 0, 0)),
                  pl.BlockSpec((None, d, tn), lambda l, j: (l, 0, j)),
                  pl.BlockSpec((None, 1, tn), lambda l, j: (l, 0, j))],
        out_specs=pl.BlockSpec((None, 2, tn), lambda l, j: (l, 0, j)),
        out_shape=jax.ShapeDtypeStruct((depth, 2, n), F32),
        scratch_shapes=[pltpu.VMEM((2, d, 128), F32)],
        compiler_params=_cparams("arbitrary", "arbitrary"),
        name="ada_mod",
    )(cb, ada_w, ada_b.reshape(depth, 1, n))


def _rope(a, cos, sin):
    lane = lax.broadcasted_iota(jnp.int32, a.shape, 1)
    partner = jnp.where((lane & 63) < 32, pltpu.roll(a, 96, axis=1), pltpu.roll(a, 32, axis=1))
    return a * cos + partner * sin


def _qkv_kernel(x0_ref, xn_ref, g_ref, sc_ref, sh_ref, w_ref, cos_ref, sin_ref, o_ref, h0_scr, h1_scr, acc_scr,
                *, rope):
    i = pl.program_id(0)
    j = pl.program_id(1)
    tm, tn = o_ref.shape
    heads = tn // HEAD_DIM
    part = tm // PROLOGUE_PARTS

    @pl.when((i == 0) & (j == 0))
    def _():
        h0_scr[...] = _norm_mod(x0_ref[...], g_ref[...], sc_ref[...], sh_ref[...]).astype(BF16)

    def step(h_cur, h_nxt):
        rows = pl.ds(pl.multiple_of(jnp.minimum(j, PROLOGUE_PARTS - 1) * part, part), part)
        h_nxt[rows, :] = _norm_mod(xn_ref[rows, :], g_ref[...], sc_ref[...], sh_ref[...]).astype(BF16)
        acc = jnp.dot(h_cur[...], w_ref[...], preferred_element_type=F32)
        acc_scr[...] = acc
        o_ref[...] = (acc * jnp.where(is_q, ATTN_SCALE, 1.0)).astype(o_ref.dtype)

    col0 = j * heads
    is_qb = (col0 >= QB_BLK) & (col0 < KB_BLK)
    is_kb = col0 == KB_BLK
    is_q = (col0 < KA_BLK) | is_qb

    @pl.when(i % 2 == 0)
    def _():
        step(h0_scr, h1_scr)

    @pl.when(i % 2 == 1)
    def _():
        step(h1_scr, h0_scr)

    if rope:
        def rotated(n_heads, scale):
            parts = []
            for hh in range(heads):
                a = acc_scr[:, hh * HEAD_DIM:(hh + 1) * HEAD_DIM]
                if hh < n_heads:
                    a = _rope(a, cos_ref[...], sin_ref[...]) * scale
                parts.append(a)
            return jnp.concatenate(parts, axis=1)

        @pl.when(is_qb)
        def _():
            o_ref[...] = rotated(heads, ATTN_SCALE).astype(o_ref.dtype)

        @pl.when(is_kb)
        def _():
            o_ref[...] = rotated(NB_KV_HEADS, 1.0).astype(o_ref.dtype)


def _qkv_proj(x, g, sc, sh, w, cos, sin, *, rope, tm):
    m, d = x.shape
    n = w.shape[1]
    tn = 512
    nt = m // tm
    assert KB_BLK % (tn // HEAD_DIM) == 0 and m % tm == 0 and n % tn == 0
    assert n // tn >= PROLOGUE_PARTS and tm % (16 * PROLOGUE_PARTS) == 0
    vec = pl.BlockSpec((1, d), lambda i, j: (0, 0))
    tab = pl.BlockSpec((tm, HEAD_DIM), lambda i, j: (i, 0))
    return pl.pallas_call(
        functools.partial(_qkv_kernel, rope=rope),
        grid=(nt, n // tn),
        in_specs=[pl.BlockSpec((tm, d), lambda i, j: (0, 0), pipeline_mode=pl.Buffered(1)),
                  pl.BlockSpec((tm, d), lambda i, j: (jnp.minimum(i + 1, nt - 1), 0)), vec, vec, vec,
                  pl.BlockSpec((d, tn), lambda i, j: (0, j)), tab, tab],
        out_specs=pl.BlockSpec((tm, tn), lambda i, j: (i, j)),
        out_shape=jax.ShapeDtypeStruct((m, n), BF16),
        scratch_shapes=[pltpu.VMEM((tm, d), BF16), pltpu.VMEM((tm, d), BF16), pltpu.VMEM((tm, tn), F32)],
        compiler_params=_cparams("arbitrary", "arbitrary"),
        name="qkv_rope" if rope else "qkv_ctx",
    )(x, x, g, sc, sh, w, cos, sin)


def _rope_tables(seq):
    quarter = HEAD_DIM // 4
    rows = seq // GRID_W
    inv_freq = ROPE_BASE ** (-jnp.arange(quarter, dtype=F32) / quarter)
    ang_r = jnp.arange(rows, dtype=F32)[:, None] * inv_freq[None, :]
    ang_c = jnp.arange(GRID_W, dtype=F32)[:, None] * inv_freq[None, :]

    def tokens(tab_r, tab_c, sign):
        r = jnp.broadcast_to(tab_r[:, None, :], (rows, GRID_W, quarter))
        c = jnp.broadcast_to(tab_c[None, :, :], (rows, GRID_W, quarter))
        return jnp.concatenate([sign * r, r, sign * c, c], axis=-1).reshape(seq, HEAD_DIM)

    return tokens(jnp.cos(ang_r), jnp.cos(ang_c), 1.0), tokens(jnp.sin(ang_r), jnp.sin(ang_c), -1.0)


def _win_kernel(sink_ref, q_ref, k0_ref, k1_ref, k2_ref, v0_ref, v1_ref, v2_ref, kc_ref, vc_ref, o_ref,
                valid_scr, *, seq):
    n = pl.program_id(0)
    blk = SW_BLOCK
    shape = (NB_GROUP * blk, 3 * blk)

    @pl.when((n <= 1) | (n == pl.num_programs(0) - 1))
    def _():
        row = lax.broadcasted_iota(jnp.int32, shape, 0)
        col = lax.broadcasted_iota(jnp.int32, shape, 1)
        rel = col - (row & (blk - 1))
        kpos = (n - 1) * blk + col
        ok = (rel >= 0) & (rel <= 2 * blk) & (kpos >= 0) & (kpos < seq)
        valid_scr[...] = ok.astype(F32)

    valid = valid_scr[...] > 0.5
    grp = lax.broadcasted_iota(jnp.int32, (NB_GROUP * blk, 1), 0) // blk
    outs = []
    for h in range(NB_KV_HEADS):
        hd = slice(h * HEAD_DIM, (h + 1) * HEAD_DIM)
        qs = jnp.concatenate([q_ref[:, (h * NB_GROUP + g) * HEAD_DIM:(h * NB_GROUP + g + 1) * HEAD_DIM]
                              for g in range(NB_GROUP)], axis=0)
        k = jnp.concatenate([k0_ref[:, hd], k1_ref[:, hd], k2_ref[:, hd]], axis=0)
        v = jnp.concatenate([v0_ref[:, hd], v1_ref[:, hd], v2_ref[:, hd]], axis=0)
        s = jnp.where(valid, _dot_nt(qs, k), NEG_INF)
        s_ctx = _dot_nt(qs, kc_ref[:, hd])
        sink = jnp.zeros((NB_GROUP * blk, 1), F32)
        for g in range(NB_GROUP):
            sink = jnp.where(grp == g, sink_ref[h * NB_GROUP + g], sink)
        m = jnp.maximum(jnp.maximum(jnp.max(s, axis=-1, keepdims=True), jnp.max(s_ctx, axis=-1, keepdims=True)),
                        sink)
        p = jnp.exp(s - m)
        p_ctx = jnp.exp(s_ctx - m)
        den = jnp.sum(p, axis=-1, keepdims=True) + jnp.sum(p_ctx, axis=-1, keepdims=True) + jnp.exp(sink - m)
        o = (jnp.dot(p.astype(BF16), v, preferred_element_type=F32)
             + jnp.dot(p_ctx.astype(BF16), vc_ref[:, hd], preferred_element_type=F32))
        o = o * (1.0 / den)
        outs += [o[g * blk:(g + 1) * blk] for g in range(NB_GROUP)]
    o_ref[...] = jnp.concatenate(outs, axis=1).astype(o_ref.dtype)


def _window_attention(qkv, qkv_ctx, sink):
    seq = qkv.shape[0]
    m = qkv_ctx.shape[0]
    nb = seq // SW_BLOCK
    assert QB_BLK % NB_Q_HEADS == 0 and KB_BLK % NB_KV_HEADS == 0 and VB_BLK % NB_KV_HEADS == 0

    def kv_spec(col_blk, shift):
        return pl.BlockSpec((SW_BLOCK, B_KV_WIDTH),
                            lambda n: (jnp.clip(n + shift, 0, nb - 1), col_blk // NB_KV_HEADS))

    return pl.pallas_call(
        functools.partial(_win_kernel, seq=seq),
        grid=(nb,),
        in_specs=[pl.BlockSpec(memory_space=pltpu.SMEM),
                  pl.BlockSpec((SW_BLOCK, B_Q_WIDTH), lambda n: (n, QB_BLK // NB_Q_HEADS)),
                  kv_spec(KB_BLK, -1), kv_spec(KB_BLK, 0), kv_spec(KB_BLK, 1),
                  kv_spec(VB_BLK, -1), kv_spec(VB_BLK, 0), kv_spec(VB_BLK, 1),
                  pl.BlockSpec((m, B_KV_WIDTH), lambda n: (0, KB_BLK // NB_KV_HEADS)),
                  pl.BlockSpec((m, B_KV_WIDTH), lambda n: (0, VB_BLK // NB_KV_HEADS))],
        out_specs=pl.BlockSpec((SW_BLOCK, B_Q_WIDTH), lambda n: (n, 0)),
        out_shape=jax.ShapeDtypeStruct((seq, B_Q_WIDTH), BF16),
        scratch_shapes=[pltpu.VMEM((NB_GROUP * SW_BLOCK, 3 * SW_BLOCK), F32)],
        compiler_params=_cparams("arbitrary"),
        name="window_attn",
    )(sink, qkv, qkv, qkv, qkv, qkv, qkv, qkv, qkv_ctx, qkv_ctx)


def _na_kernel(rpb_ref, q_ref, k0_ref, k1_ref, k2_ref, k3_ref, v0_ref, v1_ref, v2_ref, v3_ref,
               kc_ref, vc_ref, o_ref, cb_scr, bias_scr, *, rows):
    h0 = pl.program_id(0) * NA_HEADS_PER_STEP
    t = pl.program_id(1)
    n_tiles = rows // NA_QROWS
    n_dr = 2 * NA_ROWS - 1
    n_dc = 2 * NA_COLS - 1
    half = GRID_W

    @pl.when(t == 0)
    def _():
        shape = (GRID_W, 2 * half)
        cq = lax.broadcasted_iota(jnp.int32, shape, 0)
        lane = lax.broadcasted_iota(jnp.int32, shape, 1)
        ck = lane & (half - 1)
        hi = lane >= half
        bidx = jnp.clip(ck - cq + (NA_COLS - 1), 0, n_dc - 1)
        col_ok = (ck - jnp.clip(cq - NA_COLS // 2, 0, GRID_W - NA_COLS))
        col_ok = (col_ok >= 0) & (col_ok < NA_COLS)
        for hh in range(NA_HEADS_PER_STEP):
            for i in range(n_dr + 1):
                val = jnp.full(shape, NEG_INF, F32)
                for e in range(2):
                    a = i - 1 + e
                    if 0 <= a < n_dr:
                        sel = hi if e else jnp.logical_not(hi)
                        for b in range(n_dc):
                            val = jnp.where(sel & (bidx == b), rpb_ref[((h0 + hh) * n_dr + a) * n_dc + b], val)
                cb_scr[hh, i] = jnp.where(col_ok, val, NEG_INF)

    r0 = t * NA_QROWS
    kb0 = jnp.clip(r0 - NA_ROWS // 2, 0, rows - NA_KROWS)

    @pl.when((t <= 1) | (t == n_tiles - 1))
    def _():
        lane = lax.broadcasted_iota(jnp.int32, (GRID_W, 2 * half), 1)
        lo = lane < half
        for rq in range(NA_QROWS):
            rq_abs = r0 + rq
            ws = jnp.clip(rq_abs - NA_ROWS // 2, 0, rows - NA_ROWS)
            for jj in range(NA_KROWS // 2):
                rk0 = kb0 + 2 * jj
                ok0 = (rk0 >= ws) & (rk0 < ws + NA_ROWS)
                ok1 = (rk0 + 1 >= ws) & (rk0 + 1 < ws + NA_ROWS)
                idx = jnp.clip(rk0 - rq_abs + NA_ROWS, 0, n_dr)
                keep = (lo & ok0) | (jnp.logical_not(lo) & ok1)
                for hh in range(NA_HEADS_PER_STEP):
                    bias_scr[hh, rq * GRID_W:(rq + 1) * GRID_W, jj * 2 * half:(jj + 1) * 2 * half] = (
                        jnp.where(keep, cb_scr[hh, idx], NEG_INF))

    outs = []
    for hh in range(NA_HEADS_PER_STEP):
        hd = slice(hh * HEAD_DIM, (hh + 1) * HEAD_DIM)
        q = q_ref[:, hd]
        k = jnp.concatenate([k0_ref[:, hd], k1_ref[:, hd], k2_ref[:, hd], k3_ref[:, hd]], axis=0)
        v = jnp.concatenate([v0_ref[:, hd], v1_ref[:, hd], v2_ref[:, hd], v3_ref[:, hd]], axis=0)
        s = _dot_nt(q, k) + bias_scr[hh]
        s_ctx = _dot_nt(q, kc_ref[:, hd])
        m = jnp.maximum(jnp.max(s, axis=-1, keepdims=True), jnp.max(s_ctx, axis=-1, keepdims=True))
        p = jnp.exp(s - m)
        p_ctx = jnp.exp(s_ctx - m)
        den = jnp.sum(p, axis=-1, keepdims=True) + jnp.sum(p_ctx, axis=-1, keepdims=True)
        o = (jnp.dot(p.astype(BF16), v, preferred_element_type=F32)
             + jnp.dot(p_ctx.astype(BF16), vc_ref[:, hd], preferred_element_type=F32))
        outs.append(o * (1.0 / den))
    o_ref[...] = jnp.concatenate(outs, axis=1).astype(o_ref.dtype)


def _neighbourhood_attention(qkv, qkv_ctx, rpb):
    seq = qkv.shape[0]
    m = qkv_ctx.shape[0]
    rows = seq // GRID_W
    assert rows % NA_QROWS == 0 and rows >= NA_KROWS + NA_QROWS
    n_tiles = rows // NA_QROWS
    tq = NA_QROWS * GRID_W
    tk = NA_KBLK * GRID_W
    n_kblk = NA_KROWS // NA_KBLK

    hp = NA_HEADS_PER_STEP
    wh = hp * HEAD_DIM
    assert NA_HEADS % hp == 0 and QA_BLK % hp == 0 and KA_BLK % hp == 0 and VA_BLK % hp == 0

    def kv_spec(col_blk, i):
        def index(h, t):
            first = jnp.clip(t * (NA_QROWS // NA_KBLK) - 1, 0, rows // NA_KBLK - n_kblk)
            return (first + i, col_blk // hp + h)
        return pl.BlockSpec((tk, wh), index)

    return pl.pallas_call(
        functools.partial(_na_kernel, rows=rows),
        grid=(NA_HEADS // hp, n_tiles),
        in_specs=[pl.BlockSpec(memory_space=pltpu.SMEM),
                  pl.BlockSpec((tq, wh), lambda h, t: (t, QA_BLK // hp + h))]
                 + [kv_spec(KA_BLK, i) for i in range(n_kblk)]
                 + [kv_spec(VA_BLK, i) for i in range(n_kblk)]
                 + [pl.BlockSpec((m, wh), lambda h, t: (0, KA_BLK // hp + h)),
                    pl.BlockSpec((m, wh), lambda h, t: (0, VA_BLK // hp + h))],
        out_specs=pl.BlockSpec((tq, wh), lambda h, t: (t, h)),
        out_shape=jax.ShapeDtypeStruct((seq, A_WIDTH), BF16),
        scratch_shapes=[pltpu.VMEM((hp, 2 * NA_ROWS, GRID_W, 2 * GRID_W), F32),
                        pltpu.VMEM((hp, tq, NA_KROWS * GRID_W), F32)],
        compiler_params=_cparams("arbitrary", "arbitrary"),
        name="neighbourhood_attn",
    )(rpb.reshape(-1), qkv, *([qkv] * (2 * n_kblk)), qkv_ctx, qkv_ctx)


def _ctx_attn_kernel(sink_ref, q_ref, k_ref, v_ref, o_ref):
    j = pl.program_id(0)
    s = _dot_nt(q_ref[...], k_ref[...])
    has_sink = j >= NA_HEADS
    sink = jnp.where(has_sink, sink_ref[jnp.maximum(j - NA_HEADS, 0)], NEG_INF)
    m = jnp.maximum(jnp.max(s, axis=-1, keepdims=True), sink)
    p = jnp.exp(s - m)
    den = jnp.sum(p, axis=-1, keepdims=True) + jnp.where(has_sink, jnp.exp(sink - m), 0.0)
    o = jnp.dot(p.astype(BF16), v_ref[...], preferred_element_type=F32)
    o_ref[...] = (o * (1.0 / den)).astype(o_ref.dtype)


def _context_attention(qkv_ctx, sink):
    m = qkv_ctx.shape[0]

    def q_idx(j):
        return (0, jnp.where(j < NA_HEADS, QA_BLK + j, QB_BLK + j - NA_HEADS))

    def k_idx(j):
        return (0, jnp.where(j < NA_HEADS, KA_BLK + j, KB_BLK + (j - NA_HEADS) // NB_GROUP))

    def v_idx(j):
        return (0, jnp.where(j < NA_HEADS, VA_BLK + j, VB_BLK + (j - NA_HEADS) // NB_GROUP))

    return pl.pallas_call(
        _ctx_attn_kernel,
        grid=(NA_HEADS + NB_Q_HEADS,),
        in_specs=[pl.BlockSpec(memory_space=pltpu.SMEM),
                  pl.BlockSpec((m, HEAD_DIM), q_idx),
                  pl.BlockSpec((m, HEAD_DIM), k_idx),
                  pl.BlockSpec((m, HEAD_DIM), v_idx)],
        out_specs=pl.BlockSpec((m, HEAD_DIM), lambda j: (0, j)),
        out_shape=jax.ShapeDtypeStruct((m, A_WIDTH + B_Q_WIDTH), BF16),
        compiler_params=_cparams("arbitrary"),
        name="context_attn",
    )(sink, qkv_ctx, qkv_ctx, qkv_ctx)


def _out_proj_kernel(a_ref, b_ref, wa_ref, wb_ref, x_ref, g_ref, o_ref):
    y = (jnp.dot(a_ref[...], wa_ref[...], preferred_element_type=F32)
         + jnp.dot(b_ref[...], wb_ref[...], preferred_element_type=F32))
    o_ref[...] = x_ref[...] + g_ref[...] * y


def _out_proj(oa, ob, a_blk, b_blk, w, x, gate, *, tm):
    m, d = x.shape
    kh = w.shape[0] // 2
    return pl.pallas_call(
        _out_proj_kernel,
        grid=(m // tm,),
        in_specs=[pl.BlockSpec((tm, kh), lambda i: (i, a_blk)),
                  pl.BlockSpec((tm, kh), lambda i: (i, b_blk)),
                  pl.BlockSpec((kh, d), lambda i: (0, 0)),
                  pl.BlockSpec((kh, d), lambda i: (1, 0)),
                  pl.BlockSpec((tm, d), lambda i: (i, 0)),
                  pl.BlockSpec((1, d), lambda i: (0, 0))],
        out_specs=pl.BlockSpec((tm, d), lambda i: (i, 0)),
        out_shape=jax.ShapeDtypeStruct((m, d), F32),
        compiler_params=_cparams("arbitrary"),
        name="out_proj",
    )(oa, ob, w, w, x, gate)


def _ffn_kernel(x_ref, g_ref, sc_ref, sh_ref, gate_ref, gf_ref, w1_ref, w3_ref, w2_ref, o_ref, h_scr, act_scr,
                *, final_norm, n_up):
    s = pl.program_id(1)
    tf = w1_ref.shape[1]
    tn = w2_ref.shape[1]

    @pl.when(s == 0)
    def _():
        h_scr[...] = _norm_mod(x_ref[...], g_ref[...], sc_ref[...], sh_ref[...]).astype(BF16)

    @pl.when(s < n_up)
    def _():
        h = h_scr[...]
        a = jnp.dot(h, w1_ref[...], preferred_element_type=F32)
        b = jnp.dot(h, w3_ref[...], preferred_element_type=F32)
        act_scr[:, pl.ds(pl.multiple_of(s * tf, tf), tf)] = (_silu(a) * b).astype(BF16)

    @pl.when(s >= n_up)
    def _():
        cols = pl.ds(pl.multiple_of((s - n_up) * tn, tn), tn)
        y = jnp.dot(act_scr[...], w2_ref[...], preferred_element_type=F32)
        o_ref[:, cols] = x_ref[:, cols] + gate_ref[:, cols] * y

    if final_norm:
        @pl.when(s == pl.num_programs(1) - 1)
        def _():
            r = o_ref[...]
            ms = jnp.mean(r * r, axis=-1, keepdims=True)
            o_ref[...] = r * lax.rsqrt(ms + EPS) * gf_ref[...]


def _ffn(x, g, sc, sh, gate, gf, w1, w3, w2, layer, *, final_norm, tm):
    m, d = x.shape
    dff = w1.shape[-1]
    tf, tn = 512, 256
    assert dff % tf == 0 and d % tn == 0
    n_up, n_dn = dff // tf, d // tn
    vec = pl.BlockSpec((1, d), lambda i, s: (0, 0))
    once = pl.Buffered(1) if m > tm else None
    return pl.pallas_call(
        functools.partial(_ffn_kernel, final_norm=final_norm, n_up=n_up),
        grid=(m // tm, n_up + n_dn),
        in_specs=[pl.BlockSpec((tm, d), lambda i, s: (i, 0)), vec, vec, vec, vec, vec,
                  pl.BlockSpec((None, d, tf), lambda i, s: (layer, 0, jnp.minimum(s, n_up - 1))),
                  pl.BlockSpec((None, d, tf), lambda i, s: (layer, 0, jnp.minimum(s, n_up - 1))),
                  pl.BlockSpec((None, dff, tn), lambda i, s: (layer, 0, jnp.maximum(s - n_up, 0)))],
        out_specs=pl.BlockSpec((tm, d), lambda i, s: (i, 0), pipeline_mode=once),
        out_shape=jax.ShapeDtypeStruct((m, d), F32),
        scratch_shapes=[pltpu.VMEM((tm, d), BF16), pltpu.VMEM((tm, dff), BF16)],
        compiler_params=_cparams("arbitrary", "arbitrary", limit=FFN_VMEM_LIMIT),
        name="swiglu_final" if final_norm else "swiglu",
    )(x, g, sc, sh, gate, gf, w1, w3, w2)


def _chunk_perm(n, to_steps):
    r = lax.broadcasted_iota(jnp.int32, (n, n), 0)
    c = lax.broadcasted_iota(jnp.int32, (n, n), 1)
    nch = n // S5_CHUNK
    src = (r % nch) * S5_CHUNK + r // nch if to_steps else (r % S5_CHUNK) * nch + r // S5_CHUNK
    return (c == src).astype(BF16)


def _field_transpose(v):
    v = list(v)
    lane = lax.broadcasted_iota(jnp.int32, v[0].shape, 1)
    s = FIELDS // 2
    while s:
        keep = ((lane // FIELD) & s) == 0
        for i in range(FIELDS):
            if i & s:
                continue
            a, b = v[i], v[i + s]
            v[i] = jnp.where(keep, a, pltpu.roll(b, FIELD * s, axis=1))
            v[i + s] = jnp.where(keep, pltpu.roll(a, LANES - FIELD * s, axis=1), b)
        s //= 2
    return v


def _s5_input_kernel(x_ref, c_ref, g_ref, scx_ref, shx_ref, scc_ref, shc_ref, o_ref):
    i = pl.program_id(0)
    is_ctx = (i == 0) | (i == pl.num_programs(0) - 1)
    n = x_ref.shape[0]

    def emit(h):
        hp = jnp.dot(_chunk_perm(n, True), h.astype(BF16), preferred_element_type=F32)
        o_ref[...] = hp.astype(BF16).reshape(o_ref.shape)

    @pl.when(is_ctx)
    def _():
        emit(_norm_mod(c_ref[...], g_ref[...], scc_ref[...], shc_ref[...]))

    @pl.when(jnp.logical_not(is_ctx))
    def _():
        emit(_norm_mod(x_ref[...], g_ref[...], scx_ref[...], shx_ref[...]))


def _s5_input(x, ctx, g, scx, shx, scc, shc):
    seq, d = x.shape
    m = ctx.shape[0]
    assert m % (S5_CHUNK * 8) == 0 and seq % m == 0
    nx = seq // m
    nc = (seq + 2 * m) // S5_CHUNK
    vec = pl.BlockSpec((1, d), lambda i: (0, 0))
    return pl.pallas_call(
        _s5_input_kernel,
        grid=(nx + 2,),
        in_specs=[pl.BlockSpec((m, d), lambda i: (jnp.clip(i - 1, 0, nx - 1), 0)),
                  pl.BlockSpec((m, d), lambda i: (0, 0)), vec, vec, vec, vec, vec],
        out_specs=pl.BlockSpec((S5_CHUNK, m // S5_CHUNK, d), lambda i: (0, i, 0)),
        out_shape=jax.ShapeDtypeStruct((S5_CHUNK, nc, d), BF16),
        compiler_params=_cparams("arbitrary"),
        name="s5_input",
    )(x, ctx, g, scx, shx, scc, shc)


def _sublane_transpose(v):
    n = v[0].shape[0]
    v = [a.reshape(n // 8, 8, LANES) for a in v]
    row = lax.broadcasted_iota(jnp.int32, v[0].shape, 1)
    s = 4
    while s:
        keep = (row & s) == 0
        for i in range(8):
            if i & s:
                continue
            a, b = v[i], v[i + s]
            v[i] = jnp.where(keep, a, pltpu.roll(b, s, axis=1))
            v[i + s] = jnp.where(keep, pltpu.roll(a, 8 - s, axis=1), b)
        s //= 2
    return [a.reshape(n, LANES) for a in v]


def _s5_kernel(z_ref, wm_ref, wb_ref, wc_ref, lam_ref, o_ref, u_scr, b_scr, hf_scr, hb_scr, *, c_lo):
    L, nc, _ = z_ref.shape
    n_out = o_ref.shape[1]
    gs = FIELDS
    half = SSM_STATE
    for j in range(L // FIELDS):
        v = [pltpu.bitcast(z_ref[j * FIELDS + i], jnp.uint32) for i in range(FIELDS)]
        w = _field_transpose(v)
        for gi in range(gs):
            u_scr[gi, :, j * LANES:(j + 1) * LANES] = pltpu.bitcast(w[gi], BF16)

    bs = [jnp.dot(u_scr[gi], wb_ref[gi], preferred_element_type=F32) for gi in range(gs)]
    for comp in range(2):
        t = _sublane_transpose([b[:, comp * LANES:(comp + 1) * LANES] for b in bs])
        for j in range(8):
            b_scr[comp, j] = t[j]

    a_re = lam_ref[:, :2 * half]
    a_im = lam_ref[:, 2 * half:]
    fwd = lax.broadcasted_iota(jnp.int32, (gs, 2 * half), 1) < half

    def step(t, carry):
        s_re, s_im = carry
        rf = pl.ds(pl.multiple_of(t * 8, 8), 8)
        rb = pl.ds(pl.multiple_of(nc - 8 - t * 8, 8), 8)
        for jf in range(8):
            jb = 7 - jf
            hf_scr[0, jf, rf, :] = s_re
            hf_scr[1, jf, rf, :] = s_im
            hb_scr[0, jb, rb, :] = s_re
            hb_scr[1, jb, rb, :] = s_im
            v_re = jnp.where(fwd, b_scr[0, jf, rf, :], b_scr[0, jb, rb, :])
            v_im = jnp.where(fwd, b_scr[1, jf, rf, :], b_scr[1, jb, rb, :])
            s_re, s_im = a_re * s_re - a_im * s_im + v_re, a_re * s_im + a_im * s_re + v_im
        return s_re, s_im

    zero = jnp.zeros((gs, 2 * half), F32)
    lax.fori_loop(0, nc // 8, step, (zero, zero))

    fwd_all = lax.broadcasted_iota(jnp.int32, (nc, 2 * half), 1) < half
    h_in = [_sublane_transpose([jnp.where(fwd_all, hf_scr[comp, j], hb_scr[comp, j]) for j in range(8)])
            for comp in range(2)]
    ys = []
    for gi in range(gs):
        hcat = jnp.concatenate([h_in[0][gi], h_in[1][gi]], axis=1)[c_lo:c_lo + n_out].astype(BF16)
        ys.append(jnp.dot(u_scr[gi, c_lo:c_lo + n_out, :], wm_ref[gi], preferred_element_type=F32)
                  + _dot_nt(hcat, wc_ref[gi]))
    for j in range(L // FIELDS):
        w = _field_transpose([y[:, j * LANES:(j + 1) * LANES] for y in ys])
        for i in range(FIELDS):
            o_ref[j * FIELDS + i] = w[i]


def _s5_scan(zp, wm, wb, wc, lam, *, c_lo, n_out):
    L, nc, d = zp.shape
    w = wm.shape[-1]
    gs = FIELDS
    assert nc % 8 == 0 and gs == 8
    slab = pltpu.VMEM((2, 8, nc, 2 * SSM_STATE), F32)
    return pl.pallas_call(
        functools.partial(_s5_kernel, c_lo=c_lo),
        grid=(d // LANES,),
        in_specs=[pl.BlockSpec((L, nc, LANES), lambda i: (0, 0, i)),
                  pl.BlockSpec((gs, w, w), lambda i: (i, 0, 0)),
                  pl.BlockSpec((gs, w, w), lambda i: (i, 0, 0)),
                  pl.BlockSpec((gs, w, w), lambda i: (i, 0, 0)),
                  pl.BlockSpec((gs, w), lambda i: (i, 0))],
        out_specs=pl.BlockSpec((L, n_out, LANES), lambda i: (0, 0, i)),
        out_shape=jax.ShapeDtypeStruct((L, n_out, d), F32),
        scratch_shapes=[pltpu.VMEM((gs, nc, w), BF16), slab, slab, slab],
        compiler_params=_cparams("arbitrary"),
        name="s5_scan",
    )(zp, wm, wb, wc, lam)


def _shift_lanes(x, s):
    w = LANES
    x0, x1 = x[:, :w], x[:, w:]
    zero = jnp.zeros_like(x0)
    lane = lax.broadcasted_iota(jnp.int32, x0.shape, 1)
    if s == 0:
        return x
    if s > 0:
        if s >= w:
            t = s - w
            y1 = x0 if t == 0 else jnp.where(lane < t, 0.0, pltpu.roll(x0, t, axis=1))
            return jnp.concatenate([zero, y1], axis=1)
        r0, r1 = pltpu.roll(x0, s, axis=1), pltpu.roll(x1, s, axis=1)
        return jnp.concatenate([jnp.where(lane < s, 0.0, r0), jnp.where(lane < s, r0, r1)], axis=1)
    s = -s
    if s >= w:
        t = s - w
        y0 = x1 if t == 0 else jnp.where(lane >= w - t, 0.0, pltpu.roll(x1, w - t, axis=1))
        return jnp.concatenate([y0, zero], axis=1)
    r0, r1 = pltpu.roll(x0, w - s, axis=1), pltpu.roll(x1, w - s, axis=1)
    return jnp.concatenate([jnp.where(lane >= w - s, r1, r0), jnp.where(lane >= w - s, 0.0, r1)], axis=1)


def _s5_op_kernel(a_re_ref, a_im_ref, ldt_ref, b_re_ref, b_im_ref, c_re_ref, c_im_ref,
                  wm_ref, wb_ref, wc_ref, lam_ref, e_scr):
    gs = a_re_ref.shape[0]
    L = S5_CHUNK
    h = SSM_GROUP
    half = SSM_STATE
    fwd1 = lax.broadcasted_iota(jnp.int32, (1, 2 * half), 1) < half
    fwd = lax.broadcasted_iota(jnp.int32, (h, 2 * half), 1) < half

    def cmul(ar, ai, br, bi):
        return ar * br - ai * bi, ar * bi + ai * br

    for gi in range(gs):
        ar, ai = a_re_ref[gi], a_im_ref[gi]
        dt = jnp.exp(ldt_ref[gi])
        mag = jnp.exp(ar * dt)
        lr, li = mag * jnp.cos(ai * dt), mag * jnp.sin(ai * dt)
        den = ar * ar + ai * ai
        nr = lr - 1.0
        coef_r = (nr * ar + li * ai) / den
        coef_i = (li * ar - nr * ai) / den
        bb_r, bb_i = cmul(coef_r, coef_i, b_re_ref[gi], b_im_ref[gi])
        c_r, c_i = c_re_ref[gi], c_im_ref[gi]
        pw = [(jnp.ones_like(lr), jnp.zeros_like(lr))]
        for _ in range(L):
            pw.append(cmul(pw[-1][0], pw[-1][1], lr, li))

        def mixed(jf, jb):
            return (jnp.where(fwd1, pw[jf][0], pw[jb][0]), jnp.where(fwd1, pw[jf][1], pw[jb][1]))

        for l in range(L):
            s_r, s_i = cmul(*mixed(L - 1 - l, l), bb_r, bb_i)
            wb_ref[gi, l * h:(l + 1) * h, :] = jnp.concatenate([s_r, s_i], axis=1).astype(BF16)
            f_r, f_i = cmul(*mixed(l + 1, L - l), c_r, c_i)
            wc_ref[gi, l * h:(l + 1) * h, :] = jnp.concatenate([f_r, -f_i], axis=1).astype(BF16)
            e_r, e_i = cmul(*mixed(l, L - 1 - l), c_r, c_i)
            e_scr[l * h:(l + 1) * h, :] = jnp.concatenate([e_r, e_i], axis=1)

        lhs = jnp.concatenate(
            [jnp.concatenate([jnp.where(fwd, bb_r, 0.0), jnp.where(fwd, -bb_i, 0.0)], axis=1),
             jnp.concatenate([jnp.where(fwd, 0.0, bb_r), jnp.where(fwd, 0.0, -bb_i)], axis=1)], axis=0)
        kt = _dot_nt(lhs, e_scr[...], precision=lax.Precision.HIGHEST)
        kt_f, kt_b = kt[:h], kt[h:]
        for l in range(L):
            blk = _shift_lanes(kt_f, h * l) + _shift_lanes(kt_b, -h * (L - 1 - l))
            wm_ref[gi, l * h:(l + 1) * h, :] = blk.astype(BF16)
        lam_ref[gi:gi + 1, :] = jnp.concatenate([pw[L][0], pw[L][1]], axis=1)


def _s5_operators(a_re, a_im, log_dt, b_re, b_im, c_re, c_im):
    _, g, p = a_re.shape
    h = b_re.shape[-1]
    gs = FIELDS
    w = S5_CHUNK * h
    assert w == 2 * LANES and 2 * p == LANES

    def lanes(v):
        return jnp.concatenate([v[0], v[1]], axis=-1)

    def chan_rows(v):
        return lanes(jnp.swapaxes(v, -1, -2))

    a_re2 = lanes(a_re).reshape(g, 1, 2 * p)
    a_im2 = lanes(a_im).reshape(g, 1, 2 * p)
    ldt2 = jnp.repeat(log_dt.T, p, axis=1).reshape(g, 1, 2 * p)
    vec = pl.BlockSpec((gs, 1, 2 * p), lambda i: (i, 0, 0))
    mat = pl.BlockSpec((gs, h, 2 * p), lambda i: (i, 0, 0))
    return pl.pallas_call(
        _s5_op_kernel,
        grid=(g // gs,),
        in_specs=[vec, vec, vec, mat, mat, mat, mat],
        out_specs=[pl.BlockSpec((gs, w, w), lambda i: (i, 0, 0)),
                   pl.BlockSpec((gs, w, w), lambda i: (i, 0, 0)),
                   pl.BlockSpec((gs, w, w), lambda i: (i, 0, 0)),
                   pl.BlockSpec((gs, w), lambda i: (i, 0))],
        out_shape=[jax.ShapeDtypeStruct((g, w, w), BF16), jax.ShapeDtypeStruct((g, w, w), BF16),
                   jax.ShapeDtypeStruct((g, w, w), BF16), jax.ShapeDtypeStruct((g, w), F32)],
        scratch_shapes=[pltpu.VMEM((w, w), F32)],
        compiler_params=_cparams("arbitrary"),
        name="s5_operators",
    )(a_re2, a_im2, ldt2, chan_rows(b_re), chan_rows(b_im), lanes(c_re), lanes(c_im))


def _gelu_tanh(y):
    z2 = (2.0 * math.sqrt(2.0 / math.pi)) * (y + 0.044715 * (y * y * y))
    return y * (1.0 / (1.0 + jnp.exp(-z2)))


def _glu_kernel(x_ref, ys_ref, g_ref, sc_ref, sh_ref, d_ref, gate_ref, b_ref, w_ref, o_ref, act_scr):
    j = pl.program_id(1)
    tn = o_ref.shape[1]
    d_model = x_ref.shape[1]

    @pl.when(j == 0)
    def _():
        tm, d = x_ref.shape
        nb = S5_CHUNK * S5_CHUNK
        perm = _chunk_perm(nb, False)
        for blk in range(tm // nb):
            rest = ys_ref[:, blk * S5_CHUNK:(blk + 1) * S5_CHUNK, :].reshape(nb, d)
            ys = jnp.zeros_like(rest)
            for _ in range(S5_REORDER_PIECES):
                piece = rest.astype(BF16)
                rest = rest - piece.astype(F32)
                ys = ys + jnp.dot(perm, piece, preferred_element_type=F32)
            rows = slice(blk * nb, (blk + 1) * nb)
            hx = _norm_mod(x_ref[rows, :], g_ref[...], sc_ref[...], sh_ref[...])
            act_scr[rows, :] = _gelu_tanh(d_ref[...] * hx + ys).astype(BF16)

    act = act_scr[...]
    cols = pl.ds(pl.multiple_of(j * tn, tn), tn)
    gcols = pl.ds(pl.multiple_of(d_model + j * tn, tn), tn)
    val = jnp.dot(act, w_ref[:, cols], preferred_element_type=F32) + b_ref[:, cols]
    gate = jnp.dot(act, w_ref[:, gcols], preferred_element_type=F32) + b_ref[:, gcols]
    o_ref[...] = x_ref[:, cols] + gate_ref[:, cols] * (val * (1.0 / (1.0 + jnp.exp(-gate))))


def _glu(x, ys, g, sc, sh, d_skip, gate, w, b, *, tm):
    m, d = x.shape
    tn = 512
    nj = d // tn
    vec = pl.BlockSpec((1, d), lambda i, j: (0, 0))
    return pl.pallas_call(
        _glu_kernel,
        grid=(m // tm, nj),
        in_specs=[pl.BlockSpec((tm, d), lambda i, j: (i, 0)),
                  pl.BlockSpec((S5_CHUNK, tm // S5_CHUNK, d), lambda i, j: (0, i, 0)), vec, vec, vec, vec, vec,
                  pl.BlockSpec((1, 2 * d), lambda i, j: (0, 0)),
                  pl.BlockSpec((d, 2 * d), lambda i, j: (0, 0), pipeline_mode=pl.Buffered(1))],
        out_specs=pl.BlockSpec((tm, tn), lambda i, j: (i, j)),
        out_shape=jax.ShapeDtypeStruct((m, d), F32),
        scratch_shapes=[pltpu.VMEM((tm, d), BF16)],
        compiler_params=_cparams("arbitrary", "arbitrary"),
        name="s5_glu",
    )(x, ys, g, sc, sh, d_skip, gate, b, w)


def kernel(x, c, ctx, c_ctx, ada_w, ada_b, norm_mix, norm_ffn, ffn_w1, ffn_w3, ffn_w2, attn_w_in, attn_w_out,
           attn_rpb, attn_sink, ssm_a_re, ssm_a_im, ssm_log_dt, ssm_b_re, ssm_b_im, ssm_c_re, ssm_c_im,
           ssm_d, ssm_w_glu, ssm_b_glu, norm_final):
    batch, seq, d = x.shape
    assert batch == 1 and ada_w.shape[0] == 2
    m = ctx.shape[1]
    xs, cs = x[0], ctx[0]
    mods = _ada_mod(c, c_ctx, ada_w, ada_b)

    def mod(layer, who):
        return [mods[layer, who, i * d:(i + 1) * d].reshape(1, d) for i in range(6)]

    row = lambda v: v.reshape(1, d)
    tm = 512
    tm_qkv = 1024
    tm_ffn = 1024

    sh1, sc1, g1, sh2, sc2, g2 = mod(0, 0)
    csh1, csc1, cg1, csh2, csc2, cg2 = mod(0, 1)
    cos, sin = _rope_tables(seq)
    w_in = attn_w_in[0].astype(BF16)
    w_out = attn_w_out[0].astype(BF16)
    w1, w3, w2 = ffn_w1.astype(BF16), ffn_w3.astype(BF16), ffn_w2.astype(BF16)
    nm, nf = row(norm_mix[0]), row(norm_ffn[0])
    qkv = _qkv_proj(xs, nm, sc1, sh1, w_in, cos, sin, rope=True, tm=tm_qkv)
    qkv_c = _qkv_proj(cs, nm, csc1, csh1, w_in, cos[:m], sin[:m], rope=False, tm=m)
    oa = _neighbourhood_attention(qkv, qkv_c, attn_rpb[0])
    ob = _window_attention(qkv, qkv_c, attn_sink[0])
    oc = _context_attention(qkv_c, attn_sink[0])
    xs = _out_proj(oa, ob, 0, 0, w_out, xs, g1, tm=tm)
    cs = _out_proj(oc, oc, 0, 1, w_out, cs, cg1, tm=m)
    xs = _ffn(xs, nf, sc2, sh2, g2, nf, w1, w3, w2, 0, final_norm=False, tm=tm_ffn)
    cs = _ffn(cs, nf, csc2, csh2, cg2, nf, w1, w3, w2, 0, final_norm=False, tm=m)

    sh1, sc1, g1, sh2, sc2, g2 = mod(1, 0)
    csh1, csc1, _, _, _, _ = mod(1, 1)
    nm, nf = row(norm_mix[1]), row(norm_ffn[1])
    z = _s5_input(xs, cs, nm, sc1, sh1, csc1, csh1)
    wm, wb, wc, lam = _s5_operators(ssm_a_re[0], ssm_a_im[0], ssm_log_dt[0], ssm_b_re[0], ssm_b_im[0],
                                    ssm_c_re[0], ssm_c_im[0])
    ys = _s5_scan(z, wm, wb, wc, lam, c_lo=m // S5_CHUNK, n_out=seq // S5_CHUNK)
    xs = _glu(xs, ys, nm, sc1, sh1, row(ssm_d[0]), g1, ssm_w_glu[0].astype(BF16), ssm_b_glu[0].reshape(1, 2 * d),
              tm=tm)
    xs = _ffn(xs, nf, sc2, sh2, g2, row(norm_final), w1, w3, w2, 1, final_norm=True, tm=tm_ffn)
    return xs[None]
```

```python
import functools
import math

import jax
import jax.numpy as jnp
import numpy as np
from jax import lax
from jax.experimental import pallas as pl
from jax.experimental.pallas import tpu as pltpu

F32 = jnp.float32
BF16 = jnp.bfloat16

GRID_W = 64
HEAD_DIM = 128
NA_HEADS = 8
NB_Q_HEADS = 8
NB_KV_HEADS = 2
NB_GROUP = NB_Q_HEADS // NB_KV_HEADS
NA_ROWS = 8
NA_COLS = 16
SW_BLOCK = 128
ROPE_BASE = 10000.0
SSM_GROUP = 16
SSM_STATE = 64
EPS = 1e-6
NEG_INF = -1e30
ATTN_SCALE = HEAD_DIM ** -0.5

A_WIDTH = NA_HEADS * HEAD_DIM
B_Q_WIDTH = NB_Q_HEADS * HEAD_DIM
B_KV_WIDTH = NB_KV_HEADS * HEAD_DIM
QA_BLK = 0
KA_BLK = NA_HEADS
VA_BLK = 2 * NA_HEADS
QB_BLK = 3 * NA_HEADS
KB_BLK = QB_BLK + NB_Q_HEADS
VB_BLK = KB_BLK + NB_KV_HEADS

NA_QROWS = 8
NA_KROWS = 16
NA_KBLK = 4
NA_HEADS_PER_STEP = 2
PROLOGUE_PARTS = 8
S5_CHUNK = 16
LANES = 128
FIELD = SSM_GROUP
FIELDS = LANES // FIELD
VMEM_LIMIT = 56 * 1024 * 1024
FFN_VMEM_LIMIT = 62 * 1024 * 1024


def _cparams(*sem, limit=VMEM_LIMIT):
    return pltpu.CompilerParams(dimension_semantics=sem, vmem_limit_bytes=limit)


def _silu(v):
    return v * (1.0 / (1.0 + jnp.exp(-v)))


def _norm_mod(x, g, sc, sh):
    ms = jnp.mean(x * x, axis=-1, keepdims=True)
    return (x * lax.rsqrt(ms + EPS)) * (g * (1.0 + sc)) + sh


def _dot_nt(a, b, precision=None):
    return lax.dot_general(a, b, (((1,), (1,)), ((), ())), preferred_element_type=F32, precision=precision)


def _ada_kernel(cb_ref, w_ref, b_ref, o_ref, s_scr, *, rows_per_step):
    d, tn = w_ref.shape
    rep = tn // 128

    @pl.when((pl.program_id(0) == 0) & (pl.program_id(1) == 0))
    def _():
        s_scr[...] = _silu(cb_ref[...])

    def body(i, acc):
        r = pl.multiple_of(i * rows_per_step, rows_per_step)
        w = w_ref[pl.ds(r, rows_per_step), :]
        out = []
        for v in range(2):
            s = s_scr[v, pl.ds(r, rows_per_step), :]
            st = jnp.concatenate([s] * rep, axis=1)
            out.append(acc[v] + jnp.sum((w * st).reshape(rows_per_step // 8, 8, tn), axis=0))
        return tuple(out)

    zero = jnp.zeros((8, tn), F32)
    acc = lax.fori_loop(0, d // rows_per_step, body, (zero, zero))
    o_ref[...] = jnp.concatenate([jnp.sum(a, axis=0, keepdims=True) for a in acc], axis=0) + b_ref[...]


def _ada_mod(c, c_ctx, ada_w, ada_b):
    depth, d, n = ada_w.shape
    tn = 1024
    cb = jnp.stack([jnp.broadcast_to(c.reshape(d, 1), (d, 128)),
                    jnp.broadcast_to(c_ctx.reshape(d, 1), (d, 128))])
    return pl.pallas_call(
        functools.partial(_ada_kernel, rows_per_step=64),
        grid=(depth, n // tn),
        in_specs=[pl.BlockSpec((2, d, 128), lambda l, j: (0, 0, 0)),
                  pl.BlockSpec((None, d, tn), lambda l, j: (l, 0, j)),
                  pl.BlockSpec((None, 1, tn), lambda l, j: (l, 0, j))],
        out_specs=pl.BlockSpec((None, 2, tn), lambda l, j: (l, 0, j)),
        out_shape=jax.ShapeDtypeStruct((depth, 2, n), F32),
        scratch_shapes=[pltpu.VMEM((2, d, 128), F32)],
        compiler_params=_cparams("arbitrary", "arbitrary"),
        name="ada_mod",
    )(cb, ada_w, ada_b.reshape(depth, 1, n))


def _rope(a, cos, sin):
    lane = lax.broadcasted_iota(jnp.int32, a.shape, 1)
    partner = jnp.where((lane & 63) < 32, pltpu.roll(a, 96, axis=1), pltpu.roll(a, 32, axis=1))
    return a * cos + partner * sin


def _qkv_kernel(x0_ref, xn_ref, g_ref, sc_ref, sh_ref, w_ref, cos_ref, sin_ref, o_ref, h0_scr, h1_scr, acc_scr,
                *, rope):
    i = pl.program_id(0)
    j = pl.program_id(1)
    tm, tn = o_ref.shape
    heads = tn // HEAD_DIM
    part = tm // PROLOGUE_PARTS

    @pl.when((i == 0) & (j == 0))
    def _():
        h0_scr[...] = _norm_mod(x0_ref[...], g_ref[...], sc_ref[...], sh_ref[...]).astype(BF16)

    def step(h_cur, h_nxt):
        rows = pl.ds(pl.multiple_of(jnp.minimum(j, PROLOGUE_PARTS - 1) * part, part), part)
        h_nxt[rows, :] = _norm_mod(xn_ref[rows, :], g_ref[...], sc_ref[...], sh_ref[...]).astype(BF16)
        acc = jnp.dot(h_cur[...], w_ref[...], preferred_element_type=F32)
        acc_scr[...] = acc
        o_ref[...] = (acc * jnp.where(is_q, ATTN_SCALE, 1.0)).astype(o_ref.dtype)

    col0 = j * heads
    is_qb = (col0 >= QB_BLK) & (col0 < KB_BLK)
    is_kb = col0 == KB_BLK
    is_q = (col0 < KA_BLK) | is_qb

    @pl.when(i % 2 == 0)
    def _():
        step(h0_scr, h1_scr)

    @pl.when(i % 2 == 1)
    def _():
        step(h1_scr, h0_scr)

    if rope:
        def rotated(n_heads, scale):
            parts = []
            for hh in range(heads):
                a = acc_scr[:, hh * HEAD_DIM:(hh + 1) * HEAD_DIM]
                if hh < n_heads:
                    a = _rope(a, cos_ref[...], sin_ref[...]) * scale
                parts.append(a)
            return jnp.concatenate(parts, axis=1)

        @pl.when(is_qb)
        def _():
            o_ref[...] = rotated(heads, ATTN_SCALE).astype(o_ref.dtype)

        @pl.when(is_kb)
        def _():
            o_ref[...] = rotated(NB_KV_HEADS, 1.0).astype(o_ref.dtype)


def _qkv_proj(x, g, sc, sh, w, cos, sin, *, rope, tm):
    m, d = x.shape
    n = w.shape[1]
    tn = 512
    nt = m // tm
    assert KB_BLK % (tn // HEAD_DIM) == 0 and m % tm == 0 and n % tn == 0
    assert n // tn >= PROLOGUE_PARTS and tm % (16 * PROLOGUE_PARTS) == 0
    vec = pl.BlockSpec((1, d), lambda i, j: (0, 0))
    tab = pl.BlockSpec((tm, HEAD_DIM), lambda i, j: (i, 0))
    return pl.pallas_call(
        functools.partial(_qkv_kernel, rope=rope),
        grid=(nt, n // tn),
        in_specs=[pl.BlockSpec((tm, d), lambda i, j: (0, 0), pipeline_mode=pl.Buffered(1)),
                  pl.BlockSpec((tm, d), lambda i, j: (jnp.minimum(i + 1, nt - 1), 0)), vec, vec, vec,
                  pl.BlockSpec((d, tn), lambda i, j: (0, j)), tab, tab],
        out_specs=pl.BlockSpec((tm, tn), lambda i, j: (i, j)),
        out_shape=jax.ShapeDtypeStruct((m, n), BF16),
        scratch_shapes=[pltpu.VMEM((tm, d), BF16), pltpu.VMEM((tm, d), BF16), pltpu.VMEM((tm, tn), F32)],
        compiler_params=_cparams("arbitrary", "arbitrary"),
        name="qkv_rope" if rope else "qkv_ctx",
    )(x, x, g, sc, sh, w, cos, sin)


def _rope_tables(seq):
    quarter = HEAD_DIM // 4
    rows = seq // GRID_W
    f32 = np.float32
    inv_freq = (f32(ROPE_BASE) ** (-np.arange(quarter, dtype=f32) / f32(quarter))).astype(f32)
    ang_r = np.arange(rows, dtype=f32)[:, None] * inv_freq[None, :]
    ang_c = np.arange(GRID_W, dtype=f32)[:, None] * inv_freq[None, :]

    def tokens(tab_r, tab_c, sign):
        r = np.broadcast_to(tab_r[:, None, :], (rows, GRID_W, quarter))
        c = np.broadcast_to(tab_c[None, :, :], (rows, GRID_W, quarter))
        return jnp.asarray(np.concatenate([sign * r, r, sign * c, c], axis=-1).reshape(seq, HEAD_DIM), dtype=F32)

    return tokens(np.cos(ang_r), np.cos(ang_c), f32(1.0)), tokens(np.sin(ang_r), np.sin(ang_c), f32(-1.0))


def _win_kernel(sink_ref, q_ref, k0_ref, k1_ref, k2_ref, v0_ref, v1_ref, v2_ref, kc_ref, vc_ref, o_ref,
                valid_scr, *, seq):
    n = pl.program_id(0)
    blk = SW_BLOCK
    shape = (NB_GROUP * blk, 3 * blk)

    @pl.when((n <= 1) | (n == pl.num_programs(0) - 1))
    def _():
        row = lax.broadcasted_iota(jnp.int32, shape, 0)
        col = lax.broadcasted_iota(jnp.int32, shape, 1)
        rel = col - (row & (blk - 1))
        kpos = (n - 1) * blk + col
        ok = (rel >= 0) & (rel <= 2 * blk) & (kpos >= 0) & (kpos < seq)
        valid_scr[...] = ok.astype(F32)

    valid = valid_scr[...] > 0.5
    grp = lax.broadcasted_iota(jnp.int32, (NB_GROUP * blk, 1), 0) // blk
    outs = []
    for h in range(NB_KV_HEADS):
        hd = slice(h * HEAD_DIM, (h + 1) * HEAD_DIM)
        qs = jnp.concatenate([q_ref[:, (h * NB_GROUP + g) * HEAD_DIM:(h * NB_GROUP + g + 1) * HEAD_DIM]
                              for g in range(NB_GROUP)], axis=0)
        k = jnp.concatenate([k0_ref[:, hd], k1_ref[:, hd], k2_ref[:, hd]], axis=0)
        v = jnp.concatenate([v0_ref[:, hd], v1_ref[:, hd], v2_ref[:, hd]], axis=0)
        s = jnp.where(valid, _dot_nt(qs, k), NEG_INF)
        s_ctx = _dot_nt(qs, kc_ref[:, hd])
        sink = jnp.zeros((NB_GROUP * blk, 1), F32)
        for g in range(NB_GROUP):
            sink = jnp.where(grp == g, sink_ref[h * NB_GROUP + g], sink)
        m = jnp.maximum(jnp.maximum(jnp.max(s, axis=-1, keepdims=True), jnp.max(s_ctx, axis=-1, keepdims=True)),
                        sink)
        p = jnp.exp(s - m)
        p_ctx = jnp.exp(s_ctx - m)
        den = jnp.sum(p, axis=-1, keepdims=True) + jnp.sum(p_ctx, axis=-1, keepdims=True) + jnp.exp(sink - m)
        o = (jnp.dot(p.astype(BF16), v, preferred_element_type=F32)
             + jnp.dot(p_ctx.astype(BF16), vc_ref[:, hd], preferred_element_type=F32))
        o = o * (1.0 / den)
        outs += [o[g * blk:(g + 1) * blk] for g in range(NB_GROUP)]
    o_ref[...] = jnp.concatenate(outs, axis=1).astype(o_ref.dtype)


def _window_attention(qkv, qkv_ctx, sink):
    seq = qkv.shape[0]
    m = qkv_ctx.shape[0]
    nb = seq // SW_BLOCK
    assert QB_BLK % NB_Q_HEADS == 0 and KB_BLK % NB_KV_HEADS == 0 and VB_BLK % NB_KV_HEADS == 0

    def kv_spec(col_blk, shift):
        return pl.BlockSpec((SW_BLOCK, B_KV_WIDTH),
                            lambda n: (jnp.clip(n + shift, 0, nb - 1), col_blk // NB_KV_HEADS))

    return pl.pallas_call(
        functools.partial(_win_kernel, seq=seq),
        grid=(nb,),
        in_specs=[pl.BlockSpec(memory_space=pltpu.SMEM),
                  pl.BlockSpec((SW_BLOCK, B_Q_WIDTH), lambda n: (n, QB_BLK // NB_Q_HEADS)),
                  kv_spec(KB_BLK, -1), kv_spec(KB_BLK, 0), kv_spec(KB_BLK, 1),
                  kv_spec(VB_BLK, -1), kv_spec(VB_BLK, 0), kv_spec(VB_BLK, 1),
                  pl.BlockSpec((m, B_KV_WIDTH), lambda n: (0, KB_BLK // NB_KV_HEADS)),
                  pl.BlockSpec((m, B_KV_WIDTH), lambda n: (0, VB_BLK // NB_KV_HEADS))],
        out_specs=pl.BlockSpec((SW_BLOCK, B_Q_WIDTH), lambda n: (n, 0)),
        out_shape=jax.ShapeDtypeStruct((seq, B_Q_WIDTH), BF16),
        scratch_shapes=[pltpu.VMEM((NB_GROUP * SW_BLOCK, 3 * SW_BLOCK), F32)],
        compiler_params=_cparams("arbitrary"),
        name="window_attn",
    )(sink, qkv, qkv, qkv, qkv, qkv, qkv, qkv, qkv_ctx, qkv_ctx)


def _na_kernel(rpb_ref, q_ref, k0_ref, k1_ref, k2_ref, k3_ref, v0_ref, v1_ref, v2_ref, v3_ref,
               kc_ref, vc_ref, o_ref, cb_scr, bias_scr, *, rows):
    h0 = pl.program_id(0) * NA_HEADS_PER_STEP
    t = pl.program_id(1)
    n_tiles = rows // NA_QROWS
    n_dr = 2 * NA_ROWS - 1
    n_dc = 2 * NA_COLS - 1
    half = GRID_W

    @pl.when(t == 0)
    def _():
        shape = (GRID_W, 2 * half)
        cq = lax.broadcasted_iota(jnp.int32, shape, 0)
        lane = lax.broadcasted_iota(jnp.int32, shape, 1)
        ck = lane & (half - 1)
        hi = lane >= half
        bidx = jnp.clip(ck - cq + (NA_COLS - 1), 0, n_dc - 1)
        col_ok = (ck - jnp.clip(cq - NA_COLS // 2, 0, GRID_W - NA_COLS))
        col_ok = (col_ok >= 0) & (col_ok < NA_COLS)
        for hh in range(NA_HEADS_PER_STEP):
            for i in range(n_dr + 1):
                val = jnp.full(shape, NEG_INF, F32)
                for e in range(2):
                    a = i - 1 + e
                    if 0 <= a < n_dr:
                        sel = hi if e else jnp.logical_not(hi)
                        for b in range(n_dc):
                            val = jnp.where(sel & (bidx == b), rpb_ref[((h0 + hh) * n_dr + a) * n_dc + b], val)
                cb_scr[hh, i] = jnp.where(col_ok, val, NEG_INF)

    r0 = t * NA_QROWS
    kb0 = jnp.clip(r0 - NA_ROWS // 2, 0, rows - NA_KROWS)

    @pl.when((t <= 1) | (t == n_tiles - 1))
    def _():
        lane = lax.broadcasted_iota(jnp.int32, (GRID_W, 2 * half), 1)
        lo = lane < half
        for rq in range(NA_QROWS):
            rq_abs = r0 + rq
            ws = jnp.clip(rq_abs - NA_ROWS // 2, 0, rows - NA_ROWS)
            for jj in range(NA_KROWS // 2):
                rk0 = kb0 + 2 * jj
                ok0 = (rk0 >= ws) & (rk0 < ws + NA_ROWS)
                ok1 = (rk0 + 1 >= ws) & (rk0 + 1 < ws + NA_ROWS)
                idx = jnp.clip(rk0 - rq_abs + NA_ROWS, 0, n_dr)
                keep = (lo & ok0) | (jnp.logical_not(lo) & ok1)
                for hh in range(NA_HEADS_PER_STEP):
                    bias_scr[hh, rq * GRID_W:(rq + 1) * GRID_W, jj * 2 * half:(jj + 1) * 2 * half] = (
                        jnp.where(keep, cb_scr[hh, idx], NEG_INF))

    outs = []
    for hh in range(NA_HEADS_PER_STEP):
        hd = slice(hh * HEAD_DIM, (hh + 1) * HEAD_DIM)
        q = q_ref[:, hd]
        k = jnp.concatenate([k0_ref[:, hd], k1_ref[:, hd], k2_ref[:, hd], k3_ref[:, hd]], axis=0)
        v = jnp.concatenate([v0_ref[:, hd], v1_ref[:, hd], v2_ref[:, hd], v3_ref[:, hd]], axis=0)
        s = _dot_nt(q, k) + bias_scr[hh]
        s_ctx = _dot_nt(q, kc_ref[:, hd])
        m = jnp.maximum(jnp.max(s, axis=-1, keepdims=True), jnp.max(s_ctx, axis=-1, keepdims=True))
        p = jnp.exp(s - m)
        p_ctx = jnp.exp(s_ctx - m)
        den = jnp.sum(p, axis=-1, keepdims=True) + jnp.sum(p_ctx, axis=-1, keepdims=True)
        o = (jnp.dot(p.astype(BF16), v, preferred_element_type=F32)
             + jnp.dot(p_ctx.astype(BF16), vc_ref[:, hd], preferred_element_type=F32))
        outs.append(o * (1.0 / den))
    o_ref[...] = jnp.concatenate(outs, axis=1).astype(o_ref.dtype)


def _neighbourhood_attention(qkv, qkv_ctx, rpb):
    seq = qkv.shape[0]
    m = qkv_ctx.shape[0]
    rows = seq // GRID_W
    assert rows % NA_QROWS == 0 and rows >= NA_KROWS + NA_QROWS
    n_tiles = rows // NA_QROWS
    tq = NA_QROWS * GRID_W
    tk = NA_KBLK * GRID_W
    n_kblk = NA_KROWS // NA_KBLK

    hp = NA_HEADS_PER_STEP
    wh = hp * HEAD_DIM
    assert NA_HEADS % hp == 0 and QA_BLK % hp == 0 and KA_BLK % hp == 0 and VA_BLK % hp == 0

    def kv_spec(col_blk, i):
        def index(h, t):
            first = jnp.clip(t * (NA_QROWS // NA_KBLK) - 1, 0, rows // NA_KBLK - n_kblk)
            return (first + i, col_blk // hp + h)
        return pl.BlockSpec((tk, wh), index)

    return pl.pallas_call(
        functools.partial(_na_kernel, rows=rows),
        grid=(NA_HEADS // hp, n_tiles),
        in_specs=[pl.BlockSpec(memory_space=pltpu.SMEM),
                  pl.BlockSpec((tq, wh), lambda h, t: (t, QA_BLK // hp + h))]
                 + [kv_spec(KA_BLK, i) for i in range(n_kblk)]
                 + [kv_spec(VA_BLK, i) for i in range(n_kblk)]
                 + [pl.BlockSpec((m, wh), lambda h, t: (0, KA_BLK // hp + h)),
                    pl.BlockSpec((m, wh), lambda h, t: (0, VA_BLK // hp + h))],
        out_specs=pl.BlockSpec((tq, wh), lambda h, t: (t, h)),
        out_shape=jax.ShapeDtypeStruct((seq, A_WIDTH), BF16),
        scratch_shapes=[pltpu.VMEM((hp, 2 * NA_ROWS, GRID_W, 2 * GRID_W), F32),
                        pltpu.VMEM((hp, tq, NA_KROWS * GRID_W), F32)],
        compiler_params=_cparams("arbitrary", "arbitrary"),
        name="neighbourhood_attn",
    )(rpb.reshape(-1), qkv, *([qkv] * (2 * n_kblk)), qkv_ctx, qkv_ctx)


def _ctx_attn_kernel(sink_ref, q_ref, k_ref, v_ref, o_ref):
    j = pl.program_id(0)
    s = _dot_nt(q_ref[...], k_ref[...])
    has_sink = j >= NA_HEADS
    sink = jnp.where(has_sink, sink_ref[jnp.maximum(j - NA_HEADS, 0)], NEG_INF)
    m = jnp.maximum(jnp.max(s, axis=-1, keepdims=True), sink)
    p = jnp.exp(s - m)
    den = jnp.sum(p, axis=-1, keepdims=True) + jnp.where(has_sink, jnp.exp(sink - m), 0.0)
    o = jnp.dot(p.astype(BF16), v_ref[...], preferred_element_type=F32)
    o_ref[...] = (o * (1.0 / den)).astype(o_ref.dtype)


def _context_attention(qkv_ctx, sink):
    m = qkv_ctx.shape[0]

    def q_idx(j):
        return (0, jnp.where(j < NA_HEADS, QA_BLK + j, QB_BLK + j - NA_HEADS))

    def k_idx(j):
        return (0, jnp.where(j < NA_HEADS, KA_BLK + j, KB_BLK + (j - NA_HEADS) // NB_GROUP))

    def v_idx(j):
        return (0, jnp.where(j < NA_HEADS, VA_BLK + j, VB_BLK + (j - NA_HEADS) // NB_GROUP))

    return pl.pallas_call(
        _ctx_attn_kernel,
        grid=(NA_HEADS + NB_Q_HEADS,),
        in_specs=[pl.BlockSpec(memory_space=pltpu.SMEM),
                  pl.BlockSpec((m, HEAD_DIM), q_idx),
                  pl.BlockSpec((m, HEAD_DIM), k_idx),
                  pl.BlockSpec((m, HEAD_DIM), v_idx)],
        out_specs=pl.BlockSpec((m, HEAD_DIM), lambda j: (0, j)),
        out_shape=jax.ShapeDtypeStruct((m, A_WIDTH + B_Q_WIDTH), BF16),
        compiler_params=_cparams("arbitrary"),
        name="context_attn",
    )(sink, qkv_ctx, qkv_ctx, qkv_ctx)


def _out_proj_kernel(a_ref, b_ref, wa_ref, wb_ref, x_ref, g_ref, o_ref):
    y = (jnp.dot(a_ref[...], wa_ref[...], preferred_element_type=F32)
         + jnp.dot(b_ref[...], wb_ref[...], preferred_element_type=F32))
    o_ref[...] = x_ref[...] + g_ref[...] * y


def _out_proj(oa, ob, a_blk, b_blk, w, x, gate, *, tm):
    m, d = x.shape
    kh = w.shape[0] // 2
    return pl.pallas_call(
        _out_proj_kernel,
        grid=(m // tm,),
        in_specs=[pl.BlockSpec((tm, kh), lambda i: (i, a_blk)),
                  pl.BlockSpec((tm, kh), lambda i: (i, b_blk)),
                  pl.BlockSpec((kh, d), lambda i: (0, 0)),
                  pl.BlockSpec((kh, d), lambda i: (1, 0)),
                  pl.BlockSpec((tm, d), lambda i: (i, 0)),
                  pl.BlockSpec((1, d), lambda i: (0, 0))],
        out_specs=pl.BlockSpec((tm, d), lambda i: (i, 0)),
        out_shape=jax.ShapeDtypeStruct((m, d), F32),
        compiler_params=_cparams("arbitrary"),
        name="out_proj",
    )(oa, ob, w, w, x, gate)


def _ffn_kernel(x_ref, g_ref, sc_ref, sh_ref, gate_ref, gf_ref, w1_ref, w3_ref, w2_ref, o_ref, h_scr, act_scr,
                *, final_norm, n_up):
    s = pl.program_id(1)
    tf = w1_ref.shape[1]
    tn = w2_ref.shape[1]

    @pl.when(s == 0)
    def _():
        h_scr[...] = _norm_mod(x_ref[...], g_ref[...], sc_ref[...], sh_ref[...]).astype(BF16)

    @pl.when(s < n_up)
    def _():
        h = h_scr[...]
        a = jnp.dot(h, w1_ref[...], preferred_element_type=F32)
        b = jnp.dot(h, w3_ref[...], preferred_element_type=F32)
        act_scr[:, pl.ds(pl.multiple_of(s * tf, tf), tf)] = (_silu(a) * b).astype(BF16)

    @pl.when(s >= n_up)
    def _():
        cols = pl.ds(pl.multiple_of((s - n_up) * tn, tn), tn)
        y = jnp.dot(act_scr[...], w2_ref[...], preferred_element_type=F32)
        o_ref[:, cols] = x_ref[:, cols] + gate_ref[:, cols] * y

    if final_norm:
        @pl.when(s == pl.num_programs(1) - 1)
        def _():
            r = o_ref[...]
            ms = jnp.mean(r * r, axis=-1, keepdims=True)
            o_ref[...] = r * lax.rsqrt(ms + EPS) * gf_ref[...]


def _ffn(x, g, sc, sh, gate, gf, w1, w3, w2, layer, *, final_norm, tm):
    m, d = x.shape
    dff = w1.shape[-1]
    tf, tn = 512, 256
    assert dff % tf == 0 and d % tn == 0
    n_up, n_dn = dff // tf, d // tn
    vec = pl.BlockSpec((1, d), lambda i, s: (0, 0))
    once = pl.Buffered(1) if m > tm else None
    return pl.pallas_call(
        functools.partial(_ffn_kernel, final_norm=final_norm, n_up=n_up),
        grid=(m // tm, n_up + n_dn),
        in_specs=[pl.BlockSpec((tm, d), lambda i, s: (i, 0)), vec, vec, vec, vec, vec,
                  pl.BlockSpec((None, d, tf), lambda i, s: (layer, 0, jnp.minimum(s, n_up - 1))),
                  pl.BlockSpec((None, d, tf), lambda i, s: (layer, 0, jnp.minimum(s, n_up - 1))),
                  pl.BlockSpec((None, dff, tn), lambda i, s: (layer, 0, jnp.maximum(s - n_up, 0)))],
        out_specs=pl.BlockSpec((tm, d), lambda i, s: (i, 0), pipeline_mode=once),
        out_shape=jax.ShapeDtypeStruct((m, d), F32),
        scratch_shapes=[pltpu.VMEM((tm, d), BF16), pltpu.VMEM((tm, dff), BF16)],
        compiler_params=_cparams("arbitrary", "arbitrary", limit=FFN_VMEM_LIMIT),
        name="swiglu_final" if final_norm else "swiglu",
    )(x, g, sc, sh, gate, gf, w1, w3, w2)


def _chunk_perm(n):
    r = lax.broadcasted_iota(jnp.int32, (n, n), 0)
    c = lax.broadcasted_iota(jnp.int32, (n, n), 1)
    nch = n // S5_CHUNK
    return (c == (r % nch) * S5_CHUNK + r // nch).astype(BF16)


def _field_transpose(v):
    v = list(v)
    lane = lax.broadcasted_iota(jnp.int32, v[0].shape, 1)
    s = FIELDS // 2
    while s:
        keep = ((lane // FIELD) & s) == 0
        for i in range(FIELDS):
            if i & s:
                continue
            a, b = v[i], v[i + s]
            v[i] = jnp.where(keep, a, pltpu.roll(b, FIELD * s, axis=1))
            v[i + s] = jnp.where(keep, pltpu.roll(a, LANES - FIELD * s, axis=1), b)
        s //= 2
    return v


def _s5_input_kernel(x_ref, c_ref, g_ref, scx_ref, shx_ref, scc_ref, shc_ref, o_ref):
    i = pl.program_id(0)
    is_ctx = (i == 0) | (i == pl.num_programs(0) - 1)
    n = x_ref.shape[0]

    def emit(h):
        hp = jnp.dot(_chunk_perm(n), h.astype(BF16), preferred_element_type=F32)
        o_ref[...] = hp.astype(BF16).reshape(o_ref.shape)

    @pl.when(is_ctx)
    def _():
        emit(_norm_mod(c_ref[...], g_ref[...], scc_ref[...], shc_ref[...]))

    @pl.when(jnp.logical_not(is_ctx))
    def _():
        emit(_norm_mod(x_ref[...], g_ref[...], scx_ref[...], shx_ref[...]))


def _s5_input(x, ctx, g, scx, shx, scc, shc):
    seq, d = x.shape
    m = ctx.shape[0]
    assert m % (S5_CHUNK * 8) == 0 and seq % m == 0
    nx = seq // m
    nc = (seq + 2 * m) // S5_CHUNK
    vec = pl.BlockSpec((1, d), lambda i: (0, 0))
    return pl.pallas_call(
        _s5_input_kernel,
        grid=(nx + 2,),
        in_specs=[pl.BlockSpec((m, d), lambda i: (jnp.clip(i - 1, 0, nx - 1), 0)),
                  pl.BlockSpec((m, d), lambda i: (0, 0)), vec, vec, vec, vec, vec],
        out_specs=pl.BlockSpec((S5_CHUNK, m // S5_CHUNK, d), lambda i: (0, i, 0)),
        out_shape=jax.ShapeDtypeStruct((S5_CHUNK, nc, d), BF16),
        compiler_params=_cparams("arbitrary"),
        name="s5_input",
    )(x, ctx, g, scx, shx, scc, shc)


def _sublane_transpose(v):
    n = v[0].shape[0]
    v = [a.reshape(n // 8, 8, LANES) for a in v]
    row = lax.broadcasted_iota(jnp.int32, v[0].shape, 1)
    s = 4
    while s:
        keep = (row & s) == 0
        for i in range(8):
            if i & s:
                continue
            a, b = v[i], v[i + s]
            v[i] = jnp.where(keep, a, pltpu.roll(b, s, axis=1))
            v[i + s] = jnp.where(keep, pltpu.roll(a, 8 - s, axis=1), b)
        s //= 2
    return [a.reshape(n, LANES) for a in v]


def _s5_kernel(z_ref, wm_ref, wb_ref, wc_ref, lam_ref, o_ref, u_scr, b_scr, hf_scr, hb_scr, *, c_lo):
    L, nc, _ = z_ref.shape
    n_out = o_ref.shape[0] // L
    gs = FIELDS
    half = SSM_STATE
    for j in range(L // FIELDS):
        v = [pltpu.bitcast(z_ref[j * FIELDS + i], jnp.uint32) for i in range(FIELDS)]
        w = _field_transpose(v)
        for gi in range(gs):
            u_scr[gi, :, j * LANES:(j + 1) * LANES] = pltpu.bitcast(w[gi], BF16)

    bs = [jnp.dot(u_scr[gi], wb_ref[gi], preferred_element_type=F32) for gi in range(gs)]
    for comp in range(2):
        t = _sublane_transpose([b[:, comp * LANES:(comp + 1) * LANES] for b in bs])
        for j in range(8):
            b_scr[comp, j] = t[j]

    a_re = lam_ref[:, :2 * half]
    a_im = lam_ref[:, 2 * half:]
    fwd = lax.broadcasted_iota(jnp.int32, (gs, 2 * half), 1) < half

    def step(t, carry):
        s_re, s_im = carry
        rf = pl.ds(pl.multiple_of(t * 8, 8), 8)
        rb = pl.ds(pl.multiple_of(nc - 8 - t * 8, 8), 8)
        for jf in range(8):
            jb = 7 - jf
            hf_scr[0, jf, rf, :] = s_re
            hf_scr[1, jf, rf, :] = s_im
            hb_scr[0, jb, rb, :] = s_re
            hb_scr[1, jb, rb, :] = s_im
            v_re = jnp.where(fwd, b_scr[0, jf, rf, :], b_scr[0, jb, rb, :])
            v_im = jnp.where(fwd, b_scr[1, jf, rf, :], b_scr[1, jb, rb, :])
            s_re, s_im = a_re * s_re - a_im * s_im + v_re, a_re * s_im + a_im * s_re + v_im
        return s_re, s_im

    zero = jnp.zeros((gs, 2 * half), F32)
    lax.fori_loop(0, nc // 8, step, (zero, zero))

    fwd_all = lax.broadcasted_iota(jnp.int32, (nc, 2 * half), 1) < half
    h_in = [_sublane_transpose([jnp.where(fwd_all, hf_scr[comp, j], hb_scr[comp, j]) for j in range(8)])
            for comp in range(2)]
    ys = []
    for gi in range(gs):
        hcat = jnp.concatenate([h_in[0][gi], h_in[1][gi]], axis=1)[c_lo:c_lo + n_out].astype(BF16)
        ys.append(jnp.dot(u_scr[gi, c_lo:c_lo + n_out, :], wm_ref[gi], preferred_element_type=F32)
                  + _dot_nt(hcat, wc_ref[gi]))
    for j in range(L // FIELDS):
        w = _field_transpose([y[:, j * LANES:(j + 1) * LANES] for y in ys])
        for i in range(FIELDS):
            o_ref[pl.ds(j * FIELDS + i, n_out, stride=L), :] = w[i]


def _s5_scan(zp, wm, wb, wc, lam, *, c_lo, n_out):
    L, nc, d = zp.shape
    w = wm.shape[-1]
    gs = FIELDS
    assert nc % 8 == 0 and gs == 8
    slab = pltpu.VMEM((2, 8, nc, 2 * SSM_STATE), F32)
    return pl.pallas_call(
        functools.partial(_s5_kernel, c_lo=c_lo),
        grid=(d // LANES,),
        in_specs=[pl.BlockSpec((L, nc, LANES), lambda i: (0, 0, i)),
                  pl.BlockSpec((gs, w, w), lambda i: (i, 0, 0)),
                  pl.BlockSpec((gs, w, w), lambda i: (i, 0, 0)),
                  pl.BlockSpec((gs, w, w), lambda i: (i, 0, 0)),
                  pl.BlockSpec((gs, w), lambda i: (i, 0))],
        out_specs=pl.BlockSpec((L * n_out, LANES), lambda i: (0, i)),
        out_shape=jax.ShapeDtypeStruct((L * n_out, d), F32),
        scratch_shapes=[pltpu.VMEM((gs, nc, w), BF16), slab, slab, slab],
        compiler_params=_cparams("arbitrary"),
        name="s5_scan",
    )(zp, wm, wb, wc, lam)


def _shift_lanes(x, s):
    w = LANES
    x0, x1 = x[:, :w], x[:, w:]
    zero = jnp.zeros_like(x0)
    lane = lax.broadcasted_iota(jnp.int32, x0.shape, 1)
    if s == 0:
        return x
    if s > 0:
        if s >= w:
            t = s - w
            y1 = x0 if t == 0 else jnp.where(lane < t, 0.0, pltpu.roll(x0, t, axis=1))
            return jnp.concatenate([zero, y1], axis=1)
        r0, r1 = pltpu.roll(x0, s, axis=1), pltpu.roll(x1, s, axis=1)
        return jnp.concatenate([jnp.where(lane < s, 0.0, r0), jnp.where(lane < s, r0, r1)], axis=1)
    s = -s
    if s >= w:
        t = s - w
        y0 = x1 if t == 0 else jnp.where(lane >= w - t, 0.0, pltpu.roll(x1, w - t, axis=1))
        return jnp.concatenate([y0, zero], axis=1)
    r0, r1 = pltpu.roll(x0, w - s, axis=1), pltpu.roll(x1, w - s, axis=1)
    return jnp.concatenate([jnp.where(lane >= w - s, r1, r0), jnp.where(lane >= w - s, 0.0, r1)], axis=1)


def _s5_op_kernel(a_re_ref, a_im_ref, ldt_ref, b_re_ref, b_im_ref, c_re_ref, c_im_ref,
                  wm_ref, wb_ref, wc_ref, lam_ref, e_scr):
    gs = a_re_ref.shape[0]
    L = S5_CHUNK
    h = SSM_GROUP
    half = SSM_STATE
    fwd1 = lax.broadcasted_iota(jnp.int32, (1, 2 * half), 1) < half
    fwd = lax.broadcasted_iota(jnp.int32, (h, 2 * half), 1) < half

    def cmul(ar, ai, br, bi):
        return ar * br - ai * bi, ar * bi + ai * br

    for gi in range(gs):
        ar, ai = a_re_ref[gi], a_im_ref[gi]
        dt = jnp.exp(ldt_ref[gi])
        mag = jnp.exp(ar * dt)
        lr, li = mag * jnp.cos(ai * dt), mag * jnp.sin(ai * dt)
        den = ar * ar + ai * ai
        nr = lr - 1.0
        coef_r = (nr * ar + li * ai) / den
        coef_i = (li * ar - nr * ai) / den
        bb_r, bb_i = cmul(coef_r, coef_i, b_re_ref[gi], b_im_ref[gi])
        c_r, c_i = c_re_ref[gi], c_im_ref[gi]
        pw = [(jnp.ones_like(lr), jnp.zeros_like(lr))]
        for _ in range(L):
            pw.append(cmul(pw[-1][0], pw[-1][1], lr, li))

        def mixed(jf, jb):
            return (jnp.where(fwd1, pw[jf][0], pw[jb][0]), jnp.where(fwd1, pw[jf][1], pw[jb][1]))

        for l in range(L):
            s_r, s_i = cmul(*mixed(L - 1 - l, l), bb_r, bb_i)
            wb_ref[gi, l * h:(l + 1) * h, :] = jnp.concatenate([s_r, s_i], axis=1).astype(BF16)
            f_r, f_i = cmul(*mixed(l + 1, L - l), c_r, c_i)
            wc_ref[gi, l * h:(l + 1) * h, :] = jnp.concatenate([f_r, -f_i], axis=1).astype(BF16)
            e_r, e_i = cmul(*mixed(l, L - 1 - l), c_r, c_i)
            e_scr[l * h:(l + 1) * h, :] = jnp.concatenate([e_r, e_i], axis=1)

        lhs = jnp.concatenate(
            [jnp.concatenate([jnp.where(fwd, bb_r, 0.0), jnp.where(fwd, -bb_i, 0.0)], axis=1),
             jnp.concatenate([jnp.where(fwd, 0.0, bb_r), jnp.where(fwd, 0.0, -bb_i)], axis=1)], axis=0)
        kt = _dot_nt(lhs, e_scr[...], precision=lax.Precision.HIGHEST)
        kt_f, kt_b = kt[:h], kt[h:]
        for l in range(L):
            blk = _shift_lanes(kt_f, h * l) + _shift_lanes(kt_b, -h * (L - 1 - l))
            wm_ref[gi, l * h:(l + 1) * h, :] = blk.astype(BF16)
        lam_ref[gi:gi + 1, :] = jnp.concatenate([pw[L][0], pw[L][1]], axis=1)


def _s5_operators(a_re, a_im, log_dt, b_re, b_im, c_re, c_im):
    _, g, p = a_re.shape
    h = b_re.shape[-1]
    gs = FIELDS
    w = S5_CHUNK * h
    assert w == 2 * LANES and 2 * p == LANES

    def lanes(v):
        return jnp.concatenate([v[0], v[1]], axis=-1)

    def chan_rows(v):
        return lanes(jnp.swapaxes(v, -1, -2))

    a_re2 = lanes(a_re).reshape(g, 1, 2 * p)
    a_im2 = lanes(a_im).reshape(g, 1, 2 * p)
    ldt2 = jnp.repeat(log_dt.T, p, axis=1).reshape(g, 1, 2 * p)
    vec = pl.BlockSpec((gs, 1, 2 * p), lambda i: (i, 0, 0))
    mat = pl.BlockSpec((gs, h, 2 * p), lambda i: (i, 0, 0))
    return pl.pallas_call(
        _s5_op_kernel,
        grid=(g // gs,),
        in_specs=[vec, vec, vec, mat, mat, mat, mat],
        out_specs=[pl.BlockSpec((gs, w, w), lambda i: (i, 0, 0)),
                   pl.BlockSpec((gs, w, w), lambda i: (i, 0, 0)),
                   pl.BlockSpec((gs, w, w), lambda i: (i, 0, 0)),
                   pl.BlockSpec((gs, w), lambda i: (i, 0))],
        out_shape=[jax.ShapeDtypeStruct((g, w, w), BF16), jax.ShapeDtypeStruct((g, w, w), BF16),
                   jax.ShapeDtypeStruct((g, w, w), BF16), jax.ShapeDtypeStruct((g, w), F32)],
        scratch_shapes=[pltpu.VMEM((w, w), F32)],
        compiler_params=_cparams("arbitrary"),
        name="s5_operators",
    )(a_re2, a_im2, ldt2, chan_rows(b_re), chan_rows(b_im), lanes(c_re), lanes(c_im))


def _gelu_tanh(y):
    k = 2.0 * math.sqrt(2.0 / math.pi) * math.log2(math.e)
    return y * (1.0 / (1.0 + jnp.exp2(y * (-k - (k * 0.044715) * (y * y)))))


def _glu_kernel(x_ref, ys_ref, g_ref, sc_ref, sh_ref, d_ref, gate_ref, b_ref, w_ref, o_ref, act_scr):
    j = pl.program_id(1)
    tn = o_ref.shape[1]
    d_model = x_ref.shape[1]

    @pl.when(j == 0)
    def _():
        rb = 32

        def rows_pass(r, carry):
            rows = pl.ds(pl.multiple_of(r * rb, rb), rb)
            hx = _norm_mod(x_ref[rows, :], g_ref[...], sc_ref[...], sh_ref[...])
            act_scr[rows, :] = _gelu_tanh(d_ref[...] * hx + ys_ref[rows, :]).astype(BF16)
            return carry

        lax.fori_loop(0, x_ref.shape[0] // rb, rows_pass, 0, unroll=4)

    act = act_scr[...]
    cols = pl.ds(pl.multiple_of(j * tn, tn), tn)
    gcols = pl.ds(pl.multiple_of(d_model + j * tn, tn), tn)
    val = jnp.dot(act, w_ref[:, cols], preferred_element_type=F32) + b_ref[:, cols]
    gate = jnp.dot(act, w_ref[:, gcols], preferred_element_type=F32) + b_ref[:, gcols]
    o_ref[...] = x_ref[:, cols] + gate_ref[:, cols] * (val * (1.0 / (1.0 + jnp.exp(-gate))))


def _glu(x, ys, g, sc, sh, d_skip, gate, w, b, *, tm):
    m, d = x.shape
    tn = 512
    nj = d // tn
    vec = pl.BlockSpec((1, d), lambda i, j: (0, 0))
    return pl.pallas_call(
        _glu_kernel,
        grid=(m // tm, nj),
        in_specs=[pl.BlockSpec((tm, d), lambda i, j: (i, 0)),
                  pl.BlockSpec((tm, d), lambda i, j: (i, 0)), vec, vec, vec, vec, vec,
                  pl.BlockSpec((1, 2 * d), lambda i, j: (0, 0)),
                  pl.BlockSpec((d, 2 * d), lambda i, j: (0, 0), pipeline_mode=pl.Buffered(1))],
        out_specs=pl.BlockSpec((tm, tn), lambda i, j: (i, j)),
        out_shape=jax.ShapeDtypeStruct((m, d), F32),
        scratch_shapes=[pltpu.VMEM((tm, d), BF16)],
        compiler_params=_cparams("arbitrary", "arbitrary"),
        name="s5_glu",
    )(x, ys, g, sc, sh, d_skip, gate, b, w)


def kernel(x, c, ctx, c_ctx, ada_w, ada_b, norm_mix, norm_ffn, ffn_w1, ffn_w3, ffn_w2, attn_w_in, attn_w_out,
           attn_rpb, attn_sink, ssm_a_re, ssm_a_im, ssm_log_dt, ssm_b_re, ssm_b_im, ssm_c_re, ssm_c_im,
           ssm_d, ssm_w_glu, ssm_b_glu, norm_final):
    batch, seq, d = x.shape
    assert batch == 1 and ada_w.shape[0] == 2
    m = ctx.shape[1]
    xs, cs = x[0], ctx[0]
    mods = _ada_mod(c, c_ctx, ada_w, ada_b)

    def mod(layer, who):
        return [mods[layer, who, i * d:(i + 1) * d].reshape(1, d) for i in range(6)]

    row = lambda v: v.reshape(1, d)
    tm = 512
    tm_qkv = 1024
    tm_ffn = 1024

    sh1, sc1, g1, sh2, sc2, g2 = mod(0, 0)
    csh1, csc1, cg1, csh2, csc2, cg2 = mod(0, 1)
    cos, sin = _rope_tables(seq)
    w_in = attn_w_in[0].astype(BF16)
    w_out = attn_w_out[0].astype(BF16)
    w1, w3, w2 = ffn_w1.astype(BF16), ffn_w3.astype(BF16), ffn_w2.astype(BF16)
    nm, nf = row(norm_mix[0]), row(norm_ffn[0])
    qkv = _qkv_proj(xs, nm, sc1, sh1, w_in, cos, sin, rope=True, tm=tm_qkv)
    qkv_c = _qkv_proj(cs, nm, csc1, csh1, w_in, cos[:m], sin[:m], rope=False, tm=m)
    oa = _neighbourhood_attention(qkv, qkv_c, attn_rpb[0])
    ob = _window_attention(qkv, qkv_c, attn_sink[0])
    oc = _context_attention(qkv_c, attn_sink[0])
    xs = _out_proj(oa, ob, 0, 0, w_out, xs, g1, tm=tm)
    cs = _out_proj(oc, oc, 0, 1, w_out, cs, cg1, tm=m)
    xs = _ffn(xs, nf, sc2, sh2, g2, nf, w1, w3, w2, 0, final_norm=False, tm=tm_ffn)
    cs = _ffn(cs, nf, csc2, csh2, cg2, nf, w1, w3, w2, 0, final_norm=False, tm=m)

    sh1, sc1, g1, sh2, sc2, g2 = mod(1, 0)
    csh1, csc1, _, _, _, _ = mod(1, 1)
    nm, nf = row(norm_mix[1]), row(norm_ffn[1])
    z = _s5_input(xs, cs, nm, sc1, sh1, csc1, csh1)
    wm, wb, wc, lam = _s5_operators(ssm_a_re[0], ssm_a_im[0], ssm_log_dt[0], ssm_b_re[0], ssm_b_im[0],
                                    ssm_c_re[0], ssm_c_im[0])
    ys = _s5_scan(z, wm, wb, wc, lam, c_lo=m // S5_CHUNK, n_out=seq // S5_CHUNK)
    xs = _glu(xs, ys, nm, sc1, sh1, row(ssm_d[0]), g1, ssm_w_glu[0].astype(BF16), ssm_b_glu[0].reshape(1, 2 * d),
              tm=tm)
    xs = _ffn(xs, nf, sc2, sh2, g2, row(norm_final), w1, w3, w2, 1, final_norm=True, tm=tm_ffn)
    return xs[None]
```

```python
import functools
import math

import jax
import jax.numpy as jnp
import numpy as np
from jax import lax
from jax.experimental import pallas as pl
from jax.experimental.pallas import tpu as pltpu

F32 = jnp.float32
BF16 = jnp.bfloat16

GRID_W = 64
HEAD_DIM = 128
NA_HEADS = 8
NB_Q_HEADS = 8
NB_KV_HEADS = 2
NB_GROUP = NB_Q_HEADS // NB_KV_HEADS
NA_ROWS = 8
NA_COLS = 16
SW_BLOCK = 128
ROPE_BASE = 10000.0
SSM_GROUP = 16
SSM_STATE = 64
EPS = 1e-6
NEG_INF = -1e30
ATTN_SCALE = HEAD_DIM ** -0.5

A_WIDTH = NA_HEADS * HEAD_DIM
B_Q_WIDTH = NB_Q_HEADS * HEAD_DIM
B_KV_WIDTH = NB_KV_HEADS * HEAD_DIM
QA_BLK = 0
KA_BLK = NA_HEADS
VA_BLK = 2 * NA_HEADS
QB_BLK = 3 * NA_HEADS
KB_BLK = QB_BLK + NB_Q_HEADS
VB_BLK = KB_BLK + NB_KV_HEADS

NA_QROWS = 8
NA_KROWS = 16
NA_KBLK = 4
NA_HEADS_PER_STEP = 8
PROLOGUE_PARTS = 8
S5_CHUNK = 16
LANES = 128
FIELD = SSM_GROUP
FIELDS = LANES // FIELD
VMEM_LIMIT = 56 * 1024 * 1024
FFN_VMEM_LIMIT = 62 * 1024 * 1024


def _cparams(*sem, limit=VMEM_LIMIT):
    return pltpu.CompilerParams(dimension_semantics=sem, vmem_limit_bytes=limit)


def _silu(v):
    return v * (1.0 / (1.0 + jnp.exp(-v)))


def _norm_mod(x, g, sc, sh):
    ms = jnp.mean(x * x, axis=-1, keepdims=True)
    return (x * lax.rsqrt(ms + EPS)) * (g * (1.0 + sc)) + sh


def _dot_nt(a, b, precision=None):
    return lax.dot_general(a, b, (((1,), (1,)), ((), ())), preferred_element_type=F32, precision=precision)


def _ada_kernel(cb_ref, w_ref, b_ref, o_ref, s_scr, *, rows_per_step):
    d, tn = w_ref.shape
    rep = tn // 128

    @pl.when((pl.program_id(0) == 0) & (pl.program_id(1) == 0))
    def _():
        s_scr[...] = _silu(cb_ref[...])

    def body(i, acc):
        r = pl.multiple_of(i * rows_per_step, rows_per_step)
        w = w_ref[pl.ds(r, rows_per_step), :]
        out = []
        for v in range(2):
            s = s_scr[v, pl.ds(r, rows_per_step), :]
            st = jnp.concatenate([s] * rep, axis=1)
            out.append(acc[v] + jnp.sum((w * st).reshape(rows_per_step // 8, 8, tn), axis=0))
        return tuple(out)

    zero = jnp.zeros((8, tn), F32)
    acc = lax.fori_loop(0, d // rows_per_step, body, (zero, zero))
    o_ref[...] = jnp.concatenate([jnp.sum(a, axis=0, keepdims=True) for a in acc], axis=0) + b_ref[...]


def _ada_mod(c, c_ctx, ada_w, ada_b):
    depth, d, n = ada_w.shape
    tn = 1024
    cb = jnp.stack([jnp.broadcast_to(c.reshape(d, 1), (d, 128)),
                    jnp.broadcast_to(c_ctx.reshape(d, 1), (d, 128))])
    return pl.pallas_call(
        functools.partial(_ada_kernel, rows_per_step=64),
        grid=(depth, n // tn),
        in_specs=[pl.BlockSpec((2, d, 128), lambda l, j: (0, 0, 0)),
                  pl.BlockSpec((None, d, tn), lambda l, j: (l, 0, j)),
                  pl.BlockSpec((None, 1, tn), lambda l, j: (l, 0, j))],
        out_specs=pl.BlockSpec((None, 2, tn), lambda l, j: (l, 0, j)),
        out_shape=jax.ShapeDtypeStruct((depth, 2, n), F32),
        scratch_shapes=[pltpu.VMEM((2, d, 128), F32)],
        compiler_params=_cparams("arbitrary", "arbitrary"),
        name="ada_mod",
    )(cb, ada_w, ada_b.reshape(depth, 1, n))


def _rope(a, cos, sin):
    lane = lax.broadcasted_iota(jnp.int32, a.shape, 1)
    partner = jnp.where((lane & 63) < 32, pltpu.roll(a, 96, axis=1), pltpu.roll(a, 32, axis=1))
    return a * cos + partner * sin


def _qkv_kernel(x0_ref, xn_ref, g_ref, sc_ref, sh_ref, w_ref, cos_ref, sin_ref, o_ref, h0_scr, h1_scr, acc_scr,
                *, rope):
    i = pl.program_id(0)
    j = pl.program_id(1)
    tm, tn = o_ref.shape
    heads = tn // HEAD_DIM
    part = tm // PROLOGUE_PARTS

    @pl.when((i == 0) & (j == 0))
    def _():
        h0_scr[...] = _norm_mod(x0_ref[...], g_ref[...], sc_ref[...], sh_ref[...]).astype(BF16)

    def step(h_cur, h_nxt):
        rows = pl.ds(pl.multiple_of(jnp.minimum(j, PROLOGUE_PARTS - 1) * part, part), part)
        h_nxt[rows, :] = _norm_mod(xn_ref[rows, :], g_ref[...], sc_ref[...], sh_ref[...]).astype(BF16)
        acc = jnp.dot(h_cur[...], w_ref[...], preferred_element_type=F32)
        acc_scr[...] = acc
        o_ref[...] = (acc * jnp.where(is_q, ATTN_SCALE, 1.0)).astype(o_ref.dtype)

    col0 = j * heads
    is_qb = (col0 >= QB_BLK) & (col0 < KB_BLK)
    is_kb = col0 == KB_BLK
    is_q = (col0 < KA_BLK) | is_qb

    @pl.when(i % 2 == 0)
    def _():
        step(h0_scr, h1_scr)

    @pl.when(i % 2 == 1)
    def _():
        step(h1_scr, h0_scr)

    if rope:
        def rotated(n_heads, scale):
            parts = []
            for hh in range(heads):
                a = acc_scr[:, hh * HEAD_DIM:(hh + 1) * HEAD_DIM]
                if hh < n_heads:
                    a = _rope(a, cos_ref[...], sin_ref[...]) * scale
                parts.append(a)
            return jnp.concatenate(parts, axis=1)

        @pl.when(is_qb)
        def _():
            o_ref[...] = rotated(heads, ATTN_SCALE).astype(o_ref.dtype)

        @pl.when(is_kb)
        def _():
            o_ref[...] = rotated(NB_KV_HEADS, 1.0).astype(o_ref.dtype)


def _qkv_proj(x, g, sc, sh, w, cos, sin, *, rope, tm):
    m, d = x.shape
    n = w.shape[1]
    tn = 512
    nt = m // tm
    assert KB_BLK % (tn // HEAD_DIM) == 0 and m % tm == 0 and n % tn == 0
    assert n // tn >= PROLOGUE_PARTS and tm % (16 * PROLOGUE_PARTS) == 0
    vec = pl.BlockSpec((1, d), lambda i, j: (0, 0))
    tab = pl.BlockSpec((tm, HEAD_DIM), lambda i, j: (i, 0))
    return pl.pallas_call(
        functools.partial(_qkv_kernel, rope=rope),
        grid=(nt, n // tn),
        in_specs=[pl.BlockSpec((tm, d), lambda i, j: (0, 0), pipeline_mode=pl.Buffered(1)),
                  pl.BlockSpec((tm, d), lambda i, j: (jnp.minimum(i + 1, nt - 1), 0)), vec, vec, vec,
                  pl.BlockSpec((d, tn), lambda i, j: (0, j)), tab, tab],
        out_specs=pl.BlockSpec((tm, tn), lambda i, j: (i, j)),
        out_shape=jax.ShapeDtypeStruct((m, n), BF16),
        scratch_shapes=[pltpu.VMEM((tm, d), BF16), pltpu.VMEM((tm, d), BF16), pltpu.VMEM((tm, tn), F32)],
        compiler_params=_cparams("arbitrary", "arbitrary"),
        name="qkv_rope" if rope else "qkv_ctx",
    )(x, x, g, sc, sh, w, cos, sin)


def _rope_tables(seq):
    quarter = HEAD_DIM // 4
    rows = seq // GRID_W
    f32 = np.float32
    inv_freq = (f32(ROPE_BASE) ** (-np.arange(quarter, dtype=f32) / f32(quarter))).astype(f32)
    ang_r = np.arange(rows, dtype=f32)[:, None] * inv_freq[None, :]
    ang_c = np.arange(GRID_W, dtype=f32)[:, None] * inv_freq[None, :]

    def tokens(tab_r, tab_c, sign):
        r = np.broadcast_to(tab_r[:, None, :], (rows, GRID_W, quarter))
        c = np.broadcast_to(tab_c[None, :, :], (rows, GRID_W, quarter))
        return jnp.asarray(np.concatenate([sign * r, r, sign * c, c], axis=-1).reshape(seq, HEAD_DIM), dtype=F32)

    return tokens(np.cos(ang_r), np.cos(ang_c), f32(1.0)), tokens(np.sin(ang_r), np.sin(ang_c), f32(-1.0))


def _win_kernel(sink_ref, q_ref, k0_ref, k1_ref, k2_ref, v0_ref, v1_ref, v2_ref, kc_ref, vc_ref, o_ref,
                valid_scr, *, seq):
    n = pl.program_id(0)
    blk = SW_BLOCK
    shape = (NB_GROUP * blk, 3 * blk)

    @pl.when((n <= 1) | (n == pl.num_programs(0) - 1))
    def _():
        row = lax.broadcasted_iota(jnp.int32, shape, 0)
        col = lax.broadcasted_iota(jnp.int32, shape, 1)
        rel = col - (row & (blk - 1))
        kpos = (n - 1) * blk + col
        ok = (rel >= 0) & (rel <= 2 * blk) & (kpos >= 0) & (kpos < seq)
        valid_scr[...] = ok.astype(F32)

    valid = valid_scr[...] > 0.5
    grp = lax.broadcasted_iota(jnp.int32, (NB_GROUP * blk, 1), 0) // blk
    outs = []
    for h in range(NB_KV_HEADS):
        hd = slice(h * HEAD_DIM, (h + 1) * HEAD_DIM)
        qs = jnp.concatenate([q_ref[:, (h * NB_GROUP + g) * HEAD_DIM:(h * NB_GROUP + g + 1) * HEAD_DIM]
                              for g in range(NB_GROUP)], axis=0)
        k = jnp.concatenate([k0_ref[:, hd], k1_ref[:, hd], k2_ref[:, hd]], axis=0)
        v = jnp.concatenate([v0_ref[:, hd], v1_ref[:, hd], v2_ref[:, hd]], axis=0)
        s = jnp.where(valid, _dot_nt(qs, k), NEG_INF)
        s_ctx = _dot_nt(qs, kc_ref[:, hd])
        sink = jnp.zeros((NB_GROUP * blk, 1), F32)
        for g in range(NB_GROUP):
            sink = jnp.where(grp == g, sink_ref[h * NB_GROUP + g], sink)
        m = jnp.maximum(jnp.maximum(jnp.max(s, axis=-1, keepdims=True), jnp.max(s_ctx, axis=-1, keepdims=True)),
                        sink)
        p = jnp.exp(s - m)
        p_ctx = jnp.exp(s_ctx - m)
        den = jnp.sum(p, axis=-1, keepdims=True) + jnp.sum(p_ctx, axis=-1, keepdims=True) + jnp.exp(sink - m)
        o = (jnp.dot(p.astype(BF16), v, preferred_element_type=F32)
             + jnp.dot(p_ctx.astype(BF16), vc_ref[:, hd], preferred_element_type=F32))
        o = o * (1.0 / den)
        outs += [o[g * blk:(g + 1) * blk] for g in range(NB_GROUP)]
    o_ref[...] = jnp.concatenate(outs, axis=1).astype(o_ref.dtype)


def _window_attention(qkv, qkv_ctx, sink):
    seq = qkv.shape[0]
    m = qkv_ctx.shape[0]
    nb = seq // SW_BLOCK
    assert QB_BLK % NB_Q_HEADS == 0 and KB_BLK % NB_KV_HEADS == 0 and VB_BLK % NB_KV_HEADS == 0

    def kv_spec(col_blk, shift):
        return pl.BlockSpec((SW_BLOCK, B_KV_WIDTH),
                            lambda n: (jnp.clip(n + shift, 0, nb - 1), col_blk // NB_KV_HEADS))

    return pl.pallas_call(
        functools.partial(_win_kernel, seq=seq),
        grid=(nb,),
        in_specs=[pl.BlockSpec(memory_space=pltpu.SMEM),
                  pl.BlockSpec((SW_BLOCK, B_Q_WIDTH), lambda n: (n, QB_BLK // NB_Q_HEADS)),
                  kv_spec(KB_BLK, -1), kv_spec(KB_BLK, 0), kv_spec(KB_BLK, 1),
                  kv_spec(VB_BLK, -1), kv_spec(VB_BLK, 0), kv_spec(VB_BLK, 1),
                  pl.BlockSpec((m, B_KV_WIDTH), lambda n: (0, KB_BLK // NB_KV_HEADS)),
                  pl.BlockSpec((m, B_KV_WIDTH), lambda n: (0, VB_BLK // NB_KV_HEADS))],
        out_specs=pl.BlockSpec((SW_BLOCK, B_Q_WIDTH), lambda n: (n, 0)),
        out_shape=jax.ShapeDtypeStruct((seq, B_Q_WIDTH), BF16),
        scratch_shapes=[pltpu.VMEM((NB_GROUP * SW_BLOCK, 3 * SW_BLOCK), F32)],
        compiler_params=_cparams("arbitrary"),
        name="window_attn",
    )(sink, qkv, qkv, qkv, qkv, qkv, qkv, qkv, qkv_ctx, qkv_ctx)


def _na_kernel(rpb_ref, q_ref, k0_ref, k1_ref, k2_ref, k3_ref, v0_ref, v1_ref, v2_ref, v3_ref,
               kc_ref, vc_ref, o_ref, cb_scr, bias_scr, *, rows):
    h0 = pl.program_id(0) * NA_HEADS_PER_STEP
    t = pl.program_id(1)
    n_tiles = rows // NA_QROWS
    n_dr = 2 * NA_ROWS - 1
    n_dc = 2 * NA_COLS - 1
    half = GRID_W

    @pl.when(t == 0)
    def _():
        shape = (GRID_W, 2 * half)
        cq = lax.broadcasted_iota(jnp.int32, shape, 0)
        lane = lax.broadcasted_iota(jnp.int32, shape, 1)
        ck = lane & (half - 1)
        hi = lane >= half
        bidx = jnp.clip(ck - cq + (NA_COLS - 1), 0, n_dc - 1)
        col_ok = (ck - jnp.clip(cq - NA_COLS // 2, 0, GRID_W - NA_COLS))
        col_ok = (col_ok >= 0) & (col_ok < NA_COLS)
        for hh in range(NA_HEADS_PER_STEP):
            for i in range(n_dr + 1):
                val = jnp.full(shape, NEG_INF, F32)
                for e in range(2):
                    a = i - 1 + e
                    if 0 <= a < n_dr:
                        sel = hi if e else jnp.logical_not(hi)
                        for b in range(n_dc):
                            val = jnp.where(sel & (bidx == b), rpb_ref[((h0 + hh) * n_dr + a) * n_dc + b], val)
                cb_scr[hh, i] = jnp.where(col_ok, val, NEG_INF)

    r0 = t * NA_QROWS
    kb0 = jnp.clip(r0 - NA_ROWS // 2, 0, rows - NA_KROWS)

    @pl.when((t <= 1) | (t == n_tiles - 1))
    def _():
        lane = lax.broadcasted_iota(jnp.int32, (GRID_W, 2 * half), 1)
        lo = lane < half
        for rq in range(NA_QROWS):
            rq_abs = r0 + rq
            ws = jnp.clip(rq_abs - NA_ROWS // 2, 0, rows - NA_ROWS)
            for jj in range(NA_KROWS // 2):
                rk0 = kb0 + 2 * jj
                ok0 = (rk0 >= ws) & (rk0 < ws + NA_ROWS)
                ok1 = (rk0 + 1 >= ws) & (rk0 + 1 < ws + NA_ROWS)
                idx = jnp.clip(rk0 - rq_abs + NA_ROWS, 0, n_dr)
                keep = (lo & ok0) | (jnp.logical_not(lo) & ok1)
                for hh in range(NA_HEADS_PER_STEP):
                    bias_scr[hh, rq * GRID_W:(rq + 1) * GRID_W, jj * 2 * half:(jj + 1) * 2 * half] = (
                        jnp.where(keep, cb_scr[hh, idx], NEG_INF))

    outs = []
    for hh in range(NA_HEADS_PER_STEP):
        hd = slice(hh * HEAD_DIM, (hh + 1) * HEAD_DIM)
        q = q_ref[:, hd]
        k = jnp.concatenate([k0_ref[:, hd], k1_ref[:, hd], k2_ref[:, hd], k3_ref[:, hd]], axis=0)
        v = jnp.concatenate([v0_ref[:, hd], v1_ref[:, hd], v2_ref[:, hd], v3_ref[:, hd]], axis=0)
        s = _dot_nt(q, k) + bias_scr[hh]
        s_ctx = _dot_nt(q, kc_ref[:, hd])
        m = jnp.maximum(jnp.max(s, axis=-1, keepdims=True), jnp.max(s_ctx, axis=-1, keepdims=True))
        p = jnp.exp(s - m)
        p_ctx = jnp.exp(s_ctx - m)
        den = jnp.sum(p, axis=-1, keepdims=True) + jnp.sum(p_ctx, axis=-1, keepdims=True)
        o = (jnp.dot(p.astype(BF16), v, preferred_element_type=F32)
             + jnp.dot(p_ctx.astype(BF16), vc_ref[:, hd], preferred_element_type=F32))
        outs.append(o * (1.0 / den))
    o_ref[...] = jnp.concatenate(outs, axis=1).astype(o_ref.dtype)


def _neighbourhood_attention(qkv, qkv_ctx, rpb):
    seq = qkv.shape[0]
    m = qkv_ctx.shape[0]
    rows = seq // GRID_W
    assert rows % NA_QROWS == 0 and rows >= NA_KROWS + NA_QROWS
    n_tiles = rows // NA_QROWS
    tq = NA_QROWS * GRID_W
    tk = NA_KBLK * GRID_W
    n_kblk = NA_KROWS // NA_KBLK

    hp = NA_HEADS_PER_STEP
    wh = hp * HEAD_DIM
    assert NA_HEADS % hp == 0 and QA_BLK % hp == 0 and KA_BLK % hp == 0 and VA_BLK % hp == 0

    def kv_spec(col_blk, i):
        def index(h, t):
            first = jnp.clip(t * (NA_QROWS // NA_KBLK) - 1, 0, rows // NA_KBLK - n_kblk)
            return (first + i, col_blk // hp + h)
        return pl.BlockSpec((tk, wh), index)

    return pl.pallas_call(
        functools.partial(_na_kernel, rows=rows),
        grid=(NA_HEADS // hp, n_tiles),
        in_specs=[pl.BlockSpec(memory_space=pltpu.SMEM),
                  pl.BlockSpec((tq, wh), lambda h, t: (t, QA_BLK // hp + h))]
                 + [kv_spec(KA_BLK, i) for i in range(n_kblk)]
                 + [kv_spec(VA_BLK, i) for i in range(n_kblk)]
                 + [pl.BlockSpec((m, wh), lambda h, t: (0, KA_BLK // hp + h)),
                    pl.BlockSpec((m, wh), lambda h, t: (0, VA_BLK // hp + h))],
        out_specs=pl.BlockSpec((tq, wh), lambda h, t: (t, h)),
        out_shape=jax.ShapeDtypeStruct((seq, A_WIDTH), BF16),
        scratch_shapes=[pltpu.VMEM((hp, 2 * NA_ROWS, GRID_W, 2 * GRID_W), F32),
                        pltpu.VMEM((hp, tq, NA_KROWS * GRID_W), F32)],
        compiler_params=_cparams("arbitrary", "arbitrary"),
        name="neighbourhood_attn",
    )(rpb.reshape(-1), qkv, *([qkv] * (2 * n_kblk)), qkv_ctx, qkv_ctx)


def _ctx_attn_kernel(sink_ref, q_ref, k_ref, v_ref, o_ref):
    j = pl.program_id(0)
    s = _dot_nt(q_ref[...], k_ref[...])
    has_sink = j >= NA_HEADS
    sink = jnp.where(has_sink, sink_ref[jnp.maximum(j - NA_HEADS, 0)], NEG_INF)
    m = jnp.maximum(jnp.max(s, axis=-1, keepdims=True), sink)
    p = jnp.exp(s - m)
    den = jnp.sum(p, axis=-1, keepdims=True) + jnp.where(has_sink, jnp.exp(sink - m), 0.0)
    o = jnp.dot(p.astype(BF16), v_ref[...], preferred_element_type=F32)
    o_ref[...] = (o * (1.0 / den)).astype(o_ref.dtype)


def _context_attention(qkv_ctx, sink):
    m = qkv_ctx.shape[0]

    def q_idx(j):
        return (0, jnp.where(j < NA_HEADS, QA_BLK + j, QB_BLK + j - NA_HEADS))

    def k_idx(j):
        return (0, jnp.where(j < NA_HEADS, KA_BLK + j, KB_BLK + (j - NA_HEADS) // NB_GROUP))

    def v_idx(j):
        return (0, jnp.where(j < NA_HEADS, VA_BLK + j, VB_BLK + (j - NA_HEADS) // NB_GROUP))

    return pl.pallas_call(
        _ctx_attn_kernel,
        grid=(NA_HEADS + NB_Q_HEADS,),
        in_specs=[pl.BlockSpec(memory_space=pltpu.SMEM),
                  pl.BlockSpec((m, HEAD_DIM), q_idx),
                  pl.BlockSpec((m, HEAD_DIM), k_idx),
                  pl.BlockSpec((m, HEAD_DIM), v_idx)],
        out_specs=pl.BlockSpec((m, HEAD_DIM), lambda j: (0, j)),
        out_shape=jax.ShapeDtypeStruct((m, A_WIDTH + B_Q_WIDTH), BF16),
        compiler_params=_cparams("arbitrary"),
        name="context_attn",
    )(sink, qkv_ctx, qkv_ctx, qkv_ctx)


def _out_proj_kernel(a_ref, b_ref, wa_ref, wb_ref, x_ref, g_ref, o_ref):
    y = (jnp.dot(a_ref[...], wa_ref[...], preferred_element_type=F32)
         + jnp.dot(b_ref[...], wb_ref[...], preferred_element_type=F32))
    o_ref[...] = x_ref[...] + g_ref[...] * y


def _out_proj(oa, ob, a_blk, b_blk, w, x, gate, *, tm):
    m, d = x.shape
    kh = w.shape[0] // 2
    return pl.pallas_call(
        _out_proj_kernel,
        grid=(m // tm,),
        in_specs=[pl.BlockSpec((tm, kh), lambda i: (i, a_blk)),
                  pl.BlockSpec((tm, kh), lambda i: (i, b_blk)),
                  pl.BlockSpec((kh, d), lambda i: (0, 0)),
                  pl.BlockSpec((kh, d), lambda i: (1, 0)),
                  pl.BlockSpec((tm, d), lambda i: (i, 0)),
                  pl.BlockSpec((1, d), lambda i: (0, 0))],
        out_specs=pl.BlockSpec((tm, d), lambda i: (i, 0)),
        out_shape=jax.ShapeDtypeStruct((m, d), F32),
        compiler_params=_cparams("arbitrary"),
        name="out_proj",
    )(oa, ob, w, w, x, gate)


def _ffn_kernel(x_ref, g_ref, sc_ref, sh_ref, gate_ref, gf_ref, w1_ref, w3_ref, w2_ref, o_ref, h_scr, act_scr,
                *, final_norm, n_up):
    s = pl.program_id(1)
    tf = w1_ref.shape[1]
    tn = w2_ref.shape[1]

    @pl.when(s == 0)
    def _():
        h_scr[...] = _norm_mod(x_ref[...], g_ref[...], sc_ref[...], sh_ref[...]).astype(BF16)

    @pl.when(s < n_up)
    def _():
        h = h_scr[...]
        a = jnp.dot(h, w1_ref[...], preferred_element_type=F32)
        b = jnp.dot(h, w3_ref[...], preferred_element_type=F32)
        act_scr[:, pl.ds(pl.multiple_of(s * tf, tf), tf)] = (_silu(a) * b).astype(BF16)

    @pl.when(s >= n_up)
    def _():
        cols = pl.ds(pl.multiple_of((s - n_up) * tn, tn), tn)
        y = jnp.dot(act_scr[...], w2_ref[...], preferred_element_type=F32)
        o_ref[:, cols] = x_ref[:, cols] + gate_ref[:, cols] * y

    if final_norm:
        @pl.when(s == pl.num_programs(1) - 1)
        def _():
            r = o_ref[...]
            ms = jnp.mean(r * r, axis=-1, keepdims=True)
            o_ref[...] = r * lax.rsqrt(ms + EPS) * gf_ref[...]


def _ffn(x, g, sc, sh, gate, gf, w1, w3, w2, layer, *, final_norm, tm):
    m, d = x.shape
    dff = w1.shape[-1]
    tf, tn = 512, 256
    assert dff % tf == 0 and d % tn == 0
    n_up, n_dn = dff // tf, d // tn
    vec = pl.BlockSpec((1, d), lambda i, s: (0, 0))
    once = pl.Buffered(1) if m > tm else None
    return pl.pallas_call(
        functools.partial(_ffn_kernel, final_norm=final_norm, n_up=n_up),
        grid=(m // tm, n_up + n_dn),
        in_specs=[pl.BlockSpec((tm, d), lambda i, s: (i, 0)), vec, vec, vec, vec, vec,
                  pl.BlockSpec((None, d, tf), lambda i, s: (layer, 0, jnp.minimum(s, n_up - 1))),
                  pl.BlockSpec((None, d, tf), lambda i, s: (layer, 0, jnp.minimum(s, n_up - 1))),
                  pl.BlockSpec((None, dff, tn), lambda i, s: (layer, 0, jnp.maximum(s - n_up, 0)))],
        out_specs=pl.BlockSpec((tm, d), lambda i, s: (i, 0), pipeline_mode=once),
        out_shape=jax.ShapeDtypeStruct((m, d), F32),
        scratch_shapes=[pltpu.VMEM((tm, d), BF16), pltpu.VMEM((tm, dff), BF16)],
        compiler_params=_cparams("arbitrary", "arbitrary", limit=FFN_VMEM_LIMIT),
        name="swiglu_final" if final_norm else "swiglu",
    )(x, g, sc, sh, gate, gf, w1, w3, w2)


def _chunk_perm(n):
    r = lax.broadcasted_iota(jnp.int32, (n, n), 0)
    c = lax.broadcasted_iota(jnp.int32, (n, n), 1)
    nch = n // S5_CHUNK
    return (c == (r % nch) * S5_CHUNK + r // nch).astype(BF16)


def _field_transpose(v):
    v = list(v)
    lane = lax.broadcasted_iota(jnp.int32, v[0].shape, 1)
    s = FIELDS // 2
    while s:
        keep = ((lane // FIELD) & s) == 0
        for i in range(FIELDS):
            if i & s:
                continue
            a, b = v[i], v[i + s]
            v[i] = jnp.where(keep, a, pltpu.roll(b, FIELD * s, axis=1))
            v[i + s] = jnp.where(keep, pltpu.roll(a, LANES - FIELD * s, axis=1), b)
        s //= 2
    return v


def _s5_input_kernel(x_ref, c_ref, g_ref, scx_ref, shx_ref, scc_ref, shc_ref, o_ref):
    i = pl.program_id(0)
    is_ctx = (i == 0) | (i == pl.num_programs(0) - 1)
    n = x_ref.shape[0]

    def emit(h):
        hp = jnp.dot(_chunk_perm(n), h.astype(BF16), preferred_element_type=F32)
        o_ref[...] = hp.astype(BF16).reshape(o_ref.shape)

    @pl.when(is_ctx)
    def _():
        emit(_norm_mod(c_ref[...], g_ref[...], scc_ref[...], shc_ref[...]))

    @pl.when(jnp.logical_not(is_ctx))
    def _():
        emit(_norm_mod(x_ref[...], g_ref[...], scx_ref[...], shx_ref[...]))


def _s5_input(x, ctx, g, scx, shx, scc, shc):
    seq, d = x.shape
    m = ctx.shape[0]
    assert m % (S5_CHUNK * 8) == 0 and seq % m == 0
    nx = seq // m
    nc = (seq + 2 * m) // S5_CHUNK
    vec = pl.BlockSpec((1, d), lambda i: (0, 0))
    return pl.pallas_call(
        _s5_input_kernel,
        grid=(nx + 2,),
        in_specs=[pl.BlockSpec((m, d), lambda i: (jnp.clip(i - 1, 0, nx - 1), 0)),
                  pl.BlockSpec((m, d), lambda i: (0, 0)), vec, vec, vec, vec, vec],
        out_specs=pl.BlockSpec((S5_CHUNK, m // S5_CHUNK, d), lambda i: (0, i, 0)),
        out_shape=jax.ShapeDtypeStruct((S5_CHUNK, nc, d), BF16),
        compiler_params=_cparams("arbitrary"),
        name="s5_input",
    )(x, ctx, g, scx, shx, scc, shc)


def _sublane_transpose(v):
    n = v[0].shape[0]
    v = [a.reshape(n // 8, 8, LANES) for a in v]
    row = lax.broadcasted_iota(jnp.int32, v[0].shape, 1)
    s = 4
    while s:
        keep = (row & s) == 0
        for i in range(8):
            if i & s:
                continue
            a, b = v[i], v[i + s]
            v[i] = jnp.where(keep, a, pltpu.roll(b, s, axis=1))
            v[i + s] = jnp.where(keep, pltpu.roll(a, 8 - s, axis=1), b)
        s //= 2
    return [a.reshape(n, LANES) for a in v]


def _s5_kernel(z_ref, wm_ref, wb_ref, wc_ref, lam_ref, o_ref, u_scr, b_scr, hf_scr, hb_scr, *, c_lo):
    L, nc, _ = z_ref.shape
    n_out = o_ref.shape[0] // L
    gs = FIELDS
    half = SSM_STATE
    for j in range(L // FIELDS):
        v = [pltpu.bitcast(z_ref[j * FIELDS + i], jnp.uint32) for i in range(FIELDS)]
        w = _field_transpose(v)
        for gi in range(gs):
            u_scr[gi, :, j * LANES:(j + 1) * LANES] = pltpu.bitcast(w[gi], BF16)

    bs = [jnp.dot(u_scr[gi], wb_ref[gi], preferred_element_type=F32) for gi in range(gs)]
    for comp in range(2):
        t = _sublane_transpose([b[:, comp * LANES:(comp + 1) * LANES] for b in bs])
        for j in range(8):
            b_scr[comp, j] = t[j]

    a_re = lam_ref[:, :2 * half]
    a_im = lam_ref[:, 2 * half:]
    fwd = lax.broadcasted_iota(jnp.int32, (gs, 2 * half), 1) < half

    def step(t, carry):
        s_re, s_im = carry
        rf = pl.ds(pl.multiple_of(t * 8, 8), 8)
        rb = pl.ds(pl.multiple_of(nc - 8 - t * 8, 8), 8)
        for jf in range(8):
            jb = 7 - jf
            hf_scr[0, jf, rf, :] = s_re
            hf_scr[1, jf, rf, :] = s_im
            hb_scr[0, jb, rb, :] = s_re
            hb_scr[1, jb, rb, :] = s_im
            v_re = jnp.where(fwd, b_scr[0, jf, rf, :], b_scr[0, jb, rb, :])
            v_im = jnp.where(fwd, b_scr[1, jf, rf, :], b_scr[1, jb, rb, :])
            s_re, s_im = a_re * s_re - a_im * s_im + v_re, a_re * s_im + a_im * s_re + v_im
        return s_re, s_im

    zero = jnp.zeros((gs, 2 * half), F32)
    lax.fori_loop(0, nc // 8, step, (zero, zero))

    fwd_all = lax.broadcasted_iota(jnp.int32, (nc, 2 * half), 1) < half
    h_in = [_sublane_transpose([jnp.where(fwd_all, hf_scr[comp, j], hb_scr[comp, j]) for j in range(8)])
            for comp in range(2)]
    ys = []
    for gi in range(gs):
        hcat = jnp.concatenate([h_in[0][gi], h_in[1][gi]], axis=1)[c_lo:c_lo + n_out].astype(BF16)
        ys.append(jnp.dot(u_scr[gi, c_lo:c_lo + n_out, :], wm_ref[gi], preferred_element_type=F32)
                  + _dot_nt(hcat, wc_ref[gi]))
    for j in range(L // FIELDS):
        w = _field_transpose([y[:, j * LANES:(j + 1) * LANES] for y in ys])
        for i in range(FIELDS):
            o_ref[pl.ds(j * FIELDS + i, n_out, stride=L), :] = w[i]


def _s5_scan(zp, wm, wb, wc, lam, *, c_lo, n_out):
    L, nc, d = zp.shape
    w = wm.shape[-1]
    gs = FIELDS
    assert nc % 8 == 0 and gs == 8
    slab = pltpu.VMEM((2, 8, nc, 2 * SSM_STATE), F32)
    return pl.pallas_call(
        functools.partial(_s5_kernel, c_lo=c_lo),
        grid=(d // LANES,),
        in_specs=[pl.BlockSpec((L, nc, LANES), lambda i: (0, 0, i)),
                  pl.BlockSpec((gs, w, w), lambda i: (i, 0, 0)),
                  pl.BlockSpec((gs, w, w), lambda i: (i, 0, 0)),
                  pl.BlockSpec((gs, w, w), lambda i: (i, 0, 0)),
                  pl.BlockSpec((gs, w), lambda i: (i, 0))],
        out_specs=pl.BlockSpec((L * n_out, LANES), lambda i: (0, i)),
        out_shape=jax.ShapeDtypeStruct((L * n_out, d), F32),
        scratch_shapes=[pltpu.VMEM((gs, nc, w), BF16), slab, slab, slab],
        compiler_params=_cparams("arbitrary"),
        name="s5_scan",
    )(zp, wm, wb, wc, lam)


def _shift_lanes(x, s):
    w = LANES
    x0, x1 = x[:, :w], x[:, w:]
    zero = jnp.zeros_like(x0)
    lane = lax.broadcasted_iota(jnp.int32, x0.shape, 1)
    if s == 0:
        return x
    if s > 0:
        if s >= w:
            t = s - w
            y1 = x0 if t == 0 else jnp.where(lane < t, 0.0, pltpu.roll(x0, t, axis=1))
            return jnp.concatenate([zero, y1], axis=1)
        r0, r1 = pltpu.roll(x0, s, axis=1), pltpu.roll(x1, s, axis=1)
        return jnp.concatenate([jnp.where(lane < s, 0.0, r0), jnp.where(lane < s, r0, r1)], axis=1)
    s = -s
    if s >= w:
        t = s - w
        y0 = x1 if t == 0 else jnp.where(lane >= w - t, 0.0, pltpu.roll(x1, w - t, axis=1))
        return jnp.concatenate([y0, zero], axis=1)
    r0, r1 = pltpu.roll(x0, w - s, axis=1), pltpu.roll(x1, w - s, axis=1)
    return jnp.concatenate([jnp.where(lane >= w - s, r1, r0), jnp.where(lane >= w - s, 0.0, r1)], axis=1)


def _s5_op_kernel(a_re_ref, a_im_ref, ldt_ref, b_re_ref, b_im_ref, c_re_ref, c_im_ref,
                  wm_ref, wb_ref, wc_ref, lam_ref, e_scr):
    gs = a_re_ref.shape[0]
    L = S5_CHUNK
    h = SSM_GROUP
    half = SSM_STATE
    fwd1 = lax.broadcasted_iota(jnp.int32, (1, 2 * half), 1) < half
    fwd = lax.broadcasted_iota(jnp.int32, (h, 2 * half), 1) < half

    def cmul(ar, ai, br, bi):
        return ar * br - ai * bi, ar * bi + ai * br

    for gi in range(gs):
        ar, ai = a_re_ref[gi], a_im_ref[gi]
        dt = jnp.exp(ldt_ref[gi])
        mag = jnp.exp(ar * dt)
        lr, li = mag * jnp.cos(ai * dt), mag * jnp.sin(ai * dt)
        den = ar * ar + ai * ai
        nr = lr - 1.0
        coef_r = (nr * ar + li * ai) / den
        coef_i = (li * ar - nr * ai) / den
        bb_r, bb_i = cmul(coef_r, coef_i, b_re_ref[gi], b_im_ref[gi])
        c_r, c_i = c_re_ref[gi], c_im_ref[gi]
        pw = [(jnp.ones_like(lr), jnp.zeros_like(lr))]
        for _ in range(L):
            pw.append(cmul(pw[-1][0], pw[-1][1], lr, li))

        def mixed(jf, jb):
            return (jnp.where(fwd1, pw[jf][0], pw[jb][0]), jnp.where(fwd1, pw[jf][1], pw[jb][1]))

        for l in range(L):
            s_r, s_i = cmul(*mixed(L - 1 - l, l), bb_r, bb_i)
            wb_ref[gi, l * h:(l + 1) * h, :] = jnp.concatenate([s_r, s_i], axis=1).astype(BF16)
            f_r, f_i = cmul(*mixed(l + 1, L - l), c_r, c_i)
            wc_ref[gi, l * h:(l + 1) * h, :] = jnp.concatenate([f_r, -f_i], axis=1).astype(BF16)
            e_r, e_i = cmul(*mixed(l, L - 1 - l), c_r, c_i)
            e_scr[l * h:(l + 1) * h, :] = jnp.concatenate([e_r, e_i], axis=1)

        lhs = jnp.concatenate(
            [jnp.concatenate([jnp.where(fwd, bb_r, 0.0), jnp.where(fwd, -bb_i, 0.0)], axis=1),
             jnp.concatenate([jnp.where(fwd, 0.0, bb_r), jnp.where(fwd, 0.0, -bb_i)], axis=1)], axis=0)
        kt = _dot_nt(lhs, e_scr[...], precision=lax.Precision.HIGHEST)
        kt_f, kt_b = kt[:h], kt[h:]
        for l in range(L):
            blk = _shift_lanes(kt_f, h * l) + _shift_lanes(kt_b, -h * (L - 1 - l))
            wm_ref[gi, l * h:(l + 1) * h, :] = blk.astype(BF16)
        lam_ref[gi:gi + 1, :] = jnp.concatenate([pw[L][0], pw[L][1]], axis=1)


def _s5_operators(a_re, a_im, log_dt, b_re, b_im, c_re, c_im):
    _, g, p = a_re.shape
    h = b_re.shape[-1]
    gs = FIELDS
    w = S5_CHUNK * h
    assert w == 2 * LANES and 2 * p == LANES

    def lanes(v):
        return jnp.concatenate([v[0], v[1]], axis=-1)

    def chan_rows(v):
        return lanes(jnp.swapaxes(v, -1, -2))

    a_re2 = lanes(a_re).reshape(g, 1, 2 * p)
    a_im2 = lanes(a_im).reshape(g, 1, 2 * p)
    ldt2 = jnp.repeat(log_dt.T, p, axis=1).reshape(g, 1, 2 * p)
    vec = pl.BlockSpec((gs, 1, 2 * p), lambda i: (i, 0, 0))
    mat = pl.BlockSpec((gs, h, 2 * p), lambda i: (i, 0, 0))
    return pl.pallas_call(
        _s5_op_kernel,
        grid=(g // gs,),
        in_specs=[vec, vec, vec, mat, mat, mat, mat],
        out_specs=[pl.BlockSpec((gs, w, w), lambda i: (i, 0, 0)),
                   pl.BlockSpec((gs, w, w), lambda i: (i, 0, 0)),
                   pl.BlockSpec((gs, w, w), lambda i: (i, 0, 0)),
                   pl.BlockSpec((gs, w), lambda i: (i, 0))],
        out_shape=[jax.ShapeDtypeStruct((g, w, w), BF16), jax.ShapeDtypeStruct((g, w, w), BF16),
                   jax.ShapeDtypeStruct((g, w, w), BF16), jax.ShapeDtypeStruct((g, w), F32)],
        scratch_shapes=[pltpu.VMEM((w, w), F32)],
        compiler_params=_cparams("arbitrary"),
        name="s5_operators",
    )(a_re2, a_im2, ldt2, chan_rows(b_re), chan_rows(b_im), lanes(c_re), lanes(c_im))


def _gelu_tanh(y):
    k = 2.0 * math.sqrt(2.0 / math.pi) * math.log2(math.e)
    return y * (1.0 / (1.0 + jnp.exp2(y * (-k - (k * 0.044715) * (y * y)))))


def _glu_kernel(x_ref, ys_ref, g_ref, sc_ref, sh_ref, d_ref, gate_ref, b_ref, w_ref, o_ref, act_scr):
    j = pl.program_id(1)
    tn = o_ref.shape[1]
    d_model = x_ref.shape[1]

    @pl.when(j == 0)
    def _():
        rb = 32

        def rows_pass(r, carry):
            rows = pl.ds(pl.multiple_of(r * rb, rb), rb)
            hx = _norm_mod(x_ref[rows, :], g_ref[...], sc_ref[...], sh_ref[...])
            act_scr[rows, :] = _gelu_tanh(d_ref[...] * hx + ys_ref[rows, :]).astype(BF16)
            return carry

        lax.fori_loop(0, x_ref.shape[0] // rb, rows_pass, 0, unroll=4)

    act = act_scr[...]
    cols = pl.ds(pl.multiple_of(j * tn, tn), tn)
    gcols = pl.ds(pl.multiple_of(d_model + j * tn, tn), tn)
    val = jnp.dot(act, w_ref[:, cols], preferred_element_type=F32) + b_ref[:, cols]
    gate = jnp.dot(act, w_ref[:, gcols], preferred_element_type=F32) + b_ref[:, gcols]
    o_ref[...] = x_ref[:, cols] + gate_ref[:, cols] * (val * (1.0 / (1.0 + jnp.exp(-gate))))


def _glu(x, ys, g, sc, sh, d_skip, gate, w, b, *, tm):
    m, d = x.shape
    tn = 512
    nj = d // tn
    vec = pl.BlockSpec((1, d), lambda i, j: (0, 0))
    return pl.pallas_call(
        _glu_kernel,
        grid=(m // tm, nj),
        in_specs=[pl.BlockSpec((tm, d), lambda i, j: (i, 0)),
                  pl.BlockSpec((tm, d), lambda i, j: (i, 0)), vec, vec, vec, vec, vec,
                  pl.BlockSpec((1, 2 * d), lambda i, j: (0, 0)),
                  pl.BlockSpec((d, 2 * d), lambda i, j: (0, 0), pipeline_mode=pl.Buffered(1))],
        out_specs=pl.BlockSpec((tm, tn), lambda i, j: (i, j)),
        out_shape=jax.ShapeDtypeStruct((m, d), F32),
        scratch_shapes=[pltpu.VMEM((tm, d), BF16)],
        compiler_params=_cparams("arbitrary", "arbitrary"),
        name="s5_glu",
    )(x, ys, g, sc, sh, d_skip, gate, b, w)


def kernel(x, c, ctx, c_ctx, ada_w, ada_b, norm_mix, norm_ffn, ffn_w1, ffn_w3, ffn_w2, attn_w_in, attn_w_out,
           attn_rpb, attn_sink, ssm_a_re, ssm_a_im, ssm_log_dt, ssm_b_re, ssm_b_im, ssm_c_re, ssm_c_im,
           ssm_d, ssm_w_glu, ssm_b_glu, norm_final):
    batch, seq, d = x.shape
    assert batch == 1 and ada_w.shape[0] == 2
    m = ctx.shape[1]
    xs, cs = x[0], ctx[0]
    mods = _ada_mod(c, c_ctx, ada_w, ada_b)

    def mod(layer, who):
        return [mods[layer, who, i * d:(i + 1) * d].reshape(1, d) for i in range(6)]

    row = lambda v: v.reshape(1, d)
    tm = 512
    tm_qkv = 1024
    tm_ffn = 1024

    sh1, sc1, g1, sh2, sc2, g2 = mod(0, 0)
    csh1, csc1, cg1, csh2, csc2, cg2 = mod(0, 1)
    cos, sin = _rope_tables(seq)
    w_in = attn_w_in[0].astype(BF16)
    w_out = attn_w_out[0].astype(BF16)
    w1, w3, w2 = ffn_w1.astype(BF16), ffn_w3.astype(BF16), ffn_w2.astype(BF16)
    nm, nf = row(norm_mix[0]), row(norm_ffn[0])
    qkv = _qkv_proj(xs, nm, sc1, sh1, w_in, cos, sin, rope=True, tm=tm_qkv)
    qkv_c = _qkv_proj(cs, nm, csc1, csh1, w_in, cos[:m], sin[:m], rope=False, tm=m)
    oa = _neighbourhood_attention(qkv, qkv_c, attn_rpb[0])
    ob = _window_attention(qkv, qkv_c, attn_sink[0])
    oc = _context_attention(qkv_c, attn_sink[0])
    xs = _out_proj(oa, ob, 0, 0, w_out, xs, g1, tm=tm)
    cs = _out_proj(oc, oc, 0, 1, w_out, cs, cg1, tm=m)
    xs = _ffn(xs, nf, sc2, sh2, g2, nf, w1, w3, w2, 0, final_norm=False, tm=tm_ffn)
    cs = _ffn(cs, nf, csc2, csh2, cg2, nf, w1, w3, w2, 0, final_norm=False, tm=m)

    sh1, sc1, g1, sh2, sc2, g2 = mod(1, 0)
    csh1, csc1, _, _, _, _ = mod(1, 1)
    nm, nf = row(norm_mix[1]), row(norm_ffn[1])
    z = _s5_input(xs, cs, nm, sc1, sh1, csc1, csh1)
    wm, wb, wc, lam = _s5_operators(ssm_a_re[0], ssm_a_im[0], ssm_log_dt[0], ssm_b_re[0], ssm_b_im[0],
                                    ssm_c_re[0], ssm_c_im[0])
    ys = _s5_scan(z, wm, wb, wc, lam, c_lo=m // S5_CHUNK, n_out=seq // S5_CHUNK)
    xs = _glu(xs, ys, nm, sc1, sh1, row(ssm_d[0]), g1, ssm_w_glu[0].astype(BF16), ssm_b_glu[0].reshape(1, 2 * d),
              tm=tm)
    xs = _ffn(xs, nf, sc2, sh2, g2, row(norm_final), w1, w3, w2, 1, final_norm=True, tm=tm_ffn)
    return xs[None]
```

```python
import functools
import math

import jax
import jax.numpy as jnp
import numpy as np
from jax import lax
from jax.experimental import pallas as pl
from jax.experimental.pallas import tpu as pltpu

F32 = jnp.float32
BF16 = jnp.bfloat16

GRID_W = 64
HEAD_DIM = 128
NA_HEADS = 8
NB_Q_HEADS = 8
NB_KV_HEADS = 2
NB_GROUP = NB_Q_HEADS // NB_KV_HEADS
NA_ROWS = 8
NA_COLS = 16
SW_BLOCK = 128
SW_QBLOCKS = 4
ROPE_BASE = 10000.0
SSM_GROUP = 16
SSM_STATE = 64
EPS = 1e-6
NEG_INF = -1e30
ATTN_SCALE = HEAD_DIM ** -0.5

A_WIDTH = NA_HEADS * HEAD_DIM
B_Q_WIDTH = NB_Q_HEADS * HEAD_DIM
B_KV_WIDTH = NB_KV_HEADS * HEAD_DIM
QA_BLK = 0
KA_BLK = NA_HEADS
VA_BLK = 2 * NA_HEADS
QB_BLK = 3 * NA_HEADS
KB_BLK = QB_BLK + NB_Q_HEADS
VB_BLK = KB_BLK + NB_KV_HEADS

NA_QROWS = 8
NA_KROWS = 16
NA_KBLK = 4
NA_HEADS_PER_STEP = 8
PROLOGUE_PARTS = 8
S5_CHUNK = 16
LANES = 128
FIELD = SSM_GROUP
FIELDS = LANES // FIELD
VMEM_LIMIT = 56 * 1024 * 1024
FFN_VMEM_LIMIT = 62 * 1024 * 1024


def _cparams(*sem, limit=VMEM_LIMIT):
    return pltpu.CompilerParams(dimension_semantics=sem, vmem_limit_bytes=limit)


def _silu(v):
    return v * (1.0 / (1.0 + jnp.exp(-v)))


def _norm_mod(x, g, sc, sh):
    ms = jnp.mean(x * x, axis=-1, keepdims=True)
    return (x * lax.rsqrt(ms + EPS)) * (g * (1.0 + sc)) + sh


def _dot_nt(a, b, precision=None):
    return lax.dot_general(a, b, (((1,), (1,)), ((), ())), preferred_element_type=F32, precision=precision)


def _ada_kernel(cb_ref, w_ref, b_ref, o_ref, s_scr, *, rows_per_step):
    d, tn = w_ref.shape
    rep = tn // 128

    @pl.when((pl.program_id(0) == 0) & (pl.program_id(1) == 0))
    def _():
        s_scr[...] = _silu(cb_ref[...])

    def body(i, acc):
        r = pl.multiple_of(i * rows_per_step, rows_per_step)
        w = w_ref[pl.ds(r, rows_per_step), :]
        out = []
        for v in range(2):
            s = s_scr[v, pl.ds(r, rows_per_step), :]
            st = jnp.concatenate([s] * rep, axis=1)
            out.append(acc[v] + jnp.sum((w * st).reshape(rows_per_step // 8, 8, tn), axis=0))
        return tuple(out)

    zero = jnp.zeros((8, tn), F32)
    acc = lax.fori_loop(0, d // rows_per_step, body, (zero, zero))
    o_ref[...] = jnp.concatenate([jnp.sum(a, axis=0, keepdims=True) for a in acc], axis=0) + b_ref[...]


def _ada_mod(c, c_ctx, ada_w, ada_b):
    depth, d, n = ada_w.shape
    tn = 1024
    cb = jnp.stack([jnp.broadcast_to(c.reshape(d, 1), (d, 128)),
                    jnp.broadcast_to(c_ctx.reshape(d, 1), (d, 128))])
    return pl.pallas_call(
        functools.partial(_ada_kernel, rows_per_step=64),
        grid=(depth, n // tn),
        in_specs=[pl.BlockSpec((2, d, 128), lambda l, j: (0, 0, 0)),
                  pl.BlockSpec((None, d, tn), lambda l, j: (l, 0, j)),
                  pl.BlockSpec((None, 1, tn), lambda l, j: (l, 0, j))],
        out_specs=pl.BlockSpec((None, 2, tn), lambda l, j: (l, 0, j)),
        out_shape=jax.ShapeDtypeStruct((depth, 2, n), F32),
        scratch_shapes=[pltpu.VMEM((2, d, 128), F32)],
        compiler_params=_cparams("arbitrary", "arbitrary"),
        name="ada_mod",
    )(cb, ada_w, ada_b.reshape(depth, 1, n))


def _rope(a, cos, sin):
    lane = lax.broadcasted_iota(jnp.int32, a.shape, 1)
    partner = jnp.where((lane & 63) < 32, pltpu.roll(a, 96, axis=1), pltpu.roll(a, 32, axis=1))
    return a * cos + partner * sin


def _qkv_kernel(x0_ref, xn_ref, g_ref, sc_ref, sh_ref, w_ref, cos_ref, sin_ref, o_ref, h0_scr, h1_scr, acc_scr,
                *, rope):
    i = pl.program_id(0)
    j = pl.program_id(1)
    tm, tn = o_ref.shape
    heads = tn // HEAD_DIM
    part = tm // PROLOGUE_PARTS

    @pl.when((i == 0) & (j == 0))
    def _():
        h0_scr[...] = _norm_mod(x0_ref[...], g_ref[...], sc_ref[...], sh_ref[...]).astype(BF16)

    def step(h_cur, h_nxt):
        rows = pl.ds(pl.multiple_of(jnp.minimum(j, PROLOGUE_PARTS - 1) * part, part), part)
        h_nxt[rows, :] = _norm_mod(xn_ref[rows, :], g_ref[...], sc_ref[...], sh_ref[...]).astype(BF16)
        acc = jnp.dot(h_cur[...], w_ref[...], preferred_element_type=F32)
        acc_scr[...] = acc
        o_ref[...] = (acc * jnp.where(is_q, ATTN_SCALE, 1.0)).astype(o_ref.dtype)

    col0 = j * heads
    is_qb = (col0 >= QB_BLK) & (col0 < KB_BLK)
    is_kb = col0 == KB_BLK
    is_q = (col0 < KA_BLK) | is_qb

    @pl.when(i % 2 == 0)
    def _():
        step(h0_scr, h1_scr)

    @pl.when(i % 2 == 1)
    def _():
        step(h1_scr, h0_scr)

    if rope:
        def rotated(n_heads, scale):
            parts = []
            for hh in range(heads):
                a = acc_scr[:, hh * HEAD_DIM:(hh + 1) * HEAD_DIM]
                if hh < n_heads:
                    a = _rope(a, cos_ref[...], sin_ref[...]) * scale
                parts.append(a)
            return jnp.concatenate(parts, axis=1)

        @pl.when(is_qb)
        def _():
            o_ref[...] = rotated(heads, ATTN_SCALE).astype(o_ref.dtype)

        @pl.when(is_kb)
        def _():
            o_ref[...] = rotated(NB_KV_HEADS, 1.0).astype(o_ref.dtype)


def _qkv_proj(x, g, sc, sh, w, cos, sin, *, rope, tm):
    m, d = x.shape
    n = w.shape[1]
    tn = 512
    nt = m // tm
    assert KB_BLK % (tn // HEAD_DIM) == 0 and m % tm == 0 and n % tn == 0
    assert n // tn >= PROLOGUE_PARTS and tm % (16 * PROLOGUE_PARTS) == 0
    vec = pl.BlockSpec((1, d), lambda i, j: (0, 0))
    tab = pl.BlockSpec((tm, HEAD_DIM), lambda i, j: (i, 0))
    return pl.pallas_call(
        functools.partial(_qkv_kernel, rope=rope),
        grid=(nt, n // tn),
        in_specs=[pl.BlockSpec((tm, d), lambda i, j: (0, 0), pipeline_mode=pl.Buffered(1)),
                  pl.BlockSpec((tm, d), lambda i, j: (jnp.minimum(i + 1, nt - 1), 0)), vec, vec, vec,
                  pl.BlockSpec((d, tn), lambda i, j: (0, j)), tab, tab],
        out_specs=pl.BlockSpec((tm, tn), lambda i, j: (i, j)),
        out_shape=jax.ShapeDtypeStruct((m, n), BF16),
        scratch_shapes=[pltpu.VMEM((tm, d), BF16), pltpu.VMEM((tm, d), BF16), pltpu.VMEM((tm, tn), F32)],
        compiler_params=_cparams("arbitrary", "arbitrary"),
        name="qkv_rope" if rope else "qkv_ctx",
    )(x, x, g, sc, sh, w, cos, sin)


def _rope_tables(seq):
    quarter = HEAD_DIM // 4
    rows = seq // GRID_W
    f32 = np.float32
    inv_freq = (f32(ROPE_BASE) ** (-np.arange(quarter, dtype=f32) / f32(quarter))).astype(f32)
    ang_r = np.arange(rows, dtype=f32)[:, None] * inv_freq[None, :]
    ang_c = np.arange(GRID_W, dtype=f32)[:, None] * inv_freq[None, :]

    def tokens(tab_r, tab_c, sign):
        r = np.broadcast_to(tab_r[:, None, :], (rows, GRID_W, quarter))
        c = np.broadcast_to(tab_c[None, :, :], (rows, GRID_W, quarter))
        return jnp.asarray(np.concatenate([sign * r, r, sign * c, c], axis=-1).reshape(seq, HEAD_DIM), dtype=F32)

    return tokens(np.cos(ang_r), np.cos(ang_c), f32(1.0)), tokens(np.sin(ang_r), np.sin(ang_c), f32(-1.0))


def _win_kernel(sink_ref, q_ref, *refs, seq):
    nkv = SW_QBLOCKS + 2
    k_refs, v_refs = refs[:nkv], refs[nkv:2 * nkv]
    kc_ref, vc_ref, o_ref, valid_scr = refs[2 * nkv:]
    n = pl.program_id(0)
    blk = SW_BLOCK
    nb = seq // blk
    shape = (NB_GROUP * blk, 3 * blk)

    @pl.when(n == 0)
    def _():
        row = lax.broadcasted_iota(jnp.int32, shape, 0)
        col = lax.broadcasted_iota(jnp.int32, shape, 1)
        rel = col - (row & (blk - 1))
        ok = (rel >= 0) & (rel <= 2 * blk)
        for first in range(2):
            for last in range(2):
                okb = ok
                if first:
                    okb = okb & (col >= blk)
                if last:
                    okb = okb & (col < 2 * blk)
                valid_scr[first + 2 * last] = okb.astype(F32)

    grp = lax.broadcasted_iota(jnp.int32, (NB_GROUP * blk, 1), 0) // blk
    for b in range(SW_QBLOCKS):
        gb = n * SW_QBLOCKS + b
        pattern = (gb == 0).astype(jnp.int32) + 2 * (gb == nb - 1).astype(jnp.int32)
        valid = valid_scr[pattern] > 0.5
        rows = slice(b * blk, (b + 1) * blk)
        outs = []
        for h in range(NB_KV_HEADS):
            hd = slice(h * HEAD_DIM, (h + 1) * HEAD_DIM)
            qs = jnp.concatenate([q_ref[rows, (h * NB_GROUP + g) * HEAD_DIM:(h * NB_GROUP + g + 1) * HEAD_DIM]
                                  for g in range(NB_GROUP)], axis=0)
            k = jnp.concatenate([r[:, hd] for r in k_refs[b:b + 3]], axis=0)
            v = jnp.concatenate([r[:, hd] for r in v_refs[b:b + 3]], axis=0)
            s = jnp.where(valid, _dot_nt(qs, k), NEG_INF)
            s_ctx = _dot_nt(qs, kc_ref[:, hd])
            sink = jnp.zeros((NB_GROUP * blk, 1), F32)
            for g in range(NB_GROUP):
                sink = jnp.where(grp == g, sink_ref[h * NB_GROUP + g], sink)
            m = jnp.maximum(jnp.maximum(jnp.max(s, axis=-1, keepdims=True),
                                        jnp.max(s_ctx, axis=-1, keepdims=True)), sink)
            p = jnp.exp(s - m)
            p_ctx = jnp.exp(s_ctx - m)
            den = jnp.sum(p, axis=-1, keepdims=True) + jnp.sum(p_ctx, axis=-1, keepdims=True) + jnp.exp(sink - m)
            o = (jnp.dot(p.astype(BF16), v, preferred_element_type=F32)
                 + jnp.dot(p_ctx.astype(BF16), vc_ref[:, hd], preferred_element_type=F32))
            o = o * (1.0 / den)
            outs += [o[g * blk:(g + 1) * blk] for g in range(NB_GROUP)]
        o_ref[rows, :] = jnp.concatenate(outs, axis=1).astype(o_ref.dtype)


def _window_attention(qkv, qkv_ctx, sink):
    seq = qkv.shape[0]
    m = qkv_ctx.shape[0]
    nb = seq // SW_BLOCK
    assert QB_BLK % NB_Q_HEADS == 0 and KB_BLK % NB_KV_HEADS == 0 and VB_BLK % NB_KV_HEADS == 0
    assert nb % SW_QBLOCKS == 0
    nkv = SW_QBLOCKS + 2

    def kv_spec(col_blk, shift):
        return pl.BlockSpec((SW_BLOCK, B_KV_WIDTH),
                            lambda n: (jnp.clip(n * SW_QBLOCKS + shift, 0, nb - 1), col_blk // NB_KV_HEADS))

    return pl.pallas_call(
        functools.partial(_win_kernel, seq=seq),
        grid=(nb // SW_QBLOCKS,),
        in_specs=[pl.BlockSpec(memory_space=pltpu.SMEM),
                  pl.BlockSpec((SW_QBLOCKS * SW_BLOCK, B_Q_WIDTH), lambda n: (n, QB_BLK // NB_Q_HEADS))]
                 + [kv_spec(KB_BLK, i - 1) for i in range(nkv)]
                 + [kv_spec(VB_BLK, i - 1) for i in range(nkv)]
                 + [pl.BlockSpec((m, B_KV_WIDTH), lambda n: (0, KB_BLK // NB_KV_HEADS)),
                    pl.BlockSpec((m, B_KV_WIDTH), lambda n: (0, VB_BLK // NB_KV_HEADS))],
        out_specs=pl.BlockSpec((SW_QBLOCKS * SW_BLOCK, B_Q_WIDTH), lambda n: (n, 0)),
        out_shape=jax.ShapeDtypeStruct((seq, B_Q_WIDTH), BF16),
        scratch_shapes=[pltpu.VMEM((4, NB_GROUP * SW_BLOCK, 3 * SW_BLOCK), F32)],
        compiler_params=_cparams("arbitrary"),
        name="window_attn",
    )(sink, qkv, *([qkv] * (2 * nkv)), qkv_ctx, qkv_ctx)


def _na_kernel(rpb_ref, q_ref, k0_ref, k1_ref, k2_ref, k3_ref, v0_ref, v1_ref, v2_ref, v3_ref,
               kc_ref, vc_ref, o_ref, cb_scr, bias_scr, *, rows):
    h0 = pl.program_id(0) * NA_HEADS_PER_STEP
    t = pl.program_id(1)
    n_tiles = rows // NA_QROWS
    n_dr = 2 * NA_ROWS - 1
    n_dc = 2 * NA_COLS - 1
    half = GRID_W

    @pl.when(t == 0)
    def _():
        shape = (GRID_W, 2 * half)
        cq = lax.broadcasted_iota(jnp.int32, shape, 0)
        lane = lax.broadcasted_iota(jnp.int32, shape, 1)
        ck = lane & (half - 1)
        hi = lane >= half
        bidx = jnp.clip(ck - cq + (NA_COLS - 1), 0, n_dc - 1)
        col_ok = (ck - jnp.clip(cq - NA_COLS // 2, 0, GRID_W - NA_COLS))
        col_ok = (col_ok >= 0) & (col_ok < NA_COLS)
        for hh in range(NA_HEADS_PER_STEP):
            for i in range(n_dr + 1):
                val = jnp.full(shape, NEG_INF, F32)
                for e in range(2):
                    a = i - 1 + e
                    if 0 <= a < n_dr:
                        sel = hi if e else jnp.logical_not(hi)
                        for b in range(n_dc):
                            val = jnp.where(sel & (bidx == b), rpb_ref[((h0 + hh) * n_dr + a) * n_dc + b], val)
                cb_scr[hh, i] = jnp.where(col_ok, val, NEG_INF)

    r0 = t * NA_QROWS
    kb0 = jnp.clip(r0 - NA_ROWS // 2, 0, rows - NA_KROWS)

    @pl.when((t <= 1) | (t == n_tiles - 1))
    def _():
        lane = lax.broadcasted_iota(jnp.int32, (GRID_W, 2 * half), 1)
        lo = lane < half
        for rq in range(NA_QROWS):
            rq_abs = r0 + rq
            ws = jnp.clip(rq_abs - NA_ROWS // 2, 0, rows - NA_ROWS)
            for jj in range(NA_KROWS // 2):
                rk0 = kb0 + 2 * jj
                ok0 = (rk0 >= ws) & (rk0 < ws + NA_ROWS)
                ok1 = (rk0 + 1 >= ws) & (rk0 + 1 < ws + NA_ROWS)
                idx = jnp.clip(rk0 - rq_abs + NA_ROWS, 0, n_dr)
                keep = (lo & ok0) | (jnp.logical_not(lo) & ok1)
                for hh in range(NA_HEADS_PER_STEP):
                    bias_scr[hh, rq * GRID_W:(rq + 1) * GRID_W, jj * 2 * half:(jj + 1) * 2 * half] = (
                        jnp.where(keep, cb_scr[hh, idx], NEG_INF))

    outs = []
    for hh in range(NA_HEADS_PER_STEP):
        hd = slice(hh * HEAD_DIM, (hh + 1) * HEAD_DIM)
        q = q_ref[:, hd]
        k = jnp.concatenate([k0_ref[:, hd], k1_ref[:, hd], k2_ref[:, hd], k3_ref[:, hd]], axis=0)
        v = jnp.concatenate([v0_ref[:, hd], v1_ref[:, hd], v2_ref[:, hd], v3_ref[:, hd]], axis=0)
        s = _dot_nt(q, k) + bias_scr[hh]
        s_ctx = _dot_nt(q, kc_ref[:, hd])
        m = jnp.maximum(jnp.max(s, axis=-1, keepdims=True), jnp.max(s_ctx, axis=-1, keepdims=True))
        p = jnp.exp(s - m)
        p_ctx = jnp.exp(s_ctx - m)
        den = jnp.sum(p, axis=-1, keepdims=True) + jnp.sum(p_ctx, axis=-1, keepdims=True)
        o = (jnp.dot(p.astype(BF16), v, preferred_element_type=F32)
             + jnp.dot(p_ctx.astype(BF16), vc_ref[:, hd], preferred_element_type=F32))
        outs.append(o * (1.0 / den))
    o_ref[...] = jnp.concatenate(outs, axis=1).astype(o_ref.dtype)


def _neighbourhood_attention(qkv, qkv_ctx, rpb):
    seq = qkv.shape[0]
    m = qkv_ctx.shape[0]
    rows = seq // GRID_W
    assert rows % NA_QROWS == 0 and rows >= NA_KROWS + NA_QROWS
    n_tiles = rows // NA_QROWS
    tq = NA_QROWS * GRID_W
    tk = NA_KBLK * GRID_W
    n_kblk = NA_KROWS // NA_KBLK

    hp = NA_HEADS_PER_STEP
    wh = hp * HEAD_DIM
    assert NA_HEADS % hp == 0 and QA_BLK % hp == 0 and KA_BLK % hp == 0 and VA_BLK % hp == 0

    def kv_spec(col_blk, i):
        def index(h, t):
            first = jnp.clip(t * (NA_QROWS // NA_KBLK) - 1, 0, rows // NA_KBLK - n_kblk)
            return (first + i, col_blk // hp + h)
        return pl.BlockSpec((tk, wh), index)

    return pl.pallas_call(
        functools.partial(_na_kernel, rows=rows),
        grid=(NA_HEADS // hp, n_tiles),
        in_specs=[pl.BlockSpec(memory_space=pltpu.SMEM),
                  pl.BlockSpec((tq, wh), lambda h, t: (t, QA_BLK // hp + h))]
                 + [kv_spec(KA_BLK, i) for i in range(n_kblk)]
                 + [kv_spec(VA_BLK, i) for i in range(n_kblk)]
                 + [pl.BlockSpec((m, wh), lambda h, t: (0, KA_BLK // hp + h)),
                    pl.BlockSpec((m, wh), lambda h, t: (0, VA_BLK // hp + h))],
        out_specs=pl.BlockSpec((tq, wh), lambda h, t: (t, h)),
        out_shape=jax.ShapeDtypeStruct((seq, A_WIDTH), BF16),
        scratch_shapes=[pltpu.VMEM((hp, 2 * NA_ROWS, GRID_W, 2 * GRID_W), F32),
                        pltpu.VMEM((hp, tq, NA_KROWS * GRID_W), F32)],
        compiler_params=_cparams("arbitrary", "arbitrary"),
        name="neighbourhood_attn",
    )(rpb.reshape(-1), qkv, *([qkv] * (2 * n_kblk)), qkv_ctx, qkv_ctx)


def _ctx_attn_kernel(sink_ref, q_ref, k_ref, v_ref, o_ref):
    j = pl.program_id(0)
    s = _dot_nt(q_ref[...], k_ref[...])
    has_sink = j >= NA_HEADS
    sink = jnp.where(has_sink, sink_ref[jnp.maximum(j - NA_HEADS, 0)], NEG_INF)
    m = jnp.maximum(jnp.max(s, axis=-1, keepdims=True), sink)
    p = jnp.exp(s - m)
    den = jnp.sum(p, axis=-1, keepdims=True) + jnp.where(has_sink, jnp.exp(sink - m), 0.0)
    o = jnp.dot(p.astype(BF16), v_ref[...], preferred_element_type=F32)
    o_ref[...] = (o * (1.0 / den)).astype(o_ref.dtype)


def _context_attention(qkv_ctx, sink):
    m = qkv_ctx.shape[0]

    def q_idx(j):
        return (0, jnp.where(j < NA_HEADS, QA_BLK + j, QB_BLK + j - NA_HEADS))

    def k_idx(j):
        return (0, jnp.where(j < NA_HEADS, KA_BLK + j, KB_BLK + (j - NA_HEADS) // NB_GROUP))

    def v_idx(j):
        return (0, jnp.where(j < NA_HEADS, VA_BLK + j, VB_BLK + (j - NA_HEADS) // NB_GROUP))

    return pl.pallas_call(
        _ctx_attn_kernel,
        grid=(NA_HEADS + NB_Q_HEADS,),
        in_specs=[pl.BlockSpec(memory_space=pltpu.SMEM),
                  pl.BlockSpec((m, HEAD_DIM), q_idx),
                  pl.BlockSpec((m, HEAD_DIM), k_idx),
                  pl.BlockSpec((m, HEAD_DIM), v_idx)],
        out_specs=pl.BlockSpec((m, HEAD_DIM), lambda j: (0, j)),
        out_shape=jax.ShapeDtypeStruct((m, A_WIDTH + B_Q_WIDTH), BF16),
        compiler_params=_cparams("arbitrary"),
        name="context_attn",
    )(sink, qkv_ctx, qkv_ctx, qkv_ctx)


def _out_proj_kernel(a_ref, b_ref, wa_ref, wb_ref, x_ref, g_ref, o_ref):
    y = (jnp.dot(a_ref[...], wa_ref[...], preferred_element_type=F32)
         + jnp.dot(b_ref[...], wb_ref[...], preferred_element_type=F32))
    o_ref[...] = x_ref[...] + g_ref[...] * y


def _out_proj(oa, ob, a_blk, b_blk, w, x, gate, *, tm):
    m, d = x.shape
    kh = w.shape[0] // 2
    return pl.pallas_call(
        _out_proj_kernel,
        grid=(m // tm,),
        in_specs=[pl.BlockSpec((tm, kh), lambda i: (i, a_blk)),
                  pl.BlockSpec((tm, kh), lambda i: (i, b_blk)),
                  pl.BlockSpec((kh, d), lambda i: (0, 0)),
                  pl.BlockSpec((kh, d), lambda i: (1, 0)),
                  pl.BlockSpec((tm, d), lambda i: (i, 0)),
                  pl.BlockSpec((1, d), lambda i: (0, 0))],
        out_specs=pl.BlockSpec((tm, d), lambda i: (i, 0)),
        out_shape=jax.ShapeDtypeStruct((m, d), F32),
        compiler_params=_cparams("arbitrary"),
        name="out_proj",
    )(oa, ob, w, w, x, gate)


def _ffn_kernel(x_ref, g_ref, sc_ref, sh_ref, gate_ref, gf_ref, w1_ref, w3_ref, w2_ref, o_ref, h_scr, act_scr,
                *, final_norm, n_up):
    s = pl.program_id(1)
    tf = w1_ref.shape[1]
    tn = w2_ref.shape[1]

    @pl.when(s == 0)
    def _():
        h_scr[...] = _norm_mod(x_ref[...], g_ref[...], sc_ref[...], sh_ref[...]).astype(BF16)

    @pl.when(s < n_up)
    def _():
        h = h_scr[...]
        a = jnp.dot(h, w1_ref[...], preferred_element_type=F32)
        b = jnp.dot(h, w3_ref[...], preferred_element_type=F32)
        act_scr[:, pl.ds(pl.multiple_of(s * tf, tf), tf)] = (_silu(a) * b).astype(BF16)

    @pl.when(s >= n_up)
    def _():
        cols = pl.ds(pl.multiple_of((s - n_up) * tn, tn), tn)
        y = jnp.dot(act_scr[...], w2_ref[...], preferred_element_type=F32)
        o_ref[:, cols] = x_ref[:, cols] + gate_ref[:, cols] * y

    if final_norm:
        @pl.when(s == pl.num_programs(1) - 1)
        def _():
            r = o_ref[...]
            ms = jnp.mean(r * r, axis=-1, keepdims=True)
            o_ref[...] = r * lax.rsqrt(ms + EPS) * gf_ref[...]


def _ffn(x, g, sc, sh, gate, gf, w1, w3, w2, layer, *, final_norm, tm):
    m, d = x.shape
    dff = w1.shape[-1]
    tf, tn = 512, 256
    assert dff % tf == 0 and d % tn == 0
    n_up, n_dn = dff // tf, d // tn
    vec = pl.BlockSpec((1, d), lambda i, s: (0, 0))
    once = pl.Buffered(1) if m > tm else None
    return pl.pallas_call(
        functools.partial(_ffn_kernel, final_norm=final_norm, n_up=n_up),
        grid=(m // tm, n_up + n_dn),
        in_specs=[pl.BlockSpec((tm, d), lambda i, s: (i, 0)), vec, vec, vec, vec, vec,
                  pl.BlockSpec((None, d, tf), lambda i, s: (layer, 0, jnp.minimum(s, n_up - 1))),
                  pl.BlockSpec((None, d, tf), lambda i, s: (layer, 0, jnp.minimum(s, n_up - 1))),
                  pl.BlockSpec((None, dff, tn), lambda i, s: (layer, 0, jnp.maximum(s - n_up, 0)))],
        out_specs=pl.BlockSpec((tm, d), lambda i, s: (i, 0), pipeline_mode=once),
        out_shape=jax.ShapeDtypeStruct((m, d), F32),
        scratch_shapes=[pltpu.VMEM((tm, d), BF16), pltpu.VMEM((tm, dff), BF16)],
        compiler_params=_cparams("arbitrary", "arbitrary", limit=FFN_VMEM_LIMIT),
        name="swiglu_final" if final_norm else "swiglu",
    )(x, g, sc, sh, gate, gf, w1, w3, w2)


def _chunk_perm(n):
    r = lax.broadcasted_iota(jnp.int32, (n, n), 0)
    c = lax.broadcasted_iota(jnp.int32, (n, n), 1)
    nch = n // S5_CHUNK
    return (c == (r % nch) * S5_CHUNK + r // nch).astype(BF16)


def _field_transpose(v):
    v = list(v)
    lane = lax.broadcasted_iota(jnp.int32, v[0].shape, 1)
    s = FIELDS // 2
    while s:
        keep = ((lane // FIELD) & s) == 0
        for i in range(FIELDS):
            if i & s:
                continue
            a, b = v[i], v[i + s]
            v[i] = jnp.where(keep, a, pltpu.roll(b, FIELD * s, axis=1))
            v[i + s] = jnp.where(keep, pltpu.roll(a, LANES - FIELD * s, axis=1), b)
        s //= 2
    return v


def _s5_input_kernel(x_ref, c_ref, g_ref, scx_ref, shx_ref, scc_ref, shc_ref, o_ref):
    i = pl.program_id(0)
    is_ctx = (i == 0) | (i == pl.num_programs(0) - 1)
    n = x_ref.shape[0]

    def emit(h):
        hp = jnp.dot(_chunk_perm(n), h.astype(BF16), preferred_element_type=F32)
        o_ref[...] = hp.astype(BF16).reshape(o_ref.shape)

    @pl.when(is_ctx)
    def _():
        emit(_norm_mod(c_ref[...], g_ref[...], scc_ref[...], shc_ref[...]))

    @pl.when(jnp.logical_not(is_ctx))
    def _():
        emit(_norm_mod(x_ref[...], g_ref[...], scx_ref[...], shx_ref[...]))


def _s5_input(x, ctx, g, scx, shx, scc, shc):
    seq, d = x.shape
    m = ctx.shape[0]
    assert m % (S5_CHUNK * 8) == 0 and seq % m == 0
    nx = seq // m
    nc = (seq + 2 * m) // S5_CHUNK
    vec = pl.BlockSpec((1, d), lambda i: (0, 0))
    return pl.pallas_call(
        _s5_input_kernel,
        grid=(nx + 2,),
        in_specs=[pl.BlockSpec((m, d), lambda i: (jnp.clip(i - 1, 0, nx - 1), 0)),
                  pl.BlockSpec((m, d), lambda i: (0, 0)), vec, vec, vec, vec, vec],
        out_specs=pl.BlockSpec((S5_CHUNK, m // S5_CHUNK, d), lambda i: (0, i, 0)),
        out_shape=jax.ShapeDtypeStruct((S5_CHUNK, nc, d), BF16),
        compiler_params=_cparams("arbitrary"),
        name="s5_input",
    )(x, ctx, g, scx, shx, scc, shc)


def _sublane_transpose(v):
    n = v[0].shape[0]
    v = [a.reshape(n // 8, 8, LANES) for a in v]
    row = lax.broadcasted_iota(jnp.int32, v[0].shape, 1)
    s = 4
    while s:
        keep = (row & s) == 0
        for i in range(8):
            if i & s:
                continue
            a, b = v[i], v[i + s]
            v[i] = jnp.where(keep, a, pltpu.roll(b, s, axis=1))
            v[i + s] = jnp.where(keep, pltpu.roll(a, 8 - s, axis=1), b)
        s //= 2
    return [a.reshape(n, LANES) for a in v]


def _s5_kernel(z_ref, wm_ref, wb_ref, wc_ref, lam_ref, o_ref, u_scr, b_scr, hf_scr, hb_scr, *, c_lo):
    L, nc, _ = z_ref.shape
    n_out = o_ref.shape[0] // L
    gs = FIELDS
    half = SSM_STATE
    for j in range(L // FIELDS):
        v = [pltpu.bitcast(z_ref[j * FIELDS + i], jnp.uint32) for i in range(FIELDS)]
        w = _field_transpose(v)
        for gi in range(gs):
            u_scr[gi, :, j * LANES:(j + 1) * LANES] = pltpu.bitcast(w[gi], BF16)

    bs = [jnp.dot(u_scr[gi], wb_ref[gi], preferred_element_type=F32) for gi in range(gs)]
    for comp in range(2):
        t = _sublane_transpose([b[:, comp * LANES:(comp + 1) * LANES] for b in bs])
        for j in range(8):
            b_scr[comp, j] = t[j]

    a_re = lam_ref[:, :2 * half]
    a_im = lam_ref[:, 2 * half:]
    fwd = lax.broadcasted_iota(jnp.int32, (gs, 2 * half), 1) < half

    def step(t, carry):
        s_re, s_im = carry
        rf = pl.ds(pl.multiple_of(t * 8, 8), 8)
        rb = pl.ds(pl.multiple_of(nc - 8 - t * 8, 8), 8)
        for jf in range(8):
            jb = 7 - jf
            hf_scr[0, jf, rf, :] = s_re
            hf_scr[1, jf, rf, :] = s_im
            hb_scr[0, jb, rb, :] = s_re
            hb_scr[1, jb, rb, :] = s_im
            v_re = jnp.where(fwd, b_scr[0, jf, rf, :], b_scr[0, jb, rb, :])
            v_im = jnp.where(fwd, b_scr[1, jf, rf, :], b_scr[1, jb, rb, :])
            s_re, s_im = a_re * s_re - a_im * s_im + v_re, a_re * s_im + a_im * s_re + v_im
        return s_re, s_im

    zero = jnp.zeros((gs, 2 * half), F32)
    lax.fori_loop(0, nc // 8, step, (zero, zero))

    fwd_all = lax.broadcasted_iota(jnp.int32, (nc, 2 * half), 1) < half
    h_in = [_sublane_transpose([jnp.where(fwd_all, hf_scr[comp, j], hb_scr[comp, j]) for j in range(8)])
            for comp in range(2)]
    ys = []
    for gi in range(gs):
        hcat = jnp.concatenate([h_in[0][gi], h_in[1][gi]], axis=1)[c_lo:c_lo + n_out].astype(BF16)
        ys.append(jnp.dot(u_scr[gi, c_lo:c_lo + n_out, :], wm_ref[gi], preferred_element_type=F32)
                  + _dot_nt(hcat, wc_ref[gi]))
    for j in range(L // FIELDS):
        w = _field_transpose([y[:, j * LANES:(j + 1) * LANES] for y in ys])
        for i in range(FIELDS):
            o_ref[pl.ds(j * FIELDS + i, n_out, stride=L), :] = w[i]


def _s5_scan(zp, wm, wb, wc, lam, *, c_lo, n_out):
    L, nc, d = zp.shape
    w = wm.shape[-1]
    gs = FIELDS
    assert nc % 8 == 0 and gs == 8
    slab = pltpu.VMEM((2, 8, nc, 2 * SSM_STATE), F32)
    return pl.pallas_call(
        functools.partial(_s5_kernel, c_lo=c_lo),
        grid=(d // LANES,),
        in_specs=[pl.BlockSpec((L, nc, LANES), lambda i: (0, 0, i)),
                  pl.BlockSpec((gs, w, w), lambda i: (i, 0, 0)),
                  pl.BlockSpec((gs, w, w), lambda i: (i, 0, 0)),
                  pl.BlockSpec((gs, w, w), lambda i: (i, 0, 0)),
                  pl.BlockSpec((gs, w), lambda i: (i, 0))],
        out_specs=pl.BlockSpec((L * n_out, LANES), lambda i: (0, i)),
        out_shape=jax.ShapeDtypeStruct((L * n_out, d), F32),
        scratch_shapes=[pltpu.VMEM((gs, nc, w), BF16), slab, slab, slab],
        compiler_params=_cparams("arbitrary"),
        name="s5_scan",
    )(zp, wm, wb, wc, lam)


def _shift_lanes(x, s):
    w = LANES
    x0, x1 = x[:, :w], x[:, w:]
    zero = jnp.zeros_like(x0)
    lane = lax.broadcasted_iota(jnp.int32, x0.shape, 1)
    if s == 0:
        return x
    if s > 0:
        if s >= w:
            t = s - w
            y1 = x0 if t == 0 else jnp.where(lane < t, 0.0, pltpu.roll(x0, t, axis=1))
            return jnp.concatenate([zero, y1], axis=1)
        r0, r1 = pltpu.roll(x0, s, axis=1), pltpu.roll(x1, s, axis=1)
        return jnp.concatenate([jnp.where(lane < s, 0.0, r0), jnp.where(lane < s, r0, r1)], axis=1)
    s = -s
    if s >= w:
        t = s - w
        y0 = x1 if t == 0 else jnp.where(lane >= w - t, 0.0, pltpu.roll(x1, w - t, axis=1))
        return jnp.concatenate([y0, zero], axis=1)
    r0, r1 = pltpu.roll(x0, w - s, axis=1), pltpu.roll(x1, w - s, axis=1)
    return jnp.concatenate([jnp.where(lane >= w - s, r1, r0), jnp.where(lane >= w - s, 0.0, r1)], axis=1)


def _s5_op_kernel(a_re_ref, a_im_ref, ldt_ref, b_re_ref, b_im_ref, c_re_ref, c_im_ref,
                  wm_ref, wb_ref, wc_ref, lam_ref, e_scr):
    gs = a_re_ref.shape[0]
    L = S5_CHUNK
    h = SSM_GROUP
    half = SSM_STATE
    fwd1 = lax.broadcasted_iota(jnp.int32, (1, 2 * half), 1) < half
    fwd = lax.broadcasted_iota(jnp.int32, (h, 2 * half), 1) < half

    def cmul(ar, ai, br, bi):
        return ar * br - ai * bi, ar * bi + ai * br

    for gi in range(gs):
        ar, ai = a_re_ref[gi], a_im_ref[gi]
        dt = jnp.exp(ldt_ref[gi])
        mag = jnp.exp(ar * dt)
        lr, li = mag * jnp.cos(ai * dt), mag * jnp.sin(ai * dt)
        den = ar * ar + ai * ai
        nr = lr - 1.0
        coef_r = (nr * ar + li * ai) / den
        coef_i = (li * ar - nr * ai) / den
        bb_r, bb_i = cmul(coef_r, coef_i, b_re_ref[gi], b_im_ref[gi])
        c_r, c_i = c_re_ref[gi], c_im_ref[gi]
        pw = [(jnp.ones_like(lr), jnp.zeros_like(lr))]
        for _ in range(L):
            pw.append(cmul(pw[-1][0], pw[-1][1], lr, li))

        def mixed(jf, jb):
            return (jnp.where(fwd1, pw[jf][0], pw[jb][0]), jnp.where(fwd1, pw[jf][1], pw[jb][1]))

        for l in range(L):
            s_r, s_i = cmul(*mixed(L - 1 - l, l), bb_r, bb_i)
            wb_ref[gi, l * h:(l + 1) * h, :] = jnp.concatenate([s_r, s_i], axis=1).astype(BF16)
            f_r, f_i = cmul(*mixed(l + 1, L - l), c_r, c_i)
            wc_ref[gi, l * h:(l + 1) * h, :] = jnp.concatenate([f_r, -f_i], axis=1).astype(BF16)
            e_r, e_i = cmul(*mixed(l, L - 1 - l), c_r, c_i)
            e_scr[l * h:(l + 1) * h, :] = jnp.concatenate([e_r, e_i], axis=1)

        lhs = jnp.concatenate(
            [jnp.concatenate([jnp.where(fwd, bb_r, 0.0), jnp.where(fwd, -bb_i, 0.0)], axis=1),
             jnp.concatenate([jnp.where(fwd, 0.0, bb_r), jnp.where(fwd, 0.0, -bb_i)], axis=1)], axis=0)
        kt = _dot_nt(lhs, e_scr[...], precision=lax.Precision.HIGHEST)
        kt_f, kt_b = kt[:h], kt[h:]
        for l in range(L):
            blk = _shift_lanes(kt_f, h * l) + _shift_lanes(kt_b, -h * (L - 1 - l))
            wm_ref[gi, l * h:(l + 1) * h, :] = blk.astype(BF16)
        lam_ref[gi:gi + 1, :] = jnp.concatenate([pw[L][0], pw[L][1]], axis=1)


def _s5_operators(a_re, a_im, log_dt, b_re, b_im, c_re, c_im):
    _, g, p = a_re.shape
    h = b_re.shape[-1]
    gs = FIELDS
    w = S5_CHUNK * h
    assert w == 2 * LANES and 2 * p == LANES

    def lanes(v):
        return jnp.concatenate([v[0], v[1]], axis=-1)

    def chan_rows(v):
        return lanes(jnp.swapaxes(v, -1, -2))

    a_re2 = lanes(a_re).reshape(g, 1, 2 * p)
    a_im2 = lanes(a_im).reshape(g, 1, 2 * p)
    ldt2 = jnp.repeat(log_dt.T, p, axis=1).reshape(g, 1, 2 * p)
    vec = pl.BlockSpec((gs, 1, 2 * p), lambda i: (i, 0, 0))
    mat = pl.BlockSpec((gs, h, 2 * p), lambda i: (i, 0, 0))
    return pl.pallas_call(
        _s5_op_kernel,
        grid=(g // gs,),
        in_specs=[vec, vec, vec, mat, mat, mat, mat],
        out_specs=[pl.BlockSpec((gs, w, w), lambda i: (i, 0, 0)),
                   pl.BlockSpec((gs, w, w), lambda i: (i, 0, 0)),
                   pl.BlockSpec((gs, w, w), lambda i: (i, 0, 0)),
                   pl.BlockSpec((gs, w), lambda i: (i, 0))],
        out_shape=[jax.ShapeDtypeStruct((g, w, w), BF16), jax.ShapeDtypeStruct((g, w, w), BF16),
                   jax.ShapeDtypeStruct((g, w, w), BF16), jax.ShapeDtypeStruct((g, w), F32)],
        scratch_shapes=[pltpu.VMEM((w, w), F32)],
        compiler_params=_cparams("arbitrary"),
        name="s5_operators",
    )(a_re2, a_im2, ldt2, chan_rows(b_re), chan_rows(b_im), lanes(c_re), lanes(c_im))


def _gelu_tanh(y):
    k = 2.0 * math.sqrt(2.0 / math.pi) * math.log2(math.e)
    return y * (1.0 / (1.0 + jnp.exp2(y * (-k - (k * 0.044715) * (y * y)))))


def _glu_kernel(x_ref, ys_ref, g_ref, sc_ref, sh_ref, d_ref, gate_ref, b_ref, w_ref, o_ref, act_scr):
    j = pl.program_id(1)
    tn = o_ref.shape[1]
    d_model = x_ref.shape[1]

    @pl.when(j == 0)
    def _():
        rb = 32

        def rows_pass(r, carry):
            rows = pl.ds(pl.multiple_of(r * rb, rb), rb)
            hx = _norm_mod(x_ref[rows, :], g_ref[...], sc_ref[...], sh_ref[...])
            act_scr[rows, :] = _gelu_tanh(d_ref[...] * hx + ys_ref[rows, :]).astype(BF16)
            return carry

        lax.fori_loop(0, x_ref.shape[0] // rb, rows_pass, 0, unroll=4)

    act = act_scr[...]
    cols = pl.ds(pl.multiple_of(j * tn, tn), tn)
    gcols = pl.ds(pl.multiple_of(d_model + j * tn, tn), tn)
    val = jnp.dot(act, w_ref[:, cols], preferred_element_type=F32) + b_ref[:, cols]
    gate = jnp.dot(act, w_ref[:, gcols], preferred_element_type=F32) + b_ref[:, gcols]
    o_ref[...] = x_ref[:, cols] + gate_ref[:, cols] * (val * (1.0 / (1.0 + jnp.exp(-gate))))


def _glu(x, ys, g, sc, sh, d_skip, gate, w, b, *, tm):
    m, d = x.shape
    tn = 512
    nj = d // tn
    vec = pl.BlockSpec((1, d), lambda i, j: (0, 0))
    return pl.pallas_call(
        _glu_kernel,
        grid=(m // tm, nj),
        in_specs=[pl.BlockSpec((tm, d), lambda i, j: (i, 0)),
                  pl.BlockSpec((tm, d), lambda i, j: (i, 0)), vec, vec, vec, vec, vec,
                  pl.BlockSpec((1, 2 * d), lambda i, j: (0, 0)),
                  pl.BlockSpec((d, 2 * d), lambda i, j: (0, 0), pipeline_mode=pl.Buffered(1))],
        out_specs=pl.BlockSpec((tm, tn), lambda i, j: (i, j)),
        out_shape=jax.ShapeDtypeStruct((m, d), F32),
        scratch_shapes=[pltpu.VMEM((tm, d), BF16)],
        compiler_params=_cparams("arbitrary", "arbitrary"),
        name="s5_glu",
    )(x, ys, g, sc, sh, d_skip, gate, b, w)


def kernel(x, c, ctx, c_ctx, ada_w, ada_b, norm_mix, norm_ffn, ffn_w1, ffn_w3, ffn_w2, attn_w_in, attn_w_out,
           attn_rpb, attn_sink, ssm_a_re, ssm_a_im, ssm_log_dt, ssm_b_re, ssm_b_im, ssm_c_re, ssm_c_im,
           ssm_d, ssm_w_glu, ssm_b_glu, norm_final):
    batch, seq, d = x.shape
    assert batch == 1 and ada_w.shape[0] == 2
    m = ctx.shape[1]
    xs, cs = x[0], ctx[0]
    mods = _ada_mod(c, c_ctx, ada_w, ada_b)

    def mod(layer, who):
        return [mods[layer, who, i * d:(i + 1) * d].reshape(1, d) for i in range(6)]

    row = lambda v: v.reshape(1, d)
    tm = 512
    tm_qkv = 1024
    tm_ffn = 1024

    sh1, sc1, g1, sh2, sc2, g2 = mod(0, 0)
    csh1, csc1, cg1, csh2, csc2, cg2 = mod(0, 1)
    cos, sin = _rope_tables(seq)
    w_in = attn_w_in[0].astype(BF16)
    w_out = attn_w_out[0].astype(BF16)
    w1, w3, w2 = ffn_w1.astype(BF16), ffn_w3.astype(BF16), ffn_w2.astype(BF16)
    nm, nf = row(norm_mix[0]), row(norm_ffn[0])
    qkv = _qkv_proj(xs, nm, sc1, sh1, w_in, cos, sin, rope=True, tm=tm_qkv)
    qkv_c = _qkv_proj(cs, nm, csc1, csh1, w_in, cos[:m], sin[:m], rope=False, tm=m)
    oa = _neighbourhood_attention(qkv, qkv_c, attn_rpb[0])
    ob = _window_attention(qkv, qkv_c, attn_sink[0])
    oc = _context_attention(qkv_c, attn_sink[0])
    xs = _out_proj(oa, ob, 0, 0, w_out, xs, g1, tm=tm)
    cs = _out_proj(oc, oc, 0, 1, w_out, cs, cg1, tm=m)
    xs = _ffn(xs, nf, sc2, sh2, g2, nf, w1, w3, w2, 0, final_norm=False, tm=tm_ffn)
    cs = _ffn(cs, nf, csc2, csh2, cg2, nf, w1, w3, w2, 0, final_norm=False, tm=m)

    sh1, sc1, g1, sh2, sc2, g2 = mod(1, 0)
    csh1, csc1, _, _, _, _ = mod(1, 1)
    nm, nf = row(norm_mix[1]), row(norm_ffn[1])
    z = _s5_input(xs, cs, nm, sc1, sh1, csc1, csh1)
    wm, wb, wc, lam = _s5_operators(ssm_a_re[0], ssm_a_im[0], ssm_log_dt[0], ssm_b_re[0], ssm_b_im[0],
                                    ssm_c_re[0], ssm_c_im[0])
    ys = _s5_scan(z, wm, wb, wc, lam, c_lo=m // S5_CHUNK, n_out=seq // S5_CHUNK)
    xs = _glu(xs, ys, nm, sc1, sh1, row(ssm_d[0]), g1, ssm_w_glu[0].astype(BF16), ssm_b_glu[0].reshape(1, 2 * d),
              tm=tm)
    xs = _ffn(xs, nf, sc2, sh2, g2, row(norm_final), w1, w3, w2, 1, final_norm=True, tm=tm_ffn)
    return xs[None]
```

```python
import functools
import math

import jax
import jax.numpy as jnp
import numpy as np
from jax import lax
from jax.experimental import pallas as pl
from jax.experimental.pallas import tpu as pltpu

F32 = jnp.float32
BF16 = jnp.bfloat16

GRID_W = 64
HEAD_DIM = 128
NA_HEADS = 8
NB_Q_HEADS = 8
NB_KV_HEADS = 2
NB_GROUP = NB_Q_HEADS // NB_KV_HEADS
NA_ROWS = 8
NA_COLS = 16
SW_BLOCK = 128
SW_QBLOCKS = 4
ROPE_BASE = 10000.0
SSM_GROUP = 16
SSM_STATE = 64
EPS = 1e-6
NEG_INF = -1e30
ATTN_SCALE = HEAD_DIM ** -0.5

A_WIDTH = NA_HEADS * HEAD_DIM
B_Q_WIDTH = NB_Q_HEADS * HEAD_DIM
B_KV_WIDTH = NB_KV_HEADS * HEAD_DIM
QA_BLK = 0
KA_BLK = NA_HEADS
VA_BLK = 2 * NA_HEADS
QB_BLK = 3 * NA_HEADS
KB_BLK = QB_BLK + NB_Q_HEADS
VB_BLK = KB_BLK + NB_KV_HEADS

NA_QROWS = 8
NA_KROWS = 16
NA_KBLK = 4
NA_HEADS_PER_STEP = 8
PROLOGUE_PARTS = 8
S5_CHUNK = 16
LANES = 128
FIELD = SSM_GROUP
FIELDS = LANES // FIELD
VMEM_LIMIT = 56 * 1024 * 1024
FFN_VMEM_LIMIT = 62 * 1024 * 1024


def _cparams(*sem, limit=VMEM_LIMIT):
    return pltpu.CompilerParams(dimension_semantics=sem, vmem_limit_bytes=limit)


def _silu(v):
    return v * (1.0 / (1.0 + jnp.exp(-v)))


def _norm_mod(x, g, sc, sh):
    ms = jnp.mean(x * x, axis=-1, keepdims=True)
    return (x * lax.rsqrt(ms + EPS)) * (g * (1.0 + sc)) + sh


def _dot_nt(a, b, precision=None):
    return lax.dot_general(a, b, (((1,), (1,)), ((), ())), preferred_element_type=F32, precision=precision)


def _ada_kernel(cb_ref, w_ref, b_ref, o_ref, s_scr, *, rows_per_step):
    d, tn = w_ref.shape
    rep = tn // 128

    @pl.when((pl.program_id(0) == 0) & (pl.program_id(1) == 0))
    def _():
        s_scr[...] = _silu(cb_ref[...])

    def body(i, acc):
        r = pl.multiple_of(i * rows_per_step, rows_per_step)
        w = w_ref[pl.ds(r, rows_per_step), :]
        out = []
        for v in range(2):
            s = s_scr[v, pl.ds(r, rows_per_step), :]
            st = jnp.concatenate([s] * rep, axis=1)
            out.append(acc[v] + jnp.sum((w * st).reshape(rows_per_step // 8, 8, tn), axis=0))
        return tuple(out)

    zero = jnp.zeros((8, tn), F32)
    acc = lax.fori_loop(0, d // rows_per_step, body, (zero, zero))
    o_ref[...] = jnp.concatenate([jnp.sum(a, axis=0, keepdims=True) for a in acc], axis=0) + b_ref[...]


def _ada_mod(c, c_ctx, ada_w, ada_b):
    depth, d, n = ada_w.shape
    tn = 1024
    cb = jnp.stack([jnp.broadcast_to(c.reshape(d, 1), (d, 128)),
                    jnp.broadcast_to(c_ctx.reshape(d, 1), (d, 128))])
    return pl.pallas_call(
        functools.partial(_ada_kernel, rows_per_step=64),
        grid=(depth, n // tn),
        in_specs=[pl.BlockSpec((2, d, 128), lambda l, j: (0, 0, 0)),
                  pl.BlockSpec((None, d, tn), lambda l, j: (l, 0, j)),
                  pl.BlockSpec((None, 1, tn), lambda l, j: (l, 0, j))],
        out_specs=pl.BlockSpec((None, 2, tn), lambda l, j: (l, 0, j)),
        out_shape=jax.ShapeDtypeStruct((depth, 2, n), F32),
        scratch_shapes=[pltpu.VMEM((2, d, 128), F32)],
        compiler_params=_cparams("arbitrary", "arbitrary"),
        name="ada_mod",
    )(cb, ada_w, ada_b.reshape(depth, 1, n))


def _rope(a, cos, sin):
    lane = lax.broadcasted_iota(jnp.int32, a.shape, 1)
    partner = jnp.where((lane & 63) < 32, pltpu.roll(a, 96, axis=1), pltpu.roll(a, 32, axis=1))
    return a * cos + partner * sin


def _qkv_kernel(x0_ref, xn_ref, g_ref, sc_ref, sh_ref, w_ref, cos_ref, sin_ref, o_ref, h0_scr, h1_scr, acc_scr,
                *, rope):
    i = pl.program_id(0)
    j = pl.program_id(1)
    tm, tn = o_ref.shape
    heads = tn // HEAD_DIM
    part = tm // PROLOGUE_PARTS

    @pl.when((i == 0) & (j == 0))
    def _():
        h0_scr[...] = _norm_mod(x0_ref[...], g_ref[...], sc_ref[...], sh_ref[...]).astype(BF16)

    def step(h_cur, h_nxt):
        rows = pl.ds(pl.multiple_of(jnp.minimum(j, PROLOGUE_PARTS - 1) * part, part), part)
        h_nxt[rows, :] = _norm_mod(xn_ref[rows, :], g_ref[...], sc_ref[...], sh_ref[...]).astype(BF16)
        acc = jnp.dot(h_cur[...], w_ref[...], preferred_element_type=F32)
        acc_scr[...] = acc
        o_ref[...] = (acc * jnp.where(is_q, ATTN_SCALE, 1.0)).astype(o_ref.dtype)

    col0 = j * heads
    is_qb = (col0 >= QB_BLK) & (col0 < KB_BLK)
    is_kb = col0 == KB_BLK
    is_q = (col0 < KA_BLK) | is_qb

    @pl.when(i % 2 == 0)
    def _():
        step(h0_scr, h1_scr)

    @pl.when(i % 2 == 1)
    def _():
        step(h1_scr, h0_scr)

    if rope:
        def rotated(n_heads, scale):
            parts = []
            for hh in range(heads):
                a = acc_scr[:, hh * HEAD_DIM:(hh + 1) * HEAD_DIM]
                if hh < n_heads:
                    a = _rope(a, cos_ref[...], sin_ref[...]) * scale
                parts.append(a)
            return jnp.concatenate(parts, axis=1)

        @pl.when(is_qb)
        def _():
            o_ref[...] = rotated(heads, ATTN_SCALE).astype(o_ref.dtype)

        @pl.when(is_kb)
        def _():
            o_ref[...] = rotated(NB_KV_HEADS, 1.0).astype(o_ref.dtype)


def _qkv_proj(x, g, sc, sh, w, cos, sin, *, rope, tm):
    m, d = x.shape
    n = w.shape[1]
    tn = 512
    nt = m // tm
    assert KB_BLK % (tn // HEAD_DIM) == 0 and m % tm == 0 and n % tn == 0
    assert n // tn >= PROLOGUE_PARTS and tm % (16 * PROLOGUE_PARTS) == 0
    vec = pl.BlockSpec((1, d), lambda i, j: (0, 0))
    tab = pl.BlockSpec((tm, HEAD_DIM), lambda i, j: (i, 0))
    return pl.pallas_call(
        functools.partial(_qkv_kernel, rope=rope),
        grid=(nt, n // tn),
        in_specs=[pl.BlockSpec((tm, d), lambda i, j: (0, 0), pipeline_mode=pl.Buffered(1)),
                  pl.BlockSpec((tm, d), lambda i, j: (jnp.minimum(i + 1, nt - 1), 0)), vec, vec, vec,
                  pl.BlockSpec((d, tn), lambda i, j: (0, j)), tab, tab],
        out_specs=pl.BlockSpec((tm, tn), lambda i, j: (i, j)),
        out_shape=jax.ShapeDtypeStruct((m, n), BF16),
        scratch_shapes=[pltpu.VMEM((tm, d), BF16), pltpu.VMEM((tm, d), BF16), pltpu.VMEM((tm, tn), F32)],
        compiler_params=_cparams("arbitrary", "arbitrary"),
        name="qkv_rope" if rope else "qkv_ctx",
    )(x, x, g, sc, sh, w, cos, sin)


def _rope_tables(seq):
    quarter = HEAD_DIM // 4
    rows = seq // GRID_W
    f32 = np.float32
    inv_freq = (f32(ROPE_BASE) ** (-np.arange(quarter, dtype=f32) / f32(quarter))).astype(f32)
    ang_r = np.arange(rows, dtype=f32)[:, None] * inv_freq[None, :]
    ang_c = np.arange(GRID_W, dtype=f32)[:, None] * inv_freq[None, :]

    def tokens(tab_r, tab_c, sign):
        r = np.broadcast_to(tab_r[:, None, :], (rows, GRID_W, quarter))
        c = np.broadcast_to(tab_c[None, :, :], (rows, GRID_W, quarter))
        return jnp.asarray(np.concatenate([sign * r, r, sign * c, c], axis=-1).reshape(seq, HEAD_DIM), dtype=F32)

    return tokens(np.cos(ang_r), np.cos(ang_c), f32(1.0)), tokens(np.sin(ang_r), np.sin(ang_c), f32(-1.0))


def _win_kernel(sink_ref, q_ref, *refs, seq):
    nkv = SW_QBLOCKS + 2
    k_refs, v_refs = refs[:nkv], refs[nkv:2 * nkv]
    kc_ref, vc_ref, o_ref, valid_scr = refs[2 * nkv:]
    n = pl.program_id(0)
    blk = SW_BLOCK
    nb = seq // blk
    shape = (NB_GROUP * blk, 3 * blk)

    @pl.when(n == 0)
    def _():
        row = lax.broadcasted_iota(jnp.int32, shape, 0)
        col = lax.broadcasted_iota(jnp.int32, shape, 1)
        rel = col - (row & (blk - 1))
        ok = (rel >= 0) & (rel <= 2 * blk)
        for first in range(2):
            for last in range(2):
                okb = ok
                if first:
                    okb = okb & (col >= blk)
                if last:
                    okb = okb & (col < 2 * blk)
                valid_scr[first + 2 * last] = okb.astype(F32)

    grp = lax.broadcasted_iota(jnp.int32, (NB_GROUP * blk, 1), 0) // blk
    for b in range(SW_QBLOCKS):
        gb = n * SW_QBLOCKS + b
        pattern = (gb == 0).astype(jnp.int32) + 2 * (gb == nb - 1).astype(jnp.int32)
        valid = valid_scr[pattern] > 0.5
        rows = slice(b * blk, (b + 1) * blk)
        outs = []
        for h in range(NB_KV_HEADS):
            hd = slice(h * HEAD_DIM, (h + 1) * HEAD_DIM)
            qs = jnp.concatenate([q_ref[rows, (h * NB_GROUP + g) * HEAD_DIM:(h * NB_GROUP + g + 1) * HEAD_DIM]
                                  for g in range(NB_GROUP)], axis=0)
            k = jnp.concatenate([r[:, hd] for r in k_refs[b:b + 3]], axis=0)
            v = jnp.concatenate([r[:, hd] for r in v_refs[b:b + 3]], axis=0)
            s = jnp.where(valid, _dot_nt(qs, k), NEG_INF)
            s_ctx = _dot_nt(qs, kc_ref[:, hd])
            sink = jnp.zeros((NB_GROUP * blk, 1), F32)
            for g in range(NB_GROUP):
                sink = jnp.where(grp == g, sink_ref[h * NB_GROUP + g], sink)
            m = jnp.maximum(jnp.maximum(jnp.max(s, axis=-1, keepdims=True),
                                        jnp.max(s_ctx, axis=-1, keepdims=True)), sink)
            p = jnp.exp(s - m)
            p_ctx = jnp.exp(s_ctx - m)
            den = jnp.sum(p, axis=-1, keepdims=True) + jnp.sum(p_ctx, axis=-1, keepdims=True) + jnp.exp(sink - m)
            o = (jnp.dot(p.astype(BF16), v, preferred_element_type=F32)
                 + jnp.dot(p_ctx.astype(BF16), vc_ref[:, hd], preferred_element_type=F32))
            o = o * (1.0 / den)
            outs += [o[g * blk:(g + 1) * blk] for g in range(NB_GROUP)]
        o_ref[rows, :] = jnp.concatenate(outs, axis=1).astype(o_ref.dtype)


def _window_attention(qkv, qkv_ctx, sink):
    seq = qkv.shape[0]
    m = qkv_ctx.shape[0]
    nb = seq // SW_BLOCK
    assert QB_BLK % NB_Q_HEADS == 0 and KB_BLK % NB_KV_HEADS == 0 and VB_BLK % NB_KV_HEADS == 0
    assert nb % SW_QBLOCKS == 0
    nkv = SW_QBLOCKS + 2

    def kv_spec(col_blk, shift):
        return pl.BlockSpec((SW_BLOCK, B_KV_WIDTH),
                            lambda n: (jnp.clip(n * SW_QBLOCKS + shift, 0, nb - 1), col_blk // NB_KV_HEADS))

    return pl.pallas_call(
        functools.partial(_win_kernel, seq=seq),
        grid=(nb // SW_QBLOCKS,),
        in_specs=[pl.BlockSpec(memory_space=pltpu.SMEM),
                  pl.BlockSpec((SW_QBLOCKS * SW_BLOCK, B_Q_WIDTH), lambda n: (n, QB_BLK // NB_Q_HEADS))]
                 + [kv_spec(KB_BLK, i - 1) for i in range(nkv)]
                 + [kv_spec(VB_BLK, i - 1) for i in range(nkv)]
                 + [pl.BlockSpec((m, B_KV_WIDTH), lambda n: (0, KB_BLK // NB_KV_HEADS)),
                    pl.BlockSpec((m, B_KV_WIDTH), lambda n: (0, VB_BLK // NB_KV_HEADS))],
        out_specs=pl.BlockSpec((SW_QBLOCKS * SW_BLOCK, B_Q_WIDTH), lambda n: (n, 0)),
        out_shape=jax.ShapeDtypeStruct((seq, B_Q_WIDTH), BF16),
        scratch_shapes=[pltpu.VMEM((4, NB_GROUP * SW_BLOCK, 3 * SW_BLOCK), F32)],
        compiler_params=_cparams("arbitrary"),
        name="window_attn",
    )(sink, qkv, *([qkv] * (2 * nkv)), qkv_ctx, qkv_ctx)


def _na_kernel(rpb_ref, q_ref, k0_ref, k1_ref, k2_ref, k3_ref, v0_ref, v1_ref, v2_ref, v3_ref,
               kc_ref, vc_ref, o_ref, cb_scr, bias_scr, *, rows):
    h0 = pl.program_id(0) * NA_HEADS_PER_STEP
    t = pl.program_id(1)
    n_tiles = rows // NA_QROWS
    n_dr = 2 * NA_ROWS - 1
    n_dc = 2 * NA_COLS - 1
    half = GRID_W

    @pl.when(t == 0)
    def _():
        shape = (GRID_W, 2 * half)
        cq = lax.broadcasted_iota(jnp.int32, shape, 0)
        lane = lax.broadcasted_iota(jnp.int32, shape, 1)
        ck = lane & (half - 1)
        hi = lane >= half
        bidx = jnp.clip(ck - cq + (NA_COLS - 1), 0, n_dc - 1)
        col_ok = (ck - jnp.clip(cq - NA_COLS // 2, 0, GRID_W - NA_COLS))
        col_ok = (col_ok >= 0) & (col_ok < NA_COLS)
        for hh in range(NA_HEADS_PER_STEP):
            for i in range(n_dr + 1):
                val = jnp.full(shape, NEG_INF, F32)
                for e in range(2):
                    a = i - 1 + e
                    if 0 <= a < n_dr:
                        sel = hi if e else jnp.logical_not(hi)
                        for b in range(n_dc):
                            val = jnp.where(sel & (bidx == b), rpb_ref[((h0 + hh) * n_dr + a) * n_dc + b], val)
                cb_scr[hh, i] = jnp.where(col_ok, val, NEG_INF)

    r0 = t * NA_QROWS
    kb0 = jnp.clip(r0 - NA_ROWS // 2, 0, rows - NA_KROWS)

    @pl.when((t <= 1) | (t == n_tiles - 1))
    def _():
        lane = lax.broadcasted_iota(jnp.int32, (GRID_W, 2 * half), 1)
        lo = lane < half
        for rq in range(NA_QROWS):
            rq_abs = r0 + rq
            ws = jnp.clip(rq_abs - NA_ROWS // 2, 0, rows - NA_ROWS)
            for jj in range(NA_KROWS // 2):
                rk0 = kb0 + 2 * jj
                ok0 = (rk0 >= ws) & (rk0 < ws + NA_ROWS)
                ok1 = (rk0 + 1 >= ws) & (rk0 + 1 < ws + NA_ROWS)
                idx = jnp.clip(rk0 - rq_abs + NA_ROWS, 0, n_dr)
                keep = (lo & ok0) | (jnp.logical_not(lo) & ok1)
                for hh in range(NA_HEADS_PER_STEP):
                    bias_scr[hh, rq * GRID_W:(rq + 1) * GRID_W, jj * 2 * half:(jj + 1) * 2 * half] = (
                        jnp.where(keep, cb_scr[hh, idx], NEG_INF))

    outs = []
    for hh in range(NA_HEADS_PER_STEP):
        hd = slice(hh * HEAD_DIM, (hh + 1) * HEAD_DIM)
        q = q_ref[:, hd]
        k = jnp.concatenate([k0_ref[:, hd], k1_ref[:, hd], k2_ref[:, hd], k3_ref[:, hd]], axis=0)
        v = jnp.concatenate([v0_ref[:, hd], v1_ref[:, hd], v2_ref[:, hd], v3_ref[:, hd]], axis=0)
        s = _dot_nt(q, k) + bias_scr[hh]
        s_ctx = _dot_nt(q, kc_ref[:, hd])
        m = jnp.maximum(jnp.max(s, axis=-1, keepdims=True), jnp.max(s_ctx, axis=-1, keepdims=True))
        p = jnp.exp(s - m)
        p_ctx = jnp.exp(s_ctx - m)
        den = jnp.sum(p, axis=-1, keepdims=True) + jnp.sum(p_ctx, axis=-1, keepdims=True)
        o = (jnp.dot(p.astype(BF16), v, preferred_element_type=F32)
             + jnp.dot(p_ctx.astype(BF16), vc_ref[:, hd], preferred_element_type=F32))
        outs.append(o * (1.0 / den))
    o_ref[...] = jnp.concatenate(outs, axis=1).astype(o_ref.dtype)


def _neighbourhood_attention(qkv, qkv_ctx, rpb):
    seq = qkv.shape[0]
    m = qkv_ctx.shape[0]
    rows = seq // GRID_W
    assert rows % NA_QROWS == 0 and rows >= NA_KROWS + NA_QROWS
    n_tiles = rows // NA_QROWS
    tq = NA_QROWS * GRID_W
    tk = NA_KBLK * GRID_W
    n_kblk = NA_KROWS // NA_KBLK

    hp = NA_HEADS_PER_STEP
    wh = hp * HEAD_DIM
    assert NA_HEADS % hp == 0 and QA_BLK % hp == 0 and KA_BLK % hp == 0 and VA_BLK % hp == 0

    def kv_spec(col_blk, i):
        def index(h, t):
            first = jnp.clip(t * (NA_QROWS // NA_KBLK) - 1, 0, rows // NA_KBLK - n_kblk)
            return (first + i, col_blk // hp + h)
        return pl.BlockSpec((tk, wh), index)

    return pl.pallas_call(
        functools.partial(_na_kernel, rows=rows),
        grid=(NA_HEADS // hp, n_tiles),
        in_specs=[pl.BlockSpec(memory_space=pltpu.SMEM),
                  pl.BlockSpec((tq, wh), lambda h, t: (t, QA_BLK // hp + h))]
                 + [kv_spec(KA_BLK, i) for i in range(n_kblk)]
                 + [kv_spec(VA_BLK, i) for i in range(n_kblk)]
                 + [pl.BlockSpec((m, wh), lambda h, t: (0, KA_BLK // hp + h)),
                    pl.BlockSpec((m, wh), lambda h, t: (0, VA_BLK // hp + h))],
        out_specs=pl.BlockSpec((tq, wh), lambda h, t: (t, h)),
        out_shape=jax.ShapeDtypeStruct((seq, A_WIDTH), BF16),
        scratch_shapes=[pltpu.VMEM((hp, 2 * NA_ROWS, GRID_W, 2 * GRID_W), F32),
                        pltpu.VMEM((hp, tq, NA_KROWS * GRID_W), F32)],
        compiler_params=_cparams("arbitrary", "arbitrary"),
        name="neighbourhood_attn",
    )(rpb.reshape(-1), qkv, *([qkv] * (2 * n_kblk)), qkv_ctx, qkv_ctx)


def _ctx_attn_kernel(sink_ref, q_ref, k_ref, v_ref, o_ref):
    j = pl.program_id(0)
    s = _dot_nt(q_ref[...], k_ref[...])
    has_sink = j >= NA_HEADS
    sink = jnp.where(has_sink, sink_ref[jnp.maximum(j - NA_HEADS, 0)], NEG_INF)
    m = jnp.maximum(jnp.max(s, axis=-1, keepdims=True), sink)
    p = jnp.exp(s - m)
    den = jnp.sum(p, axis=-1, keepdims=True) + jnp.where(has_sink, jnp.exp(sink - m), 0.0)
    o = jnp.dot(p.astype(BF16), v_ref[...], preferred_element_type=F32)
    o_ref[...] = (o * (1.0 / den)).astype(o_ref.dtype)


def _context_attention(qkv_ctx, sink):
    m = qkv_ctx.shape[0]

    def q_idx(j):
        return (0, jnp.where(j < NA_HEADS, QA_BLK + j, QB_BLK + j - NA_HEADS))

    def k_idx(j):
        return (0, jnp.where(j < NA_HEADS, KA_BLK + j, KB_BLK + (j - NA_HEADS) // NB_GROUP))

    def v_idx(j):
        return (0, jnp.where(j < NA_HEADS, VA_BLK + j, VB_BLK + (j - NA_HEADS) // NB_GROUP))

    return pl.pallas_call(
        _ctx_attn_kernel,
        grid=(NA_HEADS + NB_Q_HEADS,),
        in_specs=[pl.BlockSpec(memory_space=pltpu.SMEM),
                  pl.BlockSpec((m, HEAD_DIM), q_idx),
                  pl.BlockSpec((m, HEAD_DIM), k_idx),
                  pl.BlockSpec((m, HEAD_DIM), v_idx)],
        out_specs=pl.BlockSpec((m, HEAD_DIM), lambda j: (0, j)),
        out_shape=jax.ShapeDtypeStruct((m, A_WIDTH + B_Q_WIDTH), BF16),
        compiler_params=_cparams("arbitrary"),
        name="context_attn",
    )(sink, qkv_ctx, qkv_ctx, qkv_ctx)


def _out_proj_kernel(a_ref, b_ref, wa_ref, wb_ref, x_ref, g_ref, o_ref):
    y = (jnp.dot(a_ref[...], wa_ref[...], preferred_element_type=F32)
         + jnp.dot(b_ref[...], wb_ref[...], preferred_element_type=F32))
    o_ref[...] = x_ref[...] + g_ref[...] * y


def _out_proj(oa, ob, a_blk, b_blk, w, x, gate, *, tm):
    m, d = x.shape
    kh = w.shape[0] // 2
    return pl.pallas_call(
        _out_proj_kernel,
        grid=(m // tm,),
        in_specs=[pl.BlockSpec((tm, kh), lambda i: (i, a_blk)),
                  pl.BlockSpec((tm, kh), lambda i: (i, b_blk)),
                  pl.BlockSpec((kh, d), lambda i: (0, 0)),
                  pl.BlockSpec((kh, d), lambda i: (1, 0)),
                  pl.BlockSpec((tm, d), lambda i: (i, 0)),
                  pl.BlockSpec((1, d), lambda i: (0, 0))],
        out_specs=pl.BlockSpec((tm, d), lambda i: (i, 0)),
        out_shape=jax.ShapeDtypeStruct((m, d), F32),
        compiler_params=_cparams("arbitrary"),
        name="out_proj",
    )(oa, ob, w, w, x, gate)


def _ffn_kernel(x_ref, g_ref, sc_ref, sh_ref, gate_ref, gf_ref, w1_ref, w3_ref, w2_ref, o_ref, h_scr, act_scr,
                *, final_norm, n_up):
    s = pl.program_id(1)
    tf = w1_ref.shape[1]
    tn = w2_ref.shape[1]

    @pl.when(s == 0)
    def _():
        h_scr[...] = _norm_mod(x_ref[...], g_ref[...], sc_ref[...], sh_ref[...]).astype(BF16)

    @pl.when(s < n_up)
    def _():
        h = h_scr[...]
        a = jnp.dot(h, w1_ref[...], preferred_element_type=F32)
        b = jnp.dot(h, w3_ref[...], preferred_element_type=F32)
        act_scr[:, pl.ds(pl.multiple_of(s * tf, tf), tf)] = (_silu(a) * b).astype(BF16)

    @pl.when(s >= n_up)
    def _():
        cols = pl.ds(pl.multiple_of((s - n_up) * tn, tn), tn)
        y = jnp.dot(act_scr[...], w2_ref[...], preferred_element_type=F32)
        r = x_ref[:, cols] + gate_ref[:, cols] * y
        if final_norm:
            o_ref[:, cols] = r
        else:
            o_ref[...] = r

    if final_norm:
        @pl.when(s == pl.num_programs(1) - 1)
        def _():
            r = o_ref[...]
            ms = jnp.mean(r * r, axis=-1, keepdims=True)
            o_ref[...] = r * lax.rsqrt(ms + EPS) * gf_ref[...]


def _ffn(x, g, sc, sh, gate, gf, w1, w3, w2, layer, *, final_norm, tm):
    m, d = x.shape
    dff = w1.shape[-1]
    tf, tn = 512, 256
    assert dff % tf == 0 and d % tn == 0
    n_up, n_dn = dff // tf, d // tn
    vec = pl.BlockSpec((1, d), lambda i, s: (0, 0))
    once = pl.Buffered(1) if m > tm else None
    return pl.pallas_call(
        functools.partial(_ffn_kernel, final_norm=final_norm, n_up=n_up),
        grid=(m // tm, n_up + n_dn),
        in_specs=[pl.BlockSpec((tm, d), lambda i, s: (i, 0)), vec, vec, vec, vec, vec,
                  pl.BlockSpec((None, d, tf), lambda i, s: (layer, 0, jnp.minimum(s, n_up - 1))),
                  pl.BlockSpec((None, d, tf), lambda i, s: (layer, 0, jnp.minimum(s, n_up - 1))),
                  pl.BlockSpec((None, dff, tn), lambda i, s: (layer, 0, jnp.maximum(s - n_up, 0)))],
        out_specs=(pl.BlockSpec((tm, d), lambda i, s: (i, 0), pipeline_mode=once) if final_norm else
                   pl.BlockSpec((tm, tn), lambda i, s: (i, jnp.maximum(s - n_up, 0)))),
        out_shape=jax.ShapeDtypeStruct((m, d), F32),
        scratch_shapes=[pltpu.VMEM((tm, d), BF16), pltpu.VMEM((tm, dff), BF16)],
        compiler_params=_cparams("arbitrary", "arbitrary", limit=FFN_VMEM_LIMIT),
        name="swiglu_final" if final_norm else "swiglu",
    )(x, g, sc, sh, gate, gf, w1, w3, w2)


def _chunk_perm(n):
    r = lax.broadcasted_iota(jnp.int32, (n, n), 0)
    c = lax.broadcasted_iota(jnp.int32, (n, n), 1)
    nch = n // S5_CHUNK
    return (c == (r % nch) * S5_CHUNK + r // nch).astype(BF16)


def _field_transpose(v):
    v = list(v)
    lane = lax.broadcasted_iota(jnp.int32, v[0].shape, 1)
    s = FIELDS // 2
    while s:
        keep = ((lane // FIELD) & s) == 0
        for i in range(FIELDS):
            if i & s:
                continue
            a, b = v[i], v[i + s]
            v[i] = jnp.where(keep, a, pltpu.roll(b, FIELD * s, axis=1))
            v[i + s] = jnp.where(keep, pltpu.roll(a, LANES - FIELD * s, axis=1), b)
        s //= 2
    return v


def _s5_input_kernel(x_ref, c_ref, g_ref, scx_ref, shx_ref, scc_ref, shc_ref, o_ref):
    i = pl.program_id(0)
    is_ctx = (i == 0) | (i == pl.num_programs(0) - 1)
    n = x_ref.shape[0]

    def emit(h):
        hp = jnp.dot(_chunk_perm(n), h.astype(BF16), preferred_element_type=F32)
        o_ref[...] = hp.astype(BF16).reshape(o_ref.shape)

    @pl.when(is_ctx)
    def _():
        emit(_norm_mod(c_ref[...], g_ref[...], scc_ref[...], shc_ref[...]))

    @pl.when(jnp.logical_not(is_ctx))
    def _():
        emit(_norm_mod(x_ref[...], g_ref[...], scx_ref[...], shx_ref[...]))


def _s5_input(x, ctx, g, scx, shx, scc, shc):
    seq, d = x.shape
    m = ctx.shape[0]
    assert m % (S5_CHUNK * 8) == 0 and seq % m == 0
    nx = seq // m
    nc = (seq + 2 * m) // S5_CHUNK
    vec = pl.BlockSpec((1, d), lambda i: (0, 0))
    return pl.pallas_call(
        _s5_input_kernel,
        grid=(nx + 2,),
        in_specs=[pl.BlockSpec((m, d), lambda i: (jnp.clip(i - 1, 0, nx - 1), 0)),
                  pl.BlockSpec((m, d), lambda i: (0, 0)), vec, vec, vec, vec, vec],
        out_specs=pl.BlockSpec((S5_CHUNK, m // S5_CHUNK, d), lambda i: (0, i, 0)),
        out_shape=jax.ShapeDtypeStruct((S5_CHUNK, nc, d), BF16),
        compiler_params=_cparams("arbitrary"),
        name="s5_input",
    )(x, ctx, g, scx, shx, scc, shc)


def _sublane_transpose(v):
    n = v[0].shape[0]
    v = [a.reshape(n // 8, 8, LANES) for a in v]
    row = lax.broadcasted_iota(jnp.int32, v[0].shape, 1)
    s = 4
    while s:
        keep = (row & s) == 0
        for i in range(8):
            if i & s:
                continue
            a, b = v[i], v[i + s]
            v[i] = jnp.where(keep, a, pltpu.roll(b, s, axis=1))
            v[i + s] = jnp.where(keep, pltpu.roll(a, 8 - s, axis=1), b)
        s //= 2
    return [a.reshape(n, LANES) for a in v]


def _s5_kernel(z_ref, wm_ref, wb_ref, wc_ref, lam_ref, o_ref, u_scr, b_scr, hf_scr, hb_scr, *, c_lo):
    L, nc, _ = z_ref.shape
    n_out = o_ref.shape[0] // L
    gs = FIELDS
    half = SSM_STATE
    for j in range(L // FIELDS):
        v = [pltpu.bitcast(z_ref[j * FIELDS + i], jnp.uint32) for i in range(FIELDS)]
        w = _field_transpose(v)
        for gi in range(gs):
            u_scr[gi, :, j * LANES:(j + 1) * LANES] = pltpu.bitcast(w[gi], BF16)

    bs = [jnp.dot(u_scr[gi], wb_ref[gi], preferred_element_type=F32) for gi in range(gs)]
    for comp in range(2):
        t = _sublane_transpose([b[:, comp * LANES:(comp + 1) * LANES] for b in bs])
        for j in range(8):
            b_scr[comp, j] = t[j]

    a_re = lam_ref[:, :2 * half]
    a_im = lam_ref[:, 2 * half:]
    fwd = lax.broadcasted_iota(jnp.int32, (gs, 2 * half), 1) < half

    def step(t, carry):
        s_re, s_im = carry
        rf = pl.ds(pl.multiple_of(t * 8, 8), 8)
        rb = pl.ds(pl.multiple_of(nc - 8 - t * 8, 8), 8)
        for jf in range(8):
            jb = 7 - jf
            hf_scr[0, jf, rf, :] = s_re
            hf_scr[1, jf, rf, :] = s_im
            hb_scr[0, jb, rb, :] = s_re
            hb_scr[1, jb, rb, :] = s_im
            v_re = jnp.where(fwd, b_scr[0, jf, rf, :], b_scr[0, jb, rb, :])
            v_im = jnp.where(fwd, b_scr[1, jf, rf, :], b_scr[1, jb, rb, :])
            s_re, s_im = a_re * s_re - a_im * s_im + v_re, a_re * s_im + a_im * s_re + v_im
        return s_re, s_im

    zero = jnp.zeros((gs, 2 * half), F32)
    lax.fori_loop(0, nc // 8, step, (zero, zero))

    fwd_all = lax.broadcasted_iota(jnp.int32, (nc, 2 * half), 1) < half
    h_in = [_sublane_transpose([jnp.where(fwd_all, hf_scr[comp, j], hb_scr[comp, j]) for j in range(8)])
            for comp in range(2)]
    ys = []
    for gi in range(gs):
        hcat = jnp.concatenate([h_in[0][gi], h_in[1][gi]], axis=1)[c_lo:c_lo + n_out].astype(BF16)
        ys.append(jnp.dot(u_scr[gi, c_lo:c_lo + n_out, :], wm_ref[gi], preferred_element_type=F32)
                  + _dot_nt(hcat, wc_ref[gi]))
    for j in range(L // FIELDS):
        w = _field_transpose([y[:, j * LANES:(j + 1) * LANES] for y in ys])
        for i in range(FIELDS):
            o_ref[pl.ds(j * FIELDS + i, n_out, stride=L), :] = w[i]


def _s5_scan(zp, wm, wb, wc, lam, *, c_lo, n_out):
    L, nc, d = zp.shape
    w = wm.shape[-1]
    gs = FIELDS
    assert nc % 8 == 0 and gs == 8
    slab = pltpu.VMEM((2, 8, nc, 2 * SSM_STATE), F32)
    return pl.pallas_call(
        functools.partial(_s5_kernel, c_lo=c_lo),
        grid=(d // LANES,),
        in_specs=[pl.BlockSpec((L, nc, LANES), lambda i: (0, 0, i)),
                  pl.BlockSpec((gs, w, w), lambda i: (i, 0, 0)),
                  pl.BlockSpec((gs, w, w), lambda i: (i, 0, 0)),
                  pl.BlockSpec((gs, w, w), lambda i: (i, 0, 0)),
                  pl.BlockSpec((gs, w), lambda i: (i, 0))],
        out_specs=pl.BlockSpec((L * n_out, LANES), lambda i: (0, i)),
        out_shape=jax.ShapeDtypeStruct((L * n_out, d), F32),
        scratch_shapes=[pltpu.VMEM((gs, nc, w), BF16), slab, slab, slab],
        compiler_params=_cparams("arbitrary"),
        name="s5_scan",
    )(zp, wm, wb, wc, lam)


def _shift_lanes(x, s):
    w = LANES
    x0, x1 = x[:, :w], x[:, w:]
    zero = jnp.zeros_like(x0)
    lane = lax.broadcasted_iota(jnp.int32, x0.shape, 1)
    if s == 0:
        return x
    if s > 0:
        if s >= w:
            t = s - w
            y1 = x0 if t == 0 else jnp.where(lane < t, 0.0, pltpu.roll(x0, t, axis=1))
            return jnp.concatenate([zero, y1], axis=1)
        r0, r1 = pltpu.roll(x0, s, axis=1), pltpu.roll(x1, s, axis=1)
        return jnp.concatenate([jnp.where(lane < s, 0.0, r0), jnp.where(lane < s, r0, r1)], axis=1)
    s = -s
    if s >= w:
        t = s - w
        y0 = x1 if t == 0 else jnp.where(lane >= w - t, 0.0, pltpu.roll(x1, w - t, axis=1))
        return jnp.concatenate([y0, zero], axis=1)
    r0, r1 = pltpu.roll(x0, w - s, axis=1), pltpu.roll(x1, w - s, axis=1)
    return jnp.concatenate([jnp.where(lane >= w - s, r1, r0), jnp.where(lane >= w - s, 0.0, r1)], axis=1)


def _s5_op_kernel(a_re_ref, a_im_ref, ldt_ref, b_re_ref, b_im_ref, c_re_ref, c_im_ref,
                  wm_ref, wb_ref, wc_ref, lam_ref, e_scr):
    gs = a_re_ref.shape[0]
    L = S5_CHUNK
    h = SSM_GROUP
    half = SSM_STATE
    fwd1 = lax.broadcasted_iota(jnp.int32, (1, 2 * half), 1) < half
    fwd = lax.broadcasted_iota(jnp.int32, (h, 2 * half), 1) < half

    def cmul(ar, ai, br, bi):
        return ar * br - ai * bi, ar * bi + ai * br

    for gi in range(gs):
        ar, ai = a_re_ref[gi], a_im_ref[gi]
        dt = jnp.exp(ldt_ref[gi])
        mag = jnp.exp(ar * dt)
        lr, li = mag * jnp.cos(ai * dt), mag * jnp.sin(ai * dt)
        den = ar * ar + ai * ai
        nr = lr - 1.0
        coef_r = (nr * ar + li * ai) / den
        coef_i = (li * ar - nr * ai) / den
        bb_r, bb_i = cmul(coef_r, coef_i, b_re_ref[gi], b_im_ref[gi])
        c_r, c_i = c_re_ref[gi], c_im_ref[gi]
        pw = [(jnp.ones_like(lr), jnp.zeros_like(lr))]
        for _ in range(L):
            pw.append(cmul(pw[-1][0], pw[-1][1], lr, li))

        def mixed(jf, jb):
            return (jnp.where(fwd1, pw[jf][0], pw[jb][0]), jnp.where(fwd1, pw[jf][1], pw[jb][1]))

        for l in range(L):
            s_r, s_i = cmul(*mixed(L - 1 - l, l), bb_r, bb_i)
            wb_ref[gi, l * h:(l + 1) * h, :] = jnp.concatenate([s_r, s_i], axis=1).astype(BF16)
            f_r, f_i = cmul(*mixed(l + 1, L - l), c_r, c_i)
            wc_ref[gi, l * h:(l + 1) * h, :] = jnp.concatenate([f_r, -f_i], axis=1).astype(BF16)
            e_r, e_i = cmul(*mixed(l, L - 1 - l), c_r, c_i)
            e_scr[l * h:(l + 1) * h, :] = jnp.concatenate([e_r, e_i], axis=1)

        lhs = jnp.concatenate(
            [jnp.concatenate([jnp.where(fwd, bb_r, 0.0), jnp.where(fwd, -bb_i, 0.0)], axis=1),
             jnp.concatenate([jnp.where(fwd, 0.0, bb_r), jnp.where(fwd, 0.0, -bb_i)], axis=1)], axis=0)
        kt = _dot_nt(lhs, e_scr[...], precision=lax.Precision.HIGHEST)
        kt_f, kt_b = kt[:h], kt[h:]
        for l in range(L):
            blk = _shift_lanes(kt_f, h * l) + _shift_lanes(kt_b, -h * (L - 1 - l))
            wm_ref[gi, l * h:(l + 1) * h, :] = blk.astype(BF16)
        lam_ref[gi:gi + 1, :] = jnp.concatenate([pw[L][0], pw[L][1]], axis=1)


def _s5_operators(a_re, a_im, log_dt, b_re, b_im, c_re, c_im):
    _, g, p = a_re.shape
    h = b_re.shape[-1]
    gs = FIELDS
    w = S5_CHUNK * h
    assert w == 2 * LANES and 2 * p == LANES

    def lanes(v):
        return jnp.concatenate([v[0], v[1]], axis=-1)

    def chan_rows(v):
        return lanes(jnp.swapaxes(v, -1, -2))

    a_re2 = lanes(a_re).reshape(g, 1, 2 * p)
    a_im2 = lanes(a_im).reshape(g, 1, 2 * p)
    ldt2 = jnp.repeat(log_dt.T, p, axis=1).reshape(g, 1, 2 * p)
    vec = pl.BlockSpec((gs, 1, 2 * p), lambda i: (i, 0, 0))
    mat = pl.BlockSpec((gs, h, 2 * p), lambda i: (i, 0, 0))
    return pl.pallas_call(
        _s5_op_kernel,
        grid=(g // gs,),
        in_specs=[vec, vec, vec, mat, mat, mat, mat],
        out_specs=[pl.BlockSpec((gs, w, w), lambda i: (i, 0, 0)),
                   pl.BlockSpec((gs, w, w), lambda i: (i, 0, 0)),
                   pl.BlockSpec((gs, w, w), lambda i: (i, 0, 0)),
                   pl.BlockSpec((gs, w), lambda i: (i, 0))],
        out_shape=[jax.ShapeDtypeStruct((g, w, w), BF16), jax.ShapeDtypeStruct((g, w, w), BF16),
                   jax.ShapeDtypeStruct((g, w, w), BF16), jax.ShapeDtypeStruct((g, w), F32)],
        scratch_shapes=[pltpu.VMEM((w, w), F32)],
        compiler_params=_cparams("arbitrary"),
        name="s5_operators",
    )(a_re2, a_im2, ldt2, chan_rows(b_re), chan_rows(b_im), lanes(c_re), lanes(c_im))


def _gelu_tanh(y):
    k = 2.0 * math.sqrt(2.0 / math.pi) * math.log2(math.e)
    return y * (1.0 / (1.0 + jnp.exp2(y * (-k - (k * 0.044715) * (y * y)))))


def _glu_kernel(x_ref, ys_ref, g_ref, sc_ref, sh_ref, d_ref, gate_ref, b_ref, w_ref, o_ref, act_scr):
    j = pl.program_id(1)
    tn = o_ref.shape[1]
    d_model = x_ref.shape[1]

    @pl.when(j == 0)
    def _():
        rb = 32

        def rows_pass(r, carry):
            rows = pl.ds(pl.multiple_of(r * rb, rb), rb)
            hx = _norm_mod(x_ref[rows, :], g_ref[...], sc_ref[...], sh_ref[...])
            act_scr[rows, :] = _gelu_tanh(d_ref[...] * hx + ys_ref[rows, :]).astype(BF16)
            return carry

        lax.fori_loop(0, x_ref.shape[0] // rb, rows_pass, 0, unroll=4)

    act = act_scr[...]
    cols = pl.ds(pl.multiple_of(j * tn, tn), tn)
    gcols = pl.ds(pl.multiple_of(d_model + j * tn, tn), tn)
    val = jnp.dot(act, w_ref[:, cols], preferred_element_type=F32) + b_ref[:, cols]
    gate = jnp.dot(act, w_ref[:, gcols], preferred_element_type=F32) + b_ref[:, gcols]
    o_ref[...] = x_ref[:, cols] + gate_ref[:, cols] * (val * (1.0 / (1.0 + jnp.exp(-gate))))


def _glu(x, ys, g, sc, sh, d_skip, gate, w, b, *, tm):
    m, d = x.shape
    tn = 512
    nj = d // tn
    vec = pl.BlockSpec((1, d), lambda i, j: (0, 0))
    return pl.pallas_call(
        _glu_kernel,
        grid=(m // tm, nj),
        in_specs=[pl.BlockSpec((tm, d), lambda i, j: (i, 0)),
                  pl.BlockSpec((tm, d), lambda i, j: (i, 0)), vec, vec, vec, vec, vec,
                  pl.BlockSpec((1, 2 * d), lambda i, j: (0, 0)),
                  pl.BlockSpec((d, 2 * d), lambda i, j: (0, 0), pipeline_mode=pl.Buffered(1))],
        out_specs=pl.BlockSpec((tm, tn), lambda i, j: (i, j)),
        out_shape=jax.ShapeDtypeStruct((m, d), F32),
        scratch_shapes=[pltpu.VMEM((tm, d), BF16)],
        compiler_params=_cparams("arbitrary", "arbitrary"),
        name="s5_glu",
    )(x, ys, g, sc, sh, d_skip, gate, b, w)


def kernel(x, c, ctx, c_ctx, ada_w, ada_b, norm_mix, norm_ffn, ffn_w1, ffn_w3, ffn_w2, attn_w_in, attn_w_out,
           attn_rpb, attn_sink, ssm_a_re, ssm_a_im, ssm_log_dt, ssm_b_re, ssm_b_im, ssm_c_re, ssm_c_im,
           ssm_d, ssm_w_glu, ssm_b_glu, norm_final):
    batch, seq, d = x.shape
    assert batch == 1 and ada_w.shape[0] == 2
    m = ctx.shape[1]
    xs, cs = x[0], ctx[0]
    mods = _ada_mod(c, c_ctx, ada_w, ada_b)

    def mod(layer, who):
        return [mods[layer, who, i * d:(i + 1) * d].reshape(1, d) for i in range(6)]

    row = lambda v: v.reshape(1, d)
    tm = 512
    tm_qkv = 1024
    tm_ffn = 1024

    sh1, sc1, g1, sh2, sc2, g2 = mod(0, 0)
    csh1, csc1, cg1, csh2, csc2, cg2 = mod(0, 1)
    cos, sin = _rope_tables(seq)
    w_in = attn_w_in[0].astype(BF16)
    w_out = attn_w_out[0].astype(BF16)
    w1, w3, w2 = ffn_w1.astype(BF16), ffn_w3.astype(BF16), ffn_w2.astype(BF16)
    nm, nf = row(norm_mix[0]), row(norm_ffn[0])
    qkv = _qkv_proj(xs, nm, sc1, sh1, w_in, cos, sin, rope=True, tm=tm_qkv)
    qkv_c = _qkv_proj(cs, nm, csc1, csh1, w_in, cos[:m], sin[:m], rope=False, tm=m)
    oa = _neighbourhood_attention(qkv, qkv_c, attn_rpb[0])
    ob = _window_attention(qkv, qkv_c, attn_sink[0])
    oc = _context_attention(qkv_c, attn_sink[0])
    xs = _out_proj(oa, ob, 0, 0, w_out, xs, g1, tm=tm)
    cs = _out_proj(oc, oc, 0, 1, w_out, cs, cg1, tm=m)
    xs = _ffn(xs, nf, sc2, sh2, g2, nf, w1, w3, w2, 0, final_norm=False, tm=tm_ffn)
    cs = _ffn(cs, nf, csc2, csh2, cg2, nf, w1, w3, w2, 0, final_norm=False, tm=m)

    sh1, sc1, g1, sh2, sc2, g2 = mod(1, 0)
    csh1, csc1, _, _, _, _ = mod(1, 1)
    nm, nf = row(norm_mix[1]), row(norm_ffn[1])
    z = _s5_input(xs, cs, nm, sc1, sh1, csc1, csh1)
    wm, wb, wc, lam = _s5_operators(ssm_a_re[0], ssm_a_im[0], ssm_log_dt[0], ssm_b_re[0], ssm_b_im[0],
                                    ssm_c_re[0], ssm_c_im[0])
    ys = _s5_scan(z, wm, wb, wc, lam, c_lo=m // S5_CHUNK, n_out=seq // S5_CHUNK)
    xs = _glu(xs, ys, nm, sc1, sh1, row(ssm_d[0]), g1, ssm_w_glu[0].astype(BF16), ssm_b_glu[0].reshape(1, 2 * d),
              tm=tm)
    xs = _ffn(xs, nf, sc2, sh2, g2, row(norm_final), w1, w3, w2, 1, final_norm=True, tm=tm_ffn)
    return xs[None]
```

```python
import functools
import math

import jax
import jax.numpy as jnp
import numpy as np
from jax import lax
from jax.experimental import pallas as pl
from jax.experimental.pallas import tpu as pltpu

F32 = jnp.float32
BF16 = jnp.bfloat16

GRID_W = 64
HEAD_DIM = 128
NA_HEADS = 8
NB_Q_HEADS = 8
NB_KV_HEADS = 2
NB_GROUP = NB_Q_HEADS // NB_KV_HEADS
NA_ROWS = 8
NA_COLS = 16
SW_BLOCK = 128
SW_QBLOCKS = 4
ROPE_BASE = 10000.0
SSM_GROUP = 16
SSM_STATE = 64
EPS = 1e-6
NEG_INF = -1e30
ATTN_SCALE = HEAD_DIM ** -0.5

A_WIDTH = NA_HEADS * HEAD_DIM
B_Q_WIDTH = NB_Q_HEADS * HEAD_DIM
B_KV_WIDTH = NB_KV_HEADS * HEAD_DIM
QA_BLK = 0
KA_BLK = NA_HEADS
VA_BLK = 2 * NA_HEADS
QB_BLK = 3 * NA_HEADS
KB_BLK = QB_BLK + NB_Q_HEADS
VB_BLK = KB_BLK + NB_KV_HEADS

NA_QROWS = 8
NA_KROWS = 16
NA_KBLK = 4
NA_HEADS_PER_STEP = 8
PROLOGUE_PARTS = 8
S5_CHUNK = 16
LANES = 128
FIELD = SSM_GROUP
FIELDS = LANES // FIELD
VMEM_LIMIT = 56 * 1024 * 1024
FFN_VMEM_LIMIT = 62 * 1024 * 1024


def _cparams(*sem, limit=VMEM_LIMIT):
    return pltpu.CompilerParams(dimension_semantics=sem, vmem_limit_bytes=limit)


def _silu(v):
    return v * (1.0 / (1.0 + jnp.exp(-v)))


def _norm_mod(x, g, sc, sh):
    ms = jnp.mean(x * x, axis=-1, keepdims=True)
    return (x * lax.rsqrt(ms + EPS)) * (g * (1.0 + sc)) + sh


def _dot_nt(a, b, precision=None):
    return lax.dot_general(a, b, (((1,), (1,)), ((), ())), preferred_element_type=F32, precision=precision)


def _ada_kernel(cb_ref, w_ref, b_ref, o_ref, s_scr, *, rows_per_step):
    d, tn = w_ref.shape
    rep = tn // 128

    @pl.when((pl.program_id(0) == 0) & (pl.program_id(1) == 0))
    def _():
        s_scr[...] = _silu(cb_ref[...])

    def body(i, acc):
        r = pl.multiple_of(i * rows_per_step, rows_per_step)
        w = w_ref[pl.ds(r, rows_per_step), :]
        out = []
        for v in range(2):
            s = s_scr[v, pl.ds(r, rows_per_step), :]
            st = jnp.concatenate([s] * rep, axis=1)
            out.append(acc[v] + jnp.sum((w * st).reshape(rows_per_step // 8, 8, tn), axis=0))
        return tuple(out)

    zero = jnp.zeros((8, tn), F32)
    acc = lax.fori_loop(0, d // rows_per_step, body, (zero, zero))
    o_ref[...] = jnp.concatenate([jnp.sum(a, axis=0, keepdims=True) for a in acc], axis=0) + b_ref[...]


def _ada_mod(c, c_ctx, ada_w, ada_b):
    depth, d, n = ada_w.shape
    tn = 1024
    cb = jnp.stack([jnp.broadcast_to(c.reshape(d, 1), (d, 128)),
                    jnp.broadcast_to(c_ctx.reshape(d, 1), (d, 128))])
    return pl.pallas_call(
        functools.partial(_ada_kernel, rows_per_step=64),
        grid=(depth, n // tn),
        in_specs=[pl.BlockSpec((2, d, 128), lambda l, j: (0, 0, 0)),
                  pl.BlockSpec((None, d, tn), lambda l, j: (l, 0, j)),
                  pl.BlockSpec((None, 1, tn), lambda l, j: (l, 0, j))],
        out_specs=pl.BlockSpec((None, 2, tn), lambda l, j: (l, 0, j)),
        out_shape=jax.ShapeDtypeStruct((depth, 2, n), F32),
        scratch_shapes=[pltpu.VMEM((2, d, 128), F32)],
        compiler_params=_cparams("arbitrary", "arbitrary"),
        name="ada_mod",
    )(cb, ada_w, ada_b.reshape(depth, 1, n))


def _rope(a, cos, sin):
    lane = lax.broadcasted_iota(jnp.int32, a.shape, 1)
    partner = jnp.where((lane & 63) < 32, pltpu.roll(a, 96, axis=1), pltpu.roll(a, 32, axis=1))
    return a * cos + partner * sin


def _qkv_kernel(x0_ref, xn_ref, g_ref, sc_ref, sh_ref, w_ref, cos_ref, sin_ref, o_ref, h0_scr, h1_scr, acc_scr,
                *, rope):
    i = pl.program_id(0)
    j = pl.program_id(1)
    tm, tn = o_ref.shape
    heads = tn // HEAD_DIM
    part = tm // PROLOGUE_PARTS

    @pl.when((i == 0) & (j == 0))
    def _():
        h0_scr[...] = _norm_mod(x0_ref[...], g_ref[...], sc_ref[...], sh_ref[...]).astype(BF16)

    def step(h_cur, h_nxt):
        rows = pl.ds(pl.multiple_of(jnp.minimum(j, PROLOGUE_PARTS - 1) * part, part), part)
        h_nxt[rows, :] = _norm_mod(xn_ref[rows, :], g_ref[...], sc_ref[...], sh_ref[...]).astype(BF16)
        acc = jnp.dot(h_cur[...], w_ref[...], preferred_element_type=F32)
        acc_scr[...] = acc
        o_ref[...] = (acc * jnp.where(is_q, ATTN_SCALE, 1.0)).astype(o_ref.dtype)

    col0 = j * heads
    is_qb = (col0 >= QB_BLK) & (col0 < KB_BLK)
    is_kb = col0 == KB_BLK
    is_q = (col0 < KA_BLK) | is_qb

    @pl.when(i % 2 == 0)
    def _():
        step(h0_scr, h1_scr)

    @pl.when(i % 2 == 1)
    def _():
        step(h1_scr, h0_scr)

    if rope:
        def rotated(n_heads, scale):
            parts = []
            for hh in range(heads):
                a = acc_scr[:, hh * HEAD_DIM:(hh + 1) * HEAD_DIM]
                if hh < n_heads:
                    a = _rope(a, cos_ref[...], sin_ref[...]) * scale
                parts.append(a)
            return jnp.concatenate(parts, axis=1)

        @pl.when(is_qb)
        def _():
            o_ref[...] = rotated(heads, ATTN_SCALE).astype(o_ref.dtype)

        @pl.when(is_kb)
        def _():
            o_ref[...] = rotated(NB_KV_HEADS, 1.0).astype(o_ref.dtype)


def _qkv_proj(x, g, sc, sh, w, cos, sin, *, rope, tm):
    m, d = x.shape
    n = w.shape[1]
    tn = 512
    nt = m // tm
    assert KB_BLK % (tn // HEAD_DIM) == 0 and m % tm == 0 and n % tn == 0
    assert n // tn >= PROLOGUE_PARTS and tm % (16 * PROLOGUE_PARTS) == 0
    vec = pl.BlockSpec((1, d), lambda i, j: (0, 0))
    tab = pl.BlockSpec((tm, HEAD_DIM), lambda i, j: (i, 0))
    return pl.pallas_call(
        functools.partial(_qkv_kernel, rope=rope),
        grid=(nt, n // tn),
        in_specs=[pl.BlockSpec((tm, d), lambda i, j: (0, 0), pipeline_mode=pl.Buffered(1)),
                  pl.BlockSpec((tm, d), lambda i, j: (jnp.minimum(i + 1, nt - 1), 0)), vec, vec, vec,
                  pl.BlockSpec((d, tn), lambda i, j: (0, j)), tab, tab],
        out_specs=pl.BlockSpec((tm, tn), lambda i, j: (i, j)),
        out_shape=jax.ShapeDtypeStruct((m, n), BF16),
        scratch_shapes=[pltpu.VMEM((tm, d), BF16), pltpu.VMEM((tm, d), BF16), pltpu.VMEM((tm, tn), F32)],
        compiler_params=_cparams("arbitrary", "arbitrary"),
        name="qkv_rope" if rope else "qkv_ctx",
    )(x, x, g, sc, sh, w, cos, sin)


def _rope_tables(seq):
    quarter = HEAD_DIM // 4
    rows = seq // GRID_W
    f32 = np.float32
    inv_freq = (f32(ROPE_BASE) ** (-np.arange(quarter, dtype=f32) / f32(quarter))).astype(f32)
    ang_r = np.arange(rows, dtype=f32)[:, None] * inv_freq[None, :]
    ang_c = np.arange(GRID_W, dtype=f32)[:, None] * inv_freq[None, :]

    def tokens(tab_r, tab_c, sign):
        r = np.broadcast_to(tab_r[:, None, :], (rows, GRID_W, quarter))
        c = np.broadcast_to(tab_c[None, :, :], (rows, GRID_W, quarter))
        return jnp.asarray(np.concatenate([sign * r, r, sign * c, c], axis=-1).reshape(seq, HEAD_DIM), dtype=F32)

    return tokens(np.cos(ang_r), np.cos(ang_c), f32(1.0)), tokens(np.sin(ang_r), np.sin(ang_c), f32(-1.0))


def _win_kernel(sink_ref, q_ref, *refs, seq):
    nkv = SW_QBLOCKS + 2
    k_refs, v_refs = refs[:nkv], refs[nkv:2 * nkv]
    kc_ref, vc_ref, o_ref, valid_scr = refs[2 * nkv:]
    n = pl.program_id(0)
    blk = SW_BLOCK
    nb = seq // blk
    shape = (NB_GROUP * blk, 3 * blk)

    @pl.when(n == 0)
    def _():
        row = lax.broadcasted_iota(jnp.int32, shape, 0)
        col = lax.broadcasted_iota(jnp.int32, shape, 1)
        rel = col - (row & (blk - 1))
        ok = (rel >= 0) & (rel <= 2 * blk)
        for first in range(2):
            for last in range(2):
                okb = ok
                if first:
                    okb = okb & (col >= blk)
                if last:
                    okb = okb & (col < 2 * blk)
                valid_scr[first + 2 * last] = okb.astype(F32)

    grp = lax.broadcasted_iota(jnp.int32, (NB_GROUP * blk, 1), 0) // blk
    for b in range(SW_QBLOCKS):
        gb = n * SW_QBLOCKS + b
        pattern = (gb == 0).astype(jnp.int32) + 2 * (gb == nb - 1).astype(jnp.int32)
        valid = valid_scr[pattern] > 0.5
        rows = slice(b * blk, (b + 1) * blk)
        outs = []
        for h in range(NB_KV_HEADS):
            hd = slice(h * HEAD_DIM, (h + 1) * HEAD_DIM)
            qs = jnp.concatenate([q_ref[rows, (h * NB_GROUP + g) * HEAD_DIM:(h * NB_GROUP + g + 1) * HEAD_DIM]
                                  for g in range(NB_GROUP)], axis=0)
            k = jnp.concatenate([r[:, hd] for r in k_refs[b:b + 3]], axis=0)
            v = jnp.concatenate([r[:, hd] for r in v_refs[b:b + 3]], axis=0)
            s = jnp.where(valid, _dot_nt(qs, k), NEG_INF)
            s_ctx = _dot_nt(qs, kc_ref[:, hd])
            sink = jnp.zeros((NB_GROUP * blk, 1), F32)
            for g in range(NB_GROUP):
                sink = jnp.where(grp == g, sink_ref[h * NB_GROUP + g], sink)
            m = jnp.maximum(jnp.maximum(jnp.max(s, axis=-1, keepdims=True),
                                        jnp.max(s_ctx, axis=-1, keepdims=True)), sink)
            p = jnp.exp(s - m)
            p_ctx = jnp.exp(s_ctx - m)
            den = jnp.sum(p, axis=-1, keepdims=True) + jnp.sum(p_ctx, axis=-1, keepdims=True) + jnp.exp(sink - m)
            o = (jnp.dot(p.astype(BF16), v, preferred_element_type=F32)
                 + jnp.dot(p_ctx.astype(BF16), vc_ref[:, hd], preferred_element_type=F32))
            o = o * (1.0 / den)
            outs += [o[g * blk:(g + 1) * blk] for g in range(NB_GROUP)]
        o_ref[rows, :] = jnp.concatenate(outs, axis=1).astype(o_ref.dtype)


def _window_attention(qkv, qkv_ctx, sink):
    seq = qkv.shape[0]
    m = qkv_ctx.shape[0]
    nb = seq // SW_BLOCK
    assert QB_BLK % NB_Q_HEADS == 0 and KB_BLK % NB_KV_HEADS == 0 and VB_BLK % NB_KV_HEADS == 0
    assert nb % SW_QBLOCKS == 0
    nkv = SW_QBLOCKS + 2

    def kv_spec(col_blk, shift):
        return pl.BlockSpec((SW_BLOCK, B_KV_WIDTH),
                            lambda n: (jnp.clip(n * SW_QBLOCKS + shift, 0, nb - 1), col_blk // NB_KV_HEADS))

    return pl.pallas_call(
        functools.partial(_win_kernel, seq=seq),
        grid=(nb // SW_QBLOCKS,),
        in_specs=[pl.BlockSpec(memory_space=pltpu.SMEM),
                  pl.BlockSpec((SW_QBLOCKS * SW_BLOCK, B_Q_WIDTH), lambda n: (n, QB_BLK // NB_Q_HEADS))]
                 + [kv_spec(KB_BLK, i - 1) for i in range(nkv)]
                 + [kv_spec(VB_BLK, i - 1) for i in range(nkv)]
                 + [pl.BlockSpec((m, B_KV_WIDTH), lambda n: (0, KB_BLK // NB_KV_HEADS)),
                    pl.BlockSpec((m, B_KV_WIDTH), lambda n: (0, VB_BLK // NB_KV_HEADS))],
        out_specs=pl.BlockSpec((SW_QBLOCKS * SW_BLOCK, B_Q_WIDTH), lambda n: (n, 0)),
        out_shape=jax.ShapeDtypeStruct((seq, B_Q_WIDTH), BF16),
        scratch_shapes=[pltpu.VMEM((4, NB_GROUP * SW_BLOCK, 3 * SW_BLOCK), F32)],
        compiler_params=_cparams("arbitrary"),
        name="window_attn",
    )(sink, qkv, *([qkv] * (2 * nkv)), qkv_ctx, qkv_ctx)


def _na_kernel(rpb_ref, q_ref, k0_ref, k1_ref, k2_ref, k3_ref, v0_ref, v1_ref, v2_ref, v3_ref,
               kc_ref, vc_ref, o_ref, cb_scr, bias_scr, *, rows):
    h0 = pl.program_id(0) * NA_HEADS_PER_STEP
    t = pl.program_id(1)
    n_tiles = rows // NA_QROWS
    n_dr = 2 * NA_ROWS - 1
    n_dc = 2 * NA_COLS - 1
    half = GRID_W

    @pl.when(t == 0)
    def _():
        shape = (GRID_W, 2 * half)
        cq = lax.broadcasted_iota(jnp.int32, shape, 0)
        lane = lax.broadcasted_iota(jnp.int32, shape, 1)
        ck = lane & (half - 1)
        hi = lane >= half
        bidx = jnp.clip(ck - cq + (NA_COLS - 1), 0, n_dc - 1)
        col_ok = (ck - jnp.clip(cq - NA_COLS // 2, 0, GRID_W - NA_COLS))
        col_ok = (col_ok >= 0) & (col_ok < NA_COLS)
        for hh in range(NA_HEADS_PER_STEP):
            for i in range(n_dr + 1):
                val = jnp.full(shape, NEG_INF, F32)
                for e in range(2):
                    a = i - 1 + e
                    if 0 <= a < n_dr:
                        sel = hi if e else jnp.logical_not(hi)
                        for b in range(n_dc):
                            val = jnp.where(sel & (bidx == b), rpb_ref[((h0 + hh) * n_dr + a) * n_dc + b], val)
                cb_scr[hh, i] = jnp.where(col_ok, val, NEG_INF)

    r0 = t * NA_QROWS
    kb0 = jnp.clip(r0 - NA_ROWS // 2, 0, rows - NA_KROWS)

    @pl.when((t <= 1) | (t == n_tiles - 1))
    def _():
        lane = lax.broadcasted_iota(jnp.int32, (GRID_W, 2 * half), 1)
        lo = lane < half
        for rq in range(NA_QROWS):
            rq_abs = r0 + rq
            ws = jnp.clip(rq_abs - NA_ROWS // 2, 0, rows - NA_ROWS)
            for jj in range(NA_KROWS // 2):
                rk0 = kb0 + 2 * jj
                ok0 = (rk0 >= ws) & (rk0 < ws + NA_ROWS)
                ok1 = (rk0 + 1 >= ws) & (rk0 + 1 < ws + NA_ROWS)
                idx = jnp.clip(rk0 - rq_abs + NA_ROWS, 0, n_dr)
                keep = (lo & ok0) | (jnp.logical_not(lo) & ok1)
                for hh in range(NA_HEADS_PER_STEP):
                    bias_scr[hh, rq * GRID_W:(rq + 1) * GRID_W, jj * 2 * half:(jj + 1) * 2 * half] = (
                        jnp.where(keep, cb_scr[hh, idx], NEG_INF))

    outs = []
    for hh in range(NA_HEADS_PER_STEP):
        hd = slice(hh * HEAD_DIM, (hh + 1) * HEAD_DIM)
        q = q_ref[:, hd]
        k = jnp.concatenate([k0_ref[:, hd], k1_ref[:, hd], k2_ref[:, hd], k3_ref[:, hd]], axis=0)
        v = jnp.concatenate([v0_ref[:, hd], v1_ref[:, hd], v2_ref[:, hd], v3_ref[:, hd]], axis=0)
        s = _dot_nt(q, k) + bias_scr[hh]
        s_ctx = _dot_nt(q, kc_ref[:, hd])
        m = jnp.maximum(jnp.max(s, axis=-1, keepdims=True), jnp.max(s_ctx, axis=-1, keepdims=True))
        p = jnp.exp(s - m)
        p_ctx = jnp.exp(s_ctx - m)
        den = jnp.sum(p, axis=-1, keepdims=True) + jnp.sum(p_ctx, axis=-1, keepdims=True)
        o = (jnp.dot(p.astype(BF16), v, preferred_element_type=F32)
             + jnp.dot(p_ctx.astype(BF16), vc_ref[:, hd], preferred_element_type=F32))
        outs.append(o * (1.0 / den))
    o_ref[...] = jnp.concatenate(outs, axis=1).astype(o_ref.dtype)


def _neighbourhood_attention(qkv, qkv_ctx, rpb):
    seq = qkv.shape[0]
    m = qkv_ctx.shape[0]
    rows = seq // GRID_W
    assert rows % NA_QROWS == 0 and rows >= NA_KROWS + NA_QROWS
    n_tiles = rows // NA_QROWS
    tq = NA_QROWS * GRID_W
    tk = NA_KBLK * GRID_W
    n_kblk = NA_KROWS // NA_KBLK

    hp = NA_HEADS_PER_STEP
    wh = hp * HEAD_DIM
    assert NA_HEADS % hp == 0 and QA_BLK % hp == 0 and KA_BLK % hp == 0 and VA_BLK % hp == 0

    def kv_spec(col_blk, i):
        def index(h, t):
            first = jnp.clip(t * (NA_QROWS // NA_KBLK) - 1, 0, rows // NA_KBLK - n_kblk)
            return (first + i, col_blk // hp + h)
        return pl.BlockSpec((tk, wh), index)

    return pl.pallas_call(
        functools.partial(_na_kernel, rows=rows),
        grid=(NA_HEADS // hp, n_tiles),
        in_specs=[pl.BlockSpec(memory_space=pltpu.SMEM),
                  pl.BlockSpec((tq, wh), lambda h, t: (t, QA_BLK // hp + h))]
                 + [kv_spec(KA_BLK, i) for i in range(n_kblk)]
                 + [kv_spec(VA_BLK, i) for i in range(n_kblk)]
                 + [pl.BlockSpec((m, wh), lambda h, t: (0, KA_BLK // hp + h)),
                    pl.BlockSpec((m, wh), lambda h, t: (0, VA_BLK // hp + h))],
        out_specs=pl.BlockSpec((tq, wh), lambda h, t: (t, h)),
        out_shape=jax.ShapeDtypeStruct((seq, A_WIDTH), BF16),
        scratch_shapes=[pltpu.VMEM((hp, 2 * NA_ROWS, GRID_W, 2 * GRID_W), F32),
                        pltpu.VMEM((hp, tq, NA_KROWS * GRID_W), F32)],
        compiler_params=_cparams("arbitrary", "arbitrary"),
        name="neighbourhood_attn",
    )(rpb.reshape(-1), qkv, *([qkv] * (2 * n_kblk)), qkv_ctx, qkv_ctx)


def _ctx_attn_kernel(sink_ref, q_ref, k_ref, v_ref, o_ref):
    j = pl.program_id(0)
    s = _dot_nt(q_ref[...], k_ref[...])
    has_sink = j >= NA_HEADS
    sink = jnp.where(has_sink, sink_ref[jnp.maximum(j - NA_HEADS, 0)], NEG_INF)
    m = jnp.maximum(jnp.max(s, axis=-1, keepdims=True), sink)
    p = jnp.exp(s - m)
    den = jnp.sum(p, axis=-1, keepdims=True) + jnp.where(has_sink, jnp.exp(sink - m), 0.0)
    o = jnp.dot(p.astype(BF16), v_ref[...], preferred_element_type=F32)
    o_ref[...] = (o * (1.0 / den)).astype(o_ref.dtype)


def _context_attention(qkv_ctx, sink):
    m = qkv_ctx.shape[0]

    def q_idx(j):
        return (0, jnp.where(j < NA_HEADS, QA_BLK + j, QB_BLK + j - NA_HEADS))

    def k_idx(j):
        return (0, jnp.where(j < NA_HEADS, KA_BLK + j, KB_BLK + (j - NA_HEADS) // NB_GROUP))

    def v_idx(j):
        return (0, jnp.where(j < NA_HEADS, VA_BLK + j, VB_BLK + (j - NA_HEADS) // NB_GROUP))

    return pl.pallas_call(
        _ctx_attn_kernel,
        grid=(NA_HEADS + NB_Q_HEADS,),
        in_specs=[pl.BlockSpec(memory_space=pltpu.SMEM),
                  pl.BlockSpec((m, HEAD_DIM), q_idx),
                  pl.BlockSpec((m, HEAD_DIM), k_idx),
                  pl.BlockSpec((m, HEAD_DIM), v_idx)],
        out_specs=pl.BlockSpec((m, HEAD_DIM), lambda j: (0, j)),
        out_shape=jax.ShapeDtypeStruct((m, A_WIDTH + B_Q_WIDTH), BF16),
        compiler_params=_cparams("arbitrary"),
        name="context_attn",
    )(sink, qkv_ctx, qkv_ctx, qkv_ctx)


def _out_proj_kernel(a_ref, b_ref, wa_ref, wb_ref, x_ref, g_ref, o_ref):
    y = (jnp.dot(a_ref[...], wa_ref[...], preferred_element_type=F32)
         + jnp.dot(b_ref[...], wb_ref[...], preferred_element_type=F32))
    o_ref[...] = x_ref[...] + g_ref[...] * y


def _out_proj(oa, ob, a_blk, b_blk, w, x, gate, *, tm):
    m, d = x.shape
    kh = w.shape[0] // 2
    return pl.pallas_call(
        _out_proj_kernel,
        grid=(m // tm,),
        in_specs=[pl.BlockSpec((tm, kh), lambda i: (i, a_blk)),
                  pl.BlockSpec((tm, kh), lambda i: (i, b_blk)),
                  pl.BlockSpec((kh, d), lambda i: (0, 0)),
                  pl.BlockSpec((kh, d), lambda i: (1, 0)),
                  pl.BlockSpec((tm, d), lambda i: (i, 0)),
                  pl.BlockSpec((1, d), lambda i: (0, 0))],
        out_specs=pl.BlockSpec((tm, d), lambda i: (i, 0)),
        out_shape=jax.ShapeDtypeStruct((m, d), F32),
        compiler_params=_cparams("arbitrary"),
        name="out_proj",
    )(oa, ob, w, w, x, gate)


def _ffn_kernel(x_ref, g_ref, sc_ref, sh_ref, gate_ref, gf_ref, w1_ref, w3_ref, w2_ref, o_ref, h_scr, act_scr,
                *, final_norm, n_up):
    s = pl.program_id(1)
    tf = w1_ref.shape[1]
    tn = w2_ref.shape[1]

    @pl.when(s == 0)
    def _():
        h_scr[...] = _norm_mod(x_ref[...], g_ref[...], sc_ref[...], sh_ref[...]).astype(BF16)

    @pl.when(s < n_up)
    def _():
        h = h_scr[...]
        a = jnp.dot(h, w1_ref[...], preferred_element_type=F32)
        b = jnp.dot(h, w3_ref[...], preferred_element_type=F32)
        act_scr[:, pl.ds(pl.multiple_of(s * tf, tf), tf)] = (_silu(a) * b).astype(BF16)

    @pl.when(s >= n_up)
    def _():
        cols = pl.ds(pl.multiple_of((s - n_up) * tn, tn), tn)
        y = jnp.dot(act_scr[...], w2_ref[...], preferred_element_type=F32)
        r = x_ref[:, cols] + gate_ref[:, cols] * y
        if final_norm:
            o_ref[:, cols] = r
        else:
            o_ref[...] = r

    if final_norm:
        @pl.when(s == pl.num_programs(1) - 1)
        def _():
            r = o_ref[...]
            ms = jnp.mean(r * r, axis=-1, keepdims=True)
            o_ref[...] = r * lax.rsqrt(ms + EPS) * gf_ref[...]


def _ffn(x, g, sc, sh, gate, gf, w1, w3, w2, layer, *, final_norm, tm):
    m, d = x.shape
    dff = w1.shape[-1]
    tf, tn = 512, 256
    assert dff % tf == 0 and d % tn == 0
    n_up, n_dn = dff // tf, d // tn
    vec = pl.BlockSpec((1, d), lambda i, s: (0, 0))
    once = pl.Buffered(1) if m > tm else None
    return pl.pallas_call(
        functools.partial(_ffn_kernel, final_norm=final_norm, n_up=n_up),
        grid=(m // tm, n_up + n_dn),
        in_specs=[pl.BlockSpec((tm, d), lambda i, s: (i, 0)), vec, vec, vec, vec, vec,
                  pl.BlockSpec((None, d, tf), lambda i, s: (layer, 0, jnp.minimum(s, n_up - 1))),
                  pl.BlockSpec((None, d, tf), lambda i, s: (layer, 0, jnp.minimum(s, n_up - 1))),
                  pl.BlockSpec((None, dff, tn), lambda i, s: (layer, 0, jnp.maximum(s - n_up, 0)))],
        out_specs=(pl.BlockSpec((tm, d), lambda i, s: (i, 0), pipeline_mode=once) if final_norm else
                   pl.BlockSpec((tm, tn), lambda i, s: (i, jnp.maximum(s - n_up, 0)))),
        out_shape=jax.ShapeDtypeStruct((m, d), F32),
        scratch_shapes=[pltpu.VMEM((tm, d), BF16), pltpu.VMEM((tm, dff), BF16)],
        compiler_params=_cparams("arbitrary", "arbitrary", limit=FFN_VMEM_LIMIT),
        name="swiglu_final" if final_norm else "swiglu",
    )(x, g, sc, sh, gate, gf, w1, w3, w2)


def _chunk_perm(n):
    r = lax.broadcasted_iota(jnp.int32, (n, n), 0)
    c = lax.broadcasted_iota(jnp.int32, (n, n), 1)
    nch = n // S5_CHUNK
    return (c == (r % nch) * S5_CHUNK + r // nch).astype(BF16)


def _field_transpose(v):
    v = list(v)
    lane = lax.broadcasted_iota(jnp.int32, v[0].shape, 1)
    s = FIELDS // 2
    while s:
        keep = ((lane // FIELD) & s) == 0
        for i in range(FIELDS):
            if i & s:
                continue
            a, b = v[i], v[i + s]
            v[i] = jnp.where(keep, a, pltpu.roll(b, FIELD * s, axis=1))
            v[i + s] = jnp.where(keep, pltpu.roll(a, LANES - FIELD * s, axis=1), b)
        s //= 2
    return v


def _s5_input_kernel(x_ref, c_ref, g_ref, scx_ref, shx_ref, scc_ref, shc_ref, o_ref):
    i = pl.program_id(0)
    is_ctx = (i == 0) | (i == pl.num_programs(0) - 1)
    n = x_ref.shape[0]

    def emit(h):
        hp = jnp.dot(_chunk_perm(n), h.astype(BF16), preferred_element_type=F32)
        o_ref[...] = hp.astype(BF16).reshape(o_ref.shape)

    @pl.when(is_ctx)
    def _():
        emit(_norm_mod(c_ref[...], g_ref[...], scc_ref[...], shc_ref[...]))

    @pl.when(jnp.logical_not(is_ctx))
    def _():
        emit(_norm_mod(x_ref[...], g_ref[...], scx_ref[...], shx_ref[...]))


def _s5_input(x, ctx, g, scx, shx, scc, shc):
    seq, d = x.shape
    m = ctx.shape[0]
    assert m % (S5_CHUNK * 8) == 0 and seq % m == 0
    nx = seq // m
    nc = (seq + 2 * m) // S5_CHUNK
    vec = pl.BlockSpec((1, d), lambda i: (0, 0))
    return pl.pallas_call(
        _s5_input_kernel,
        grid=(nx + 2,),
        in_specs=[pl.BlockSpec((m, d), lambda i: (jnp.clip(i - 1, 0, nx - 1), 0)),
                  pl.BlockSpec((m, d), lambda i: (0, 0)), vec, vec, vec, vec, vec],
        out_specs=pl.BlockSpec((S5_CHUNK, m // S5_CHUNK, d), lambda i: (0, i, 0)),
        out_shape=jax.ShapeDtypeStruct((S5_CHUNK, nc, d), BF16),
        compiler_params=_cparams("arbitrary"),
        name="s5_input",
    )(x, ctx, g, scx, shx, scc, shc)


def _sublane_transpose(v):
    n = v[0].shape[0]
    v = [a.reshape(n // 8, 8, LANES) for a in v]
    row = lax.broadcasted_iota(jnp.int32, v[0].shape, 1)
    s = 4
    while s:
        keep = (row & s) == 0
        for i in range(8):
            if i & s:
                continue
            a, b = v[i], v[i + s]
            v[i] = jnp.where(keep, a, pltpu.roll(b, s, axis=1))
            v[i + s] = jnp.where(keep, pltpu.roll(a, 8 - s, axis=1), b)
        s //= 2
    return [a.reshape(n, LANES) for a in v]


def _s5_kernel(z_ref, wm_ref, wb_ref, wc_ref, lam_ref, o_ref, u_scr, b_scr, hf_scr, hb_scr, *, c_lo):
    L, nc, _ = z_ref.shape
    n_out = o_ref.shape[0] // L
    gs = FIELDS
    half = SSM_STATE
    for j in range(L // FIELDS):
        v = [pltpu.bitcast(z_ref[j * FIELDS + i], jnp.uint32) for i in range(FIELDS)]
        w = _field_transpose(v)
        for gi in range(gs):
            u_scr[gi, :, j * LANES:(j + 1) * LANES] = pltpu.bitcast(w[gi], BF16)

    bs = [jnp.dot(u_scr[gi], wb_ref[gi], preferred_element_type=F32) for gi in range(gs)]
    for comp in range(2):
        t = _sublane_transpose([b[:, comp * LANES:(comp + 1) * LANES] for b in bs])
        for j in range(8):
            b_scr[comp, j] = t[j]

    a_re = lam_ref[:, :2 * half]
    a_im = lam_ref[:, 2 * half:]
    fwd = lax.broadcasted_iota(jnp.int32, (gs, 2 * half), 1) < half

    def step(t, carry):
        s_re, s_im = carry
        rf = pl.ds(pl.multiple_of(t * 8, 8), 8)
        rb = pl.ds(pl.multiple_of(nc - 8 - t * 8, 8), 8)
        for jf in range(8):
            jb = 7 - jf
            hf_scr[0, jf, rf, :] = s_re
            hf_scr[1, jf, rf, :] = s_im
            hb_scr[0, jb, rb, :] = s_re
            hb_scr[1, jb, rb, :] = s_im
            v_re = jnp.where(fwd, b_scr[0, jf, rf, :], b_scr[0, jb, rb, :])
            v_im = jnp.where(fwd, b_scr[1, jf, rf, :], b_scr[1, jb, rb, :])
            s_re, s_im = a_re * s_re - a_im * s_im + v_re, a_re * s_im + a_im * s_re + v_im
        return s_re, s_im

    zero = jnp.zeros((gs, 2 * half), F32)
    lax.fori_loop(0, nc // 8, step, (zero, zero))

    fwd_all = lax.broadcasted_iota(jnp.int32, (nc, 2 * half), 1) < half
    h_in = [_sublane_transpose([jnp.where(fwd_all, hf_scr[comp, j], hb_scr[comp, j]) for j in range(8)])
            for comp in range(2)]
    ys = []
    for gi in range(gs):
        hcat = jnp.concatenate([h_in[0][gi], h_in[1][gi]], axis=1)[c_lo:c_lo + n_out].astype(BF16)
        ys.append(jnp.dot(u_scr[gi, c_lo:c_lo + n_out, :], wm_ref[gi], preferred_element_type=F32)
                  + _dot_nt(hcat, wc_ref[gi]))
    for j in range(L // FIELDS):
        w = _field_transpose([y[:, j * LANES:(j + 1) * LANES] for y in ys])
        for i in range(FIELDS):
            o_ref[pl.ds(j * FIELDS + i, n_out, stride=L), :] = w[i]


def _s5_scan(zp, wm, wb, wc, lam, *, c_lo, n_out):
    L, nc, d = zp.shape
    w = wm.shape[-1]
    gs = FIELDS
    assert nc % 8 == 0 and gs == 8
    slab = pltpu.VMEM((2, 8, nc, 2 * SSM_STATE), F32)
    return pl.pallas_call(
        functools.partial(_s5_kernel, c_lo=c_lo),
        grid=(d // LANES,),
        in_specs=[pl.BlockSpec((L, nc, LANES), lambda i: (0, 0, i)),
                  pl.BlockSpec((gs, w, w), lambda i: (i, 0, 0)),
                  pl.BlockSpec((gs, w, w), lambda i: (i, 0, 0)),
                  pl.BlockSpec((gs, w, w), lambda i: (i, 0, 0)),
                  pl.BlockSpec((gs, w), lambda i: (i, 0))],
        out_specs=pl.BlockSpec((L * n_out, LANES), lambda i: (0, i)),
        out_shape=jax.ShapeDtypeStruct((L * n_out, d), F32),
        scratch_shapes=[pltpu.VMEM((gs, nc, w), BF16), slab, slab, slab],
        compiler_params=_cparams("arbitrary"),
        name="s5_scan",
    )(zp, wm, wb, wc, lam)


def _shift_lanes(x, s):
    w = LANES
    x0, x1 = x[:, :w], x[:, w:]
    zero = jnp.zeros_like(x0)
    lane = lax.broadcasted_iota(jnp.int32, x0.shape, 1)
    if s == 0:
        return x
    if s > 0:
        if s >= w:
            t = s - w
            y1 = x0 if t == 0 else jnp.where(lane < t, 0.0, pltpu.roll(x0, t, axis=1))
            return jnp.concatenate([zero, y1], axis=1)
        r0, r1 = pltpu.roll(x0, s, axis=1), pltpu.roll(x1, s, axis=1)
        return jnp.concatenate([jnp.where(lane < s, 0.0, r0), jnp.where(lane < s, r0, r1)], axis=1)
    s = -s
    if s >= w:
        t = s - w
        y0 = x1 if t == 0 else jnp.where(lane >= w - t, 0.0, pltpu.roll(x1, w - t, axis=1))
        return jnp.concatenate([y0, zero], axis=1)
    r0, r1 = pltpu.roll(x0, w - s, axis=1), pltpu.roll(x1, w - s, axis=1)
    return jnp.concatenate([jnp.where(lane >= w - s, r1, r0), jnp.where(lane >= w - s, 0.0, r1)], axis=1)


def _s5_op_kernel(a_re_ref, a_im_ref, ldt_ref, b_re_ref, b_im_ref, c_re_ref, c_im_ref,
                  wm_ref, wb_ref, wc_ref, lam_ref, e_scr):
    gs = a_re_ref.shape[0]
    L = S5_CHUNK
    h = SSM_GROUP
    half = SSM_STATE
    fwd1 = lax.broadcasted_iota(jnp.int32, (1, 2 * half), 1) < half
    fwd = lax.broadcasted_iota(jnp.int32, (h, 2 * half), 1) < half

    def cmul(ar, ai, br, bi):
        return ar * br - ai * bi, ar * bi + ai * br

    for gi in range(gs):
        ar, ai = a_re_ref[gi], a_im_ref[gi]
        dt = jnp.exp(ldt_ref[gi])
        mag = jnp.exp(ar * dt)
        lr, li = mag * jnp.cos(ai * dt), mag * jnp.sin(ai * dt)
        den = ar * ar + ai * ai
        nr = lr - 1.0
        coef_r = (nr * ar + li * ai) / den
        coef_i = (li * ar - nr * ai) / den
        bb_r, bb_i = cmul(coef_r, coef_i, b_re_ref[gi], b_im_ref[gi])
        c_r, c_i = c_re_ref[gi], c_im_ref[gi]
        pw = [(jnp.ones_like(lr), jnp.zeros_like(lr))]
        for _ in range(L):
            pw.append(cmul(pw[-1][0], pw[-1][1], lr, li))

        def mixed(jf, jb):
            return (jnp.where(fwd1, pw[jf][0], pw[jb][0]), jnp.where(fwd1, pw[jf][1], pw[jb][1]))

        for l in range(L):
            s_r, s_i = cmul(*mixed(L - 1 - l, l), bb_r, bb_i)
            wb_ref[gi, l * h:(l + 1) * h, :] = jnp.concatenate([s_r, s_i], axis=1).astype(BF16)
            f_r, f_i = cmul(*mixed(l + 1, L - l), c_r, c_i)
            wc_ref[gi, l * h:(l + 1) * h, :] = jnp.concatenate([f_r, -f_i], axis=1).astype(BF16)
            e_r, e_i = cmul(*mixed(l, L - 1 - l), c_r, c_i)
            e_scr[l * h:(l + 1) * h, :] = jnp.concatenate([e_r, e_i], axis=1)

        lhs = jnp.concatenate(
            [jnp.concatenate([jnp.where(fwd, bb_r, 0.0), jnp.where(fwd, -bb_i, 0.0)], axis=1),
             jnp.concatenate([jnp.where(fwd, 0.0, bb_r), jnp.where(fwd, 0.0, -bb_i)], axis=1)], axis=0)
        kt = _dot_nt(lhs, e_scr[...], precision=lax.Precision.HIGHEST)
        kt_f, kt_b = kt[:h], kt[h:]
        for l in range(L):
            blk = _shift_lanes(kt_f, h * l) + _shift_lanes(kt_b, -h * (L - 1 - l))
            wm_ref[gi, l * h:(l + 1) * h, :] = blk.astype(BF16)
        lam_ref[gi:gi + 1, :] = jnp.concatenate([pw[L][0], pw[L][1]], axis=1)


def _s5_operators(a_re, a_im, log_dt, b_re, b_im, c_re, c_im):
    _, g, p = a_re.shape
    h = b_re.shape[-1]
    gs = FIELDS
    w = S5_CHUNK * h
    assert w == 2 * LANES and 2 * p == LANES

    def lanes(v):
        return jnp.concatenate([v[0], v[1]], axis=-1)

    def chan_rows(v):
        return lanes(jnp.swapaxes(v, -1, -2))

    a_re2 = lanes(a_re).reshape(g, 1, 2 * p)
    a_im2 = lanes(a_im).reshape(g, 1, 2 * p)
    ldt2 = jnp.repeat(log_dt.T, p, axis=1).reshape(g, 1, 2 * p)
    vec = pl.BlockSpec((gs, 1, 2 * p), lambda i: (i, 0, 0))
    mat = pl.BlockSpec((gs, h, 2 * p), lambda i: (i, 0, 0))
    return pl.pallas_call(
        _s5_op_kernel,
        grid=(g // gs,),
        in_specs=[vec, vec, vec, mat, mat, mat, mat],
        out_specs=[pl.BlockSpec((gs, w, w), lambda i: (i, 0, 0)),
                   pl.BlockSpec((gs, w, w), lambda i: (i, 0, 0)),
                   pl.BlockSpec((gs, w, w), lambda i: (i, 0, 0)),
                   pl.BlockSpec((gs, w), lambda i: (i, 0))],
        out_shape=[jax.ShapeDtypeStruct((g, w, w), BF16), jax.ShapeDtypeStruct((g, w, w), BF16),
                   jax.ShapeDtypeStruct((g, w, w), BF16), jax.ShapeDtypeStruct((g, w), F32)],
        scratch_shapes=[pltpu.VMEM((w, w), F32)],
        compiler_params=_cparams("arbitrary"),
        name="s5_operators",
    )(a_re2, a_im2, ldt2, chan_rows(b_re), chan_rows(b_im), lanes(c_re), lanes(c_im))


def _gelu_tanh(y):
    k = 2.0 * math.sqrt(2.0 / math.pi) * math.log2(math.e)
    return y * (1.0 / (1.0 + jnp.exp2(y * (-k - (k * 0.044715) * (y * y)))))


def _glu_kernel(x_ref, ys_ref, g_ref, sc_ref, sh_ref, d_ref, gate_ref, b_ref, w_ref, o_ref, act_scr):
    j = pl.program_id(1)
    tn = o_ref.shape[1]
    d_model = x_ref.shape[1]

    @pl.when(j == 0)
    def _():
        rb = 32

        def rows_pass(r, carry):
            rows = pl.ds(pl.multiple_of(r * rb, rb), rb)
            hx = _norm_mod(x_ref[rows, :], g_ref[...], sc_ref[...], sh_ref[...])
            act_scr[rows, :] = _gelu_tanh(d_ref[...] * hx + ys_ref[rows, :]).astype(BF16)
            return carry

        lax.fori_loop(0, x_ref.shape[0] // rb, rows_pass, 0, unroll=4)

    act = act_scr[...]
    cols = pl.ds(pl.multiple_of(j * tn, tn), tn)
    gcols = pl.ds(pl.multiple_of(d_model + j * tn, tn), tn)
    val = jnp.dot(act, w_ref[:, cols], preferred_element_type=F32) + b_ref[:, cols]
    gate = jnp.dot(act, w_ref[:, gcols], preferred_element_type=F32) + b_ref[:, gcols]
    o_ref[...] = x_ref[:, cols] + gate_ref[:, cols] * (val * (1.0 / (1.0 + jnp.exp(-gate))))


def _glu(x, ys, g, sc, sh, d_skip, gate, w, b, *, tm):
    m, d = x.shape
    tn = 1024
    nj = d // tn
    vec = pl.BlockSpec((1, d), lambda i, j: (0, 0))
    return pl.pallas_call(
        _glu_kernel,
        grid=(m // tm, nj),
        in_specs=[pl.BlockSpec((tm, d), lambda i, j: (i, 0)),
                  pl.BlockSpec((tm, d), lambda i, j: (i, 0)), vec, vec, vec, vec, vec,
                  pl.BlockSpec((1, 2 * d), lambda i, j: (0, 0)),
                  pl.BlockSpec((d, 2 * d), lambda i, j: (0, 0), pipeline_mode=pl.Buffered(1))],
        out_specs=pl.BlockSpec((tm, tn), lambda i, j: (i, j)),
        out_shape=jax.ShapeDtypeStruct((m, d), F32),
        scratch_shapes=[pltpu.VMEM((tm, d), BF16)],
        compiler_params=_cparams("arbitrary", "arbitrary"),
        name="s5_glu",
    )(x, ys, g, sc, sh, d_skip, gate, b, w)


def kernel(x, c, ctx, c_ctx, ada_w, ada_b, norm_mix, norm_ffn, ffn_w1, ffn_w3, ffn_w2, attn_w_in, attn_w_out,
           attn_rpb, attn_sink, ssm_a_re, ssm_a_im, ssm_log_dt, ssm_b_re, ssm_b_im, ssm_c_re, ssm_c_im,
           ssm_d, ssm_w_glu, ssm_b_glu, norm_final):
    batch, seq, d = x.shape
    assert batch == 1 and ada_w.shape[0] == 2
    m = ctx.shape[1]
    xs, cs = x[0], ctx[0]
    mods = _ada_mod(c, c_ctx, ada_w, ada_b)

    def mod(layer, who):
        return [mods[layer, who, i * d:(i + 1) * d].reshape(1, d) for i in range(6)]

    row = lambda v: v.reshape(1, d)
    tm = 512
    tm_qkv = 1024
    tm_ffn = 1024

    sh1, sc1, g1, sh2, sc2, g2 = mod(0, 0)
    csh1, csc1, cg1, csh2, csc2, cg2 = mod(0, 1)
    cos, sin = _rope_tables(seq)
    w_in = attn_w_in[0].astype(BF16)
    w_out = attn_w_out[0].astype(BF16)
    w1, w3, w2 = ffn_w1.astype(BF16), ffn_w3.astype(BF16), ffn_w2.astype(BF16)
    nm, nf = row(norm_mix[0]), row(norm_ffn[0])
    qkv = _qkv_proj(xs, nm, sc1, sh1, w_in, cos, sin, rope=True, tm=tm_qkv)
    qkv_c = _qkv_proj(cs, nm, csc1, csh1, w_in, cos[:m], sin[:m], rope=False, tm=m)
    oa = _neighbourhood_attention(qkv, qkv_c, attn_rpb[0])
    ob = _window_attention(qkv, qkv_c, attn_sink[0])
    oc = _context_attention(qkv_c, attn_sink[0])
    xs = _out_proj(oa, ob, 0, 0, w_out, xs, g1, tm=tm)
    cs = _out_proj(oc, oc, 0, 1, w_out, cs, cg1, tm=m)
    xs = _ffn(xs, nf, sc2, sh2, g2, nf, w1, w3, w2, 0, final_norm=False, tm=tm_ffn)
    cs = _ffn(cs, nf, csc2, csh2, cg2, nf, w1, w3, w2, 0, final_norm=False, tm=m)

    sh1, sc1, g1, sh2, sc2, g2 = mod(1, 0)
    csh1, csc1, _, _, _, _ = mod(1, 1)
    nm, nf = row(norm_mix[1]), row(norm_ffn[1])
    z = _s5_input(xs, cs, nm, sc1, sh1, csc1, csh1)
    wm, wb, wc, lam = _s5_operators(ssm_a_re[0], ssm_a_im[0], ssm_log_dt[0], ssm_b_re[0], ssm_b_im[0],
                                    ssm_c_re[0], ssm_c_im[0])
    ys = _s5_scan(z, wm, wb, wc, lam, c_lo=m // S5_CHUNK, n_out=seq // S5_CHUNK)
    xs = _glu(xs, ys, nm, sc1, sh1, row(ssm_d[0]), g1, ssm_w_glu[0].astype(BF16), ssm_b_glu[0].reshape(1, 2 * d),
              tm=tm)
    xs = _ffn(xs, nf, sc2, sh2, g2, row(norm_final), w1, w3, w2, 1, final_norm=True, tm=tm_ffn)
    return xs[None]
```

```python
import functools
import math

import jax
import jax.numpy as jnp
import numpy as np
from jax import lax
from jax.experimental import pallas as pl
from jax.experimental.pallas import tpu as pltpu

F32 = jnp.float32
BF16 = jnp.bfloat16

GRID_W = 64
HEAD_DIM = 128
NA_HEADS = 8
NB_Q_HEADS = 8
NB_KV_HEADS = 2
NB_GROUP = NB_Q_HEADS // NB_KV_HEADS
NA_ROWS = 8
NA_COLS = 16
SW_BLOCK = 128
SW_QBLOCKS = 4
ROPE_BASE = 10000.0
SSM_GROUP = 16
SSM_STATE = 64
EPS = 1e-6
NEG_INF = -1e30
ATTN_SCALE = HEAD_DIM ** -0.5

A_WIDTH = NA_HEADS * HEAD_DIM
B_Q_WIDTH = NB_Q_HEADS * HEAD_DIM
B_KV_WIDTH = NB_KV_HEADS * HEAD_DIM
QA_BLK = 0
KA_BLK = NA_HEADS
VA_BLK = 2 * NA_HEADS
QB_BLK = 3 * NA_HEADS
KB_BLK = QB_BLK + NB_Q_HEADS
VB_BLK = KB_BLK + NB_KV_HEADS

NA_QROWS = 8
NA_KROWS = 16
NA_KBLK = 4
NA_HEADS_PER_STEP = 8
PROLOGUE_PARTS = 8
S5_CHUNK = 16
LANES = 128
FIELD = SSM_GROUP
FIELDS = LANES // FIELD
VMEM_LIMIT = 56 * 1024 * 1024
FFN_VMEM_LIMIT = 62 * 1024 * 1024


def _cparams(*sem, limit=VMEM_LIMIT):
    return pltpu.CompilerParams(dimension_semantics=sem, vmem_limit_bytes=limit)


def _silu(v):
    return v * (1.0 / (1.0 + jnp.exp(-v)))


def _norm_mod(x, g, sc, sh):
    ms = jnp.mean(x * x, axis=-1, keepdims=True)
    return (x * lax.rsqrt(ms + EPS)) * (g * (1.0 + sc)) + sh


def _dot_nt(a, b, precision=None):
    return lax.dot_general(a, b, (((1,), (1,)), ((), ())), preferred_element_type=F32, precision=precision)


def _ada_kernel(cb_ref, w_ref, b_ref, o_ref, s_scr, *, rows_per_step):
    d, tn = w_ref.shape
    rep = tn // 128

    @pl.when((pl.program_id(0) == 0) & (pl.program_id(1) == 0))
    def _():
        s_scr[...] = _silu(cb_ref[...])

    def body(i, acc):
        r = pl.multiple_of(i * rows_per_step, rows_per_step)
        w = w_ref[pl.ds(r, rows_per_step), :]
        out = []
        for v in range(2):
            s = s_scr[v, pl.ds(r, rows_per_step), :]
            st = jnp.concatenate([s] * rep, axis=1)
            out.append(acc[v] + jnp.sum((w * st).reshape(rows_per_step // 8, 8, tn), axis=0))
        return tuple(out)

    zero = jnp.zeros((8, tn), F32)
    acc = lax.fori_loop(0, d // rows_per_step, body, (zero, zero))
    o_ref[...] = jnp.concatenate([jnp.sum(a, axis=0, keepdims=True) for a in acc], axis=0) + b_ref[...]


def _ada_mod(c, c_ctx, ada_w, ada_b):
    depth, d, n = ada_w.shape
    tn = 1024
    cb = jnp.stack([jnp.broadcast_to(c.reshape(d, 1), (d, 128)),
                    jnp.broadcast_to(c_ctx.reshape(d, 1), (d, 128))])
    return pl.pallas_call(
        functools.partial(_ada_kernel, rows_per_step=64),
        grid=(depth, n // tn),
        in_specs=[pl.BlockSpec((2, d, 128), lambda l, j: (0, 0, 0)),
                  pl.BlockSpec((None, d, tn), lambda l, j: (l, 0, j)),
                  pl.BlockSpec((None, 1, tn), lambda l, j: (l, 0, j))],
        out_specs=pl.BlockSpec((None, 2, tn), lambda l, j: (l, 0, j)),
        out_shape=jax.ShapeDtypeStruct((depth, 2, n), F32),
        scratch_shapes=[pltpu.VMEM((2, d, 128), F32)],
        compiler_params=_cparams("arbitrary", "arbitrary"),
        name="ada_mod",
    )(cb, ada_w, ada_b.reshape(depth, 1, n))


def _rope(a, cos, sin):
    lane = lax.broadcasted_iota(jnp.int32, a.shape, 1)
    partner = jnp.where((lane & 63) < 32, pltpu.roll(a, 96, axis=1), pltpu.roll(a, 32, axis=1))
    return a * cos + partner * sin


def _qkv_kernel(x0_ref, xn_ref, g_ref, sc_ref, sh_ref, w_ref, cos_ref, sin_ref, o_ref, h0_scr, h1_scr, acc_scr,
                *, rope):
    i = pl.program_id(0)
    j = pl.program_id(1)
    tm, tn = o_ref.shape
    heads = tn // HEAD_DIM
    part = tm // PROLOGUE_PARTS

    @pl.when((i == 0) & (j == 0))
    def _():
        h0_scr[...] = _norm_mod(x0_ref[...], g_ref[...], sc_ref[...], sh_ref[...]).astype(BF16)

    def step(h_cur, h_nxt):
        rows = pl.ds(pl.multiple_of(jnp.minimum(j, PROLOGUE_PARTS - 1) * part, part), part)
        h_nxt[rows, :] = _norm_mod(xn_ref[rows, :], g_ref[...], sc_ref[...], sh_ref[...]).astype(BF16)
        acc = jnp.dot(h_cur[...], w_ref[...], preferred_element_type=F32)
        acc_scr[...] = acc
        o_ref[...] = (acc * jnp.where(is_q, ATTN_SCALE, 1.0)).astype(o_ref.dtype)

    col0 = j * heads
    is_qb = (col0 >= QB_BLK) & (col0 < KB_BLK)
    is_kb = col0 == KB_BLK
    is_q = (col0 < KA_BLK) | is_qb

    @pl.when(i % 2 == 0)
    def _():
        step(h0_scr, h1_scr)

    @pl.when(i % 2 == 1)
    def _():
        step(h1_scr, h0_scr)

    if rope:
        def rotated(n_heads, scale):
            parts = []
            for hh in range(heads):
                a = acc_scr[:, hh * HEAD_DIM:(hh + 1) * HEAD_DIM]
                if hh < n_heads:
                    a = _rope(a, cos_ref[...], sin_ref[...]) * scale
                parts.append(a)
            return jnp.concatenate(parts, axis=1)

        @pl.when(is_qb)
        def _():
            o_ref[...] = rotated(heads, ATTN_SCALE).astype(o_ref.dtype)

        @pl.when(is_kb)
        def _():
            o_ref[...] = rotated(NB_KV_HEADS, 1.0).astype(o_ref.dtype)


def _qkv_proj(x, g, sc, sh, w, cos, sin, *, rope, tm):
    m, d = x.shape
    n = w.shape[1]
    tn = 512
    nt = m // tm
    assert KB_BLK % (tn // HEAD_DIM) == 0 and m % tm == 0 and n % tn == 0
    assert n // tn >= PROLOGUE_PARTS and tm % (16 * PROLOGUE_PARTS) == 0
    vec = pl.BlockSpec((1, d), lambda i, j: (0, 0))
    tab = pl.BlockSpec((tm, HEAD_DIM), lambda i, j: (i, 0))
    return pl.pallas_call(
        functools.partial(_qkv_kernel, rope=rope),
        grid=(nt, n // tn),
        in_specs=[pl.BlockSpec((tm, d), lambda i, j: (0, 0), pipeline_mode=pl.Buffered(1)),
                  pl.BlockSpec((tm, d), lambda i, j: (jnp.minimum(i + 1, nt - 1), 0)), vec, vec, vec,
                  pl.BlockSpec((d, tn), lambda i, j: (0, j)), tab, tab],
        out_specs=pl.BlockSpec((tm, tn), lambda i, j: (i, j)),
        out_shape=jax.ShapeDtypeStruct((m, n), BF16),
        scratch_shapes=[pltpu.VMEM((tm, d), BF16), pltpu.VMEM((tm, d), BF16), pltpu.VMEM((tm, tn), F32)],
        compiler_params=_cparams("arbitrary", "arbitrary"),
        name="qkv_rope" if rope else "qkv_ctx",
    )(x, x, g, sc, sh, w, cos, sin)


def _rope_tables(seq):
    quarter = HEAD_DIM // 4
    rows = seq // GRID_W
    f32 = np.float32
    inv_freq = (f32(ROPE_BASE) ** (-np.arange(quarter, dtype=f32) / f32(quarter))).astype(f32)
    ang_r = np.arange(rows, dtype=f32)[:, None] * inv_freq[None, :]
    ang_c = np.arange(GRID_W, dtype=f32)[:, None] * inv_freq[None, :]

    def tokens(tab_r, tab_c, sign):
        r = np.broadcast_to(tab_r[:, None, :], (rows, GRID_W, quarter))
        c = np.broadcast_to(tab_c[None, :, :], (rows, GRID_W, quarter))
        return jnp.asarray(np.concatenate([sign * r, r, sign * c, c], axis=-1).reshape(seq, HEAD_DIM), dtype=F32)

    return tokens(np.cos(ang_r), np.cos(ang_c), f32(1.0)), tokens(np.sin(ang_r), np.sin(ang_c), f32(-1.0))


def _win_kernel(sink_ref, q_ref, *refs, seq):
    nkv = SW_QBLOCKS + 2
    k_refs, v_refs = refs[:nkv], refs[nkv:2 * nkv]
    kc_ref, vc_ref, o_ref, valid_scr = refs[2 * nkv:]
    n = pl.program_id(0)
    blk = SW_BLOCK
    nb = seq // blk
    shape = (NB_GROUP * blk, 3 * blk)

    @pl.when(n == 0)
    def _():
        row = lax.broadcasted_iota(jnp.int32, shape, 0)
        col = lax.broadcasted_iota(jnp.int32, shape, 1)
        rel = col - (row & (blk - 1))
        ok = (rel >= 0) & (rel <= 2 * blk)
        for first in range(2):
            for last in range(2):
                okb = ok
                if first:
                    okb = okb & (col >= blk)
                if last:
                    okb = okb & (col < 2 * blk)
                valid_scr[first + 2 * last] = okb.astype(F32)

    grp = lax.broadcasted_iota(jnp.int32, (NB_GROUP * blk, 1), 0) // blk
    for b in range(SW_QBLOCKS):
        gb = n * SW_QBLOCKS + b
        pattern = (gb == 0).astype(jnp.int32) + 2 * (gb == nb - 1).astype(jnp.int32)
        valid = valid_scr[pattern] > 0.5
        rows = slice(b * blk, (b + 1) * blk)
        outs = []
        for h in range(NB_KV_HEADS):
            hd = slice(h * HEAD_DIM, (h + 1) * HEAD_DIM)
            qs = jnp.concatenate([q_ref[rows, (h * NB_GROUP + g) * HEAD_DIM:(h * NB_GROUP + g + 1) * HEAD_DIM]
                                  for g in range(NB_GROUP)], axis=0)
            k = jnp.concatenate([r[:, hd] for r in k_refs[b:b + 3]], axis=0)
            v = jnp.concatenate([r[:, hd] for r in v_refs[b:b + 3]], axis=0)
            s = jnp.where(valid, _dot_nt(qs, k), NEG_INF)
            s_ctx = _dot_nt(qs, kc_ref[:, hd])
            sink = jnp.zeros((NB_GROUP * blk, 1), F32)
            for g in range(NB_GROUP):
                sink = jnp.where(grp == g, sink_ref[h * NB_GROUP + g], sink)
            m = jnp.maximum(jnp.maximum(jnp.max(s, axis=-1, keepdims=True),
                                        jnp.max(s_ctx, axis=-1, keepdims=True)), sink)
            p = jnp.exp(s - m)
            p_ctx = jnp.exp(s_ctx - m)
            den = jnp.sum(p, axis=-1, keepdims=True) + jnp.sum(p_ctx, axis=-1, keepdims=True) + jnp.exp(sink - m)
            o = (jnp.dot(p.astype(BF16), v, preferred_element_type=F32)
                 + jnp.dot(p_ctx.astype(BF16), vc_ref[:, hd], preferred_element_type=F32))
            o = o * (1.0 / den)
            outs += [o[g * blk:(g + 1) * blk] for g in range(NB_GROUP)]
        o_ref[rows, :] = jnp.concatenate(outs, axis=1).astype(o_ref.dtype)


def _window_attention(qkv, qkv_ctx, sink):
    seq = qkv.shape[0]
    m = qkv_ctx.shape[0]
    nb = seq // SW_BLOCK
    assert QB_BLK % NB_Q_HEADS == 0 and KB_BLK % NB_KV_HEADS == 0 and VB_BLK % NB_KV_HEADS == 0
    assert nb % SW_QBLOCKS == 0
    nkv = SW_QBLOCKS + 2

    def kv_spec(col_blk, shift):
        return pl.BlockSpec((SW_BLOCK, B_KV_WIDTH),
                            lambda n: (jnp.clip(n * SW_QBLOCKS + shift, 0, nb - 1), col_blk // NB_KV_HEADS))

    return pl.pallas_call(
        functools.partial(_win_kernel, seq=seq),
        grid=(nb // SW_QBLOCKS,),
        in_specs=[pl.BlockSpec(memory_space=pltpu.SMEM),
                  pl.BlockSpec((SW_QBLOCKS * SW_BLOCK, B_Q_WIDTH), lambda n: (n, QB_BLK // NB_Q_HEADS))]
                 + [kv_spec(KB_BLK, i - 1) for i in range(nkv)]
                 + [kv_spec(VB_BLK, i - 1) for i in range(nkv)]
                 + [pl.BlockSpec((m, B_KV_WIDTH), lambda n: (0, KB_BLK // NB_KV_HEADS)),
                    pl.BlockSpec((m, B_KV_WIDTH), lambda n: (0, VB_BLK // NB_KV_HEADS))],
        out_specs=pl.BlockSpec((SW_QBLOCKS * SW_BLOCK, B_Q_WIDTH), lambda n: (n, 0)),
        out_shape=jax.ShapeDtypeStruct((seq, B_Q_WIDTH), BF16),
        scratch_shapes=[pltpu.VMEM((4, NB_GROUP * SW_BLOCK, 3 * SW_BLOCK), F32)],
        compiler_params=_cparams("arbitrary"),
        name="window_attn",
    )(sink, qkv, *([qkv] * (2 * nkv)), qkv_ctx, qkv_ctx)


def _na_kernel(rpb_ref, q_ref, k0_ref, k1_ref, k2_ref, k3_ref, v0_ref, v1_ref, v2_ref, v3_ref,
               kc_ref, vc_ref, o_ref, cb_scr, bias_scr, *, rows):
    h0 = pl.program_id(0) * NA_HEADS_PER_STEP
    t = pl.program_id(1)
    n_tiles = rows // NA_QROWS
    n_dr = 2 * NA_ROWS - 1
    n_dc = 2 * NA_COLS - 1
    half = GRID_W

    @pl.when(t == 0)
    def _():
        shape = (GRID_W, 2 * half)
        cq = lax.broadcasted_iota(jnp.int32, shape, 0)
        lane = lax.broadcasted_iota(jnp.int32, shape, 1)
        ck = lane & (half - 1)
        hi = lane >= half
        bidx = jnp.clip(ck - cq + (NA_COLS - 1), 0, n_dc - 1)
        col_ok = (ck - jnp.clip(cq - NA_COLS // 2, 0, GRID_W - NA_COLS))
        col_ok = (col_ok >= 0) & (col_ok < NA_COLS)
        for hh in range(NA_HEADS_PER_STEP):
            for i in range(n_dr + 1):
                val = jnp.full(shape, NEG_INF, F32)
                for e in range(2):
                    a = i - 1 + e
                    if 0 <= a < n_dr:
                        sel = hi if e else jnp.logical_not(hi)
                        for b in range(n_dc):
                            val = jnp.where(sel & (bidx == b), rpb_ref[((h0 + hh) * n_dr + a) * n_dc + b], val)
                cb_scr[hh, i] = jnp.where(col_ok, val, NEG_INF)

    r0 = t * NA_QROWS
    kb0 = jnp.clip(r0 - NA_ROWS // 2, 0, rows - NA_KROWS)

    @pl.when((t <= 1) | (t == n_tiles - 1))
    def _():
        lane = lax.broadcasted_iota(jnp.int32, (GRID_W, 2 * half), 1)
        lo = lane < half
        for rq in range(NA_QROWS):
            rq_abs = r0 + rq
            ws = jnp.clip(rq_abs - NA_ROWS // 2, 0, rows - NA_ROWS)
            for jj in range(NA_KROWS // 2):
                rk0 = kb0 + 2 * jj
                ok0 = (rk0 >= ws) & (rk0 < ws + NA_ROWS)
                ok1 = (rk0 + 1 >= ws) & (rk0 + 1 < ws + NA_ROWS)
                idx = jnp.clip(rk0 - rq_abs + NA_ROWS, 0, n_dr)
                keep = (lo & ok0) | (jnp.logical_not(lo) & ok1)
                for hh in range(NA_HEADS_PER_STEP):
                    bias_scr[hh, rq * GRID_W:(rq + 1) * GRID_W, jj * 2 * half:(jj + 1) * 2 * half] = (
                        jnp.where(keep, cb_scr[hh, idx], NEG_INF))

    outs = []
    for hh in range(NA_HEADS_PER_STEP):
        hd = slice(hh * HEAD_DIM, (hh + 1) * HEAD_DIM)
        q = q_ref[:, hd]
        k = jnp.concatenate([k0_ref[:, hd], k1_ref[:, hd], k2_ref[:, hd], k3_ref[:, hd]], axis=0)
        v = jnp.concatenate([v0_ref[:, hd], v1_ref[:, hd], v2_ref[:, hd], v3_ref[:, hd]], axis=0)
        s = _dot_nt(q, k) + bias_scr[hh]
        s_ctx = _dot_nt(q, kc_ref[:, hd])
        m = jnp.maximum(jnp.max(s, axis=-1, keepdims=True), jnp.max(s_ctx, axis=-1, keepdims=True))
        p = jnp.exp(s - m)
        p_ctx = jnp.exp(s_ctx - m)
        den = jnp.sum(p, axis=-1, keepdims=True) + jnp.sum(p_ctx, axis=-1, keepdims=True)
        o = (jnp.dot(p.astype(BF16), v, preferred_element_type=F32)
             + jnp.dot(p_ctx.astype(BF16), vc_ref[:, hd], preferred_element_type=F32))
        outs.append(o * (1.0 / den))
    o_ref[...] = jnp.concatenate(outs, axis=1).astype(o_ref.dtype)


def _neighbourhood_attention(qkv, qkv_ctx, rpb):
    seq = qkv.shape[0]
    m = qkv_ctx.shape[0]
    rows = seq // GRID_W
    assert rows % NA_QROWS == 0 and rows >= NA_KROWS + NA_QROWS
    n_tiles = rows // NA_QROWS
    tq = NA_QROWS * GRID_W
    tk = NA_KBLK * GRID_W
    n_kblk = NA_KROWS // NA_KBLK

    hp = NA_HEADS_PER_STEP
    wh = hp * HEAD_DIM
    assert NA_HEADS % hp == 0 and QA_BLK % hp == 0 and KA_BLK % hp == 0 and VA_BLK % hp == 0

    def kv_spec(col_blk, i):
        def index(h, t):
            first = jnp.clip(t * (NA_QROWS // NA_KBLK) - 1, 0, rows // NA_KBLK - n_kblk)
            return (first + i, col_blk // hp + h)
        return pl.BlockSpec((tk, wh), index)

    return pl.pallas_call(
        functools.partial(_na_kernel, rows=rows),
        grid=(NA_HEADS // hp, n_tiles),
        in_specs=[pl.BlockSpec(memory_space=pltpu.SMEM),
                  pl.BlockSpec((tq, wh), lambda h, t: (t, QA_BLK // hp + h))]
                 + [kv_spec(KA_BLK, i) for i in range(n_kblk)]
                 + [kv_spec(VA_BLK, i) for i in range(n_kblk)]
                 + [pl.BlockSpec((m, wh), lambda h, t: (0, KA_BLK // hp + h)),
                    pl.BlockSpec((m, wh), lambda h, t: (0, VA_BLK // hp + h))],
        out_specs=pl.BlockSpec((tq, wh), lambda h, t: (t, h)),
        out_shape=jax.ShapeDtypeStruct((seq, A_WIDTH), BF16),
        scratch_shapes=[pltpu.VMEM((hp, 2 * NA_ROWS, GRID_W, 2 * GRID_W), F32),
                        pltpu.VMEM((hp, tq, NA_KROWS * GRID_W), F32)],
        compiler_params=_cparams("arbitrary", "arbitrary"),
        name="neighbourhood_attn",
    )(rpb.reshape(-1), qkv, *([qkv] * (2 * n_kblk)), qkv_ctx, qkv_ctx)


def _ctx_attn_kernel(sink_ref, q_ref, k_ref, v_ref, o_ref):
    j = pl.program_id(0)
    s = _dot_nt(q_ref[...], k_ref[...])
    has_sink = j >= NA_HEADS
    sink = jnp.where(has_sink, sink_ref[jnp.maximum(j - NA_HEADS, 0)], NEG_INF)
    m = jnp.maximum(jnp.max(s, axis=-1, keepdims=True), sink)
    p = jnp.exp(s - m)
    den = jnp.sum(p, axis=-1, keepdims=True) + jnp.where(has_sink, jnp.exp(sink - m), 0.0)
    o = jnp.dot(p.astype(BF16), v_ref[...], preferred_element_type=F32)
    o_ref[...] = (o * (1.0 / den)).astype(o_ref.dtype)


def _context_attention(qkv_ctx, sink):
    m = qkv_ctx.shape[0]

    def q_idx(j):
        return (0, jnp.where(j < NA_HEADS, QA_BLK + j, QB_BLK + j - NA_HEADS))

    def k_idx(j):
        return (0, jnp.where(j < NA_HEADS, KA_BLK + j, KB_BLK + (j - NA_HEADS) // NB_GROUP))

    def v_idx(j):
        return (0, jnp.where(j < NA_HEADS, VA_BLK + j, VB_BLK + (j - NA_HEADS) // NB_GROUP))

    return pl.pallas_call(
        _ctx_attn_kernel,
        grid=(NA_HEADS + NB_Q_HEADS,),
        in_specs=[pl.BlockSpec(memory_space=pltpu.SMEM),
                  pl.BlockSpec((m, HEAD_DIM), q_idx),
                  pl.BlockSpec((m, HEAD_DIM), k_idx),
                  pl.BlockSpec((m, HEAD_DIM), v_idx)],
        out_specs=pl.BlockSpec((m, HEAD_DIM), lambda j: (0, j)),
        out_shape=jax.ShapeDtypeStruct((m, A_WIDTH + B_Q_WIDTH), BF16),
        compiler_params=_cparams("arbitrary"),
        name="context_attn",
    )(sink, qkv_ctx, qkv_ctx, qkv_ctx)


def _out_proj_kernel(a_ref, b_ref, wa_ref, wb_ref, x_ref, g_ref, o_ref):
    y = (jnp.dot(a_ref[...], wa_ref[...], preferred_element_type=F32)
         + jnp.dot(b_ref[...], wb_ref[...], preferred_element_type=F32))
    o_ref[...] = x_ref[...] + g_ref[...] * y


def _out_proj(oa, ob, a_blk, b_blk, w, x, gate, *, tm):
    m, d = x.shape
    kh = w.shape[0] // 2
    return pl.pallas_call(
        _out_proj_kernel,
        grid=(m // tm,),
        in_specs=[pl.BlockSpec((tm, kh), lambda i: (i, a_blk)),
                  pl.BlockSpec((tm, kh), lambda i: (i, b_blk)),
                  pl.BlockSpec((kh, d), lambda i: (0, 0)),
                  pl.BlockSpec((kh, d), lambda i: (1, 0)),
                  pl.BlockSpec((tm, d), lambda i: (i, 0)),
                  pl.BlockSpec((1, d), lambda i: (0, 0))],
        out_specs=pl.BlockSpec((tm, d), lambda i: (i, 0)),
        out_shape=jax.ShapeDtypeStruct((m, d), F32),
        compiler_params=_cparams("arbitrary"),
        name="out_proj",
    )(oa, ob, w, w, x, gate)


def _ffn_kernel(x_ref, g_ref, sc_ref, sh_ref, gate_ref, gf_ref, w1_ref, w3_ref, w2_ref, o_ref, h_scr, act_scr,
                *, final_norm, n_up):
    s = pl.program_id(1)
    tf = w1_ref.shape[1]
    tn = w2_ref.shape[1]

    @pl.when(s == 0)
    def _():
        h_scr[...] = _norm_mod(x_ref[...], g_ref[...], sc_ref[...], sh_ref[...]).astype(BF16)

    @pl.when(s < n_up)
    def _():
        h = h_scr[...]
        a = jnp.dot(h, w1_ref[...], preferred_element_type=F32)
        b = jnp.dot(h, w3_ref[...], preferred_element_type=F32)
        act_scr[:, pl.ds(pl.multiple_of(s * tf, tf), tf)] = (_silu(a) * b).astype(BF16)

    @pl.when(s >= n_up)
    def _():
        cols = pl.ds(pl.multiple_of((s - n_up) * tn, tn), tn)
        y = jnp.dot(act_scr[...], w2_ref[...], preferred_element_type=F32)
        r = x_ref[:, cols] + gate_ref[:, cols] * y
        if final_norm:
            o_ref[:, cols] = r
        else:
            o_ref[...] = r

    if final_norm:
        @pl.when(s == pl.num_programs(1) - 1)
        def _():
            r = o_ref[...]
            ms = jnp.mean(r * r, axis=-1, keepdims=True)
            o_ref[...] = r * lax.rsqrt(ms + EPS) * gf_ref[...]


def _ffn(x, g, sc, sh, gate, gf, w1, w3, w2, layer, *, final_norm, tm):
    m, d = x.shape
    dff = w1.shape[-1]
    tf, tn = 512, (256 if final_norm else 512)
    assert dff % tf == 0 and d % tn == 0
    n_up, n_dn = dff // tf, d // tn
    vec = pl.BlockSpec((1, d), lambda i, s: (0, 0))
    once = pl.Buffered(1) if m > tm else None
    return pl.pallas_call(
        functools.partial(_ffn_kernel, final_norm=final_norm, n_up=n_up),
        grid=(m // tm, n_up + n_dn),
        in_specs=[pl.BlockSpec((tm, d), lambda i, s: (i, 0)), vec, vec, vec, vec, vec,
                  pl.BlockSpec((None, d, tf), lambda i, s: (layer, 0, jnp.minimum(s, n_up - 1))),
                  pl.BlockSpec((None, d, tf), lambda i, s: (layer, 0, jnp.minimum(s, n_up - 1))),
                  pl.BlockSpec((None, dff, tn), lambda i, s: (layer, 0, jnp.maximum(s - n_up, 0)))],
        out_specs=(pl.BlockSpec((tm, d), lambda i, s: (i, 0), pipeline_mode=once) if final_norm else
                   pl.BlockSpec((tm, tn), lambda i, s: (i, jnp.maximum(s - n_up, 0)))),
        out_shape=jax.ShapeDtypeStruct((m, d), F32),
        scratch_shapes=[pltpu.VMEM((tm, d), BF16), pltpu.VMEM((tm, dff), BF16)],
        compiler_params=_cparams("arbitrary", "arbitrary", limit=FFN_VMEM_LIMIT),
        name="swiglu_final" if final_norm else "swiglu",
    )(x, g, sc, sh, gate, gf, w1, w3, w2)


def _chunk_perm(n):
    r = lax.broadcasted_iota(jnp.int32, (n, n), 0)
    c = lax.broadcasted_iota(jnp.int32, (n, n), 1)
    nch = n // S5_CHUNK
    return (c == (r % nch) * S5_CHUNK + r // nch).astype(BF16)


def _field_transpose(v):
    v = list(v)
    lane = lax.broadcasted_iota(jnp.int32, v[0].shape, 1)
    s = FIELDS // 2
    while s:
        keep = ((lane // FIELD) & s) == 0
        for i in range(FIELDS):
            if i & s:
                continue
            a, b = v[i], v[i + s]
            v[i] = jnp.where(keep, a, pltpu.roll(b, FIELD * s, axis=1))
            v[i + s] = jnp.where(keep, pltpu.roll(a, LANES - FIELD * s, axis=1), b)
        s //= 2
    return v


def _s5_input_kernel(x_ref, c_ref, g_ref, scx_ref, shx_ref, scc_ref, shc_ref, o_ref):
    i = pl.program_id(0)
    is_ctx = (i == 0) | (i == pl.num_programs(0) - 1)
    n = x_ref.shape[0]

    def emit(h):
        hp = jnp.dot(_chunk_perm(n), h.astype(BF16), preferred_element_type=F32)
        o_ref[...] = hp.astype(BF16).reshape(o_ref.shape)

    @pl.when(is_ctx)
    def _():
        emit(_norm_mod(c_ref[...], g_ref[...], scc_ref[...], shc_ref[...]))

    @pl.when(jnp.logical_not(is_ctx))
    def _():
        emit(_norm_mod(x_ref[...], g_ref[...], scx_ref[...], shx_ref[...]))


def _s5_input(x, ctx, g, scx, shx, scc, shc):
    seq, d = x.shape
    m = ctx.shape[0]
    assert m % (S5_CHUNK * 8) == 0 and seq % m == 0
    nx = seq // m
    nc = (seq + 2 * m) // S5_CHUNK
    vec = pl.BlockSpec((1, d), lambda i: (0, 0))
    return pl.pallas_call(
        _s5_input_kernel,
        grid=(nx + 2,),
        in_specs=[pl.BlockSpec((m, d), lambda i: (jnp.clip(i - 1, 0, nx - 1), 0)),
                  pl.BlockSpec((m, d), lambda i: (0, 0)), vec, vec, vec, vec, vec],
        out_specs=pl.BlockSpec((S5_CHUNK, m // S5_CHUNK, d), lambda i: (0, i, 0)),
        out_shape=jax.ShapeDtypeStruct((S5_CHUNK, nc, d), BF16),
        compiler_params=_cparams("arbitrary"),
        name="s5_input",
    )(x, ctx, g, scx, shx, scc, shc)


def _sublane_transpose(v):
    n = v[0].shape[0]
    v = [a.reshape(n // 8, 8, LANES) for a in v]
    row = lax.broadcasted_iota(jnp.int32, v[0].shape, 1)
    s = 4
    while s:
        keep = (row & s) == 0
        for i in range(8):
            if i & s:
                continue
            a, b = v[i], v[i + s]
            v[i] = jnp.where(keep, a, pltpu.roll(b, s, axis=1))
            v[i + s] = jnp.where(keep, pltpu.roll(a, 8 - s, axis=1), b)
        s //= 2
    return [a.reshape(n, LANES) for a in v]


def _s5_kernel(z_ref, wm_ref, wb_ref, wc_ref, lam_ref, o_ref, u_scr, b_scr, hf_scr, hb_scr, *, c_lo):
    L, nc, _ = z_ref.shape
    n_out = o_ref.shape[0] // L
    gs = FIELDS
    half = SSM_STATE
    for j in range(L // FIELDS):
        v = [pltpu.bitcast(z_ref[j * FIELDS + i], jnp.uint32) for i in range(FIELDS)]
        w = _field_transpose(v)
        for gi in range(gs):
            u_scr[gi, :, j * LANES:(j + 1) * LANES] = pltpu.bitcast(w[gi], BF16)

    bs = [jnp.dot(u_scr[gi], wb_ref[gi], preferred_element_type=F32) for gi in range(gs)]
    for comp in range(2):
        t = _sublane_transpose([b[:, comp * LANES:(comp + 1) * LANES] for b in bs])
        for j in range(8):
            b_scr[comp, j] = t[j]

    a_re = lam_ref[:, :2 * half]
    a_im = lam_ref[:, 2 * half:]
    fwd = lax.broadcasted_iota(jnp.int32, (gs, 2 * half), 1) < half

    def step(t, carry):
        s_re, s_im = carry
        rf = pl.ds(pl.multiple_of(t * 8, 8), 8)
        rb = pl.ds(pl.multiple_of(nc - 8 - t * 8, 8), 8)
        for jf in range(8):
            jb = 7 - jf
            hf_scr[0, jf, rf, :] = s_re
            hf_scr[1, jf, rf, :] = s_im
            hb_scr[0, jb, rb, :] = s_re
            hb_scr[1, jb, rb, :] = s_im
            v_re = jnp.where(fwd, b_scr[0, jf, rf, :], b_scr[0, jb, rb, :])
            v_im = jnp.where(fwd, b_scr[1, jf, rf, :], b_scr[1, jb, rb, :])
            s_re, s_im = a_re * s_re - a_im * s_im + v_re, a_re * s_im + a_im * s_re + v_im
        return s_re, s_im

    zero = jnp.zeros((gs, 2 * half), F32)
    lax.fori_loop(0, nc // 8, step, (zero, zero))

    fwd_all = lax.broadcasted_iota(jnp.int32, (nc, 2 * half), 1) < half
    h_in = [_sublane_transpose([jnp.where(fwd_all, hf_scr[comp, j], hb_scr[comp, j]) for j in range(8)])
            for comp in range(2)]
    ys = []
    for gi in range(gs):
        hcat = jnp.concatenate([h_in[0][gi], h_in[1][gi]], axis=1)[c_lo:c_lo + n_out].astype(BF16)
        ys.append(jnp.dot(u_scr[gi, c_lo:c_lo + n_out, :], wm_ref[gi], preferred_element_type=F32)
                  + _dot_nt(hcat, wc_ref[gi]))
    for j in range(L // FIELDS):
        w = _field_transpose([y[:, j * LANES:(j + 1) * LANES] for y in ys])
        for i in range(FIELDS):
            o_ref[pl.ds(j * FIELDS + i, n_out, stride=L), :] = w[i]


def _s5_scan(zp, wm, wb, wc, lam, *, c_lo, n_out):
    L, nc, d = zp.shape
    w = wm.shape[-1]
    gs = FIELDS
    assert nc % 8 == 0 and gs == 8
    slab = pltpu.VMEM((2, 8, nc, 2 * SSM_STATE), F32)
    return pl.pallas_call(
        functools.partial(_s5_kernel, c_lo=c_lo),
        grid=(d // LANES,),
        in_specs=[pl.BlockSpec((L, nc, LANES), lambda i: (0, 0, i)),
                  pl.BlockSpec((gs, w, w), lambda i: (i, 0, 0)),
                  pl.BlockSpec((gs, w, w), lambda i: (i, 0, 0)),
                  pl.BlockSpec((gs, w, w), lambda i: (i, 0, 0)),
                  pl.BlockSpec((gs, w), lambda i: (i, 0))],
        out_specs=pl.BlockSpec((L * n_out, LANES), lambda i: (0, i)),
        out_shape=jax.ShapeDtypeStruct((L * n_out, d), F32),
        scratch_shapes=[pltpu.VMEM((gs, nc, w), BF16), slab, slab, slab],
        compiler_params=_cparams("arbitrary"),
        name="s5_scan",
    )(zp, wm, wb, wc, lam)


def _shift_lanes(x, s):
    w = LANES
    x0, x1 = x[:, :w], x[:, w:]
    zero = jnp.zeros_like(x0)
    lane = lax.broadcasted_iota(jnp.int32, x0.shape, 1)
    if s == 0:
        return x
    if s > 0:
        if s >= w:
            t = s - w
            y1 = x0 if t == 0 else jnp.where(lane < t, 0.0, pltpu.roll(x0, t, axis=1))
            return jnp.concatenate([zero, y1], axis=1)
        r0, r1 = pltpu.roll(x0, s, axis=1), pltpu.roll(x1, s, axis=1)
        return jnp.concatenate([jnp.where(lane < s, 0.0, r0), jnp.where(lane < s, r0, r1)], axis=1)
    s = -s
    if s >= w:
        t = s - w
        y0 = x1 if t == 0 else jnp.where(lane >= w - t, 0.0, pltpu.roll(x1, w - t, axis=1))
        return jnp.concatenate([y0, zero], axis=1)
    r0, r1 = pltpu.roll(x0, w - s, axis=1), pltpu.roll(x1, w - s, axis=1)
    return jnp.concatenate([jnp.where(lane >= w - s, r1, r0), jnp.where(lane >= w - s, 0.0, r1)], axis=1)


def _s5_op_kernel(a_re_ref, a_im_ref, ldt_ref, b_re_ref, b_im_ref, c_re_ref, c_im_ref,
                  wm_ref, wb_ref, wc_ref, lam_ref, e_scr):
    gs = a_re_ref.shape[0]
    L = S5_CHUNK
    h = SSM_GROUP
    half = SSM_STATE
    fwd1 = lax.broadcasted_iota(jnp.int32, (1, 2 * half), 1) < half
    fwd = lax.broadcasted_iota(jnp.int32, (h, 2 * half), 1) < half

    def cmul(ar, ai, br, bi):
        return ar * br - ai * bi, ar * bi + ai * br

    for gi in range(gs):
        ar, ai = a_re_ref[gi], a_im_ref[gi]
        dt = jnp.exp(ldt_ref[gi])
        mag = jnp.exp(ar * dt)
        lr, li = mag * jnp.cos(ai * dt), mag * jnp.sin(ai * dt)
        den = ar * ar + ai * ai
        nr = lr - 1.0
        coef_r = (nr * ar + li * ai) / den
        coef_i = (li * ar - nr * ai) / den
        bb_r, bb_i = cmul(coef_r, coef_i, b_re_ref[gi], b_im_ref[gi])
        c_r, c_i = c_re_ref[gi], c_im_ref[gi]
        pw = [(jnp.ones_like(lr), jnp.zeros_like(lr))]
        for _ in range(L):
            pw.append(cmul(pw[-1][0], pw[-1][1], lr, li))

        def mixed(jf, jb):
            return (jnp.where(fwd1, pw[jf][0], pw[jb][0]), jnp.where(fwd1, pw[jf][1], pw[jb][1]))

        for l in range(L):
            s_r, s_i = cmul(*mixed(L - 1 - l, l), bb_r, bb_i)
            wb_ref[gi, l * h:(l + 1) * h, :] = jnp.concatenate([s_r, s_i], axis=1).astype(BF16)
            f_r, f_i = cmul(*mixed(l + 1, L - l), c_r, c_i)
            wc_ref[gi, l * h:(l + 1) * h, :] = jnp.concatenate([f_r, -f_i], axis=1).astype(BF16)
            e_r, e_i = cmul(*mixed(l, L - 1 - l), c_r, c_i)
            e_scr[l * h:(l + 1) * h, :] = jnp.concatenate([e_r, e_i], axis=1)

        lhs = jnp.concatenate(
            [jnp.concatenate([jnp.where(fwd, bb_r, 0.0), jnp.where(fwd, -bb_i, 0.0)], axis=1),
             jnp.concatenate([jnp.where(fwd, 0.0, bb_r), jnp.where(fwd, 0.0, -bb_i)], axis=1)], axis=0)
        kt = _dot_nt(lhs, e_scr[...], precision=lax.Precision.HIGHEST)
        kt_f, kt_b = kt[:h], kt[h:]
        for l in range(L):
            blk = _shift_lanes(kt_f, h * l) + _shift_lanes(kt_b, -h * (L - 1 - l))
            wm_ref[gi, l * h:(l + 1) * h, :] = blk.astype(BF16)
        lam_ref[gi:gi + 1, :] = jnp.concatenate([pw[L][0], pw[L][1]], axis=1)


def _s5_operators(a_re, a_im, log_dt, b_re, b_im, c_re, c_im):
    _, g, p = a_re.shape
    h = b_re.shape[-1]
    gs = FIELDS
    w = S5_CHUNK * h
    assert w == 2 * LANES and 2 * p == LANES

    def lanes(v):
        return jnp.concatenate([v[0], v[1]], axis=-1)

    def chan_rows(v):
        return lanes(jnp.swapaxes(v, -1, -2))

    a_re2 = lanes(a_re).reshape(g, 1, 2 * p)
    a_im2 = lanes(a_im).reshape(g, 1, 2 * p)
    ldt2 = jnp.repeat(log_dt.T, p, axis=1).reshape(g, 1, 2 * p)
    vec = pl.BlockSpec((gs, 1, 2 * p), lambda i: (i, 0, 0))
    mat = pl.BlockSpec((gs, h, 2 * p), lambda i: (i, 0, 0))
    return pl.pallas_call(
        _s5_op_kernel,
        grid=(g // gs,),
        in_specs=[vec, vec, vec, mat, mat, mat, mat],
        out_specs=[pl.BlockSpec((gs, w, w), lambda i: (i, 0, 0)),
                   pl.BlockSpec((gs, w, w), lambda i: (i, 0, 0)),
                   pl.BlockSpec((gs, w, w), lambda i: (i, 0, 0)),
                   pl.BlockSpec((gs, w), lambda i: (i, 0))],
        out_shape=[jax.ShapeDtypeStruct((g, w, w), BF16), jax.ShapeDtypeStruct((g, w, w), BF16),
                   jax.ShapeDtypeStruct((g, w, w), BF16), jax.ShapeDtypeStruct((g, w), F32)],
        scratch_shapes=[pltpu.VMEM((w, w), F32)],
        compiler_params=_cparams("arbitrary"),
        name="s5_operators",
    )(a_re2, a_im2, ldt2, chan_rows(b_re), chan_rows(b_im), lanes(c_re), lanes(c_im))


def _gelu_tanh(y):
    k = 2.0 * math.sqrt(2.0 / math.pi) * math.log2(math.e)
    return y * (1.0 / (1.0 + jnp.exp2(y * (-k - (k * 0.044715) * (y * y)))))


def _glu_kernel(x_ref, ys_ref, g_ref, sc_ref, sh_ref, d_ref, gate_ref, b_ref, w_ref, o_ref, act_scr):
    j = pl.program_id(1)
    tn = o_ref.shape[1]
    d_model = x_ref.shape[1]

    @pl.when(j == 0)
    def _():
        rb = 32

        def rows_pass(r, carry):
            rows = pl.ds(pl.multiple_of(r * rb, rb), rb)
            hx = _norm_mod(x_ref[rows, :], g_ref[...], sc_ref[...], sh_ref[...])
            act_scr[rows, :] = _gelu_tanh(d_ref[...] * hx + ys_ref[rows, :]).astype(BF16)
            return carry

        lax.fori_loop(0, x_ref.shape[0] // rb, rows_pass, 0, unroll=4)

    act = act_scr[...]
    cols = pl.ds(pl.multiple_of(j * tn, tn), tn)
    gcols = pl.ds(pl.multiple_of(d_model + j * tn, tn), tn)
    val = jnp.dot(act, w_ref[:, cols], preferred_element_type=F32) + b_ref[:, cols]
    gate = jnp.dot(act, w_ref[:, gcols], preferred_element_type=F32) + b_ref[:, gcols]
    o_ref[...] = x_ref[:, cols] + gate_ref[:, cols] * (val * (1.0 / (1.0 + jnp.exp(-gate))))


def _glu(x, ys, g, sc, sh, d_skip, gate, w, b, *, tm):
    m, d = x.shape
    tn = 2048
    nj = d // tn
    vec = pl.BlockSpec((1, d), lambda i, j: (0, 0))
    return pl.pallas_call(
        _glu_kernel,
        grid=(m // tm, nj),
        in_specs=[pl.BlockSpec((tm, d), lambda i, j: (i, 0)),
                  pl.BlockSpec((tm, d), lambda i, j: (i, 0)), vec, vec, vec, vec, vec,
                  pl.BlockSpec((1, 2 * d), lambda i, j: (0, 0)),
                  pl.BlockSpec((d, 2 * d), lambda i, j: (0, 0), pipeline_mode=pl.Buffered(1))],
        out_specs=pl.BlockSpec((tm, tn), lambda i, j: (i, j)),
        out_shape=jax.ShapeDtypeStruct((m, d), F32),
        scratch_shapes=[pltpu.VMEM((tm, d), BF16)],
        compiler_params=_cparams("arbitrary", "arbitrary"),
        name="s5_glu",
    )(x, ys, g, sc, sh, d_skip, gate, b, w)


def kernel(x, c, ctx, c_ctx, ada_w, ada_b, norm_mix, norm_ffn, ffn_w1, ffn_w3, ffn_w2, attn_w_in, attn_w_out,
           attn_rpb, attn_sink, ssm_a_re, ssm_a_im, ssm_log_dt, ssm_b_re, ssm_b_im, ssm_c_re, ssm_c_im,
           ssm_d, ssm_w_glu, ssm_b_glu, norm_final):
    batch, seq, d = x.shape
    assert batch == 1 and ada_w.shape[0] == 2
    m = ctx.shape[1]
    xs, cs = x[0], ctx[0]
    mods = _ada_mod(c, c_ctx, ada_w, ada_b)

    def mod(layer, who):
        return [mods[layer, who, i * d:(i + 1) * d].reshape(1, d) for i in range(6)]

    row = lambda v: v.reshape(1, d)
    tm = 512
    tm_qkv = 1024
    tm_ffn = 1024

    sh1, sc1, g1, sh2, sc2, g2 = mod(0, 0)
    csh1, csc1, cg1, csh2, csc2, cg2 = mod(0, 1)
    cos, sin = _rope_tables(seq)
    w_in = attn_w_in[0].astype(BF16)
    w_out = attn_w_out[0].astype(BF16)
    w1, w3, w2 = ffn_w1.astype(BF16), ffn_w3.astype(BF16), ffn_w2.astype(BF16)
    nm, nf = row(norm_mix[0]), row(norm_ffn[0])
    qkv = _qkv_proj(xs, nm, sc1, sh1, w_in, cos, sin, rope=True, tm=tm_qkv)
    qkv_c = _qkv_proj(cs, nm, csc1, csh1, w_in, cos[:m], sin[:m], rope=False, tm=m)
    oa = _neighbourhood_attention(qkv, qkv_c, attn_rpb[0])
    ob = _window_attention(qkv, qkv_c, attn_sink[0])
    oc = _context_attention(qkv_c, attn_sink[0])
    xs = _out_proj(oa, ob, 0, 0, w_out, xs, g1, tm=tm)
    cs = _out_proj(oc, oc, 0, 1, w_out, cs, cg1, tm=m)
    xs = _ffn(xs, nf, sc2, sh2, g2, nf, w1, w3, w2, 0, final_norm=False, tm=tm_ffn)
    cs = _ffn(cs, nf, csc2, csh2, cg2, nf, w1, w3, w2, 0, final_norm=False, tm=m)

    sh1, sc1, g1, sh2, sc2, g2 = mod(1, 0)
    csh1, csc1, _, _, _, _ = mod(1, 1)
    nm, nf = row(norm_mix[1]), row(norm_ffn[1])
    z = _s5_input(xs, cs, nm, sc1, sh1, csc1, csh1)
    wm, wb, wc, lam = _s5_operators(ssm_a_re[0], ssm_a_im[0], ssm_log_dt[0], ssm_b_re[0], ssm_b_im[0],
                                    ssm_c_re[0], ssm_c_im[0])
    ys = _s5_scan(z, wm, wb, wc, lam, c_lo=m // S5_CHUNK, n_out=seq // S5_CHUNK)
    xs = _glu(xs, ys, nm, sc1, sh1, row(ssm_d[0]), g1, ssm_w_glu[0].astype(BF16), ssm_b_glu[0].reshape(1, 2 * d),
              tm=tm)
    xs = _ffn(xs, nf, sc2, sh2, g2, row(norm_final), w1, w3, w2, 1, final_norm=True, tm=tm_ffn)
    return xs[None]
```
